```python
import math
import jax
import jax.numpy as jnp
from jax import lax
import numpy as np

D_MODEL = 2048
BATCH = 4
SEQ = 2048
DEPTH = 2
DEC_BATCH = 128
DEC_SEQ = 4
PAST_LEN = 8192
PAGE_SIZE = 128

N_AB_LAYERS = (DEPTH + 1) // 2
N_C_LAYERS = DEPTH // 2

A_HEADS = 8
A_DK = 128
A_DV = 128
A_KEY_WIDTH = A_HEADS * A_DK
A_VAL_WIDTH = A_HEADS * A_DV
A_CHUNK = 64

B_HEADS = 8
B_HEAD_DIM = 128
B_WIDTH = B_HEADS * B_HEAD_DIM
DILATED_PATTERNS = ((128, 1), (512, 4), (2048, 16))
B_MAX_WINDOW = 2048
B_QBLOCK = 32

REL_BUCKETS = 32
REL_MAX_DISTANCE = 2048

AB_SPLITS = (A_KEY_WIDTH, 2 * A_KEY_WIDTH, 2 * A_KEY_WIDTH + A_VAL_WIDTH, 2 * A_KEY_WIDTH + 2 * A_VAL_WIDTH, 2 * A_KEY_WIDTH + 2 * A_VAL_WIDTH + B_WIDTH, 2 * A_KEY_WIDTH + 2 * A_VAL_WIDTH + 2 * B_WIDTH)
AB_IN_WIDTH = 2 * A_KEY_WIDTH + 2 * A_VAL_WIDTH + 3 * B_WIDTH
AB_OUT_WIDTH = A_VAL_WIDTH + B_WIDTH

C_HEADS = 16
C_NOPE = 128
C_ROPE = 64
C_V = 128
C_Q_LORA = 512
C_KV_LORA = 512
C_IN_WIDTH = C_Q_LORA + C_KV_LORA + C_ROPE
C_QBLOCK = 128
ROPE_THETA = 10000.0

N_EXPERTS = 32
TOP_K = 4
D_EXPERT = 2048
SWIGLU_LIMIT = 7.0
SWIGLU_ALPHA = 1.702
MOE_BLOCK = 64

EPS = 1e-6
F32 = jnp.float32

kernel_name = 'hybrid_hgrn2_dilated_mla_moe_step'


def rms_norm(x, g):
    xf = x.astype(F32)
    xf = xf * lax.rsqrt(jnp.mean(xf * xf, axis=-1, keepdims=True) + EPS)
    return xf * g.astype(F32)


def adaln(c, w, b):
    m = jax.nn.silu(c.astype(F32)) @ w + b
    shift, scale, gate = jnp.split(m[:, None, :], 3, axis=-1)
    return shift, scale, gate


def rel_bucket(dist):
    exact = REL_BUCKETS // 2
    far = exact + (jnp.log(jnp.maximum(dist, 1).astype(F32) / exact) / math.log(REL_MAX_DISTANCE / exact) * (REL_BUCKETS - exact)).astype(jnp.int32)
    return jnp.where(dist < exact, dist, jnp.minimum(far, REL_BUCKETS - 1))


def rope_angles(pos):
    half = C_ROPE // 2
    inv = ROPE_THETA ** (-jnp.arange(half, dtype=F32) / half)
    ang = pos.astype(F32)[:, None] * inv[None, :]
    return jnp.cos(ang), jnp.sin(ang)


def apply_rope(x, cos, sin):
    x1, x2 = jnp.split(x, 2, axis=-1)
    return jnp.concatenate([x1 * cos - x2 * sin, x1 * sin + x2 * cos], axis=-1)


def hgrn2_chunked(q, log_f, k, v, s0):
    b, t, h, dk = q.shape
    dv = v.shape[-1]
    csz = min(A_CHUNK, t)
    n = t // csz

    def chunks(a):
        return a.reshape(b, n, csz, h, a.shape[-1]).transpose(1, 0, 3, 2, 4)

    causal = jnp.tril(jnp.ones((csz, csz), dtype=bool))

    def step(s, xs):
        qc, gc, kc, vc = xs
        cum = jnp.cumsum(gc, axis=2)
        o_state = jnp.einsum('bhtd,bhde->bhte', qc * jnp.exp(cum), s)
        rel = cum[:, :, :, None, :] - cum[:, :, None, :, :]
        decay = jnp.exp(jnp.where(causal[:, :, None], rel, -jnp.inf))
        scores = jnp.einsum('bhtd,bhsd,bhtsd->bhts', qc, kc, decay)
        o = o_state + jnp.einsum('bhts,bhse->bhte', scores, vc)
        last = cum[:, :, -1, :]
        s = jnp.exp(last)[..., None] * s + jnp.einsum('bhsd,bhse->bhde', kc * jnp.exp(last[:, :, None, :] - cum), vc)
        return s, o

    s_fin, o = lax.scan(step, s0, (chunks(q), chunks(log_f), chunks(k), chunks(v)))
    return o.transpose(1, 0, 3, 2, 4).reshape(b, t, h, dv), s_fin


def dilated_core(q, ks, vs, q_idx, rel_bias):
    scale = B_HEAD_DIM ** -0.5
    maxes, dens, outs = [], [], []
    for window, dil in DILATED_PATTERNS:
        dist = jnp.arange(window // dil + 1) * dil
        idx = q_idx[:, None] - dist[None, :]
        valid = idx >= 0
        idx = jnp.maximum(idx, 0)
        kg = jnp.take(ks, idx, axis=1)
        vg = jnp.take(vs, idx, axis=1)
        bias = rel_bias.astype(F32)[rel_bucket(dist)].T
        logits = jnp.einsum('bthe,btmhe->bthm', q, kg).astype(F32) * scale + bias
        logits = jnp.where(valid[None, :, None, :], logits, -jnp.inf)
        mx = jnp.max(logits, axis=-1)
        p = jnp.exp(logits - mx[..., None])
        den = jnp.sum(p, axis=-1)
        outs.append(jnp.einsum('bthm,btmhe->bthe', p, vg.astype(F32)) / den[..., None])
        maxes.append(mx)
        dens.append(den)
    m_all = maxes[0]
    for mx in maxes[1:]:
        m_all = jnp.maximum(m_all, mx)
    num = jnp.zeros_like(outs[0])
    tot = jnp.zeros_like(dens[0])
    for mx, den, o in zip(maxes, dens, outs):
        w = den * jnp.exp(mx - m_all)
        num = num + w[..., None] * o
        tot = tot + w
    return num / tot[..., None]


def dilated_prompt(q, k, v, rel_bias):
    b, t, h, e = q.shape
    nb = t // B_QBLOCK
    qb = q.reshape(b, nb, B_QBLOCK, h, e).swapaxes(0, 1)
    qidx = jnp.arange(t).reshape(nb, B_QBLOCK)
    o = lax.map(lambda a: dilated_core(a[0], k, v, a[1], rel_bias), (qb, qidx))
    return o.swapaxes(0, 1).reshape(b, t, h, e)


def dilated_sample(q, k, v, swa_cache, ai, rel_bias):
    t = q.shape[1]
    buf_len = swa_cache.shape[2]
    qidx = buf_len + jnp.arange(t)

    def one(a):
        q_s, k_s, v_s, bi = a
        buf = swa_cache[ai, bi]
        ks = jnp.concatenate([buf[:, 0].astype(F32), k_s], axis=0)
        vs = jnp.concatenate([buf[:, 1].astype(F32), v_s], axis=0)
        return dilated_core(q_s[None], ks[None], vs[None], qidx, rel_bias)[0]

    return lax.map(one, (q, k, v, jnp.arange(q.shape[0])))


def mixer_ab(h, layer, ai, s0, swa_cache, ab_w_in, ab_w_out, hgrn_lb_logits, hgrn_norm_g, swa_q_g, swa_k_g, rel_bias):
    b, t, _ = h.shape
    z = h @ ab_w_in[ai]
    aq, af, av, ag, bq, bk, bv = jnp.split(z, AB_SPLITS, axis=-1)

    def heads(a, n_heads, e):
        return a.reshape(b, t, n_heads, e).astype(F32)

    lb = jnp.cumsum(jax.nn.softmax(hgrn_lb_logits.astype(F32), axis=0), axis=0)[layer]
    f = lb + (1.0 - lb) * jax.nn.sigmoid(af.astype(F32))
    o_a, s_new = hgrn2_chunked(heads(aq, A_HEADS, A_DK), heads(jnp.log(f), A_HEADS, A_DK), heads(1.0 - f, A_HEADS, A_DK), heads(av, A_HEADS, A_DV), s0.astype(F32))
    o_a = rms_norm(o_a, hgrn_norm_g[ai]) * jax.nn.sigmoid(heads(ag, A_HEADS, A_DV))
    q = rms_norm(heads(bq, B_HEADS, B_HEAD_DIM), swa_q_g[ai])
    k = rms_norm(heads(bk, B_HEADS, B_HEAD_DIM), swa_k_g[ai])
    v = heads(bv, B_HEADS, B_HEAD_DIM)
    if swa_cache is None:
        o_b = dilated_prompt(q, k, v, rel_bias)
    else:
        o_b = dilated_sample(q, k, v, swa_cache, ai, rel_bias)
    y = jnp.concatenate([o_a.reshape(b, t, A_VAL_WIDTH), o_b.reshape(b, t, B_WIDTH)], axis=-1) @ ab_w_out[ai]
    return y, s_new, jnp.stack([k, v], axis=2)


def mla_core(qn, qr, kn, kr, lat, mask):
    scale = (C_NOPE + C_ROPE) ** -0.5
    logits = (jnp.einsum('bqhd,bkhd->bhqk', qn, kn) + jnp.einsum('bqhd,bkd->bhqk', qr, kr)).astype(F32) * scale
    p = jax.nn.softmax(jnp.where(mask, logits, -jnp.inf), axis=-1)
    return jnp.einsum('bhqk,bkr->bqhr', p, lat.astype(F32))


def mla_prompt(qn, qr, lat, kr, w_uk, kn_g):
    b, t = qn.shape[:2]
    kn = rms_norm(jnp.einsum('bkr,rhd->bkhd', lat, w_uk), kn_g)
    kpos = jnp.arange(t)
    nb = t // C_QBLOCK

    def blk(a):
        qn_b, qr_b, qpos = a
        return mla_core(qn_b, qr_b, kn, kr, lat, kpos[None, :] <= qpos[:, None])

    ctx = lax.map(blk, (qn.reshape(b, nb, C_QBLOCK, C_HEADS, C_NOPE).swapaxes(0, 1), qr.reshape(b, nb, C_QBLOCK, C_HEADS, C_ROPE).swapaxes(0, 1), kpos.reshape(nb, C_QBLOCK)))
    return ctx.swapaxes(0, 1).reshape(b, t, C_HEADS, C_KV_LORA)


def mla_sample(qn, qr, lat_new, kr_new, lat_cache, kr_cache, ci, page_table, w_uk, kn_g):
    t = qn.shape[1]
    past = page_table.shape[1] * PAGE_SIZE
    mask = jnp.arange(past + t)[None, :] <= (past + jnp.arange(t))[:, None]

    def one(a):
        qn_s, qr_s, lat_s, kr_s, pages = a
        lat_all = jnp.concatenate([lat_cache[ci, pages].reshape(past, C_KV_LORA).astype(F32), lat_s], axis=0)
        kr_all = jnp.concatenate([kr_cache[ci, pages].reshape(past, C_ROPE).astype(F32), kr_s], axis=0)
        kn = rms_norm(jnp.einsum('kr,rhd->khd', lat_all, w_uk), kn_g)
        return mla_core(qn_s[None], qr_s[None], kn[None], kr_all[None], lat_all[None], mask)[0]

    return lax.map(one, (qn, qr, lat_new, kr_new, page_table))


def mixer_c(h, ci, pos, lat_cache, kr_cache, page_table, mla_w_in, mla_q_a_g, mla_kv_a_g, mla_w_uq, mla_w_uk, mla_w_uv, mla_qn_g, mla_qr_g, mla_kn_g, mla_kr_g, mla_w_o):
    b, t, _ = h.shape
    z = h @ mla_w_in[ci]
    cq, ckv, kr = jnp.split(z, (C_Q_LORA, C_Q_LORA + C_KV_LORA), axis=-1)
    cos, sin = rope_angles(pos)
    q = jnp.einsum('btr,rhd->bthd', rms_norm(cq, mla_q_a_g[ci]), mla_w_uq[ci])
    qn = rms_norm(q[..., :C_NOPE], mla_qn_g[ci])
    qr = apply_rope(rms_norm(q[..., C_NOPE:], mla_qr_g[ci]), cos[:, None, :], sin[:, None, :])
    lat = rms_norm(ckv, mla_kv_a_g[ci])
    kr = apply_rope(rms_norm(kr, mla_kr_g[ci]), cos, sin)
    if lat_cache is None:
        ctx = mla_prompt(qn, qr, lat, kr, mla_w_uk[ci], mla_kn_g[ci])
    else:
        ctx = mla_sample(qn, qr, lat, kr, lat_cache, kr_cache, ci, page_table, mla_w_uk[ci], mla_kn_g[ci])
    o = jnp.einsum('bthr,rhv->bthv', ctx, mla_w_uv[ci]).reshape(b, t, C_HEADS * C_V)
    return o @ mla_w_o[ci], lat, kr


def moe_ffn(h, layer, router_w, router_b, w_gu, b_gu, w_dn, b_dn):
    lead = h.shape[:-1]
    xt = h.reshape(-1, D_MODEL)
    n = xt.shape[0]
    logits = (xt @ router_w[layer] + router_b[layer]).astype(F32)
    top_val, top_idx = lax.top_k(logits, TOP_K)
    gate = jax.nn.softmax(top_val, axis=-1)
    nk = n * TOP_K
    flat_e = top_idx.reshape(nk)
    flat_tok = jnp.arange(nk, dtype=jnp.int32) // TOP_K
    order = jnp.argsort(flat_e)
    sorted_e = flat_e[order]
    counts = jnp.bincount(flat_e, length=N_EXPERTS)
    padded = (counts + MOE_BLOCK - 1) // MOE_BLOCK * MOE_BLOCK
    pad_end = jnp.cumsum(padded)
    pad_start = pad_end - padded
    start = jnp.cumsum(counts) - counts
    dest = pad_start[sorted_e] + jnp.arange(nk) - start[sorted_e]
    n_blocks = -(-nk // MOE_BLOCK) + N_EXPERTS
    n_slots = n_blocks * MOE_BLOCK
    slot_tok = jnp.full((n_slots,), n, jnp.int32).at[dest].set(flat_tok[order])
    slot_gate = jnp.zeros((n_slots,), F32).at[dest].set(gate.reshape(nk)[order])
    block_e = jnp.minimum(jnp.searchsorted(pad_end, jnp.arange(n_blocks) * MOE_BLOCK, side='right'), N_EXPERTS - 1)
    x_pad = jnp.concatenate([xt, jnp.zeros((1, D_MODEL), xt.dtype)], axis=0)

    def run_block(a):
        tok, e = a
        gu = x_pad[tok] @ w_gu[layer, e] + b_gu[layer, e]
        g = jnp.minimum(gu[:, 0::2], SWIGLU_LIMIT)
        u = jnp.clip(gu[:, 1::2], -SWIGLU_LIMIT, SWIGLU_LIMIT)
        return ((u + 1.0) * g * jax.nn.sigmoid(SWIGLU_ALPHA * g)) @ w_dn[layer, e] + b_dn[layer, e]

    yb = lax.map(run_block, (slot_tok.reshape(n_blocks, MOE_BLOCK), block_e))
    y = jnp.zeros((n + 1, D_MODEL), F32).at[slot_tok].add(slot_gate[:, None] * yb.reshape(n_slots, D_MODEL))
    return y[:n].reshape(lead + (D_MODEL,))


def setup_inputs(seed: int = 0) -> dict:
    key = jax.random.key(seed)
    keys = iter(jax.random.split(key, 48))

    def nrm(shape, scale):
        return jax.random.normal(next(keys), shape, F32) * scale

    def gain(shape):
        return 1.0 + nrm(shape, 0.1)

    n_pages = PAST_LEN // PAGE_SIZE
    n_used = DEC_BATCH * n_pages
    n_pool = n_used + n_used // 4
    buf_len = min(B_MAX_WINDOW, PAST_LEN)
    page_table = jax.random.permutation(next(keys), n_pool)[:n_used].reshape(DEC_BATCH, n_pages).astype(jnp.int32)
    return {
        'x_prompt': nrm((BATCH, SEQ, D_MODEL), 1.0),
        'x_sample': nrm((DEC_BATCH, DEC_SEQ, D_MODEL), 1.0),
        'state_hgrn': nrm((N_AB_LAYERS, DEC_BATCH, A_HEADS, A_DK, A_DV), 0.5),
        'cache_swa_kv': nrm((N_AB_LAYERS, DEC_BATCH, buf_len, 2, B_HEADS, B_HEAD_DIM), 1.0),
        'cache_mla_latent': nrm((N_C_LAYERS, n_pool, PAGE_SIZE, C_KV_LORA), 1.0),
        'cache_mla_krope': nrm((N_C_LAYERS, n_pool, PAGE_SIZE, C_ROPE), 1.0),
        'page_table': page_table,
        'c_prompt': nrm((BATCH, D_MODEL), 1.0),
        'c_sample': nrm((DEC_BATCH, D_MODEL), 1.0),
        'ada_w': nrm((DEPTH, 2, D_MODEL, 3 * D_MODEL), 0.5 * D_MODEL ** -0.5),
        'ada_b': nrm((DEPTH, 2, 3 * D_MODEL), 0.02),
        'norm_g': gain((DEPTH, 2, D_MODEL)),
        'rel_bias': nrm((REL_BUCKETS, B_HEADS), 0.5),
        'ab_w_in': nrm((N_AB_LAYERS, D_MODEL, AB_IN_WIDTH), D_MODEL ** -0.5),
        'ab_w_out': nrm((N_AB_LAYERS, AB_OUT_WIDTH, D_MODEL), AB_OUT_WIDTH ** -0.5),
        'hgrn_lb_logits': nrm((DEPTH + 1, A_KEY_WIDTH), 0.5),
        'hgrn_norm_g': gain((N_AB_LAYERS, A_DV)),
        'swa_q_g': gain((N_AB_LAYERS, B_HEAD_DIM)),
        'swa_k_g': gain((N_AB_LAYERS, B_HEAD_DIM)),
        'mla_w_in': nrm((N_C_LAYERS, D_MODEL, C_IN_WIDTH), D_MODEL ** -0.5),
        'mla_q_a_g': gain((N_C_LAYERS, C_Q_LORA)),
        'mla_kv_a_g': gain((N_C_LAYERS, C_KV_LORA)),
        'mla_w_uq': nrm((N_C_LAYERS, C_Q_LORA, C_HEADS, C_NOPE + C_ROPE), C_Q_LORA ** -0.5),
        'mla_w_uk': nrm((N_C_LAYERS, C_KV_LORA, C_HEADS, C_NOPE), C_KV_LORA ** -0.5),
        'mla_w_uv': nrm((N_C_LAYERS, C_KV_LORA, C_HEADS, C_V), C_KV_LORA ** -0.5),
        'mla_qn_g': gain((N_C_LAYERS, C_NOPE)),
        'mla_qr_g': gain((N_C_LAYERS, C_ROPE)),
        'mla_kn_g': gain((N_C_LAYERS, C_NOPE)),
        'mla_kr_g': gain((N_C_LAYERS, C_ROPE)),
        'mla_w_o': nrm((N_C_LAYERS, C_HEADS * C_V, D_MODEL), (C_HEADS * C_V) ** -0.5),
        'moe_router_w': nrm((DEPTH, D_MODEL, N_EXPERTS), D_MODEL ** -0.5),
        'moe_router_b': nrm((DEPTH, N_EXPERTS), 0.01),
        'moe_w_gu': nrm((DEPTH, N_EXPERTS, D_MODEL, 2 * D_EXPERT), D_MODEL ** -0.5),
        'moe_b_gu': nrm((DEPTH, N_EXPERTS, 2 * D_EXPERT), 0.01),
        'moe_w_dn': nrm((DEPTH, N_EXPERTS, D_EXPERT, D_MODEL), D_EXPERT ** -0.5),
        'moe_b_dn': nrm((DEPTH, N_EXPERTS, D_MODEL), 0.01),
    }


def reference(x_prompt, x_sample, state_hgrn, cache_swa_kv, cache_mla_latent, cache_mla_krope, page_table, c_prompt, c_sample, ada_w, ada_b, norm_g, rel_bias, ab_w_in, ab_w_out, hgrn_lb_logits, hgrn_norm_g, swa_q_g, swa_k_g, mla_w_in, mla_q_a_g, mla_kv_a_g, mla_w_uq, mla_w_uk, mla_w_uv, mla_qn_g, mla_qr_g, mla_kn_g, mla_kr_g, mla_w_o, moe_router_w, moe_router_b, moe_w_gu, moe_b_gu, moe_w_dn, moe_b_dn):
    past = page_table.shape[1] * PAGE_SIZE
    pos_p = jnp.arange(x_prompt.shape[1])
    pos_s = past + jnp.arange(x_sample.shape[1])
    hp = x_prompt.astype(F32)
    hs = x_sample.astype(F32)
    hgrn_p, hgrn_s, swa_p, swa_s = [], [], [], []
    lat_p, lat_s, kr_p, kr_s = [], [], [], []
    for layer in range(DEPTH):
        sh_p, sc_p, g_p = adaln(c_prompt, ada_w[layer, 0], ada_b[layer, 0])
        sh_s, sc_s, g_s = adaln(c_sample, ada_w[layer, 0], ada_b[layer, 0])
        up = rms_norm(hp, norm_g[layer, 0]) * (1.0 + sc_p) + sh_p
        us = rms_norm(hs, norm_g[layer, 0]) * (1.0 + sc_s) + sh_s
        if layer % 2 == 0:
            ai = layer // 2
            s0_p = jnp.zeros((x_prompt.shape[0], A_HEADS, A_DK, A_DV), F32)
            mp, st, kv = mixer_ab(up, layer, ai, s0_p, None, ab_w_in, ab_w_out, hgrn_lb_logits, hgrn_norm_g, swa_q_g, swa_k_g, rel_bias)
            hgrn_p.append(st)
            swa_p.append(kv[:, kv.shape[1] - min(B_MAX_WINDOW, kv.shape[1]):])
            ms, st, kv = mixer_ab(us, layer, ai, state_hgrn[ai], cache_swa_kv, ab_w_in, ab_w_out, hgrn_lb_logits, hgrn_norm_g, swa_q_g, swa_k_g, rel_bias)
            hgrn_s.append(st)
            swa_s.append(kv)
        else:
            ci = layer // 2
            mp, lat, kr = mixer_c(up, ci, pos_p, None, None, None, mla_w_in, mla_q_a_g, mla_kv_a_g, mla_w_uq, mla_w_uk, mla_w_uv, mla_qn_g, mla_qr_g, mla_kn_g, mla_kr_g, mla_w_o)
            lat_p.append(lat)
            kr_p.append(kr)
            ms, lat, kr = mixer_c(us, ci, pos_s, cache_mla_latent, cache_mla_krope, page_table, mla_w_in, mla_q_a_g, mla_kv_a_g, mla_w_uq, mla_w_uk, mla_w_uv, mla_qn_g, mla_qr_g, mla_kn_g, mla_kr_g, mla_w_o)
            lat_s.append(lat)
            kr_s.append(kr)
        hp = hp + g_p * mp
        hs = hs + g_s * ms
        sh_p, sc_p, g_p = adaln(c_prompt, ada_w[layer, 1], ada_b[layer, 1])
        sh_s, sc_s, g_s = adaln(c_sample, ada_w[layer, 1], ada_b[layer, 1])
        up = rms_norm(hp, norm_g[layer, 1]) * (1.0 + sc_p) + sh_p
        us = rms_norm(hs, norm_g[layer, 1]) * (1.0 + sc_s) + sh_s
        hp = hp + g_p * moe_ffn(up, layer, moe_router_w, moe_router_b, moe_w_gu, moe_b_gu, moe_w_dn, moe_b_dn)
        hs = hs + g_s * moe_ffn(us, layer, moe_router_w, moe_router_b, moe_w_gu, moe_b_gu, moe_w_dn, moe_b_dn)
    y_prompt = hp.astype(x_prompt.dtype)
    y_sample = hs.astype(x_sample.dtype)
    hgrn_prompt = jnp.stack(hgrn_p).astype(state_hgrn.dtype)
    hgrn_sample = jnp.stack(hgrn_s).astype(state_hgrn.dtype)
    swa_prompt = jnp.stack(swa_p).astype(cache_swa_kv.dtype)
    swa_sample = jnp.stack(swa_s).astype(cache_swa_kv.dtype)
    lat_prompt = jnp.stack(lat_p).astype(cache_mla_latent.dtype)
    lat_sample = jnp.stack(lat_s).astype(cache_mla_latent.dtype)
    krope_prompt = jnp.stack(kr_p).astype(cache_mla_krope.dtype)
    krope_sample = jnp.stack(kr_s).astype(cache_mla_krope.dtype)
    return (y_prompt, y_sample, hgrn_prompt, hgrn_sample, swa_prompt, swa_sample, lat_prompt, lat_sample, krope_prompt, krope_sample)
```

```python
import functools
import math

import numpy as np
import jax
import jax.numpy as jnp
from jax import lax
from jax.experimental import pallas as pl
from jax.experimental.pallas import tpu as pltpu

F32 = jnp.float32
BF16 = jnp.bfloat16
EPS = 1e-6

D = 2048
TM = 512
N_EXPERTS = 32
TOP_K = 4
D_EXPERT = 2048
SWIGLU_LIMIT = 7.0
SWIGLU_ALPHA = 1.702
PAGE = 128
V7X_VMEM_LIMIT = 56 * 1024 * 1024

NN = ((1,), (0,))
NT = ((1,), (1,))
TN = ((0,), (0,))


def _cparams(sem, vmem=V7X_VMEM_LIMIT):
    return pltpu.CompilerParams(dimension_semantics=sem, vmem_limit_bytes=vmem)


def _dot1(a, b, dims=NN):
    return lax.dot_general(a.astype(BF16), b.astype(BF16), (dims, ((), ())), preferred_element_type=F32)


def _split2(x):
    hi = x.astype(BF16)
    lo = (x - hi.astype(F32)).astype(BF16)
    return hi, lo


def _split3(x):
    hi = x.astype(BF16)
    r = x - hi.astype(F32)
    mid = r.astype(BF16)
    lo = (r - mid.astype(F32)).astype(BF16)
    return hi, mid, lo


def _dot_exact_rhs(a01, x):
    hi, mid, lo = _split3(x)
    f = lambda p: lax.dot_general(a01, p, (NN, ((), ())), preferred_element_type=F32)
    return f(hi) + f(mid) + f(lo)


def _dot3(a, b, dims=NN):
    ah, al = _split2(a)
    bh, bl = _split2(b)
    f = lambda p, q: lax.dot_general(p, q, (dims, ((), ())), preferred_element_type=F32)
    return f(ah, bh) + f(ah, bl) + f(al, bh)


def _sigmoid(x):
    return 1.0 / (1.0 + jnp.exp(-x))


def _rms(x, g):
    return x * lax.rsqrt(jnp.mean(x * x, axis=-1, keepdims=True) + EPS) * g


def _ada_kernel(c_ref, w_ref, b_ref, o_ref):
    c = c_ref[...]
    s = c * _sigmoid(c)
    o_ref[0] = _dot1(s, w_ref[0, 0]) + b_ref[0, 0]


def _adaln(c_all, ada_w, ada_b):
    r = c_all.shape[0]
    tn = 1024
    depth = ada_w.shape[0]
    b4 = ada_b.reshape(depth, 2, 1, 3 * D)
    return pl.pallas_call(
        _ada_kernel,
        grid=(depth * 2, 3 * D // tn),
        in_specs=[pl.BlockSpec((r, D), lambda s, j: (0, 0)),
                  pl.BlockSpec((1, 1, D, tn), lambda s, j: (s // 2, s % 2, 0, j)),
                  pl.BlockSpec((1, 1, 1, tn), lambda s, j: (s // 2, s % 2, 0, j))],
        out_specs=pl.BlockSpec((1, r, tn), lambda s, j: (s, 0, j)),
        out_shape=jax.ShapeDtypeStruct((depth * 2, r, 3 * D), F32),
        compiler_params=_cparams(("parallel", "parallel")),
        name="adaln",
    )(c_all, ada_w, b4)


def _pick(is_prompt, p_ref, s_ref):
    return jnp.where(is_prompt, p_ref[0], s_ref[...])


def _mod_specs(npt, rows_per_batch_tiles, col, tn=D, with_j=False):
    if with_j:
        pm = pl.BlockSpec((1, 1, tn), lambda i, j: (jnp.minimum(i // rows_per_batch_tiles, npt // rows_per_batch_tiles - 1), 0, col * (D // tn) + j))
        sm = pl.BlockSpec((TM, tn), lambda i, j: (0, col * (D // tn) + j))
    else:
        pm = pl.BlockSpec((1, 1, tn), lambda i, *_: (jnp.minimum(i // rows_per_batch_tiles, npt // rows_per_batch_tiles - 1), 0, col))
        sm = pl.BlockSpec((TM, tn), lambda i, *_: (0, col))
    return pm, sm


def _nm_mm_kernel(x_ref, g_ref, psh, psc, ssh, ssc, w_ref, o_ref, xn_ref, *, npt):
    i = pl.program_id(0)
    j = pl.program_id(1)

    @pl.when(j == 0)
    def _():
        xr = _rms(x_ref[...], g_ref[...])
        sc = _pick(i < npt, psc, ssc)
        sh = _pick(i < npt, psh, ssh)
        xn_ref[...] = (xr * (1.0 + sc) + sh).astype(BF16)

    o_ref[...] = jnp.dot(xn_ref[...], w_ref[...], preferred_element_type=F32)


def _normmod_mm(x, g, pmod, smod, w_bf16, tn, npt, tiles_per_batch):
    t = x.shape[0]
    n = w_bf16.shape[1]
    psh, ssh = _mod_specs(npt, tiles_per_batch, 0)
    psc, ssc = _mod_specs(npt, tiles_per_batch, 1)
    return pl.pallas_call(
        functools.partial(_nm_mm_kernel, npt=npt),
        grid=(t // TM, n // tn),
        in_specs=[pl.BlockSpec((TM, D), lambda i, j: (i, 0)),
                  pl.BlockSpec((1, D), lambda i, j: (0, 0)),
                  psh, psc, ssh, ssc,
                  pl.BlockSpec((D, tn), lambda i, j: (0, j))],
        out_specs=pl.BlockSpec((TM, tn), lambda i, j: (i, j)),
        out_shape=jax.ShapeDtypeStruct((t, n), F32),
        scratch_shapes=[pltpu.VMEM((TM, D), BF16)],
        compiler_params=_cparams(("parallel", "arbitrary")),
        name="normmod_mm",
    )(x, g.reshape(1, D), pmod, pmod, smod, smod, w_bf16)


def _proj_res_kernel(*refs, npt, nk):
    xp = refs[0:nk]
    xs = refs[nk:2 * nk]
    ws = refs[2 * nk:3 * nk]
    h_ref, pg, sg, o_ref = refs[3 * nk:3 * nk + 4]
    xb = refs[3 * nk + 4:]
    i = pl.program_id(0)
    j = pl.program_id(1)

    @pl.when(jnp.logical_and(j == 0, i < npt))
    def _():
        for k in range(nk):
            xb[k][...] = xp[k][...].astype(BF16)

    @pl.when(jnp.logical_and(j == 0, i >= npt))
    def _():
        for k in range(nk):
            xb[k][...] = xs[k][...].astype(BF16)

    acc = jnp.dot(xb[0][...], ws[0][...], preferred_element_type=F32)
    for k in range(1, nk):
        acc = acc + jnp.dot(xb[k][...], ws[k][...], preferred_element_type=F32)
    gate = _pick(i < npt, pg, sg)
    o_ref[...] = h_ref[...] + gate * acc


def _proj_residual(xps, xss, ws, h, pmod, smod, npt, tiles_per_batch, tn=1024):
    nk = len(ws)
    t = h.shape[0]
    ks = [w.shape[0] for w in ws]
    pg, sg = _mod_specs(npt, tiles_per_batch, 2, tn=tn, with_j=True)
    in_specs = ([pl.BlockSpec((TM, k), lambda i, j: (jnp.minimum(i, npt - 1), 0)) for k in ks]
                + [pl.BlockSpec((TM, k), lambda i, j: (0, 0)) for k in ks]
                + [pl.BlockSpec((k, tn), lambda i, j: (0, j)) for k in ks]
                + [pl.BlockSpec((TM, tn), lambda i, j: (i, j)), pg, sg])
    return pl.pallas_call(
        functools.partial(_proj_res_kernel, npt=npt, nk=nk),
        grid=(t // TM, D // tn),
        in_specs=in_specs,
        out_specs=pl.BlockSpec((TM, tn), lambda i, j: (i, j)),
        out_shape=jax.ShapeDtypeStruct((t, D), F32),
        scratch_shapes=[pltpu.VMEM((TM, k), BF16) for k in ks],
        compiler_params=_cparams(("parallel", "arbitrary")),
        name="proj_residual",
    )(*xps, *xss, *ws, h, pmod, smod)


HG_C = 64
HG_B = 16
HG_STEP = 256


def _lower_bound(lbl, layer):
    e = jnp.exp(lbl - jnp.max(lbl, axis=0, keepdims=True))
    return jnp.sum(e[0:layer + 1], axis=0, keepdims=True) / jnp.sum(e, axis=0, keepdims=True)


def _hgrn_chunk(q, af, v, st, lb, tri):
    c, dk = q.shape
    nb = c // HG_B
    f = lb + (1.0 - lb) * _sigmoid(af)
    g = jnp.log(f)
    k = 1.0 - f
    cum = _dot_exact_rhs(tri, g)
    o = _dot1(q * jnp.exp(cum), st, NT)
    q3 = q.reshape(nb, HG_B, dk)
    k3 = k.reshape(nb, HG_B, dk)
    c3 = cum.reshape(nb, HG_B, dk)
    tpos = lax.broadcasted_iota(jnp.int32, (nb, HG_B, dk), 1)
    lane3 = lax.broadcasted_iota(jnp.int32, (nb, HG_B, dk), 2)
    dmat = jnp.zeros((nb, HG_B, dk), F32)
    for s in range(HG_B):
        rel = c3 - c3[:, s:s + 1, :]
        a = q3 * k3[:, s:s + 1, :] * jnp.exp(jnp.where(tpos >= s, rel, -jnp.inf))
        col = jnp.sum(a, axis=-1, keepdims=True)
        dmat = jnp.where(lane3 == s, col, dmat)
    lane_c = lax.broadcasted_iota(jnp.int32, (HG_B, c), 1)
    rows = []
    for blk in range(nb):
        r0 = HG_B * blk
        d_blk = dmat[blk]
        if blk == 0:
            rows.append(d_blk[:, :c])
            continue
        d_blk = pltpu.roll(d_blk, r0, axis=1)[:, :c]
        cb = cum[r0 - 1:r0, :]
        qs = q[r0:r0 + HG_B] * jnp.exp(cum[r0:r0 + HG_B] - cb)
        ks = k * jnp.exp(jnp.minimum(cb - cum, 0.0))
        s_off = _dot1(qs, ks, NT)
        rows.append(jnp.where(lane_c < r0, s_off, 0.0) + d_blk)
    scores = jnp.concatenate(rows, axis=0)
    o = o + _dot1(scores, v)
    last = cum[c - 1:c, :]
    st_new = jnp.exp(last) * st + _dot1(v, k * jnp.exp(last - cum), TN)
    return o, st_new


def _hgrn_p_kernel(q_ref, f_ref, v_ref, g_ref, lbl_ref, gn_ref, o_ref, s_ref, st_ref, *, layer, nsteps):
    tb = pl.program_id(2)

    @pl.when(tb == 0)
    def _():
        st_ref[...] = jnp.zeros_like(st_ref)

    lb = _lower_bound(lbl_ref[...], layer)
    ri = lax.broadcasted_iota(jnp.int32, (HG_C, HG_C), 0)
    ci = lax.broadcasted_iota(jnp.int32, (HG_C, HG_C), 1)
    tri = jnp.where(ri >= ci, 1.0, 0.0).astype(BF16)
    st = st_ref[...]
    for c in range(HG_STEP // HG_C):
        sl = slice(HG_C * c, HG_C * (c + 1))
        o, st = _hgrn_chunk(q_ref[sl, :], f_ref[sl, :], v_ref[sl, :], st, lb, tri)
        o_ref[sl, :] = (_rms(o, gn_ref[...]) * _sigmoid(g_ref[sl, :])).astype(BF16)
    st_ref[...] = st

    @pl.when(tb == nsteps - 1)
    def _():
        s_ref[0, 0] = st.T


def _hgrn_prompt(z, lb_logits, gn, nb, seq, layer):
    nh = 8
    nsteps = seq // HG_STEP
    zspec = lambda c0: pl.BlockSpec((HG_STEP, 128), lambda b, h, t: (b * nsteps + t, c0 + h))
    return pl.pallas_call(
        functools.partial(_hgrn_p_kernel, layer=layer, nsteps=nsteps),
        grid=(nb, nh, nsteps),
        in_specs=[zspec(0), zspec(8), zspec(16), zspec(24),
                  pl.BlockSpec((lb_logits.shape[0], 128), lambda b, h, t: (0, h)),
                  pl.BlockSpec((1, 128), lambda b, h, t: (0, 0))],
        out_specs=[pl.BlockSpec((HG_STEP, 128), lambda b, h, t: (b * nsteps + t, h)),
                   pl.BlockSpec((1, 1, 128, 128), lambda b, h, t: (b, h, 0, 0))],
        out_shape=[jax.ShapeDtypeStruct((nb * seq, nh * 128), BF16),
                   jax.ShapeDtypeStruct((nb, nh, 128, 128), F32)],
        scratch_shapes=[pltpu.VMEM((128, 128), F32)],
        compiler_params=_cparams(("parallel", "parallel", "arbitrary")),
        name="hgrn_prompt",
    )(z, z, z, z, lb_logits, gn.reshape(1, 128))


def _hgrn_s_kernel(q_ref, f_ref, v_ref, g_ref, lbl_ref, gn_ref, s0_ref, o_ref, s_ref, *, layer):
    nt = q_ref.shape[1]
    lb_all = _lower_bound(lbl_ref[...], layer)
    tpos = lax.broadcasted_iota(jnp.int32, (nt, 128), 0)
    outs = []
    for h in range(8):
        sl = slice(128 * h, 128 * (h + 1))
        q = q_ref[0, :, sl]
        v = v_ref[0, :, sl]
        lb = lb_all[:, sl]
        f = lb + (1.0 - lb) * _sigmoid(f_ref[0, :, sl])
        g = jnp.log(f)
        k = 1.0 - f
        crow = [g[0:1]]
        for t in range(1, nt):
            crow.append(crow[-1] + g[t:t + 1])
        cum = jnp.concatenate(crow, axis=0)
        last = crow[-1]
        s0 = s0_ref[0, 0, h]
        qe = jnp.concatenate([q * jnp.exp(cum), jnp.zeros((8 - nt, 128), F32)], axis=0)
        o = _dot1(qe, s0)[0:nt]
        for s in range(nt):
            a = q * k[s:s + 1] * jnp.exp(jnp.where(tpos >= s, cum - cum[s:s + 1], -jnp.inf))
            o = o + jnp.sum(a, axis=-1, keepdims=True) * v[s:s + 1]
        kp = k * jnp.exp(last - cum)
        w8 = jnp.concatenate([jnp.exp(last), kp, jnp.zeros((8 - 1 - nt, 128), F32)], axis=0)
        wt = jnp.concatenate([w8, jnp.zeros((120, 128), F32)], axis=0).T
        s_new = wt[:, 0:1] * s0
        for s in range(nt):
            s_new = s_new + wt[:, 1 + s:2 + s] * v[s:s + 1]
        s_ref[0, 0, h] = s_new
        outs.append(_rms(o, gn_ref[...]) * _sigmoid(g_ref[0, :, sl]))
    o_ref[0] = jnp.concatenate(outs, axis=1)


def _hgrn_sample(zs3, lb_logits, gn, state, layer):
    nb, nt, _ = zs3.shape
    zspec = lambda c: pl.BlockSpec((1, nt, 1024), lambda b: (b, 0, c))
    sspec = pl.BlockSpec((1, 1, 8, 128, 128), lambda b: (0, b, 0, 0, 0))
    return pl.pallas_call(
        functools.partial(_hgrn_s_kernel, layer=layer),
        grid=(nb,),
        in_specs=[zspec(0), zspec(1), zspec(2), zspec(3),
                  pl.BlockSpec(lb_logits.shape, lambda b: (0, 0)),
                  pl.BlockSpec((1, 128), lambda b: (0, 0)),
                  sspec],
        out_specs=[pl.BlockSpec((1, nt, 1024), lambda b: (b, 0, 0)), sspec],
        out_shape=[jax.ShapeDtypeStruct((nb, nt, 1024), F32),
                   jax.ShapeDtypeStruct(state.shape, F32)],
        compiler_params=_cparams(("parallel",)),
        name="hgrn_sample",
    )(zs3, zs3, zs3, zs3, lb_logits, gn.reshape(1, 128), state)


DILATED = ((128, 1), (512, 4), (2048, 16))
REL_BUCKETS = 32
REL_MAX_DISTANCE = 2048
ATT_SCALE = 128 ** -0.5
QB = 128


def _rel_bucket_np(dist):
    exact = REL_BUCKETS // 2
    d = np.asarray(dist)
    far = exact + (np.log(np.maximum(d, 1).astype(np.float32) / np.float32(exact))
                   / np.float32(math.log(REL_MAX_DISTANCE / exact)) * np.float32(REL_BUCKETS - exact)).astype(np.int32)
    return np.where(d < exact, d, np.minimum(far, REL_BUCKETS - 1)).astype(np.int32)


def _band_tables(rel_bias):
    rb = rel_bias.astype(F32)
    tabs = []
    for (window, dil), width in zip(DILATED, (2 * QB, 2 * QB, QB)):
        t = np.arange(QB)[:, None]
        c = np.arange(width)[None, :]
        steps = t + (width - QB) - c
        valid = (steps >= 0) & (steps <= window // dil)
        bucket = _rel_bucket_np(np.clip(steps, 0, window // dil) * dil)
        tab = jnp.where(jnp.asarray(valid)[None], jnp.transpose(rb[jnp.asarray(bucket)], (2, 0, 1)), -jnp.inf)
        tabs.append(tab)
    return tabs


def _soft_block(qb, kw, vw, bias):
    l = _dot1(qb, kw, NT) * ATT_SCALE + bias
    m = jnp.max(l, axis=-1, keepdims=True)
    p = jnp.exp(l - m)
    return m, jnp.sum(p, axis=-1, keepdims=True), _dot1(p, vw)


def _dil_p_kernel(q_ref, k_ref, v_ref, gq_ref, gk_ref, b1_ref, b2_ref, b3_ref, o_ref, ko_ref, vo_ref,
                  qs, ks, vs, a1, a2, a3, m1, m2, m3, l1, l2, l3):
    seq = q_ref.shape[0]
    kn = _rms(k_ref[...], gk_ref[...])
    v = v_ref[...]
    ko_ref[...] = kn
    vo_ref[...] = v
    qs[...] = _rms(q_ref[...], gq_ref[...])
    ks[...] = kn
    vs[...] = v
    zpad = jnp.zeros((QB, 128), F32)
    first_cols = lax.broadcasted_iota(jnp.int32, (QB, 2 * QB), 1) >= QB

    def banded(qsub, ksub, vsub, bias, n_blocks, store):
        for i in range(n_blocks):
            qb = qsub(i)
            if i == 0:
                kw = jnp.concatenate([zpad, ksub(0)], axis=0)
                vw = jnp.concatenate([zpad, vsub(0)], axis=0)
                bb = jnp.where(first_cols, bias, -jnp.inf)
            else:
                kw = jnp.concatenate([ksub(i - 1), ksub(i)], axis=0)
                vw = jnp.concatenate([vsub(i - 1), vsub(i)], axis=0)
                bb = bias
            store(i, *_soft_block(qb, kw, vw, bb))

    def st1(i, m, l, a):
        sl = slice(QB * i, QB * (i + 1))
        m1[sl, :] = m
        l1[sl, :] = l
        a1[sl, :] = a
    blk = lambda ref: (lambda i: ref[QB * i:QB * (i + 1), :])
    banded(blk(qs), blk(ks), blk(vs), b1_ref[0], seq // QB, st1)

    d2 = DILATED[1][1]
    for r in range(d2):
        sub = lambda ref: (lambda i: ref[pl.ds(r + d2 * QB * i, QB, stride=d2), :])

        def st2(i, m, l, a):
            idx = pl.ds(r + d2 * QB * i, QB, stride=d2)
            m2[idx, :] = m
            l2[idx, :] = l
            a2[idx, :] = a
        banded(sub(qs), sub(ks), sub(vs), b2_ref[0], seq // (QB * d2), st2)

    d3 = DILATED[2][1]
    for r in range(d3):
        idx = pl.ds(r, QB, stride=d3)
        m, l, a = _soft_block(qs[idx, :], ks[idx, :], vs[idx, :], b3_ref[0])
        m3[idx, :] = m
        l3[idx, :] = l
        a3[idx, :] = a

    mx = jnp.maximum(jnp.maximum(m1[...], m2[...]), m3[...])
    w1 = jnp.exp(m1[...] - mx)
    w2 = jnp.exp(m2[...] - mx)
    w3 = jnp.exp(m3[...] - mx)
    num = w1 * a1[...] + w2 * a2[...] + w3 * a3[...]
    den = w1 * l1[...] + w2 * l2[...] + w3 * l3[...]
    o_ref[...] = (num / den).astype(BF16)


def _dilated_prompt(z, gq, gk, tabs, nb, seq):
    nh = 8
    zspec = lambda c0: pl.BlockSpec((seq, 128), lambda b, h: (b, c0 + h))
    gspec = pl.BlockSpec((1, 128), lambda b, h: (0, 0))
    tspec = lambda w: pl.BlockSpec((1, QB, w), lambda b, h: (h, 0, 0))
    ospec = pl.BlockSpec((seq, 128), lambda b, h: (b, h))
    big = pltpu.VMEM((seq, 128), F32)
    col = pltpu.VMEM((seq, 1), F32)
    return pl.pallas_call(
        _dil_p_kernel,
        grid=(nb, nh),
        in_specs=[zspec(32), zspec(40), zspec(48), gspec, gspec, tspec(2 * QB), tspec(2 * QB), tspec(QB)],
        out_specs=[ospec, ospec, ospec],
        out_shape=[jax.ShapeDtypeStruct((nb * seq, nh * 128), BF16),
                   jax.ShapeDtypeStruct((nb * seq, nh * 128), F32),
                   jax.ShapeDtypeStruct((nb * seq, nh * 128), F32)],
        scratch_shapes=[big, big, big, big, big, big, col, col, col, col, col, col],
        compiler_params=_cparams(("parallel", "parallel")),
        name="dilated_prompt",
    )(z, z, z, gq.reshape(1, 128), gk.reshape(1, 128), *tabs)


def _sample_tables(rel_bias, nt, buf_len):
    rb = rel_bias.astype(F32)
    nh = rb.shape[1]
    w2 = DILATED[1][0]

    def count(delta, patterns):
        c = np.zeros(delta.shape, np.int32)
        for window, dil in patterns:
            c += ((delta >= 0) & (delta <= window) & (delta % dil == 0)).astype(np.int32)
        return c

    qpos = buf_len + np.arange(8)
    real_q = (np.arange(8) < nt)
    da = qpos[None, :] - (buf_len - w2 + np.arange(w2))[:, None]
    ca = count(da, DILATED[:2]) * real_q[None, :]
    nj = buf_len // 16
    rows_b = (16 * np.arange(nj)[None, :] + np.arange(nt)[:, None]).reshape(-1)
    db = qpos[None, :] - rows_b[:, None]
    cb = count(db, DILATED[2:]) * real_q[None, :]
    dn = qpos[None, :] - qpos[:, None]
    cn = count(dn, DILATED) * real_q[None, :] * real_q[:, None]

    def tables(delta, cnt):
        bucket = _rel_bucket_np(np.clip(delta, 0, REL_MAX_DISTANCE))
        bias = rb[jnp.asarray(bucket)]
        bias = jnp.where(jnp.asarray((cnt > 0) | ~real_q[None, :])[:, :, None], bias, -jnp.inf)
        bias = jnp.transpose(bias, (0, 2, 1)).reshape(delta.shape[0], nh * 8)
        mult = np.repeat(cnt[:, None, :], nh, axis=1).reshape(delta.shape[0], nh * 8).astype(np.float32)
        return bias, jnp.asarray(mult)

    return tables(da, ca) + tables(db, cb) + tables(dn, cn)


def _dil_s_kernel(q_ref, k_ref, v_ref, gq_ref, gk_ref, ca_ref, cb_ref,
                  ba_ref, ma_ref, bb_ref, mb_ref, bn_ref, mn_ref, o_ref, ko_ref, vo_ref):
    nt = q_ref.shape[1]
    nh = 8
    gq = gq_ref[...]
    gk = gk_ref[...]
    qn = jnp.concatenate([_rms(q_ref[0, :, 128 * h:128 * (h + 1)], gq) for h in range(nh)], axis=1)
    kn = jnp.concatenate([_rms(k_ref[0, :, 128 * h:128 * (h + 1)], gk) for h in range(nh)], axis=1)
    vn = v_ref[0]
    ko_ref[0] = kn
    vo_ref[0] = vn
    pad = jnp.zeros((8 - nt, nh * 128), F32)
    q8 = jnp.concatenate([qn, pad], axis=0)
    k8 = jnp.concatenate([kn, pad], axis=0)
    v8 = jnp.concatenate([vn, pad], axis=0)
    lane_head = lax.broadcasted_iota(jnp.int32, (8, nh * 128), 1) // 128
    qbd = jnp.concatenate([jnp.where(lane_head == h, q8, 0.0) for h in range(nh)], axis=0).astype(BF16)

    groups = [(ca_ref[0, :, 0:1024], ca_ref[0, :, 1024:2048], ba_ref[...], ma_ref[...])]
    nj = cb_ref.shape[1]
    for i in range(nt):
        groups.append((cb_ref[0, :, 2048 * i:2048 * i + 1024], cb_ref[0, :, 2048 * i + 1024:2048 * (i + 1)],
                       bb_ref[nj * i:nj * (i + 1), :], mb_ref[nj * i:nj * (i + 1), :]))
    groups.append((k8, v8, bn_ref[...], mn_ref[...]))

    logits = [_dot1(kk, qbd, NT) * ATT_SCALE + bias for kk, _, bias, _ in groups]
    mx = logits[0].max(axis=0, keepdims=True)
    for l in logits[1:]:
        mx = jnp.maximum(mx, l.max(axis=0, keepdims=True))
    ps = [mult * jnp.exp(l - mx) for l, (_, _, _, mult) in zip(logits, groups)]
    den = ps[0].sum(axis=0, keepdims=True)
    for p in ps[1:]:
        den = den + p.sum(axis=0, keepdims=True)
    inv = 1.0 / jnp.where(den > 0.0, den, 1.0)
    acc = None
    for p, (_, vv, _, _) in zip(ps, groups):
        part = _dot1(p * inv, vv, TN)
        acc = part if acc is None else acc + part
    o8 = jnp.concatenate([acc[8 * h:8 * (h + 1), 128 * h:128 * (h + 1)] for h in range(nh)], axis=1)
    o_ref[0] = o8[0:nt]


def _dilated_sample(zs3, gq, gk, cache3, tabs):
    nb, nt, _ = zs3.shape
    buf_len = cache3.shape[1]
    w2 = DILATED[1][0]
    d3 = DILATED[2][1]
    cache_b = cache3.reshape(nb, buf_len // d3, d3 * 2048)
    zspec = lambda c: pl.BlockSpec((1, nt, 1024), lambda b: (b, 0, c))
    gspec = pl.BlockSpec((1, 128), lambda b: (0, 0))
    full = lambda a: pl.BlockSpec(a.shape, lambda b: (0, 0))
    ospec = pl.BlockSpec((1, nt, 1024), lambda b: (b, 0, 0))
    return pl.pallas_call(
        _dil_s_kernel,
        grid=(nb,),
        in_specs=[zspec(4), zspec(5), zspec(6), gspec, gspec,
                  pl.BlockSpec((1, w2, 2048), lambda b: (b, buf_len // w2 - 1, 0)),
                  pl.BlockSpec((1, buf_len // d3, nt * 2048), lambda b: (b, 0, 0))]
                 + [full(a) for a in tabs],
        out_specs=[ospec, ospec, ospec],
        out_shape=[jax.ShapeDtypeStruct((nb, nt, 1024), F32)] * 3,
        compiler_params=_cparams(("parallel",)),
        name="dilated_sample",
    )(zs3, zs3, zs3, gq.reshape(1, 128), gk.reshape(1, 128), cache3, cache_b, *tabs)


def _route_kernel(x_ref, g_ref, psh, psc, ssh, ssc, rw_ref, rb_ref, u_ref, idx_ref, gate_ref, *, npt):
    i = pl.program_id(0)
    xr = _rms(x_ref[...], g_ref[...])
    u = xr * (1.0 + _pick(i < npt, psc, ssc)) + _pick(i < npt, psh, ssh)
    u_ref[...] = u
    l = _dot3(u, rw_ref[...]) + rb_ref[...]
    lane = lax.broadcasted_iota(jnp.int32, l.shape, 1).astype(F32)
    out_lane = lax.broadcasted_iota(jnp.int32, (TM, 128), 1)
    vals, idxs = [], []
    for _ in range(TOP_K):
        m = jnp.max(l, axis=-1, keepdims=True)
        first = jnp.min(jnp.where(l == m, lane, float(N_EXPERTS)), axis=-1, keepdims=True)
        vals.append(m)
        idxs.append(first.astype(jnp.int32))
        l = jnp.where(lane == first, -jnp.inf, l)
    es = [jnp.exp(v - vals[0]) for v in vals]
    tot = es[0] + es[1] + es[2] + es[3]
    idx_out = jnp.zeros((TM, 128), jnp.int32)
    gate_out = jnp.zeros((TM, 128), F32)
    for k in range(TOP_K):
        idx_out = jnp.where(out_lane == k, idxs[k], idx_out)
        gate_out = jnp.where(out_lane == k, es[k] / tot, gate_out)
    idx_ref[...] = idx_out
    gate_ref[...] = gate_out


def _route(x, g, pmod, smod, rw, rb, npt, tiles_per_batch):
    t = x.shape[0]
    psh, ssh = _mod_specs(npt, tiles_per_batch, 0)
    psc, ssc = _mod_specs(npt, tiles_per_batch, 1)
    return pl.pallas_call(
        functools.partial(_route_kernel, npt=npt),
        grid=(t // TM,),
        in_specs=[pl.BlockSpec((TM, D), lambda i: (i, 0)),
                  pl.BlockSpec((1, D), lambda i: (0, 0)),
                  psh, psc, ssh, ssc,
                  pl.BlockSpec((D, N_EXPERTS), lambda i: (0, 0)),
                  pl.BlockSpec((1, N_EXPERTS), lambda i: (0, 0))],
        out_specs=[pl.BlockSpec((TM, D), lambda i: (i, 0)),
                   pl.BlockSpec((TM, 128), lambda i: (i, 0)),
                   pl.BlockSpec((TM, 128), lambda i: (i, 0))],
        out_shape=[jax.ShapeDtypeStruct((t, D), F32),
                   jax.ShapeDtypeStruct((t, 128), jnp.int32),
                   jax.ShapeDtypeStruct((t, 128), F32)],
        compiler_params=_cparams(("parallel",)),
        name="moe_route",
    )(x, g.reshape(1, D), pmod, pmod, smod, smod, rw, rb.reshape(1, N_EXPERTS))


MOE_TN = 512
MOE_TF = MOE_TN // 2


def _moe_kernel(te_ref, nu_ref, tok_ref, u_hbm, wgu_ref, bgu_ref, wdn_ref, bdn_ref, y_ref, xg, xb, acc, sem):
    t = pl.program_id(0)
    j = pl.program_id(1)
    nj = pl.num_programs(1)
    active = t < nu_ref[0]

    @pl.when(jnp.logical_and(active, j == 0))
    def _():
        def issue(r, carry):
            tok = tok_ref[t * TM + r]
            pltpu.make_async_copy(u_hbm.at[pl.ds(tok, 1)], xg.at[pl.ds(r, 1)], sem).start()
            return carry
        lax.fori_loop(0, TM, issue, 0)
        pltpu.make_async_copy(u_hbm.at[pl.ds(0, TM)], xg, sem).wait()
        xb[...] = xg[...].astype(BF16)
        acc[...] = jnp.zeros_like(acc)

    @pl.when(active)
    def _():
        gu = _dot1(xb[...], wgu_ref[0, 0]) + bgu_ref[0, 0]
        up = pltpu.roll(gu, MOE_TN - 1, axis=1)
        gate = jnp.minimum(gu, SWIGLU_LIMIT)
        up = jnp.clip(up, -SWIGLU_LIMIT, SWIGLU_LIMIT)
        h = ((up + 1.0) * gate * _sigmoid(SWIGLU_ALPHA * gate)).astype(BF16)
        ri = lax.broadcasted_iota(jnp.int32, (MOE_TN, MOE_TF), 0)
        ci = lax.broadcasted_iota(jnp.int32, (MOE_TN, MOE_TF), 1)
        pick_even = jnp.where(ri == 2 * ci, 1.0, 0.0).astype(BF16)
        hc = jnp.dot(h, pick_even, preferred_element_type=F32)
        acc[...] += _dot1(hc, wdn_ref[0, 0])

    @pl.when(jnp.logical_and(active, j == nj - 1))
    def _():
        y_ref[...] = acc[...] + bdn_ref[0, 0]

    @pl.when(jnp.logical_and(jnp.logical_not(active), j == nj - 1))
    def _():
        y_ref[...] = jnp.zeros_like(y_ref)


def _moe_experts(u, tile_e, n_used, slot_tok, layer, w_gu, b_gu, w_dn, b_dn):
    n_tiles = tile_e.shape[0]
    nj = 2 * D_EXPERT // MOE_TN
    jj = lambda t, j, nu: jnp.where(t < nu[0], j, nj - 1)
    grid_spec = pltpu.PrefetchScalarGridSpec(
        num_scalar_prefetch=3,
        grid=(n_tiles, nj),
        in_specs=[pl.BlockSpec(memory_space=pl.ANY),
                  pl.BlockSpec((1, 1, D, MOE_TN), lambda t, j, te, nu, tok: (layer, te[t], 0, jj(t, j, nu))),
                  pl.BlockSpec((1, 1, 1, MOE_TN), lambda t, j, te, nu, tok: (layer, te[t], 0, jj(t, j, nu))),
                  pl.BlockSpec((1, 1, MOE_TF, D), lambda t, j, te, nu, tok: (layer, te[t], jj(t, j, nu), 0)),
                  pl.BlockSpec((1, 1, 1, D), lambda t, j, te, nu, tok: (layer, te[t], 0, 0))],
        out_specs=pl.BlockSpec((TM, D), lambda t, j, te, nu, tok: (t, 0)),
        scratch_shapes=[pltpu.VMEM((TM, D), F32), pltpu.VMEM((TM, D), BF16), pltpu.VMEM((TM, D), F32),
                        pltpu.SemaphoreType.DMA],
    )
    nl, ne = b_gu.shape[0], b_gu.shape[1]
    return pl.pallas_call(
        _moe_kernel,
        grid_spec=grid_spec,
        out_shape=jax.ShapeDtypeStruct((n_tiles * TM, D), F32),
        compiler_params=_cparams(("arbitrary", "arbitrary")),
        name="moe_experts",
    )(tile_e, n_used, slot_tok, u, w_gu, b_gu.reshape(nl, ne, 1, 2 * D_EXPERT), w_dn, b_dn.reshape(nl, ne, 1, D))


CMB = 256


def _combine_kernel(slot_ref, h_ref, gate_ref, pg, sg, y_hbm, o_ref, buf, sem, *, npt_c):
    i = pl.program_id(0)

    def issue(r, carry):
        for k in range(TOP_K):
            s = slot_ref[(i * CMB + r) * TOP_K + k]
            pltpu.make_async_copy(y_hbm.at[pl.ds(s, 1)], buf.at[k, pl.ds(r, 1)], sem).start()
        return carry
    lax.fori_loop(0, CMB, issue, 0)
    for k in range(TOP_K):
        pltpu.make_async_copy(y_hbm.at[pl.ds(0, CMB)], buf.at[k], sem).wait()
    gates = gate_ref[...]
    mo = gates[:, 0:1] * buf[0]
    for k in range(1, TOP_K):
        mo = mo + gates[:, k:k + 1] * buf[k]
    o_ref[...] = h_ref[...] + _pick(i < npt_c, pg, sg) * mo


def _moe_combine(h, y, slot4, gate128, pmod, smod, npt_c, tiles_per_batch_c):
    t = h.shape[0]
    nsamp_tiles = (t // CMB) - npt_c
    grid_spec = pltpu.PrefetchScalarGridSpec(
        num_scalar_prefetch=1,
        grid=(t // CMB,),
        in_specs=[pl.BlockSpec((CMB, D), lambda i, s: (i, 0)),
                  pl.BlockSpec((CMB, 128), lambda i, s: (i, 0)),
                  pl.BlockSpec((1, 1, D), lambda i, s: (jnp.minimum(i // tiles_per_batch_c, npt_c // tiles_per_batch_c - 1), 0, 2)),
                  pl.BlockSpec((CMB, D), lambda i, s: (jnp.clip(i - npt_c, 0, nsamp_tiles - 1), 2)),
                  pl.BlockSpec(memory_space=pl.ANY)],
        out_specs=pl.BlockSpec((CMB, D), lambda i, s: (i, 0)),
        scratch_shapes=[pltpu.VMEM((TOP_K, CMB, D), F32), pltpu.SemaphoreType.DMA],
    )
    return pl.pallas_call(
        functools.partial(_combine_kernel, npt_c=npt_c),
        grid_spec=grid_spec,
        out_shape=jax.ShapeDtypeStruct((t, D), F32),
        compiler_params=_cparams(("arbitrary",)),
        name="moe_combine",
    )(slot4, h, gate128, pmod, smod, y)


def _moe_layer(h, g, pmod, smod, layer, rw, rb, w_gu, b_gu, w_dn, b_dn, npt, tiles_per_batch):
    t = h.shape[0]
    u, idx128, gate128 = _route(h, g, pmod, smod, rw[layer], rb[layer], npt, tiles_per_batch)
    idx4 = idx128[:, :TOP_K]
    sel = jnp.any(idx4[:, :, None] == jnp.arange(N_EXPERTS, dtype=jnp.int32)[None, None, :], axis=1).astype(jnp.int32)
    counts = jnp.sum(sel, axis=0)
    rank = jnp.cumsum(sel, axis=0) - 1
    padded = (counts + TM - 1) // TM * TM
    pad_end = jnp.cumsum(padded)
    pad_start = pad_end - padded
    n_tiles = t * TOP_K // TM + N_EXPERTS
    dest4 = jnp.take_along_axis(pad_start[None, :] + rank, idx4, axis=1).astype(jnp.int32)
    tok = jnp.broadcast_to(jnp.arange(t, dtype=jnp.int32)[:, None], (t, TOP_K))
    slot_tok = jnp.zeros((n_tiles * TM,), jnp.int32).at[dest4.reshape(-1)].set(tok.reshape(-1))
    n_used = (pad_end[-1] // TM).astype(jnp.int32)
    tile_ids = jnp.arange(n_tiles, dtype=jnp.int32)
    tile_e = jnp.minimum(jnp.searchsorted(pad_end, jnp.minimum(tile_ids, n_used - 1) * TM, side='right'),
                         N_EXPERTS - 1).astype(jnp.int32)
    y = _moe_experts(u, tile_e, n_used.reshape(1), slot_tok, layer, w_gu, b_gu, w_dn, b_dn)
    return _moe_combine(h, y, dest4.reshape(-1), gate128, pmod, smod,
                        npt * (TM // CMB), tiles_per_batch * (TM // CMB))


C_HEADS = 16
C_NOPE = 128
C_ROPE = 64
C_LORA = 512
ROPE_THETA = 10000.0
MLA_SCALE = (C_NOPE + C_ROPE) ** -0.5
HP = 128


def _rope_tables(pos):
    half = C_ROPE // 2
    inv = ROPE_THETA ** (-jnp.arange(half, dtype=F32) / half)
    ang = pos.astype(F32)[:, None] * inv[None, :]
    z = jnp.zeros_like(ang)
    cos = jnp.concatenate([jnp.cos(ang), jnp.cos(ang), z, z], axis=1)
    sin = jnp.concatenate([-jnp.sin(ang), jnp.sin(ang), z, z], axis=1)
    return cos, sin


def _rope128(x, cos, sin):
    n = x.shape[1]
    lane = lax.broadcasted_iota(jnp.int32, x.shape, 1) % HP
    partner = jnp.where(lane < C_ROPE // 2, pltpu.roll(x, n - C_ROPE // 2, axis=1), pltpu.roll(x, C_ROPE // 2, axis=1))
    return x * cos + partner * sin


def _mla_prep_kernel(z_ref, gq_ref, gkv_ref, gkr_ref, gqn_ref, gqr_ref, gkn_ref, wn_ref, wr_ref, wuk_ref,
                     cos_ref, sin_ref, lat_ref, kr_ref, krb_ref, qn_ref, qr_ref, qabs_ref, qg_ref, *, npt):
    i = pl.program_id(0)
    z = z_ref[...]
    cos = cos_ref[...]
    sin = sin_ref[...]
    lat_ref[...] = _rms(z[:, C_LORA:2 * C_LORA], gkv_ref[...])
    k128 = z[:, 2 * C_LORA:2 * C_LORA + HP]
    k128 = k128 * lax.rsqrt(jnp.sum(k128 * k128, axis=-1, keepdims=True) / C_ROPE + EPS) * gkr_ref[...]
    k128 = _rope128(k128, cos, sin)
    kr_ref[...] = k128[:, 0:C_ROPE]
    krb_ref[...] = k128.astype(BF16)
    cqn = _rms(z[:, 0:C_LORA], gq_ref[...]).astype(BF16)
    qn_all = jnp.dot(cqn, wn_ref[...], preferred_element_type=F32)
    qr_all = jnp.dot(cqn, wr_ref[...], preferred_element_type=F32)
    for h in range(C_HEADS):
        sl = slice(HP * h, HP * (h + 1))
        qn = _rms(qn_all[:, sl], gqn_ref[...]) * MLA_SCALE
        qn_ref[:, sl] = qn.astype(BF16)
        qg_ref[:, sl] = qn * gkn_ref[...]
        x = qr_all[:, sl]
        x = x * lax.rsqrt(jnp.sum(x * x, axis=-1, keepdims=True) / C_ROPE + EPS) * gqr_ref[...]
        qr_ref[:, sl] = (_rope128(x, cos, sin) * MLA_SCALE).astype(BF16)

    @pl.when(i == npt)
    def _():
        for h in range(C_HEADS):
            sl = slice(HP * h, HP * (h + 1))
            qabs_ref[:, C_LORA * h:C_LORA * (h + 1)] = _dot1(qg_ref[:, sl], wuk_ref[:, sl], NT).astype(BF16)


def _mla_prep(z1, gq, gkv, gkr, gqn, gqr, gkn, wn, wr, wuk, cos, sin, npt):
    t = z1.shape[0]
    zw = z1.shape[1]
    row = lambda w: pl.BlockSpec((TM, w), lambda i: (i, 0))
    full = lambda a: pl.BlockSpec(a.shape, lambda i: (0,) * a.ndim)
    pad64 = lambda g: jnp.concatenate([g, jnp.zeros((HP - C_ROPE,), F32)]).reshape(1, HP)
    args = (z1, gq.reshape(1, -1), gkv.reshape(1, -1), pad64(gkr), gqn.reshape(1, -1), pad64(gqr), gkn.reshape(1, -1),
            wn, wr, wuk, cos, sin)
    return pl.pallas_call(
        functools.partial(_mla_prep_kernel, npt=npt),
        grid=(t // TM,),
        in_specs=[row(zw)] + [full(a) for a in args[1:10]] + [row(HP), row(HP)],
        out_specs=[row(C_LORA), row(C_ROPE), row(HP), row(C_HEADS * HP), row(C_HEADS * HP),
                   pl.BlockSpec((TM, C_HEADS * C_LORA), lambda i: (0, 0))],
        out_shape=[jax.ShapeDtypeStruct((t, C_LORA), F32), jax.ShapeDtypeStruct((t, C_ROPE), F32),
                   jax.ShapeDtypeStruct((t, HP), BF16), jax.ShapeDtypeStruct((t, C_HEADS * HP), BF16),
                   jax.ShapeDtypeStruct((t, C_HEADS * HP), BF16), jax.ShapeDtypeStruct((TM, C_HEADS * C_LORA), BF16)],
        scratch_shapes=[pltpu.VMEM((TM, C_HEADS * HP), F32)],
        compiler_params=_cparams(("arbitrary",)),
        name="mla_prep",
    )(*args)


def _mla_kv_kernel(lat_ref, wuk_ref, wuv_ref, gkn_ref, kn_ref, v_ref):
    lat = lat_ref[...].astype(BF16)
    kn = jnp.dot(lat, wuk_ref[...], preferred_element_type=F32)
    for h in range(C_HEADS):
        sl = slice(HP * h, HP * (h + 1))
        kn_ref[:, sl] = _rms(kn[:, sl], gkn_ref[...]).astype(BF16)
    v_ref[...] = jnp.dot(lat, wuv_ref[...], preferred_element_type=F32).astype(BF16)


def _mla_kv(lat, wuk, wuv, gkn, rows):
    wide = C_HEADS * HP
    return pl.pallas_call(
        _mla_kv_kernel,
        grid=(rows // TM,),
        in_specs=[pl.BlockSpec((TM, C_LORA), lambda i: (i, 0)),
                  pl.BlockSpec((C_LORA, wide), lambda i: (0, 0)),
                  pl.BlockSpec((C_LORA, wide), lambda i: (0, 0)),
                  pl.BlockSpec((1, HP), lambda i: (0, 0))],
        out_specs=[pl.BlockSpec((TM, wide), lambda i: (i, 0))] * 2,
        out_shape=[jax.ShapeDtypeStruct((rows, wide), BF16)] * 2,
        compiler_params=_cparams(("parallel",)),
        name="mla_kv",
    )(lat, wuk, wuv, gkn.reshape(1, HP))


FQ = 512


def _mla_flash_kernel(qn_ref, qr_ref, kn_ref, kr_ref, v_ref, o_ref):
    qi = pl.program_id(2)
    qn = qn_ref[...]
    qr = qr_ref[...]
    row = lax.broadcasted_iota(jnp.int32, (FQ, FQ), 0)
    col = lax.broadcasted_iota(jnp.int32, (FQ, FQ), 1)

    def body(kb, carry):
        m, l, acc = carry
        ks = pl.ds(pl.multiple_of(kb * FQ, FQ), FQ)
        s = (lax.dot_general(qn, kn_ref[ks, :], (NT, ((), ())), preferred_element_type=F32)
             + lax.dot_general(qr, kr_ref[ks, :], (NT, ((), ())), preferred_element_type=F32))
        s = jnp.where(jnp.logical_or(kb < qi, col <= row), s, -jnp.inf)
        m_new = jnp.maximum(m, jnp.max(s, axis=-1, keepdims=True))
        alpha = jnp.exp(m - m_new)
        p = jnp.exp(s - m_new)
        l = alpha * l + jnp.sum(p, axis=-1, keepdims=True)
        acc = alpha * acc + jnp.dot(p.astype(BF16), v_ref[ks, :], preferred_element_type=F32)
        return m_new, l, acc

    m0 = jnp.full((FQ, 1), -jnp.inf, F32)
    l0 = jnp.zeros((FQ, 1), F32)
    a0 = jnp.zeros((FQ, HP), F32)
    m, l, acc = lax.fori_loop(0, qi + 1, body, (m0, l0, a0))
    o_ref[...] = (acc / l).astype(BF16)


def _mla_flash(qn, qr, kn, krb, v, nb, seq):
    nq = seq // FQ
    qspec = pl.BlockSpec((FQ, HP), lambda b, h, q: (b * nq + q, h))
    kspec = pl.BlockSpec((seq, HP), lambda b, h, q: (b, h))
    return pl.pallas_call(
        _mla_flash_kernel,
        grid=(nb, C_HEADS, nq),
        in_specs=[qspec, qspec, kspec, pl.BlockSpec((seq, HP), lambda b, h, q: (b, 0)), kspec],
        out_specs=qspec,
        out_shape=jax.ShapeDtypeStruct((nb * seq, C_HEADS * HP), BF16),
        compiler_params=_cparams(("parallel", "parallel", "parallel")),
        name="mla_flash",
    )(qn, qr, kn, krb, v)


PPS = 16


def _mla_s_kernel(*refs, n_steps):
    pt_ref = refs[0]
    lat_refs = refs[1:1 + PPS]
    kr_refs = refs[1 + PPS:1 + 2 * PPS]
    wukt_ref, qabs_ref, qr_ref, latn_ref, krn_ref, o_ref, wst, latb, lg, m_ref, l_ref, acc_ref = refs[1 + 2 * PPS:]
    b = pl.program_id(0)
    kt = pl.program_id(1)
    nq = qabs_ref.shape[1]
    nheads = C_HEADS
    ntok = nq // nheads

    @pl.when(jnp.logical_and(b == 0, kt == 0))
    def _():
        wst[0:nheads * HP, :] = wukt_ref[...]

    @pl.when(kt == 0)
    def _():
        wst[nheads * HP:nheads * HP + nq, :] = qabs_ref[0]
        m_ref[...] = jnp.full_like(m_ref, -jnp.inf)
        l_ref[...] = jnp.zeros_like(l_ref)
        acc_ref[...] = jnp.zeros_like(acc_ref)

    qr = qr_ref[0]

    def key_logits(lat_bf16, kr_f32):
        nkeys = lat_bf16.shape[0]
        r = lax.dot_general(wst[...], lat_bf16, (NT, ((), ())), preferred_element_type=F32)
        kraw = r[0:nheads * HP].reshape(nheads, HP, nkeys)
        rs = lax.rsqrt(jnp.sum(kraw * kraw, axis=1) / HP + EPS)
        lraw = r[nheads * HP:nheads * HP + nq].reshape(ntok, nheads, nkeys) * rs[None]
        return lraw.reshape(nq, nkeys) + _dot1(qr, kr_f32, NT)

    def update(logits, lat_bf16):
        m_old = m_ref[...]
        m_new = jnp.maximum(m_old, jnp.max(logits, axis=-1, keepdims=True))
        alpha = jnp.exp(m_old - m_new)
        p = jnp.exp(logits - m_new)
        l_ref[...] = alpha * l_ref[...] + jnp.sum(p, axis=-1, keepdims=True)
        acc_ref[...] = alpha * acc_ref[...] + jnp.dot(p.astype(BF16), lat_bf16, preferred_element_type=F32)
        m_ref[...] = m_new

    for pp in range(PPS // 2):
        lat2 = jnp.concatenate([lat_refs[2 * pp][0, 0], lat_refs[2 * pp + 1][0, 0]], axis=0).astype(BF16)
        kr2 = jnp.concatenate([kr_refs[2 * pp][0, 0], kr_refs[2 * pp + 1][0, 0]], axis=0)
        latb[2 * PAGE * pp:2 * PAGE * (pp + 1), :] = lat2
        lg[:, 2 * PAGE * pp:2 * PAGE * (pp + 1)] = key_logits(lat2, kr2)
    update(lg[...], latb[...])

    @pl.when(kt == n_steps - 1)
    def _():
        latn = jnp.concatenate([latn_ref[0], jnp.zeros((PAGE - 8, C_LORA), F32)], axis=0).astype(BF16)
        krn = jnp.concatenate([krn_ref[0], jnp.zeros((PAGE - 8, C_ROPE), F32)], axis=0)
        logits = key_logits(latn, krn)
        qtok = lax.broadcasted_iota(jnp.int32, (nq, PAGE), 0) // nheads
        slot = lax.broadcasted_iota(jnp.int32, (nq, PAGE), 1)
        update(jnp.where(slot <= qtok, logits, -jnp.inf), latn)
        o_ref[0] = acc_ref[...] / l_ref[...]


def _mla_sample(page_table, lat_cache, kr_cache, ci, wukt, qabs, qr, lat_new8, kr_new8):
    nb, npages = page_table.shape
    n_steps = npages // PPS
    nq = qabs.shape[1]

    def page_spec(k, width):
        return pl.BlockSpec((1, 1, PAGE, width), lambda b, t, pt: (ci, pt[b * npages + t * PPS + k], 0, 0))

    grid_spec = pltpu.PrefetchScalarGridSpec(
        num_scalar_prefetch=1,
        grid=(nb, n_steps),
        in_specs=[page_spec(k, C_LORA) for k in range(PPS)] + [page_spec(k, C_ROPE) for k in range(PPS)]
                 + [pl.BlockSpec(wukt.shape, lambda b, t, pt: (0, 0)),
                    pl.BlockSpec((1, nq, C_LORA), lambda b, t, pt: (b, 0, 0)),
                    pl.BlockSpec((1, nq, C_ROPE), lambda b, t, pt: (b, 0, 0)),
                    pl.BlockSpec((1, 8, C_LORA), lambda b, t, pt: (b, 0, 0)),
                    pl.BlockSpec((1, 8, C_ROPE), lambda b, t, pt: (b, 0, 0))],
        out_specs=pl.BlockSpec((1, nq, C_LORA), lambda b, t, pt: (b, 0, 0)),
        scratch_shapes=[pltpu.VMEM((C_HEADS * HP + nq, C_LORA), BF16),
                        pltpu.VMEM((PPS * PAGE, C_LORA), BF16),
                        pltpu.VMEM((nq, PPS * PAGE), F32),
                        pltpu.VMEM((nq, 1), F32), pltpu.VMEM((nq, 1), F32), pltpu.VMEM((nq, C_LORA), F32)],
    )
    return pl.pallas_call(
        functools.partial(_mla_s_kernel, n_steps=n_steps),
        grid_spec=grid_spec,
        out_shape=jax.ShapeDtypeStruct((nb, nq, C_LORA), F32),
        compiler_params=_cparams(("arbitrary", "arbitrary")),
        name="mla_sample",
    )(page_table.reshape(-1), *([lat_cache] * PPS), *([kr_cache] * PPS), wukt, qabs, qr, lat_new8, kr_new8)


def _mla_uv_kernel(c_ref, w_ref, o_ref):
    o_ref[...] = _dot1(c_ref[...], w_ref[0])


def _mla_uv(ctx, wuv3):
    rows = ctx.shape[0]
    return pl.pallas_call(
        _mla_uv_kernel,
        grid=(C_HEADS,),
        in_specs=[pl.BlockSpec((rows, C_LORA), lambda h: (0, h)),
                  pl.BlockSpec((1, C_LORA, HP), lambda h: (h, 0, 0))],
        out_specs=pl.BlockSpec((rows, HP), lambda h: (0, h)),
        out_shape=jax.ShapeDtypeStruct((rows, C_HEADS * HP), F32),
        compiler_params=_cparams(("parallel",)),
        name="mla_uv",
    )(ctx, wuv3)


def kernel(x_prompt, x_sample, state_hgrn, cache_swa_kv, cache_mla_latent, cache_mla_krope, page_table, c_prompt, c_sample, ada_w, ada_b, norm_g, rel_bias, ab_w_in, ab_w_out, hgrn_lb_logits, hgrn_norm_g, swa_q_g, swa_k_g, mla_w_in, mla_q_a_g, mla_kv_a_g, mla_w_uq, mla_w_uk, mla_w_uv, mla_qn_g, mla_qr_g, mla_kn_g, mla_kr_g, mla_w_o, moe_router_w, moe_router_b, moe_w_gu, moe_b_gu, moe_w_dn, moe_b_dn):
    nb, seq, _ = x_prompt.shape
    nsb, nst, _ = x_sample.shape
    tp, ts = nb * seq, nsb * nst
    assert ts == TM and seq % TM == 0 and nst <= 4
    npt, tpb = tp // TM, seq // TM
    x = jnp.concatenate([x_prompt.reshape(tp, D), x_sample.reshape(ts, D)], axis=0).astype(F32)

    nc = nb + nsb
    c_all = jnp.concatenate([c_prompt, c_sample, jnp.zeros((-nc % 8, D), c_prompt.dtype)], axis=0).astype(F32)
    mods = _adaln(c_all, ada_w, ada_b)

    def mod(s):
        return mods[s, 0:nb].reshape(nb, 1, 3 * D), jnp.repeat(mods[s, nb:nc], nst, axis=0)

    pm, sm = mod(0)
    z = _normmod_mm(x, norm_g[0, 0], pm, sm, ab_w_in[0].astype(BF16), 1024, npt, tpb)
    zs3 = z[tp:].reshape(nsb, nst, z.shape[1])
    oa_p, hg_p = _hgrn_prompt(z, hgrn_lb_logits, hgrn_norm_g[0], nb, seq, 0)
    oa_s, hg_s = _hgrn_sample(zs3, hgrn_lb_logits, hgrn_norm_g[0], state_hgrn, 0)
    ob_p, k_p, v_p = _dilated_prompt(z, swa_q_g[0], swa_k_g[0], _band_tables(rel_bias), nb, seq)
    buf_len = cache_swa_kv.shape[2]
    cache3 = cache_swa_kv[0].reshape(nsb, buf_len, 2 * 8 * 128)
    ob_s, k_s, v_s = _dilated_sample(zs3, swa_q_g[0], swa_k_g[0], cache3, _sample_tables(rel_bias, nst, buf_len))
    w_out = ab_w_out[0].astype(BF16)
    h = _proj_residual([oa_p, ob_p], [oa_s.reshape(ts, 1024), ob_s.reshape(ts, 1024)],
                       [w_out[:1024], w_out[1024:]], x, pm, sm, npt, tpb)
    pm, sm = mod(1)
    h = _moe_layer(h, norm_g[0, 1], pm, sm, 0, moe_router_w, moe_router_b, moe_w_gu, moe_b_gu, moe_w_dn, moe_b_dn, npt, tpb)

    pm, sm = mod(2)
    w_in1 = jnp.pad(mla_w_in[0], ((0, 0), (0, HP - C_ROPE))).astype(BF16)
    z1 = _normmod_mm(h, norm_g[1, 0], pm, sm, w_in1, w_in1.shape[1], npt, tpb)
    past = page_table.shape[1] * PAGE
    pos = jnp.concatenate([jnp.tile(jnp.arange(seq), nb), jnp.tile(past + jnp.arange(nst), nsb)])
    cos, sin = _rope_tables(pos)
    wq = mla_w_uq[0]
    wn = wq[:, :, :C_NOPE].reshape(C_LORA, C_HEADS * HP).astype(BF16)
    wr = jnp.pad(wq[:, :, C_NOPE:], ((0, 0), (0, 0), (0, HP - C_ROPE))).reshape(C_LORA, C_HEADS * HP).astype(BF16)
    wuk = mla_w_uk[0].reshape(C_LORA, C_HEADS * HP).astype(BF16)
    wuv = mla_w_uv[0].reshape(C_LORA, C_HEADS * HP).astype(BF16)
    lat, kr, krb, qn, qr, qabs = _mla_prep(z1, mla_q_a_g[0], mla_kv_a_g[0], mla_kr_g[0], mla_qn_g[0], mla_qr_g[0],
                                           mla_kn_g[0], wn, wr, wuk, cos, sin, npt)
    kn, v = _mla_kv(lat, wuk, wuv, mla_kn_g[0], tp)
    ctx_p = _mla_flash(qn, qr, kn, krb, v, nb, seq)
    qabs3 = qabs.reshape(nsb, nst * C_HEADS, C_LORA)
    qr_s = qr[tp:].reshape(ts, C_HEADS, HP)[:, :, :C_ROPE].reshape(nsb, nst * C_HEADS, C_ROPE)
    lat_new8 = jnp.pad(lat[tp:].reshape(nsb, nst, C_LORA), ((0, 0), (0, 8 - nst), (0, 0)))
    kr_new8 = jnp.pad(kr[tp:].reshape(nsb, nst, C_ROPE), ((0, 0), (0, 8 - nst), (0, 0)))
    ctx_s = _mla_sample(page_table, cache_mla_latent, cache_mla_krope, 0, wuk.T, qabs3, qr_s, lat_new8, kr_new8)
    wuv3 = jnp.transpose(mla_w_uv[0], (1, 0, 2)).astype(BF16)
    o_s = _mla_uv(ctx_s.reshape(ts, C_HEADS * C_LORA), wuv3)
    h = _proj_residual([ctx_p], [o_s], [mla_w_o[0].astype(BF16)], h, pm, sm, npt, tpb)
    pm, sm = mod(3)
    h = _moe_layer(h, norm_g[1, 1], pm, sm, 1, moe_router_w, moe_router_b, moe_w_gu, moe_b_gu, moe_w_dn, moe_b_dn, npt, tpb)

    y_prompt = h[:tp].reshape(nb, seq, D).astype(x_prompt.dtype)
    y_sample = h[tp:].reshape(nsb, nst, D).astype(x_sample.dtype)
    hgrn_prompt = hg_p[None].astype(state_hgrn.dtype)
    hgrn_sample = hg_s.astype(state_hgrn.dtype)
    kv = lambda k, v, b, t: jnp.stack([k.reshape(b, t, 8, 128), v.reshape(b, t, 8, 128)], axis=2)[None]
    swa_prompt = kv(k_p, v_p, nb, seq).astype(cache_swa_kv.dtype)
    swa_sample = kv(k_s, v_s, nsb, nst).astype(cache_swa_kv.dtype)
    lat_prompt = lat[:tp].reshape(1, nb, seq, C_LORA).astype(cache_mla_latent.dtype)
    lat_sample = lat[tp:].reshape(1, nsb, nst, C_LORA).astype(cache_mla_latent.dtype)
    krope_prompt = kr[:tp].reshape(1, nb, seq, C_ROPE).astype(cache_mla_krope.dtype)
    krope_sample = kr[tp:].reshape(1, nsb, nst, C_ROPE).astype(cache_mla_krope.dtype)
    return (y_prompt, y_sample, hgrn_prompt, hgrn_sample, swa_prompt, swa_sample,
            lat_prompt, lat_sample, krope_prompt, krope_sample)
```

```python
import functools
import math

import numpy as np
import jax
import jax.numpy as jnp
from jax import lax
from jax.experimental import pallas as pl
from jax.experimental.pallas import tpu as pltpu

F32 = jnp.float32
BF16 = jnp.bfloat16
EPS = 1e-6

D = 2048
TM = 512
N_EXPERTS = 32
TOP_K = 4
D_EXPERT = 2048
SWIGLU_LIMIT = 7.0
SWIGLU_ALPHA = 1.702
PAGE = 128
V7X_VMEM_LIMIT = 56 * 1024 * 1024

NN = ((1,), (0,))
NT = ((1,), (1,))
TN = ((0,), (0,))


def _cparams(sem, vmem=V7X_VMEM_LIMIT):
    return pltpu.CompilerParams(dimension_semantics=sem, vmem_limit_bytes=vmem)


def _dot1(a, b, dims=NN):
    return lax.dot_general(a.astype(BF16), b.astype(BF16), (dims, ((), ())), preferred_element_type=F32)


def _split2(x):
    hi = x.astype(BF16)
    lo = (x - hi.astype(F32)).astype(BF16)
    return hi, lo


def _split3(x):
    hi = x.astype(BF16)
    r = x - hi.astype(F32)
    mid = r.astype(BF16)
    lo = (r - mid.astype(F32)).astype(BF16)
    return hi, mid, lo


def _dot_exact_rhs(a01, x):
    hi, mid, lo = _split3(x)
    f = lambda p: lax.dot_general(a01, p, (NN, ((), ())), preferred_element_type=F32)
    return f(hi) + f(mid) + f(lo)


def _dot3(a, b, dims=NN):
    ah, al = _split2(a)
    bh, bl = _split2(b)
    f = lambda p, q: lax.dot_general(p, q, (dims, ((), ())), preferred_element_type=F32)
    return f(ah, bh) + f(ah, bl) + f(al, bh)


def _sigmoid(x):
    return 1.0 / (1.0 + jnp.exp(-x))


def _rms(x, g):
    return x * lax.rsqrt(jnp.mean(x * x, axis=-1, keepdims=True) + EPS) * g


def _ada_kernel(c_ref, w_ref, b_ref, o_ref):
    c = c_ref[...]
    s = c * _sigmoid(c)
    o_ref[0] = _dot1(s, w_ref[0, 0]) + b_ref[0, 0]


def _adaln(c_all, ada_w, ada_b):
    r = c_all.shape[0]
    tn = 1024
    depth = ada_w.shape[0]
    b4 = ada_b.reshape(depth, 2, 1, 3 * D)
    return pl.pallas_call(
        _ada_kernel,
        grid=(depth * 2, 3 * D // tn),
        in_specs=[pl.BlockSpec((r, D), lambda s, j: (0, 0)),
                  pl.BlockSpec((1, 1, D, tn), lambda s, j: (s // 2, s % 2, 0, j)),
                  pl.BlockSpec((1, 1, 1, tn), lambda s, j: (s // 2, s % 2, 0, j))],
        out_specs=pl.BlockSpec((1, r, tn), lambda s, j: (s, 0, j)),
        out_shape=jax.ShapeDtypeStruct((depth * 2, r, 3 * D), F32),
        compiler_params=_cparams(("parallel", "parallel")),
        name="adaln",
    )(c_all, ada_w, b4)


def _pick(is_prompt, p_ref, s_ref):
    return jnp.where(is_prompt, p_ref[0], s_ref[...])


def _mod_specs(npt, rows_per_batch_tiles, col, tn=D, with_j=False):
    if with_j:
        pm = pl.BlockSpec((1, 1, tn), lambda i, j: (jnp.minimum(i // rows_per_batch_tiles, npt // rows_per_batch_tiles - 1), 0, col * (D // tn) + j))
        sm = pl.BlockSpec((TM, tn), lambda i, j: (0, col * (D // tn) + j))
    else:
        pm = pl.BlockSpec((1, 1, tn), lambda i, *_: (jnp.minimum(i // rows_per_batch_tiles, npt // rows_per_batch_tiles - 1), 0, col))
        sm = pl.BlockSpec((TM, tn), lambda i, *_: (0, col))
    return pm, sm


def _nm_mm_kernel(x_ref, g_ref, psh, psc, ssh, ssc, w_ref, o_ref, xn_ref, *, npt):
    i = pl.program_id(0)
    j = pl.program_id(1)

    @pl.when(j == 0)
    def _():
        xr = _rms(x_ref[...], g_ref[...])
        sc = _pick(i < npt, psc, ssc)
        sh = _pick(i < npt, psh, ssh)
        xn_ref[...] = (xr * (1.0 + sc) + sh).astype(BF16)

    o_ref[...] = jnp.dot(xn_ref[...], w_ref[...], preferred_element_type=F32)


def _normmod_mm(x, g, pmod, smod, w_bf16, tn, npt, tiles_per_batch):
    t = x.shape[0]
    n = w_bf16.shape[1]
    psh, ssh = _mod_specs(npt, tiles_per_batch, 0)
    psc, ssc = _mod_specs(npt, tiles_per_batch, 1)
    return pl.pallas_call(
        functools.partial(_nm_mm_kernel, npt=npt),
        grid=(t // TM, n // tn),
        in_specs=[pl.BlockSpec((TM, D), lambda i, j: (i, 0)),
                  pl.BlockSpec((1, D), lambda i, j: (0, 0)),
                  psh, psc, ssh, ssc,
                  pl.BlockSpec((D, tn), lambda i, j: (0, j))],
        out_specs=pl.BlockSpec((TM, tn), lambda i, j: (i, j)),
        out_shape=jax.ShapeDtypeStruct((t, n), F32),
        scratch_shapes=[pltpu.VMEM((TM, D), BF16)],
        compiler_params=_cparams(("parallel", "arbitrary")),
        name="normmod_mm",
    )(x, g.reshape(1, D), pmod, pmod, smod, smod, w_bf16)


def _proj_res_kernel(*refs, npt, nk):
    xp = refs[0:nk]
    xs = refs[nk:2 * nk]
    ws = refs[2 * nk:3 * nk]
    h_ref, pg, sg, o_ref = refs[3 * nk:3 * nk + 4]
    xb = refs[3 * nk + 4:]
    i = pl.program_id(0)
    j = pl.program_id(1)

    @pl.when(jnp.logical_and(j == 0, i < npt))
    def _():
        for k in range(nk):
            xb[k][...] = xp[k][...].astype(BF16)

    @pl.when(jnp.logical_and(j == 0, i >= npt))
    def _():
        for k in range(nk):
            xb[k][...] = xs[k][...].astype(BF16)

    acc = jnp.dot(xb[0][...], ws[0][...], preferred_element_type=F32)
    for k in range(1, nk):
        acc = acc + jnp.dot(xb[k][...], ws[k][...], preferred_element_type=F32)
    gate = _pick(i < npt, pg, sg)
    o_ref[...] = h_ref[...] + gate * acc


def _proj_residual(xps, xss, ws, h, pmod, smod, npt, tiles_per_batch, tn=1024):
    nk = len(ws)
    t = h.shape[0]
    ks = [w.shape[0] for w in ws]
    pg, sg = _mod_specs(npt, tiles_per_batch, 2, tn=tn, with_j=True)
    in_specs = ([pl.BlockSpec((TM, k), lambda i, j: (jnp.minimum(i, npt - 1), 0)) for k in ks]
                + [pl.BlockSpec((TM, k), lambda i, j: (0, 0)) for k in ks]
                + [pl.BlockSpec((k, tn), lambda i, j: (0, j)) for k in ks]
                + [pl.BlockSpec((TM, tn), lambda i, j: (i, j)), pg, sg])
    return pl.pallas_call(
        functools.partial(_proj_res_kernel, npt=npt, nk=nk),
        grid=(t // TM, D // tn),
        in_specs=in_specs,
        out_specs=pl.BlockSpec((TM, tn), lambda i, j: (i, j)),
        out_shape=jax.ShapeDtypeStruct((t, D), F32),
        scratch_shapes=[pltpu.VMEM((TM, k), BF16) for k in ks],
        compiler_params=_cparams(("parallel", "arbitrary")),
        name="proj_residual",
    )(*xps, *xss, *ws, h, pmod, smod)


HG_C = 64
HG_B = 16
HG_STEP = 256


def _lower_bound(lbl, layer):
    e = jnp.exp(lbl - jnp.max(lbl, axis=0, keepdims=True))
    return jnp.sum(e[0:layer + 1], axis=0, keepdims=True) / jnp.sum(e, axis=0, keepdims=True)


def _hgrn_chunk(q, af, v, st, lb, tri):
    c, dk = q.shape
    nb = c // HG_B
    f = lb + (1.0 - lb) * _sigmoid(af)
    g = jnp.log(f)
    k = 1.0 - f
    cum = _dot_exact_rhs(tri, g)
    o = _dot1(q * jnp.exp(cum), st, NT)
    q3 = q.reshape(nb, HG_B, dk)
    k3 = k.reshape(nb, HG_B, dk)
    c3 = cum.reshape(nb, HG_B, dk)
    tpos = lax.broadcasted_iota(jnp.int32, (nb, HG_B, dk), 1)
    lane3 = lax.broadcasted_iota(jnp.int32, (nb, HG_B, dk), 2)
    dmat = jnp.zeros((nb, HG_B, dk), F32)
    for s in range(HG_B):
        rel = c3 - c3[:, s:s + 1, :]
        a = q3 * k3[:, s:s + 1, :] * jnp.exp(jnp.where(tpos >= s, rel, -jnp.inf))
        col = jnp.sum(a, axis=-1, keepdims=True)
        dmat = jnp.where(lane3 == s, col, dmat)
    lane_c = lax.broadcasted_iota(jnp.int32, (HG_B, c), 1)
    rows = []
    for blk in range(nb):
        r0 = HG_B * blk
        d_blk = dmat[blk]
        if blk == 0:
            rows.append(d_blk[:, :c])
            continue
        d_blk = pltpu.roll(d_blk, r0, axis=1)[:, :c]
        cb = cum[r0 - 1:r0, :]
        qs = q[r0:r0 + HG_B] * jnp.exp(cum[r0:r0 + HG_B] - cb)
        ks = k * jnp.exp(jnp.minimum(cb - cum, 0.0))
        s_off = _dot1(qs, ks, NT)
        rows.append(jnp.where(lane_c < r0, s_off, 0.0) + d_blk)
    scores = jnp.concatenate(rows, axis=0)
    o = o + _dot1(scores, v)
    last = cum[c - 1:c, :]
    st_new = jnp.exp(last) * st + _dot1(v, k * jnp.exp(last - cum), TN)
    return o, st_new


def _hgrn_p_kernel(q_ref, f_ref, v_ref, g_ref, lbl_ref, gn_ref, o_ref, s_ref, st_ref, *, layer, nsteps):
    tb = pl.program_id(2)

    @pl.when(tb == 0)
    def _():
        st_ref[...] = jnp.zeros_like(st_ref)

    lb = _lower_bound(lbl_ref[...], layer)
    ri = lax.broadcasted_iota(jnp.int32, (HG_C, HG_C), 0)
    ci = lax.broadcasted_iota(jnp.int32, (HG_C, HG_C), 1)
    tri = jnp.where(ri >= ci, 1.0, 0.0).astype(BF16)
    st = st_ref[...]
    for c in range(HG_STEP // HG_C):
        sl = slice(HG_C * c, HG_C * (c + 1))
        o, st = _hgrn_chunk(q_ref[sl, :], f_ref[sl, :], v_ref[sl, :], st, lb, tri)
        o_ref[sl, :] = (_rms(o, gn_ref[...]) * _sigmoid(g_ref[sl, :])).astype(BF16)
    st_ref[...] = st

    @pl.when(tb == nsteps - 1)
    def _():
        s_ref[0, 0] = st.T


def _hgrn_prompt(z, lb_logits, gn, nb, seq, layer):
    nh = 8
    nsteps = seq // HG_STEP
    zspec = lambda c0: pl.BlockSpec((HG_STEP, 128), lambda b, h, t: (b * nsteps + t, c0 + h))
    return pl.pallas_call(
        functools.partial(_hgrn_p_kernel, layer=layer, nsteps=nsteps),
        grid=(nb, nh, nsteps),
        in_specs=[zspec(0), zspec(8), zspec(16), zspec(24),
                  pl.BlockSpec((lb_logits.shape[0], 128), lambda b, h, t: (0, h)),
                  pl.BlockSpec((1, 128), lambda b, h, t: (0, 0))],
        out_specs=[pl.BlockSpec((HG_STEP, 128), lambda b, h, t: (b * nsteps + t, h)),
                   pl.BlockSpec((1, 1, 128, 128), lambda b, h, t: (b, h, 0, 0))],
        out_shape=[jax.ShapeDtypeStruct((nb * seq, nh * 128), BF16),
                   jax.ShapeDtypeStruct((nb, nh, 128, 128), F32)],
        scratch_shapes=[pltpu.VMEM((128, 128), F32)],
        compiler_params=_cparams(("parallel", "parallel", "arbitrary")),
        name="hgrn_prompt",
    )(z, z, z, z, lb_logits, gn.reshape(1, 128))


def _hgrn_s_kernel(q_ref, f_ref, v_ref, g_ref, lbl_ref, gn_ref, s0_ref, o_ref, s_ref, *, layer):
    nt = q_ref.shape[1]
    lb_all = _lower_bound(lbl_ref[...], layer)
    tpos = lax.broadcasted_iota(jnp.int32, (nt, 128), 0)
    outs = []
    for h in range(8):
        sl = slice(128 * h, 128 * (h + 1))
        q = q_ref[0, :, sl]
        v = v_ref[0, :, sl]
        lb = lb_all[:, sl]
        f = lb + (1.0 - lb) * _sigmoid(f_ref[0, :, sl])
        g = jnp.log(f)
        k = 1.0 - f
        crow = [g[0:1]]
        for t in range(1, nt):
            crow.append(crow[-1] + g[t:t + 1])
        cum = jnp.concatenate(crow, axis=0)
        last = crow[-1]
        s0 = s0_ref[0, 0, h]
        qe = jnp.concatenate([q * jnp.exp(cum), jnp.zeros((8 - nt, 128), F32)], axis=0)
        o = _dot1(qe, s0)[0:nt]
        for s in range(nt):
            a = q * k[s:s + 1] * jnp.exp(jnp.where(tpos >= s, cum - cum[s:s + 1], -jnp.inf))
            o = o + jnp.sum(a, axis=-1, keepdims=True) * v[s:s + 1]
        kp = k * jnp.exp(last - cum)
        w8 = jnp.concatenate([jnp.exp(last), kp, jnp.zeros((8 - 1 - nt, 128), F32)], axis=0)
        wt = jnp.concatenate([w8, jnp.zeros((120, 128), F32)], axis=0).T
        s_new = wt[:, 0:1] * s0
        for s in range(nt):
            s_new = s_new + wt[:, 1 + s:2 + s] * v[s:s + 1]
        s_ref[0, 0, h] = s_new
        outs.append(_rms(o, gn_ref[...]) * _sigmoid(g_ref[0, :, sl]))
    o_ref[0] = jnp.concatenate(outs, axis=1)


def _hgrn_sample(zs3, lb_logits, gn, state, layer):
    nb, nt, _ = zs3.shape
    zspec = lambda c: pl.BlockSpec((1, nt, 1024), lambda b: (b, 0, c))
    sspec = pl.BlockSpec((1, 1, 8, 128, 128), lambda b: (0, b, 0, 0, 0))
    return pl.pallas_call(
        functools.partial(_hgrn_s_kernel, layer=layer),
        grid=(nb,),
        in_specs=[zspec(0), zspec(1), zspec(2), zspec(3),
                  pl.BlockSpec(lb_logits.shape, lambda b: (0, 0)),
                  pl.BlockSpec((1, 128), lambda b: (0, 0)),
                  sspec],
        out_specs=[pl.BlockSpec((1, nt, 1024), lambda b: (b, 0, 0)), sspec],
        out_shape=[jax.ShapeDtypeStruct((nb, nt, 1024), F32),
                   jax.ShapeDtypeStruct(state.shape, F32)],
        compiler_params=_cparams(("parallel",)),
        name="hgrn_sample",
    )(zs3, zs3, zs3, zs3, lb_logits, gn.reshape(1, 128), state)


DILATED = ((128, 1), (512, 4), (2048, 16))
REL_BUCKETS = 32
REL_MAX_DISTANCE = 2048
ATT_SCALE = 128 ** -0.5
QB = 128


def _rel_bucket_np(dist):
    exact = REL_BUCKETS // 2
    d = np.asarray(dist)
    far = exact + (np.log(np.maximum(d, 1).astype(np.float32) / np.float32(exact))
                   / np.float32(math.log(REL_MAX_DISTANCE / exact)) * np.float32(REL_BUCKETS - exact)).astype(np.int32)
    return np.where(d < exact, d, np.minimum(far, REL_BUCKETS - 1)).astype(np.int32)


def _band_tables(rel_bias):
    rb = rel_bias.astype(F32)
    tabs = []
    for (window, dil), width in zip(DILATED, (2 * QB, 2 * QB, QB)):
        t = np.arange(QB)[:, None]
        c = np.arange(width)[None, :]
        steps = t + (width - QB) - c
        valid = (steps >= 0) & (steps <= window // dil)
        bucket = _rel_bucket_np(np.clip(steps, 0, window // dil) * dil)
        tab = jnp.where(jnp.asarray(valid)[None], jnp.transpose(rb[jnp.asarray(bucket)], (2, 0, 1)), -jnp.inf)
        tabs.append(tab)
    return tabs


def _soft_block(qb, kw, vw, bias):
    l = _dot1(qb, kw, NT) * ATT_SCALE + bias
    m = jnp.max(l, axis=-1, keepdims=True)
    p = jnp.exp(l - m)
    return m, jnp.sum(p, axis=-1, keepdims=True), _dot1(p, vw)


def _dil_p_kernel(q_ref, k_ref, v_ref, gq_ref, gk_ref, b1_ref, b2_ref, b3_ref, o_ref, ko_ref, vo_ref,
                  qs, ks, vs, a1, a2, a3, m1, m2, m3, l1, l2, l3):
    seq = q_ref.shape[0]
    kn = _rms(k_ref[...], gk_ref[...])
    v = v_ref[...]
    ko_ref[...] = kn
    vo_ref[...] = v
    qs[...] = _rms(q_ref[...], gq_ref[...])
    ks[...] = kn
    vs[...] = v
    zpad = jnp.zeros((QB, 128), F32)
    first_cols = lax.broadcasted_iota(jnp.int32, (QB, 2 * QB), 1) >= QB

    def banded(qsub, ksub, vsub, bias, n_blocks, store):
        for i in range(n_blocks):
            qb = qsub(i)
            if i == 0:
                kw = jnp.concatenate([zpad, ksub(0)], axis=0)
                vw = jnp.concatenate([zpad, vsub(0)], axis=0)
                bb = jnp.where(first_cols, bias, -jnp.inf)
            else:
                kw = jnp.concatenate([ksub(i - 1), ksub(i)], axis=0)
                vw = jnp.concatenate([vsub(i - 1), vsub(i)], axis=0)
                bb = bias
            store(i, *_soft_block(qb, kw, vw, bb))

    def st1(i, m, l, a):
        sl = slice(QB * i, QB * (i + 1))
        m1[sl, :] = m
        l1[sl, :] = l
        a1[sl, :] = a
    blk = lambda ref: (lambda i: ref[QB * i:QB * (i + 1), :])
    banded(blk(qs), blk(ks), blk(vs), b1_ref[0], seq // QB, st1)

    d2 = DILATED[1][1]
    for r in range(d2):
        sub = lambda ref: (lambda i: ref[pl.ds(r + d2 * QB * i, QB, stride=d2), :])

        def st2(i, m, l, a):
            idx = pl.ds(r + d2 * QB * i, QB, stride=d2)
            m2[idx, :] = m
            l2[idx, :] = l
            a2[idx, :] = a
        banded(sub(qs), sub(ks), sub(vs), b2_ref[0], seq // (QB * d2), st2)

    d3 = DILATED[2][1]
    for r in range(d3):
        idx = pl.ds(r, QB, stride=d3)
        m, l, a = _soft_block(qs[idx, :], ks[idx, :], vs[idx, :], b3_ref[0])
        m3[idx, :] = m
        l3[idx, :] = l
        a3[idx, :] = a

    mx = jnp.maximum(jnp.maximum(m1[...], m2[...]), m3[...])
    w1 = jnp.exp(m1[...] - mx)
    w2 = jnp.exp(m2[...] - mx)
    w3 = jnp.exp(m3[...] - mx)
    num = w1 * a1[...] + w2 * a2[...] + w3 * a3[...]
    den = w1 * l1[...] + w2 * l2[...] + w3 * l3[...]
    o_ref[...] = (num / den).astype(BF16)


def _dilated_prompt(z, gq, gk, tabs, nb, seq):
    nh = 8
    zspec = lambda c0: pl.BlockSpec((seq, 128), lambda b, h: (b, c0 + h))
    gspec = pl.BlockSpec((1, 128), lambda b, h: (0, 0))
    tspec = lambda w: pl.BlockSpec((1, QB, w), lambda b, h: (h, 0, 0))
    ospec = pl.BlockSpec((seq, 128), lambda b, h: (b, h))
    big = pltpu.VMEM((seq, 128), F32)
    col = pltpu.VMEM((seq, 1), F32)
    return pl.pallas_call(
        _dil_p_kernel,
        grid=(nb, nh),
        in_specs=[zspec(32), zspec(40), zspec(48), gspec, gspec, tspec(2 * QB), tspec(2 * QB), tspec(QB)],
        out_specs=[ospec, ospec, ospec],
        out_shape=[jax.ShapeDtypeStruct((nb * seq, nh * 128), BF16),
                   jax.ShapeDtypeStruct((nb * seq, nh * 128), F32),
                   jax.ShapeDtypeStruct((nb * seq, nh * 128), F32)],
        scratch_shapes=[big, big, big, big, big, big, col, col, col, col, col, col],
        compiler_params=_cparams(("parallel", "parallel")),
        name="dilated_prompt",
    )(z, z, z, gq.reshape(1, 128), gk.reshape(1, 128), *tabs)


def _sample_tables(rel_bias, nt, buf_len):
    rb = rel_bias.astype(F32)
    nh = rb.shape[1]
    w2 = DILATED[1][0]

    def count(delta, patterns):
        c = np.zeros(delta.shape, np.int32)
        for window, dil in patterns:
            c += ((delta >= 0) & (delta <= window) & (delta % dil == 0)).astype(np.int32)
        return c

    qpos = buf_len + np.arange(8)
    real_q = (np.arange(8) < nt)
    da = qpos[None, :] - (buf_len - w2 + np.arange(w2))[:, None]
    ca = count(da, DILATED[:2]) * real_q[None, :]
    nj = buf_len // 16
    rows_b = (16 * np.arange(nj)[None, :] + np.arange(nt)[:, None]).reshape(-1)
    db = qpos[None, :] - rows_b[:, None]
    cb = count(db, DILATED[2:]) * real_q[None, :]
    dn = qpos[None, :] - qpos[:, None]
    cn = count(dn, DILATED) * real_q[None, :] * real_q[:, None]

    def tables(delta, cnt):
        bucket = _rel_bucket_np(np.clip(delta, 0, REL_MAX_DISTANCE))
        bias = rb[jnp.asarray(bucket)]
        bias = jnp.where(jnp.asarray((cnt > 0) | ~real_q[None, :])[:, :, None], bias, -jnp.inf)
        bias = jnp.transpose(bias, (0, 2, 1)).reshape(delta.shape[0], nh * 8)
        mult = np.repeat(cnt[:, None, :], nh, axis=1).reshape(delta.shape[0], nh * 8).astype(np.float32)
        return bias, jnp.asarray(mult)

    return tables(da, ca) + tables(db, cb) + tables(dn, cn)


def _dil_s_kernel(q_ref, k_ref, v_ref, gq_ref, gk_ref, ca_ref, cb_ref,
                  ba_ref, ma_ref, bb_ref, mb_ref, bn_ref, mn_ref, o_ref, ko_ref, vo_ref):
    nt = q_ref.shape[1]
    nh = 8
    gq = gq_ref[...]
    gk = gk_ref[...]
    qn = jnp.concatenate([_rms(q_ref[0, :, 128 * h:128 * (h + 1)], gq) for h in range(nh)], axis=1)
    kn = jnp.concatenate([_rms(k_ref[0, :, 128 * h:128 * (h + 1)], gk) for h in range(nh)], axis=1)
    vn = v_ref[0]
    ko_ref[0] = kn
    vo_ref[0] = vn
    pad = jnp.zeros((8 - nt, nh * 128), F32)
    q8 = jnp.concatenate([qn, pad], axis=0)
    k8 = jnp.concatenate([kn, pad], axis=0)
    v8 = jnp.concatenate([vn, pad], axis=0)
    lane_head = lax.broadcasted_iota(jnp.int32, (8, nh * 128), 1) // 128
    qbd = jnp.concatenate([jnp.where(lane_head == h, q8, 0.0) for h in range(nh)], axis=0).astype(BF16)

    def heads(ref2d, first, nrows, stride):
        return jnp.concatenate([ref2d[pl.ds(first + h, nrows, stride=stride), :] for h in range(nh)], axis=1)

    a2 = ca_ref.at[0]
    na = a2.shape[0] // (2 * nh)
    groups = [(heads(a2, 0, na, 2 * nh), heads(a2, nh, na, 2 * nh), ba_ref[...], ma_ref[...])]
    nj = cb_ref.shape[1]
    per_j = cb_ref.shape[2]
    b2 = cb_ref.reshape(nj * per_j, 128)
    for i in range(nt):
        groups.append((heads(b2, 2 * nh * i, nj, per_j), heads(b2, 2 * nh * i + nh, nj, per_j),
                       bb_ref[nj * i:nj * (i + 1), :], mb_ref[nj * i:nj * (i + 1), :]))
    groups.append((k8, v8, bn_ref[...], mn_ref[...]))

    logits = [_dot1(kk, qbd, NT) * ATT_SCALE + bias for kk, _, bias, _ in groups]
    mx = logits[0].max(axis=0, keepdims=True)
    for l in logits[1:]:
        mx = jnp.maximum(mx, l.max(axis=0, keepdims=True))
    ps = [mult * jnp.exp(l - mx) for l, (_, _, _, mult) in zip(logits, groups)]
    den = ps[0].sum(axis=0, keepdims=True)
    for p in ps[1:]:
        den = den + p.sum(axis=0, keepdims=True)
    inv = 1.0 / jnp.where(den > 0.0, den, 1.0)
    acc = None
    for p, (_, vv, _, _) in zip(ps, groups):
        part = _dot1(p * inv, vv, TN)
        acc = part if acc is None else acc + part
    o8 = jnp.concatenate([acc[8 * h:8 * (h + 1), 128 * h:128 * (h + 1)] for h in range(nh)], axis=1)
    o_ref[0] = o8[0:nt]


def _dilated_sample(zs3, gq, gk, cache, tabs):
    nb, nt, _ = zs3.shape
    buf_len = cache.shape[1]
    w2 = DILATED[1][0]
    d3 = DILATED[2][1]
    kvh = 2 * 8
    cache_a = cache.reshape(nb, buf_len * kvh, 128)
    cache_b = cache.reshape(nb, buf_len // d3, d3 * kvh, 128)
    zspec = lambda c: pl.BlockSpec((1, nt, 1024), lambda b: (b, 0, c))
    gspec = pl.BlockSpec((1, 128), lambda b: (0, 0))
    full = lambda a: pl.BlockSpec(a.shape, lambda b: (0, 0))
    ospec = pl.BlockSpec((1, nt, 1024), lambda b: (b, 0, 0))
    return pl.pallas_call(
        _dil_s_kernel,
        grid=(nb,),
        in_specs=[zspec(4), zspec(5), zspec(6), gspec, gspec,
                  pl.BlockSpec((1, w2 * kvh, 128), lambda b: (b, buf_len // w2 - 1, 0)),
                  pl.BlockSpec((1, buf_len // d3, nt * kvh, 128), lambda b: (b, 0, 0, 0))]
                 + [full(a) for a in tabs],
        out_specs=[ospec, ospec, ospec],
        out_shape=[jax.ShapeDtypeStruct((nb, nt, 1024), F32)] * 3,
        compiler_params=_cparams(("parallel",)),
        name="dilated_sample",
    )(zs3, zs3, zs3, gq.reshape(1, 128), gk.reshape(1, 128), cache_a, cache_b, *tabs)


def _route_kernel(x_ref, g_ref, psh, psc, ssh, ssc, rw_ref, rb_ref, u_ref, idx_ref, gate_ref, *, npt):
    i = pl.program_id(0)
    xr = _rms(x_ref[...], g_ref[...])
    u = xr * (1.0 + _pick(i < npt, psc, ssc)) + _pick(i < npt, psh, ssh)
    u_ref[...] = u
    l = _dot3(u, rw_ref[...]) + rb_ref[...]
    lane = lax.broadcasted_iota(jnp.int32, l.shape, 1).astype(F32)
    out_lane = lax.broadcasted_iota(jnp.int32, (TM, 128), 1)
    vals, idxs = [], []
    for _ in range(TOP_K):
        m = jnp.max(l, axis=-1, keepdims=True)
        first = jnp.min(jnp.where(l == m, lane, float(N_EXPERTS)), axis=-1, keepdims=True)
        vals.append(m)
        idxs.append(first.astype(jnp.int32))
        l = jnp.where(lane == first, -jnp.inf, l)
    es = [jnp.exp(v - vals[0]) for v in vals]
    tot = es[0] + es[1] + es[2] + es[3]
    idx_out = jnp.zeros((TM, 128), jnp.int32)
    gate_out = jnp.zeros((TM, 128), F32)
    for k in range(TOP_K):
        idx_out = jnp.where(out_lane == k, idxs[k], idx_out)
        gate_out = jnp.where(out_lane == k, es[k] / tot, gate_out)
    idx_ref[...] = idx_out
    gate_ref[...] = gate_out


def _route(x, g, pmod, smod, rw, rb, npt, tiles_per_batch):
    t = x.shape[0]
    psh, ssh = _mod_specs(npt, tiles_per_batch, 0)
    psc, ssc = _mod_specs(npt, tiles_per_batch, 1)
    return pl.pallas_call(
        functools.partial(_route_kernel, npt=npt),
        grid=(t // TM,),
        in_specs=[pl.BlockSpec((TM, D), lambda i: (i, 0)),
                  pl.BlockSpec((1, D), lambda i: (0, 0)),
                  psh, psc, ssh, ssc,
                  pl.BlockSpec((D, N_EXPERTS), lambda i: (0, 0)),
                  pl.BlockSpec((1, N_EXPERTS), lambda i: (0, 0))],
        out_specs=[pl.BlockSpec((TM, D), lambda i: (i, 0)),
                   pl.BlockSpec((TM, 128), lambda i: (i, 0)),
                   pl.BlockSpec((TM, 128), lambda i: (i, 0))],
        out_shape=[jax.ShapeDtypeStruct((t, D), F32),
                   jax.ShapeDtypeStruct((t, 128), jnp.int32),
                   jax.ShapeDtypeStruct((t, 128), F32)],
        compiler_params=_cparams(("parallel",)),
        name="moe_route",
    )(x, g.reshape(1, D), pmod, pmod, smod, smod, rw, rb.reshape(1, N_EXPERTS))


MOE_TN = 512
MOE_TF = MOE_TN // 2


MOE_NG = 2 * D_EXPERT // MOE_TN
MOE_DN = 512
MOE_ND = D // MOE_DN


def _moe_kernel(te_ref, nu_ref, tok_ref, u_hbm, wgu_ref, bgu_ref, wdn_ref, bdn_ref, y_ref, xg, xb, hs, sem):
    t = pl.program_id(0)
    j = pl.program_id(1)
    nu = nu_ref[0]
    active = t < nu
    slot = t % 2

    def issue(tile, s):
        def body(r, carry):
            tok = tok_ref[tile * TM + r]
            pltpu.make_async_copy(u_hbm.at[pl.ds(tok, 1)], xg.at[s, pl.ds(r, 1)], sem.at[s]).start()
            return carry
        lax.fori_loop(0, TM, body, 0, unroll=8)

    @pl.when(jnp.logical_and(t == 0, j == 0))
    def _():
        issue(0, 0)

    @pl.when(jnp.logical_and(active, j == 0))
    def _():
        pltpu.make_async_copy(u_hbm.at[pl.ds(0, TM)], xg.at[slot], sem.at[slot]).wait()
        xb[...] = xg[slot].astype(BF16)

    @pl.when(jnp.logical_and(t + 1 < nu, j == 1))
    def _():
        issue(t + 1, 1 - slot)

    @pl.when(jnp.logical_and(active, j < MOE_NG))
    def _():
        gu = _dot1(xb[...], wgu_ref[0, 0]) + bgu_ref[0, 0]
        up = pltpu.roll(gu, MOE_TN - 1, axis=1)
        gate = jnp.minimum(gu, SWIGLU_LIMIT)
        up = jnp.clip(up, -SWIGLU_LIMIT, SWIGLU_LIMIT)
        h = ((up + 1.0) * gate * _sigmoid(SWIGLU_ALPHA * gate)).astype(BF16)
        ri = lax.broadcasted_iota(jnp.int32, (MOE_TN, MOE_TF), 0)
        ci = lax.broadcasted_iota(jnp.int32, (MOE_TN, MOE_TF), 1)
        pick_even = jnp.where(ri == 2 * ci, 1.0, 0.0).astype(BF16)
        hs[j] = jnp.dot(h, pick_even, preferred_element_type=F32).astype(BF16)

    @pl.when(jnp.logical_and(active, j >= MOE_NG))
    def _():
        acc = jnp.dot(hs[0], wdn_ref[0, 0, 0:MOE_TF, :].astype(BF16), preferred_element_type=F32)
        for c in range(1, MOE_NG):
            acc = acc + jnp.dot(hs[c], wdn_ref[0, 0, MOE_TF * c:MOE_TF * (c + 1), :].astype(BF16),
                                preferred_element_type=F32)
        y_ref[...] = acc + bdn_ref[0, 0]

    @pl.when(jnp.logical_and(jnp.logical_not(active), j >= MOE_NG))
    def _():
        y_ref[...] = jnp.zeros_like(y_ref)


def _moe_experts(u, tile_e, n_used, slot_tok, layer, w_gu, b_gu, w_dn, b_dn):
    n_tiles = tile_e.shape[0]
    nj = MOE_NG + MOE_ND
    jj = lambda t, j, nu: jnp.where(t < nu[0], j, nj - 1)
    jg = lambda t, j, nu: jnp.minimum(jj(t, j, nu), MOE_NG - 1)
    jd = lambda t, j, nu: jnp.maximum(jj(t, j, nu) - MOE_NG, 0)
    grid_spec = pltpu.PrefetchScalarGridSpec(
        num_scalar_prefetch=3,
        grid=(n_tiles, nj),
        in_specs=[pl.BlockSpec(memory_space=pl.ANY),
                  pl.BlockSpec((1, 1, D, MOE_TN), lambda t, j, te, nu, tok: (layer, te[t], 0, jg(t, j, nu))),
                  pl.BlockSpec((1, 1, 1, MOE_TN), lambda t, j, te, nu, tok: (layer, te[t], 0, jg(t, j, nu))),
                  pl.BlockSpec((1, 1, D_EXPERT, MOE_DN), lambda t, j, te, nu, tok: (layer, te[t], 0, jd(t, j, nu))),
                  pl.BlockSpec((1, 1, 1, MOE_DN), lambda t, j, te, nu, tok: (layer, te[t], 0, jd(t, j, nu)))],
        out_specs=pl.BlockSpec((TM, MOE_DN), lambda t, j, te, nu, tok: (t, jnp.maximum(j - MOE_NG, 0))),
        scratch_shapes=[pltpu.VMEM((2, TM, D), F32), pltpu.VMEM((TM, D), BF16),
                        pltpu.VMEM((MOE_NG, TM, MOE_TF), BF16), pltpu.SemaphoreType.DMA((2,))],
    )
    nl, ne = b_gu.shape[0], b_gu.shape[1]
    return pl.pallas_call(
        _moe_kernel,
        grid_spec=grid_spec,
        out_shape=jax.ShapeDtypeStruct((n_tiles * TM, D), F32),
        compiler_params=_cparams(("arbitrary", "arbitrary")),
        name="moe_experts",
    )(tile_e, n_used, slot_tok, u, w_gu, b_gu.reshape(nl, ne, 1, 2 * D_EXPERT), w_dn, b_dn.reshape(nl, ne, 1, D))


CMB = 256


def _combine_kernel(slot_ref, h_ref, gate_ref, pg, sg, y_hbm, o_ref, buf, sem, *, npt_c):
    i = pl.program_id(0)

    def issue(r, carry):
        for k in range(TOP_K):
            s = slot_ref[(i * CMB + r) * TOP_K + k]
            pltpu.make_async_copy(y_hbm.at[pl.ds(s, 1)], buf.at[k, pl.ds(r, 1)], sem).start()
        return carry
    lax.fori_loop(0, CMB, issue, 0)
    for k in range(TOP_K):
        pltpu.make_async_copy(y_hbm.at[pl.ds(0, CMB)], buf.at[k], sem).wait()
    gates = gate_ref[...]
    mo = gates[:, 0:1] * buf[0]
    for k in range(1, TOP_K):
        mo = mo + gates[:, k:k + 1] * buf[k]
    o_ref[...] = h_ref[...] + _pick(i < npt_c, pg, sg) * mo


def _moe_combine(h, y, slot4, gate128, pmod, smod, npt_c, tiles_per_batch_c):
    t = h.shape[0]
    nsamp_tiles = (t // CMB) - npt_c
    grid_spec = pltpu.PrefetchScalarGridSpec(
        num_scalar_prefetch=1,
        grid=(t // CMB,),
        in_specs=[pl.BlockSpec((CMB, D), lambda i, s: (i, 0)),
                  pl.BlockSpec((CMB, 128), lambda i, s: (i, 0)),
                  pl.BlockSpec((1, 1, D), lambda i, s: (jnp.minimum(i // tiles_per_batch_c, npt_c // tiles_per_batch_c - 1), 0, 2)),
                  pl.BlockSpec((CMB, D), lambda i, s: (jnp.clip(i - npt_c, 0, nsamp_tiles - 1), 2)),
                  pl.BlockSpec(memory_space=pl.ANY)],
        out_specs=pl.BlockSpec((CMB, D), lambda i, s: (i, 0)),
        scratch_shapes=[pltpu.VMEM((TOP_K, CMB, D), F32), pltpu.SemaphoreType.DMA],
    )
    return pl.pallas_call(
        functools.partial(_combine_kernel, npt_c=npt_c),
        grid_spec=grid_spec,
        out_shape=jax.ShapeDtypeStruct((t, D), F32),
        compiler_params=_cparams(("arbitrary",)),
        name="moe_combine",
    )(slot4, h, gate128, pmod, smod, y)


def _moe_layer(h, g, pmod, smod, layer, rw, rb, w_gu, b_gu, w_dn, b_dn, npt, tiles_per_batch):
    t = h.shape[0]
    u, idx128, gate128 = _route(h, g, pmod, smod, rw[layer], rb[layer], npt, tiles_per_batch)
    idx4 = idx128[:, :TOP_K]
    onehot = idx4[:, :, None] == jnp.arange(N_EXPERTS, dtype=jnp.int32)[None, None, :]
    sel = jnp.any(onehot, axis=1).astype(jnp.int32)
    counts = jnp.sum(sel, axis=0)
    rank = jnp.cumsum(sel, axis=0) - 1
    padded = (counts + TM - 1) // TM * TM
    pad_end = jnp.cumsum(padded)
    pad_start = pad_end - padded
    n_tiles = t * TOP_K // TM + N_EXPERTS
    dest4 = jnp.sum(jnp.where(onehot, (pad_start[None, :] + rank)[:, None, :], 0), axis=2).astype(jnp.int32)
    tok = jnp.broadcast_to(jnp.arange(t, dtype=jnp.int32)[:, None], (t, TOP_K))
    slot_tok = jnp.zeros((n_tiles * TM,), jnp.int32).at[dest4.reshape(-1)].set(tok.reshape(-1))
    n_used = (pad_end[-1] // TM).astype(jnp.int32)
    tile_ids = jnp.arange(n_tiles, dtype=jnp.int32)
    tile_e = jnp.minimum(jnp.searchsorted(pad_end, jnp.minimum(tile_ids, n_used - 1) * TM, side='right'),
                         N_EXPERTS - 1).astype(jnp.int32)
    y = _moe_experts(u, tile_e, n_used.reshape(1), slot_tok, layer, w_gu, b_gu, w_dn, b_dn)
    return _moe_combine(h, y, dest4.reshape(-1), gate128, pmod, smod,
                        npt * (TM // CMB), tiles_per_batch * (TM // CMB))


C_HEADS = 16
C_NOPE = 128
C_ROPE = 64
C_LORA = 512
ROPE_THETA = 10000.0
MLA_SCALE = (C_NOPE + C_ROPE) ** -0.5
HP = 128


def _rope_tables(pos):
    half = C_ROPE // 2
    inv = ROPE_THETA ** (-jnp.arange(half, dtype=F32) / half)
    ang = pos.astype(F32)[:, None] * inv[None, :]
    z = jnp.zeros_like(ang)
    cos = jnp.concatenate([jnp.cos(ang), jnp.cos(ang), z, z], axis=1)
    sin = jnp.concatenate([-jnp.sin(ang), jnp.sin(ang), z, z], axis=1)
    return cos, sin


def _rope128(x, cos, sin):
    n = x.shape[1]
    lane = lax.broadcasted_iota(jnp.int32, x.shape, 1) % HP
    partner = jnp.where(lane < C_ROPE // 2, pltpu.roll(x, n - C_ROPE // 2, axis=1), pltpu.roll(x, C_ROPE // 2, axis=1))
    return x * cos + partner * sin


def _mla_prep_kernel(z_ref, gq_ref, gkv_ref, gkr_ref, gqn_ref, gqr_ref, gkn_ref, wn_ref, wr_ref, wuk_ref,
                     cos_ref, sin_ref, lat_ref, kr_ref, krb_ref, qn_ref, qr_ref, qabs_ref, qg_ref, *, npt):
    i = pl.program_id(0)
    z = z_ref[...]
    cos = cos_ref[...]
    sin = sin_ref[...]
    lat_ref[...] = _rms(z[:, C_LORA:2 * C_LORA], gkv_ref[...])
    k128 = z[:, 2 * C_LORA:2 * C_LORA + HP]
    k128 = k128 * lax.rsqrt(jnp.sum(k128 * k128, axis=-1, keepdims=True) / C_ROPE + EPS) * gkr_ref[...]
    k128 = _rope128(k128, cos, sin)
    kr_ref[...] = k128[:, 0:C_ROPE]
    krb_ref[...] = k128.astype(BF16)
    cqn = _rms(z[:, 0:C_LORA], gq_ref[...]).astype(BF16)
    qn_all = jnp.dot(cqn, wn_ref[...], preferred_element_type=F32)
    qr_all = jnp.dot(cqn, wr_ref[...], preferred_element_type=F32)
    for h in range(C_HEADS):
        sl = slice(HP * h, HP * (h + 1))
        qn = _rms(qn_all[:, sl], gqn_ref[...]) * MLA_SCALE
        qn_ref[:, sl] = qn.astype(BF16)
        qg_ref[:, sl] = qn * gkn_ref[...]
        x = qr_all[:, sl]
        x = x * lax.rsqrt(jnp.sum(x * x, axis=-1, keepdims=True) / C_ROPE + EPS) * gqr_ref[...]
        qr_ref[:, sl] = (_rope128(x, cos, sin) * MLA_SCALE).astype(BF16)

    @pl.when(i == npt)
    def _():
        for h in range(C_HEADS):
            sl = slice(HP * h, HP * (h + 1))
            qabs_ref[:, C_LORA * h:C_LORA * (h + 1)] = _dot1(qg_ref[:, sl], wuk_ref[:, sl], NT).astype(BF16)


def _mla_prep(z1, gq, gkv, gkr, gqn, gqr, gkn, wn, wr, wuk, cos, sin, npt):
    t = z1.shape[0]
    zw = z1.shape[1]
    row = lambda w: pl.BlockSpec((TM, w), lambda i: (i, 0))
    full = lambda a: pl.BlockSpec(a.shape, lambda i: (0,) * a.ndim)
    pad64 = lambda g: jnp.concatenate([g, jnp.zeros((HP - C_ROPE,), F32)]).reshape(1, HP)
    args = (z1, gq.reshape(1, -1), gkv.reshape(1, -1), pad64(gkr), gqn.reshape(1, -1), pad64(gqr), gkn.reshape(1, -1),
            wn, wr, wuk, cos, sin)
    return pl.pallas_call(
        functools.partial(_mla_prep_kernel, npt=npt),
        grid=(t // TM,),
        in_specs=[row(zw)] + [full(a) for a in args[1:10]] + [row(HP), row(HP)],
        out_specs=[row(C_LORA), row(C_ROPE), row(HP), row(C_HEADS * HP), row(C_HEADS * HP),
                   pl.BlockSpec((TM, C_HEADS * C_LORA), lambda i: (0, 0))],
        out_shape=[jax.ShapeDtypeStruct((t, C_LORA), F32), jax.ShapeDtypeStruct((t, C_ROPE), F32),
                   jax.ShapeDtypeStruct((t, HP), BF16), jax.ShapeDtypeStruct((t, C_HEADS * HP), BF16),
                   jax.ShapeDtypeStruct((t, C_HEADS * HP), BF16), jax.ShapeDtypeStruct((TM, C_HEADS * C_LORA), BF16)],
        scratch_shapes=[pltpu.VMEM((TM, C_HEADS * HP), F32)],
        compiler_params=_cparams(("arbitrary",)),
        name="mla_prep",
    )(*args)


def _mla_kv_kernel(lat_ref, wuk_ref, wuv_ref, gkn_ref, kn_ref, v_ref):
    lat = lat_ref[...].astype(BF16)
    kn = jnp.dot(lat, wuk_ref[...], preferred_element_type=F32)
    for h in range(C_HEADS):
        sl = slice(HP * h, HP * (h + 1))
        kn_ref[:, sl] = _rms(kn[:, sl], gkn_ref[...]).astype(BF16)
    v_ref[...] = jnp.dot(lat, wuv_ref[...], preferred_element_type=F32).astype(BF16)


def _mla_kv(lat, wuk, wuv, gkn, rows):
    wide = C_HEADS * HP
    return pl.pallas_call(
        _mla_kv_kernel,
        grid=(rows // TM,),
        in_specs=[pl.BlockSpec((TM, C_LORA), lambda i: (i, 0)),
                  pl.BlockSpec((C_LORA, wide), lambda i: (0, 0)),
                  pl.BlockSpec((C_LORA, wide), lambda i: (0, 0)),
                  pl.BlockSpec((1, HP), lambda i: (0, 0))],
        out_specs=[pl.BlockSpec((TM, wide), lambda i: (i, 0))] * 2,
        out_shape=[jax.ShapeDtypeStruct((rows, wide), BF16)] * 2,
        compiler_params=_cparams(("parallel",)),
        name="mla_kv",
    )(lat, wuk, wuv, gkn.reshape(1, HP))


FQ = 512


def _mla_flash_kernel(qn_ref, qr_ref, kn_ref, kr_ref, v_ref, o_ref):
    qi = pl.program_id(2)
    qn = qn_ref[...]
    qr = qr_ref[...]
    row = lax.broadcasted_iota(jnp.int32, (FQ, FQ), 0)
    col = lax.broadcasted_iota(jnp.int32, (FQ, FQ), 1)

    def body(kb, carry):
        m, l, acc = carry
        ks = pl.ds(pl.multiple_of(kb * FQ, FQ), FQ)
        s = (lax.dot_general(qn, kn_ref[ks, :], (NT, ((), ())), preferred_element_type=F32)
             + lax.dot_general(qr, kr_ref[ks, :], (NT, ((), ())), preferred_element_type=F32))
        s = jnp.where(jnp.logical_or(kb < qi, col <= row), s, -jnp.inf)
        m_new = jnp.maximum(m, jnp.max(s, axis=-1, keepdims=True))
        alpha = jnp.exp(m - m_new)
        p = jnp.exp(s - m_new)
        l = alpha * l + jnp.sum(p, axis=-1, keepdims=True)
        acc = alpha * acc + jnp.dot(p.astype(BF16), v_ref[ks, :], preferred_element_type=F32)
        return m_new, l, acc

    m0 = jnp.full((FQ, 1), -jnp.inf, F32)
    l0 = jnp.zeros((FQ, 1), F32)
    a0 = jnp.zeros((FQ, HP), F32)
    m, l, acc = lax.fori_loop(0, qi + 1, body, (m0, l0, a0))
    o_ref[...] = (acc / l).astype(BF16)


def _mla_flash(qn, qr, kn, krb, v, nb, seq):
    nq = seq // FQ
    qspec = pl.BlockSpec((FQ, HP), lambda b, h, q: (b * nq + q, h))
    kspec = pl.BlockSpec((seq, HP), lambda b, h, q: (b, h))
    return pl.pallas_call(
        _mla_flash_kernel,
        grid=(nb, C_HEADS, nq),
        in_specs=[qspec, qspec, kspec, pl.BlockSpec((seq, HP), lambda b, h, q: (b, 0)), kspec],
        out_specs=qspec,
        out_shape=jax.ShapeDtypeStruct((nb * seq, C_HEADS * HP), BF16),
        compiler_params=_cparams(("parallel", "parallel", "parallel")),
        name="mla_flash",
    )(qn, qr, kn, krb, v)


PPS = 16
PPG = 4


def _mla_s_kernel(*refs, n_steps):
    pt_ref = refs[0]
    lat_refs = refs[1:1 + PPS]
    kr_refs = refs[1 + PPS:1 + 2 * PPS]
    wukt_ref, qabs_ref, qr_ref, latn_ref, krn_ref, o_ref, wst, latb, lg, m_ref, l_ref, acc_ref = refs[1 + 2 * PPS:]
    b = pl.program_id(0)
    kt = pl.program_id(1)
    nq = qabs_ref.shape[1]
    nheads = C_HEADS
    ntok = nq // nheads

    @pl.when(jnp.logical_and(b == 0, kt == 0))
    def _():
        wst[0:nheads * HP, :] = wukt_ref[...]

    @pl.when(kt == 0)
    def _():
        wst[nheads * HP:nheads * HP + nq, :] = qabs_ref[0]
        m_ref[...] = jnp.full_like(m_ref, -jnp.inf)
        l_ref[...] = jnp.zeros_like(l_ref)
        acc_ref[...] = jnp.zeros_like(acc_ref)

    qr = qr_ref[0]

    def up_project(lat_bf16):
        return lax.dot_general(wst[...], lat_bf16, (NT, ((), ())), preferred_element_type=F32)

    def key_logits(r, krt_f32):
        nkeys = r.shape[1]
        kraw = r[0:nheads * HP].reshape(nheads, HP, nkeys)
        rs = lax.rsqrt(jnp.sum(kraw * kraw, axis=1) / HP + EPS)
        lraw = r[nheads * HP:nheads * HP + nq].reshape(ntok, nheads, nkeys) * rs[None]
        return lraw.reshape(nq, nkeys) + _dot1(qr, krt_f32)

    def update(state, logits, lat_bf16):
        m_old, l_old, acc_old = state
        m_new = jnp.maximum(m_old, jnp.max(logits, axis=-1, keepdims=True))
        alpha = jnp.exp(m_old - m_new)
        p = jnp.exp(logits - m_new)
        l_new = alpha * l_old + jnp.sum(p, axis=-1, keepdims=True)
        acc_new = alpha * acc_old + jnp.dot(p.astype(BF16), lat_bf16, preferred_element_type=F32)
        return m_new, l_new, acc_new

    def load_group(g):
        pages = range(PPG * g, PPG * (g + 1))
        return jnp.concatenate([lat_refs[p][0, 0] for p in pages], axis=0).astype(BF16)

    gk = PPG * PAGE
    for g in range(PPS // PPG):
        lat_g = load_group(g)
        krt_g = jnp.concatenate([kr_refs[p][0, 0] for p in range(PPG * g, PPG * (g + 1))], axis=1)
        latb[gk * g:gk * (g + 1), :] = lat_g
        lg[:, gk * g:gk * (g + 1)] = key_logits(up_project(lat_g), krt_g)
    state = update((m_ref[...], l_ref[...], acc_ref[...]), lg[...], latb[...])
    m_ref[...], l_ref[...], acc_ref[...] = state

    @pl.when(kt == n_steps - 1)
    def _():
        latn = jnp.concatenate([latn_ref[0], jnp.zeros((PAGE - 8, C_LORA), F32)], axis=0).astype(BF16)
        logits = key_logits(up_project(latn), krn_ref[0])
        qtok = lax.broadcasted_iota(jnp.int32, (nq, PAGE), 0) // nheads
        slot = lax.broadcasted_iota(jnp.int32, (nq, PAGE), 1)
        _, l_fin, acc_fin = update(state, jnp.where(slot <= qtok, logits, -jnp.inf), latn)
        o_ref[0] = acc_fin / l_fin


def _mla_sample(page_table, lat_cache, kr_cache, ci, wukt, qabs, qr, lat_new8, kr_new8):
    nb, npages = page_table.shape
    n_steps = npages // PPS
    nq = qabs.shape[1]

    def page_spec(k, rows, width):
        return pl.BlockSpec((1, 1, rows, width), lambda b, t, pt: (ci, pt[b * npages + t * PPS + k], 0, 0))

    kr_cache_t = jnp.swapaxes(kr_cache, 2, 3)
    kr_new_t = jnp.pad(jnp.swapaxes(kr_new8, 1, 2), ((0, 0), (0, 0), (0, PAGE - kr_new8.shape[1])))
    grid_spec = pltpu.PrefetchScalarGridSpec(
        num_scalar_prefetch=1,
        grid=(nb, n_steps),
        in_specs=[page_spec(k, PAGE, C_LORA) for k in range(PPS)] + [page_spec(k, C_ROPE, PAGE) for k in range(PPS)]
                 + [pl.BlockSpec(wukt.shape, lambda b, t, pt: (0, 0)),
                    pl.BlockSpec((1, nq, C_LORA), lambda b, t, pt: (b, 0, 0)),
                    pl.BlockSpec((1, nq, C_ROPE), lambda b, t, pt: (b, 0, 0)),
                    pl.BlockSpec((1, 8, C_LORA), lambda b, t, pt: (b, 0, 0)),
                    pl.BlockSpec((1, C_ROPE, PAGE), lambda b, t, pt: (b, 0, 0))],
        out_specs=pl.BlockSpec((1, nq, C_LORA), lambda b, t, pt: (b, 0, 0)),
        scratch_shapes=[pltpu.VMEM((C_HEADS * HP + nq, C_LORA), BF16),
                        pltpu.VMEM((PPS * PAGE, C_LORA), BF16),
                        pltpu.VMEM((nq, PPS * PAGE), F32),
                        pltpu.VMEM((nq, 1), F32), pltpu.VMEM((nq, 1), F32), pltpu.VMEM((nq, C_LORA), F32)],
    )
    return pl.pallas_call(
        functools.partial(_mla_s_kernel, n_steps=n_steps),
        grid_spec=grid_spec,
        out_shape=jax.ShapeDtypeStruct((nb, nq, C_LORA), F32),
        compiler_params=_cparams(("arbitrary", "arbitrary")),
        name="mla_sample",
    )(page_table.reshape(-1), *([lat_cache] * PPS), *([kr_cache_t] * PPS), wukt, qabs, qr, lat_new8, kr_new_t)


def _mla_uv_kernel(c_ref, w_ref, o_ref):
    o_ref[...] = _dot1(c_ref[...], w_ref[0])


def _mla_uv(ctx, wuv3):
    rows = ctx.shape[0]
    return pl.pallas_call(
        _mla_uv_kernel,
        grid=(C_HEADS,),
        in_specs=[pl.BlockSpec((rows, C_LORA), lambda h: (0, h)),
                  pl.BlockSpec((1, C_LORA, HP), lambda h: (h, 0, 0))],
        out_specs=pl.BlockSpec((rows, HP), lambda h: (0, h)),
        out_shape=jax.ShapeDtypeStruct((rows, C_HEADS * HP), F32),
        compiler_params=_cparams(("parallel",)),
        name="mla_uv",
    )(ctx, wuv3)


def kernel(x_prompt, x_sample, state_hgrn, cache_swa_kv, cache_mla_latent, cache_mla_krope, page_table, c_prompt, c_sample, ada_w, ada_b, norm_g, rel_bias, ab_w_in, ab_w_out, hgrn_lb_logits, hgrn_norm_g, swa_q_g, swa_k_g, mla_w_in, mla_q_a_g, mla_kv_a_g, mla_w_uq, mla_w_uk, mla_w_uv, mla_qn_g, mla_qr_g, mla_kn_g, mla_kr_g, mla_w_o, moe_router_w, moe_router_b, moe_w_gu, moe_b_gu, moe_w_dn, moe_b_dn):
    nb, seq, _ = x_prompt.shape
    nsb, nst, _ = x_sample.shape
    tp, ts = nb * seq, nsb * nst
    assert ts == TM and seq % TM == 0 and nst <= 4
    npt, tpb = tp // TM, seq // TM
    x = jnp.concatenate([x_prompt.reshape(tp, D), x_sample.reshape(ts, D)], axis=0).astype(F32)

    nc = nb + nsb
    c_all = jnp.concatenate([c_prompt, c_sample, jnp.zeros((-nc % 8, D), c_prompt.dtype)], axis=0).astype(F32)
    mods = _adaln(c_all, ada_w, ada_b)

    def mod(s):
        return mods[s, 0:nb].reshape(nb, 1, 3 * D), jnp.repeat(mods[s, nb:nc], nst, axis=0)

    pm, sm = mod(0)
    z = _normmod_mm(x, norm_g[0, 0], pm, sm, ab_w_in[0].astype(BF16), 1024, npt, tpb)
    zs3 = z[tp:].reshape(nsb, nst, z.shape[1])
    oa_p, hg_p = _hgrn_prompt(z, hgrn_lb_logits, hgrn_norm_g[0], nb, seq, 0)
    oa_s, hg_s = _hgrn_sample(zs3, hgrn_lb_logits, hgrn_norm_g[0], state_hgrn, 0)
    ob_p, k_p, v_p = _dilated_prompt(z, swa_q_g[0], swa_k_g[0], _band_tables(rel_bias), nb, seq)
    buf_len = cache_swa_kv.shape[2]
    ob_s, k_s, v_s = _dilated_sample(zs3, swa_q_g[0], swa_k_g[0], cache_swa_kv[0], _sample_tables(rel_bias, nst, buf_len))
    w_out = ab_w_out[0].astype(BF16)
    h = _proj_residual([oa_p, ob_p], [oa_s.reshape(ts, 1024), ob_s.reshape(ts, 1024)],
                       [w_out[:1024], w_out[1024:]], x, pm, sm, npt, tpb)
    pm, sm = mod(1)
    h = _moe_layer(h, norm_g[0, 1], pm, sm, 0, moe_router_w, moe_router_b, moe_w_gu, moe_b_gu, moe_w_dn, moe_b_dn, npt, tpb)

    pm, sm = mod(2)
    w_in1 = jnp.pad(mla_w_in[0], ((0, 0), (0, HP - C_ROPE))).astype(BF16)
    z1 = _normmod_mm(h, norm_g[1, 0], pm, sm, w_in1, w_in1.shape[1], npt, tpb)
    past = page_table.shape[1] * PAGE
    pos = jnp.concatenate([jnp.tile(jnp.arange(seq), nb), jnp.tile(past + jnp.arange(nst), nsb)])
    cos, sin = _rope_tables(pos)
    wq = mla_w_uq[0]
    wn = wq[:, :, :C_NOPE].reshape(C_LORA, C_HEADS * HP).astype(BF16)
    wr = jnp.pad(wq[:, :, C_NOPE:], ((0, 0), (0, 0), (0, HP - C_ROPE))).reshape(C_LORA, C_HEADS * HP).astype(BF16)
    wuk = mla_w_uk[0].reshape(C_LORA, C_HEADS * HP).astype(BF16)
    wuv = mla_w_uv[0].reshape(C_LORA, C_HEADS * HP).astype(BF16)
    lat, kr, krb, qn, qr, qabs = _mla_prep(z1, mla_q_a_g[0], mla_kv_a_g[0], mla_kr_g[0], mla_qn_g[0], mla_qr_g[0],
                                           mla_kn_g[0], wn, wr, wuk, cos, sin, npt)
    kn, v = _mla_kv(lat, wuk, wuv, mla_kn_g[0], tp)
    ctx_p = _mla_flash(qn, qr, kn, krb, v, nb, seq)
    qabs3 = qabs.reshape(nsb, nst * C_HEADS, C_LORA)
    qr_s = qr[tp:].reshape(ts, C_HEADS, HP)[:, :, :C_ROPE].reshape(nsb, nst * C_HEADS, C_ROPE)
    lat_new8 = jnp.pad(lat[tp:].reshape(nsb, nst, C_LORA), ((0, 0), (0, 8 - nst), (0, 0)))
    kr_new8 = jnp.pad(kr[tp:].reshape(nsb, nst, C_ROPE), ((0, 0), (0, 8 - nst), (0, 0)))
    ctx_s = _mla_sample(page_table, cache_mla_latent, cache_mla_krope, 0, wuk.T, qabs3, qr_s, lat_new8, kr_new8)
    wuv3 = jnp.transpose(mla_w_uv[0], (1, 0, 2)).astype(BF16)
    o_s = _mla_uv(ctx_s.reshape(ts, C_HEADS * C_LORA), wuv3)
    h = _proj_residual([ctx_p], [o_s], [mla_w_o[0].astype(BF16)], h, pm, sm, npt, tpb)
    pm, sm = mod(3)
    h = _moe_layer(h, norm_g[1, 1], pm, sm, 1, moe_router_w, moe_router_b, moe_w_gu, moe_b_gu, moe_w_dn, moe_b_dn, npt, tpb)

    y_prompt = h[:tp].reshape(nb, seq, D).astype(x_prompt.dtype)
    y_sample = h[tp:].reshape(nsb, nst, D).astype(x_sample.dtype)
    hgrn_prompt = hg_p[None].astype(state_hgrn.dtype)
    hgrn_sample = hg_s.astype(state_hgrn.dtype)
    kv = lambda k, v, b, t: jnp.stack([k.reshape(b, t, 8, 128), v.reshape(b, t, 8, 128)], axis=2)[None]
    swa_prompt = kv(k_p, v_p, nb, seq).astype(cache_swa_kv.dtype)
    swa_sample = kv(k_s, v_s, nsb, nst).astype(cache_swa_kv.dtype)
    lat_prompt = lat[:tp].reshape(1, nb, seq, C_LORA).astype(cache_mla_latent.dtype)
    lat_sample = lat[tp:].reshape(1, nsb, nst, C_LORA).astype(cache_mla_latent.dtype)
    krope_prompt = kr[:tp].reshape(1, nb, seq, C_ROPE).astype(cache_mla_krope.dtype)
    krope_sample = kr[tp:].reshape(1, nsb, nst, C_ROPE).astype(cache_mla_krope.dtype)
    return (y_prompt, y_sample, hgrn_prompt, hgrn_sample, swa_prompt, swa_sample,
            lat_prompt, lat_sample, krope_prompt, krope_sample)
```

```python
import functools
import math

import numpy as np
import jax
import jax.numpy as jnp
from jax import lax
from jax.experimental import pallas as pl
from jax.experimental.pallas import tpu as pltpu

F32 = jnp.float32
BF16 = jnp.bfloat16
EPS = 1e-6

D = 2048
TM = 512
N_EXPERTS = 32
TOP_K = 4
D_EXPERT = 2048
SWIGLU_LIMIT = 7.0
SWIGLU_ALPHA = 1.702
PAGE = 128
V7X_VMEM_LIMIT = 56 * 1024 * 1024

NN = ((1,), (0,))
NT = ((1,), (1,))
TN = ((0,), (0,))


def _cparams(sem, vmem=V7X_VMEM_LIMIT):
    return pltpu.CompilerParams(dimension_semantics=sem, vmem_limit_bytes=vmem)


def _dot1(a, b, dims=NN):
    return lax.dot_general(a.astype(BF16), b.astype(BF16), (dims, ((), ())), preferred_element_type=F32)


def _split2(x):
    hi = x.astype(BF16)
    lo = (x - hi.astype(F32)).astype(BF16)
    return hi, lo


def _split3(x):
    hi = x.astype(BF16)
    r = x - hi.astype(F32)
    mid = r.astype(BF16)
    lo = (r - mid.astype(F32)).astype(BF16)
    return hi, mid, lo


def _dot_exact_rhs(a01, x):
    hi, mid, lo = _split3(x)
    f = lambda p: lax.dot_general(a01, p, (NN, ((), ())), preferred_element_type=F32)
    return f(hi) + f(mid) + f(lo)


def _dot3(a, b, dims=NN):
    ah, al = _split2(a)
    bh, bl = _split2(b)
    f = lambda p, q: lax.dot_general(p, q, (dims, ((), ())), preferred_element_type=F32)
    return f(ah, bh) + f(ah, bl) + f(al, bh)


def _sigmoid(x):
    return 1.0 / (1.0 + jnp.exp(-x))


def _rms(x, g):
    return x * lax.rsqrt(jnp.mean(x * x, axis=-1, keepdims=True) + EPS) * g


def _ada_kernel(c_ref, w_ref, b_ref, o_ref):
    c = c_ref[...]
    s = c * _sigmoid(c)
    o_ref[0] = _dot1(s, w_ref[0, 0]) + b_ref[0, 0]


def _adaln(c_all, ada_w, ada_b):
    r = c_all.shape[0]
    tn = 1024
    depth = ada_w.shape[0]
    b4 = ada_b.reshape(depth, 2, 1, 3 * D)
    return pl.pallas_call(
        _ada_kernel,
        grid=(depth * 2, 3 * D // tn),
        in_specs=[pl.BlockSpec((r, D), lambda s, j: (0, 0)),
                  pl.BlockSpec((1, 1, D, tn), lambda s, j: (s // 2, s % 2, 0, j)),
                  pl.BlockSpec((1, 1, 1, tn), lambda s, j: (s // 2, s % 2, 0, j))],
        out_specs=pl.BlockSpec((1, r, tn), lambda s, j: (s, 0, j)),
        out_shape=jax.ShapeDtypeStruct((depth * 2, r, 3 * D), F32),
        compiler_params=_cparams(("parallel", "parallel")),
        name="adaln",
    )(c_all, ada_w, b4)


def _pick(is_prompt, p_ref, s_ref):
    return jnp.where(is_prompt, p_ref[0], s_ref[...])


def _mod_specs(npt, rows_per_batch_tiles, col, tn=D, with_j=False):
    if with_j:
        pm = pl.BlockSpec((1, 1, tn), lambda i, j: (jnp.minimum(i // rows_per_batch_tiles, npt // rows_per_batch_tiles - 1), 0, col * (D // tn) + j))
        sm = pl.BlockSpec((TM, tn), lambda i, j: (0, col * (D // tn) + j))
    else:
        pm = pl.BlockSpec((1, 1, tn), lambda i, *_: (jnp.minimum(i // rows_per_batch_tiles, npt // rows_per_batch_tiles - 1), 0, col))
        sm = pl.BlockSpec((TM, tn), lambda i, *_: (0, col))
    return pm, sm


def _nm_mm_kernel(x_ref, g_ref, psh, psc, ssh, ssc, w_ref, o_ref, xn_ref, *, npt):
    i = pl.program_id(0)
    j = pl.program_id(1)

    @pl.when(j == 0)
    def _():
        xr = _rms(x_ref[...], g_ref[...])
        sc = _pick(i < npt, psc, ssc)
        sh = _pick(i < npt, psh, ssh)
        xn_ref[...] = (xr * (1.0 + sc) + sh).astype(BF16)

    o_ref[...] = jnp.dot(xn_ref[...], w_ref[...], preferred_element_type=F32)


def _normmod_mm(x, g, pmod, smod, w_bf16, tn, npt, tiles_per_batch):
    t = x.shape[0]
    n = w_bf16.shape[1]
    psh, ssh = _mod_specs(npt, tiles_per_batch, 0)
    psc, ssc = _mod_specs(npt, tiles_per_batch, 1)
    return pl.pallas_call(
        functools.partial(_nm_mm_kernel, npt=npt),
        grid=(t // TM, n // tn),
        in_specs=[pl.BlockSpec((TM, D), lambda i, j: (i, 0)),
                  pl.BlockSpec((1, D), lambda i, j: (0, 0)),
                  psh, psc, ssh, ssc,
                  pl.BlockSpec((D, tn), lambda i, j: (0, j))],
        out_specs=pl.BlockSpec((TM, tn), lambda i, j: (i, j)),
        out_shape=jax.ShapeDtypeStruct((t, n), F32),
        scratch_shapes=[pltpu.VMEM((TM, D), BF16)],
        compiler_params=_cparams(("parallel", "arbitrary")),
        name="normmod_mm",
    )(x, g.reshape(1, D), pmod, pmod, smod, smod, w_bf16)


def _proj_res_kernel(*refs, npt, nk):
    xp = refs[0:nk]
    xs = refs[nk:2 * nk]
    ws = refs[2 * nk:3 * nk]
    h_ref, pg, sg, o_ref = refs[3 * nk:3 * nk + 4]
    xb = refs[3 * nk + 4:]
    i = pl.program_id(0)
    j = pl.program_id(1)

    @pl.when(jnp.logical_and(j == 0, i < npt))
    def _():
        for k in range(nk):
            xb[k][...] = xp[k][...].astype(BF16)

    @pl.when(jnp.logical_and(j == 0, i >= npt))
    def _():
        for k in range(nk):
            xb[k][...] = xs[k][...].astype(BF16)

    acc = jnp.dot(xb[0][...], ws[0][...], preferred_element_type=F32)
    for k in range(1, nk):
        acc = acc + jnp.dot(xb[k][...], ws[k][...], preferred_element_type=F32)
    gate = _pick(i < npt, pg, sg)
    o_ref[...] = h_ref[...] + gate * acc


def _proj_residual(xps, xss, ws, h, pmod, smod, npt, tiles_per_batch, tn=1024):
    nk = len(ws)
    t = h.shape[0]
    ks = [w.shape[0] for w in ws]
    pg, sg = _mod_specs(npt, tiles_per_batch, 2, tn=tn, with_j=True)
    in_specs = ([pl.BlockSpec((TM, k), lambda i, j: (jnp.minimum(i, npt - 1), 0)) for k in ks]
                + [pl.BlockSpec((TM, k), lambda i, j: (0, 0)) for k in ks]
                + [pl.BlockSpec((k, tn), lambda i, j: (0, j)) for k in ks]
                + [pl.BlockSpec((TM, tn), lambda i, j: (i, j)), pg, sg])
    return pl.pallas_call(
        functools.partial(_proj_res_kernel, npt=npt, nk=nk),
        grid=(t // TM, D // tn),
        in_specs=in_specs,
        out_specs=pl.BlockSpec((TM, tn), lambda i, j: (i, j)),
        out_shape=jax.ShapeDtypeStruct((t, D), F32),
        scratch_shapes=[pltpu.VMEM((TM, k), BF16) for k in ks],
        compiler_params=_cparams(("parallel", "arbitrary")),
        name="proj_residual",
    )(*xps, *xss, *ws, h, pmod, smod)


HG_C = 64
HG_B = 16
HG_STEP = 256


def _lower_bound(lbl, layer):
    e = jnp.exp(lbl - jnp.max(lbl, axis=0, keepdims=True))
    return jnp.sum(e[0:layer + 1], axis=0, keepdims=True) / jnp.sum(e, axis=0, keepdims=True)


def _hgrn_chunk(q, af, v, st, lb, tri):
    c, dk = q.shape
    nb = c // HG_B
    f = lb + (1.0 - lb) * _sigmoid(af)
    g = jnp.log(f)
    k = 1.0 - f
    cum = _dot_exact_rhs(tri, g)
    o = _dot1(q * jnp.exp(cum), st, NT)
    q3 = q.reshape(nb, HG_B, dk)
    k3 = k.reshape(nb, HG_B, dk)
    c3 = cum.reshape(nb, HG_B, dk)
    tpos = lax.broadcasted_iota(jnp.int32, (nb, HG_B, dk), 1)
    lane3 = lax.broadcasted_iota(jnp.int32, (nb, HG_B, dk), 2)
    dmat = jnp.zeros((nb, HG_B, dk), F32)
    for s in range(HG_B):
        rel = c3 - c3[:, s:s + 1, :]
        a = q3 * k3[:, s:s + 1, :] * jnp.exp(jnp.where(tpos >= s, rel, -jnp.inf))
        col = jnp.sum(a, axis=-1, keepdims=True)
        dmat = jnp.where(lane3 == s, col, dmat)
    lane_c = lax.broadcasted_iota(jnp.int32, (HG_B, c), 1)
    rows = []
    for blk in range(nb):
        r0 = HG_B * blk
        d_blk = dmat[blk]
        if blk == 0:
            rows.append(d_blk[:, :c])
            continue
        d_blk = pltpu.roll(d_blk, r0, axis=1)[:, :c]
        cb = cum[r0 - 1:r0, :]
        qs = q[r0:r0 + HG_B] * jnp.exp(cum[r0:r0 + HG_B] - cb)
        ks = k * jnp.exp(jnp.minimum(cb - cum, 0.0))
        s_off = _dot1(qs, ks, NT)
        rows.append(jnp.where(lane_c < r0, s_off, 0.0) + d_blk)
    scores = jnp.concatenate(rows, axis=0)
    o = o + _dot1(scores, v)
    last = cum[c - 1:c, :]
    st_new = jnp.exp(last) * st + _dot1(v, k * jnp.exp(last - cum), TN)
    return o, st_new


def _hgrn_p_kernel(q_ref, f_ref, v_ref, g_ref, lbl_ref, gn_ref, o_ref, s_ref, st_ref, *, layer, nsteps):
    tb = pl.program_id(2)

    @pl.when(tb == 0)
    def _():
        st_ref[...] = jnp.zeros_like(st_ref)

    lb = _lower_bound(lbl_ref[...], layer)
    ri = lax.broadcasted_iota(jnp.int32, (HG_C, HG_C), 0)
    ci = lax.broadcasted_iota(jnp.int32, (HG_C, HG_C), 1)
    tri = jnp.where(ri >= ci, 1.0, 0.0).astype(BF16)
    st = st_ref[...]
    for c in range(HG_STEP // HG_C):
        sl = slice(HG_C * c, HG_C * (c + 1))
        o, st = _hgrn_chunk(q_ref[sl, :], f_ref[sl, :], v_ref[sl, :], st, lb, tri)
        o_ref[sl, :] = (_rms(o, gn_ref[...]) * _sigmoid(g_ref[sl, :])).astype(BF16)
    st_ref[...] = st

    @pl.when(tb == nsteps - 1)
    def _():
        s_ref[0, 0] = st.T


def _hgrn_prompt(z, lb_logits, gn, nb, seq, layer):
    nh = 8
    nsteps = seq // HG_STEP
    zspec = lambda c0: pl.BlockSpec((HG_STEP, 128), lambda b, h, t: (b * nsteps + t, c0 + h))
    return pl.pallas_call(
        functools.partial(_hgrn_p_kernel, layer=layer, nsteps=nsteps),
        grid=(nb, nh, nsteps),
        in_specs=[zspec(0), zspec(8), zspec(16), zspec(24),
                  pl.BlockSpec((lb_logits.shape[0], 128), lambda b, h, t: (0, h)),
                  pl.BlockSpec((1, 128), lambda b, h, t: (0, 0))],
        out_specs=[pl.BlockSpec((HG_STEP, 128), lambda b, h, t: (b * nsteps + t, h)),
                   pl.BlockSpec((1, 1, 128, 128), lambda b, h, t: (b, h, 0, 0))],
        out_shape=[jax.ShapeDtypeStruct((nb * seq, nh * 128), BF16),
                   jax.ShapeDtypeStruct((nb, nh, 128, 128), F32)],
        scratch_shapes=[pltpu.VMEM((128, 128), F32)],
        compiler_params=_cparams(("parallel", "parallel", "arbitrary")),
        name="hgrn_prompt",
    )(z, z, z, z, lb_logits, gn.reshape(1, 128))


def _hgrn_s_kernel(q_ref, f_ref, v_ref, g_ref, lbl_ref, gn_ref, s0_ref, o_ref, s_ref, *, layer):
    nt = q_ref.shape[1]
    lb_all = _lower_bound(lbl_ref[...], layer)
    tpos = lax.broadcasted_iota(jnp.int32, (nt, 128), 0)
    outs = []
    for h in range(8):
        sl = slice(128 * h, 128 * (h + 1))
        q = q_ref[0, :, sl]
        v = v_ref[0, :, sl]
        lb = lb_all[:, sl]
        f = lb + (1.0 - lb) * _sigmoid(f_ref[0, :, sl])
        g = jnp.log(f)
        k = 1.0 - f
        crow = [g[0:1]]
        for t in range(1, nt):
            crow.append(crow[-1] + g[t:t + 1])
        cum = jnp.concatenate(crow, axis=0)
        last = crow[-1]
        s0 = s0_ref[0, 0, h]
        qe = jnp.concatenate([q * jnp.exp(cum), jnp.zeros((8 - nt, 128), F32)], axis=0)
        o = _dot1(qe, s0)[0:nt]
        for s in range(nt):
            a = q * k[s:s + 1] * jnp.exp(jnp.where(tpos >= s, cum - cum[s:s + 1], -jnp.inf))
            o = o + jnp.sum(a, axis=-1, keepdims=True) * v[s:s + 1]
        kp = k * jnp.exp(last - cum)
        w8 = jnp.concatenate([jnp.exp(last), kp, jnp.zeros((8 - 1 - nt, 128), F32)], axis=0)
        wt = jnp.concatenate([w8, jnp.zeros((120, 128), F32)], axis=0).T
        s_new = wt[:, 0:1] * s0
        for s in range(nt):
            s_new = s_new + wt[:, 1 + s:2 + s] * v[s:s + 1]
        s_ref[0, 0, h] = s_new
        outs.append(_rms(o, gn_ref[...]) * _sigmoid(g_ref[0, :, sl]))
    o_ref[0] = jnp.concatenate(outs, axis=1)


def _hgrn_sample(zs3, lb_logits, gn, state, layer):
    nb, nt, _ = zs3.shape
    zspec = lambda c: pl.BlockSpec((1, nt, 1024), lambda b: (b, 0, c))
    sspec = pl.BlockSpec((1, 1, 8, 128, 128), lambda b: (0, b, 0, 0, 0))
    return pl.pallas_call(
        functools.partial(_hgrn_s_kernel, layer=layer),
        grid=(nb,),
        in_specs=[zspec(0), zspec(1), zspec(2), zspec(3),
                  pl.BlockSpec(lb_logits.shape, lambda b: (0, 0)),
                  pl.BlockSpec((1, 128), lambda b: (0, 0)),
                  sspec],
        out_specs=[pl.BlockSpec((1, nt, 1024), lambda b: (b, 0, 0)), sspec],
        out_shape=[jax.ShapeDtypeStruct((nb, nt, 1024), F32),
                   jax.ShapeDtypeStruct(state.shape, F32)],
        compiler_params=_cparams(("parallel",)),
        name="hgrn_sample",
    )(zs3, zs3, zs3, zs3, lb_logits, gn.reshape(1, 128), state)


DILATED = ((128, 1), (512, 4), (2048, 16))
REL_BUCKETS = 32
REL_MAX_DISTANCE = 2048
ATT_SCALE = 128 ** -0.5
QB = 128


def _rel_bucket_np(dist):
    exact = REL_BUCKETS // 2
    d = np.asarray(dist)
    far = exact + (np.log(np.maximum(d, 1).astype(np.float32) / np.float32(exact))
                   / np.float32(math.log(REL_MAX_DISTANCE / exact)) * np.float32(REL_BUCKETS - exact)).astype(np.int32)
    return np.where(d < exact, d, np.minimum(far, REL_BUCKETS - 1)).astype(np.int32)


def _band_tables(rel_bias):
    rb = rel_bias.astype(F32)
    tabs = []
    for (window, dil), width in zip(DILATED, (2 * QB, 2 * QB, QB)):
        period = width + QB
        i = np.arange(period + 1)
        k = np.where(i < width, i, i - (period + 1))
        steps = (width - QB) - k
        valid = (steps >= 0) & (steps <= window // dil)
        bucket = _rel_bucket_np(np.clip(steps, 0, window // dil) * dil)
        u = jnp.where(jnp.asarray(valid)[None, :], rb[jnp.asarray(bucket)].T, -jnp.inf)
        flat = jnp.tile(u, (1, QB))[:, :QB * period]
        tabs.append(flat.reshape(rb.shape[1], QB, period)[:, :, :width])
    return tabs


def _soft_block(qb, kw, vw, bias):
    l = _dot1(qb, kw, NT) * ATT_SCALE + bias
    m = jnp.max(l, axis=-1, keepdims=True)
    p = jnp.exp(l - m)
    return m, jnp.sum(p, axis=-1, keepdims=True), _dot1(p, vw)


def _dil_p_kernel(q_ref, k_ref, v_ref, gq_ref, gk_ref, b1_ref, b2_ref, b3_ref, o_ref, ko_ref, vo_ref,
                  qs, ks, vs, a1, a2, a3, m1, m2, m3, l1, l2, l3):
    seq = q_ref.shape[0]
    kn = _rms(k_ref[...], gk_ref[...])
    v = v_ref[...]
    ko_ref[...] = kn
    vo_ref[...] = v
    qs[...] = _rms(q_ref[...], gq_ref[...])
    ks[...] = kn
    vs[...] = v
    zpad = jnp.zeros((QB, 128), F32)
    first_cols = lax.broadcasted_iota(jnp.int32, (QB, 2 * QB), 1) >= QB

    def banded(qsub, ksub, vsub, bias, n_blocks, store):
        for i in range(n_blocks):
            qb = qsub(i)
            if i == 0:
                kw = jnp.concatenate([zpad, ksub(0)], axis=0)
                vw = jnp.concatenate([zpad, vsub(0)], axis=0)
                bb = jnp.where(first_cols, bias, -jnp.inf)
            else:
                kw = jnp.concatenate([ksub(i - 1), ksub(i)], axis=0)
                vw = jnp.concatenate([vsub(i - 1), vsub(i)], axis=0)
                bb = bias
            store(i, *_soft_block(qb, kw, vw, bb))

    def st1(i, m, l, a):
        sl = slice(QB * i, QB * (i + 1))
        m1[sl, :] = m
        l1[sl, :] = l
        a1[sl, :] = a
    blk = lambda ref: (lambda i: ref[QB * i:QB * (i + 1), :])
    banded(blk(qs), blk(ks), blk(vs), b1_ref[0], seq // QB, st1)

    d2 = DILATED[1][1]
    for r in range(d2):
        sub = lambda ref: (lambda i: ref[pl.ds(r + d2 * QB * i, QB, stride=d2), :])

        def st2(i, m, l, a):
            idx = pl.ds(r + d2 * QB * i, QB, stride=d2)
            m2[idx, :] = m
            l2[idx, :] = l
            a2[idx, :] = a
        banded(sub(qs), sub(ks), sub(vs), b2_ref[0], seq // (QB * d2), st2)

    d3 = DILATED[2][1]
    for r in range(d3):
        idx = pl.ds(r, QB, stride=d3)
        m, l, a = _soft_block(qs[idx, :], ks[idx, :], vs[idx, :], b3_ref[0])
        m3[idx, :] = m
        l3[idx, :] = l
        a3[idx, :] = a

    mx = jnp.maximum(jnp.maximum(m1[...], m2[...]), m3[...])
    w1 = jnp.exp(m1[...] - mx)
    w2 = jnp.exp(m2[...] - mx)
    w3 = jnp.exp(m3[...] - mx)
    num = w1 * a1[...] + w2 * a2[...] + w3 * a3[...]
    den = w1 * l1[...] + w2 * l2[...] + w3 * l3[...]
    o_ref[...] = (num / den).astype(BF16)


def _dilated_prompt(z, gq, gk, tabs, nb, seq):
    nh = 8
    zspec = lambda c0: pl.BlockSpec((seq, 128), lambda b, h: (b, c0 + h))
    gspec = pl.BlockSpec((1, 128), lambda b, h: (0, 0))
    tspec = lambda w: pl.BlockSpec((1, QB, w), lambda b, h: (h, 0, 0))
    ospec = pl.BlockSpec((seq, 128), lambda b, h: (b, h))
    big = pltpu.VMEM((seq, 128), F32)
    col = pltpu.VMEM((seq, 1), F32)
    return pl.pallas_call(
        _dil_p_kernel,
        grid=(nb, nh),
        in_specs=[zspec(32), zspec(40), zspec(48), gspec, gspec, tspec(2 * QB), tspec(2 * QB), tspec(QB)],
        out_specs=[ospec, ospec, ospec],
        out_shape=[jax.ShapeDtypeStruct((nb * seq, nh * 128), BF16),
                   jax.ShapeDtypeStruct((nb * seq, nh * 128), F32),
                   jax.ShapeDtypeStruct((nb * seq, nh * 128), F32)],
        scratch_shapes=[big, big, big, big, big, big, col, col, col, col, col, col],
        compiler_params=_cparams(("parallel", "parallel")),
        name="dilated_prompt",
    )(z, z, z, gq.reshape(1, 128), gk.reshape(1, 128), *tabs)


def _sample_tables(rel_bias, nt, buf_len):
    rb = rel_bias.astype(F32)
    nh = rb.shape[1]
    w2 = DILATED[1][0]

    def count(delta, patterns):
        c = np.zeros(delta.shape, np.int32)
        for window, dil in patterns:
            c += ((delta >= 0) & (delta <= window) & (delta % dil == 0)).astype(np.int32)
        return c

    qpos = buf_len + np.arange(8)
    real_q = (np.arange(8) < nt)
    da = qpos[None, :] - (buf_len - w2 + np.arange(w2))[:, None]
    ca = count(da, DILATED[:2]) * real_q[None, :]
    nj = buf_len // 16
    rows_b = (16 * np.arange(nj)[None, :] + np.arange(nt)[:, None]).reshape(-1)
    db = qpos[None, :] - rows_b[:, None]
    cb = count(db, DILATED[2:]) * real_q[None, :]
    dn = qpos[None, :] - qpos[:, None]
    cn = count(dn, DILATED) * real_q[None, :] * real_q[:, None]

    def tables(delta, cnt):
        bucket = _rel_bucket_np(np.clip(delta, 0, REL_MAX_DISTANCE))
        onehot = (bucket.reshape(-1, 1) == np.arange(REL_BUCKETS)[None, :]).astype(np.float32)
        bias = jnp.dot(jnp.asarray(onehot), rb, precision=lax.Precision.HIGHEST).reshape(*delta.shape, nh)
        bias = jnp.where(jnp.asarray((cnt > 0) | ~real_q[None, :])[:, :, None], bias, -jnp.inf)
        bias = jnp.transpose(bias, (0, 2, 1)).reshape(delta.shape[0], nh * 8)
        mult = np.repeat(cnt[:, None, :], nh, axis=1).reshape(delta.shape[0], nh * 8).astype(np.float32)
        return bias, jnp.asarray(mult)

    return tables(da, ca) + tables(db, cb) + tables(dn, cn)


def _dil_s_kernel(q_ref, k_ref, v_ref, gq_ref, gk_ref, ca_ref, cb_ref,
                  ba_ref, ma_ref, bb_ref, mb_ref, bn_ref, mn_ref, o_ref, ko_ref, vo_ref):
    nt = q_ref.shape[1]
    nh = 8
    gq = gq_ref[...]
    gk = gk_ref[...]
    qn = jnp.concatenate([_rms(q_ref[0, :, 128 * h:128 * (h + 1)], gq) for h in range(nh)], axis=1)
    kn = jnp.concatenate([_rms(k_ref[0, :, 128 * h:128 * (h + 1)], gk) for h in range(nh)], axis=1)
    vn = v_ref[0]
    ko_ref[0] = kn
    vo_ref[0] = vn
    pad = jnp.zeros((8 - nt, nh * 128), F32)
    q8 = jnp.concatenate([qn, pad], axis=0)
    k8 = jnp.concatenate([kn, pad], axis=0)
    v8 = jnp.concatenate([vn, pad], axis=0)
    lane_head = lax.broadcasted_iota(jnp.int32, (8, nh * 128), 1) // 128
    qbd = jnp.concatenate([jnp.where(lane_head == h, q8, 0.0) for h in range(nh)], axis=0).astype(BF16)

    def heads(ref2d, first, nrows, stride):
        return jnp.concatenate([ref2d[pl.ds(first + h, nrows, stride=stride), :] for h in range(nh)], axis=1)

    a2 = ca_ref.at[0]
    na = a2.shape[0] // (2 * nh)
    groups = [(heads(a2, 0, na, 2 * nh), heads(a2, nh, na, 2 * nh), ba_ref[...], ma_ref[...])]
    nj = cb_ref.shape[1]
    per_j = cb_ref.shape[2]
    b2 = cb_ref.reshape(nj * per_j, 128)
    for i in range(nt):
        groups.append((heads(b2, 2 * nh * i, nj, per_j), heads(b2, 2 * nh * i + nh, nj, per_j),
                       bb_ref[nj * i:nj * (i + 1), :], mb_ref[nj * i:nj * (i + 1), :]))
    groups.append((k8, v8, bn_ref[...], mn_ref[...]))

    logits = [_dot1(kk, qbd, NT) * ATT_SCALE + bias for kk, _, bias, _ in groups]
    mx = logits[0].max(axis=0, keepdims=True)
    for l in logits[1:]:
        mx = jnp.maximum(mx, l.max(axis=0, keepdims=True))
    ps = [mult * jnp.exp(l - mx) for l, (_, _, _, mult) in zip(logits, groups)]
    den = ps[0].sum(axis=0, keepdims=True)
    for p in ps[1:]:
        den = den + p.sum(axis=0, keepdims=True)
    inv = 1.0 / jnp.where(den > 0.0, den, 1.0)
    acc = None
    for p, (_, vv, _, _) in zip(ps, groups):
        part = _dot1(p * inv, vv, TN)
        acc = part if acc is None else acc + part
    o8 = jnp.concatenate([acc[8 * h:8 * (h + 1), 128 * h:128 * (h + 1)] for h in range(nh)], axis=1)
    o_ref[0] = o8[0:nt]


def _dilated_sample(zs3, gq, gk, cache, tabs):
    nb, nt, _ = zs3.shape
    buf_len = cache.shape[1]
    w2 = DILATED[1][0]
    d3 = DILATED[2][1]
    kvh = 2 * 8
    cache_a = cache.reshape(nb, buf_len * kvh, 128)
    cache_b = cache.reshape(nb, buf_len // d3, d3 * kvh, 128)
    zspec = lambda c: pl.BlockSpec((1, nt, 1024), lambda b: (b, 0, c))
    gspec = pl.BlockSpec((1, 128), lambda b: (0, 0))
    full = lambda a: pl.BlockSpec(a.shape, lambda b: (0, 0))
    ospec = pl.BlockSpec((1, nt, 1024), lambda b: (b, 0, 0))
    return pl.pallas_call(
        _dil_s_kernel,
        grid=(nb,),
        in_specs=[zspec(4), zspec(5), zspec(6), gspec, gspec,
                  pl.BlockSpec((1, w2 * kvh, 128), lambda b: (b, buf_len // w2 - 1, 0)),
                  pl.BlockSpec((1, buf_len // d3, nt * kvh, 128), lambda b: (b, 0, 0, 0))]
                 + [full(a) for a in tabs],
        out_specs=[ospec, ospec, ospec],
        out_shape=[jax.ShapeDtypeStruct((nb, nt, 1024), F32)] * 3,
        compiler_params=_cparams(("parallel",)),
        name="dilated_sample",
    )(zs3, zs3, zs3, gq.reshape(1, 128), gk.reshape(1, 128), cache_a, cache_b, *tabs)


def _route_kernel(x_ref, g_ref, psh, psc, ssh, ssc, rw_ref, rb_ref, u_ref, idx_ref, gate_ref, *, npt):
    i = pl.program_id(0)
    xr = _rms(x_ref[...], g_ref[...])
    u = xr * (1.0 + _pick(i < npt, psc, ssc)) + _pick(i < npt, psh, ssh)
    ub = u.astype(BF16).astype(F32)
    lo = lax.shift_right_logical(lax.bitcast_convert_type(ub[:, :D // 2], jnp.int32), 16)
    u_ref[...] = lax.bitcast_convert_type(ub[:, D // 2:], jnp.int32) | lo
    l = _dot3(u, rw_ref[...]) + rb_ref[...]
    lane = lax.broadcasted_iota(jnp.int32, l.shape, 1).astype(F32)
    out_lane = lax.broadcasted_iota(jnp.int32, (TM, 128), 1)
    vals, idxs = [], []
    for _ in range(TOP_K):
        m = jnp.max(l, axis=-1, keepdims=True)
        first = jnp.min(jnp.where(l == m, lane, float(N_EXPERTS)), axis=-1, keepdims=True)
        vals.append(m)
        idxs.append(first.astype(jnp.int32))
        l = jnp.where(lane == first, -jnp.inf, l)
    es = [jnp.exp(v - vals[0]) for v in vals]
    tot = es[0] + es[1] + es[2] + es[3]
    idx_out = jnp.zeros((TM, 128), jnp.int32)
    gate_out = jnp.zeros((TM, 128), F32)
    for k in range(TOP_K):
        idx_out = jnp.where(out_lane == k, idxs[k], idx_out)
        gate_out = jnp.where(out_lane == k, es[k] / tot, gate_out)
    idx_ref[...] = idx_out
    gate_ref[...] = gate_out


def _route(x, g, pmod, smod, rw, rb, npt, tiles_per_batch):
    t = x.shape[0]
    psh, ssh = _mod_specs(npt, tiles_per_batch, 0)
    psc, ssc = _mod_specs(npt, tiles_per_batch, 1)
    return pl.pallas_call(
        functools.partial(_route_kernel, npt=npt),
        grid=(t // TM,),
        in_specs=[pl.BlockSpec((TM, D), lambda i: (i, 0)),
                  pl.BlockSpec((1, D), lambda i: (0, 0)),
                  psh, psc, ssh, ssc,
                  pl.BlockSpec((D, N_EXPERTS), lambda i: (0, 0)),
                  pl.BlockSpec((1, N_EXPERTS), lambda i: (0, 0))],
        out_specs=[pl.BlockSpec((TM, D // 2), lambda i: (i, 0)),
                   pl.BlockSpec((TM, 128), lambda i: (i, 0)),
                   pl.BlockSpec((TM, 128), lambda i: (i, 0))],
        out_shape=[jax.ShapeDtypeStruct((t, D // 2), jnp.int32),
                   jax.ShapeDtypeStruct((t, 128), jnp.int32),
                   jax.ShapeDtypeStruct((t, 128), F32)],
        compiler_params=_cparams(("parallel",)),
        name="moe_route",
    )(x, g.reshape(1, D), pmod, pmod, smod, smod, rw, rb.reshape(1, N_EXPERTS))


MOE_TM = 1216
MOE_ROWS = (304, 608, MOE_TM)
MOE_TN = 512
MOE_TF = MOE_TN // 2


MOE_NG = 2 * D_EXPERT // MOE_TN
MOE_DN = 512
MOE_ND = D // MOE_DN


def _moe_kernel(te_ref, nu_ref, nv_ref, tok_ref, u_hbm, wgu_ref, bgu_ref, wdn_ref, bdn_ref, y_ref, xg, xb, hs, sem):
    t = pl.program_id(0)
    j = pl.program_id(1)
    nu = nu_ref[0]
    active = t < nu
    slot = t % 2
    nv = nv_ref[t]

    def issue(tile, s):
        def body(r, carry):
            tok = tok_ref[tile * MOE_TM + r]
            pltpu.make_async_copy(u_hbm.at[pl.ds(tok, 1)], xg.at[s, pl.ds(r, 1)], sem.at[s]).start()
            return carry
        lax.fori_loop(0, nv_ref[tile], body, 0)

    @pl.when(jnp.logical_and(t == 0, j == 0))
    def _():
        xg[...] = jnp.zeros_like(xg)
        issue(0, 0)

    @pl.when(jnp.logical_and(active, j == 0))
    def _():
        n8 = pl.multiple_of((nv // 8) * 8, 8)

        @pl.when(n8 > 0)
        def _():
            pltpu.make_async_copy(u_hbm.at[pl.ds(0, n8)], xg.at[slot, pl.ds(0, n8)], sem.at[slot]).wait()

        def wait_row(r, carry):
            pltpu.make_async_copy(u_hbm.at[pl.ds(0, 1)], xg.at[slot, pl.ds(0, 1)], sem.at[slot]).wait()
            return carry
        lax.fori_loop(0, nv - n8, wait_row, 0)
        w = xg[slot]
        xb[:, :D // 2] = lax.bitcast_convert_type(lax.shift_left(w, 16), F32).astype(BF16)
        xb[:, D // 2:] = lax.bitcast_convert_type(w & -65536, F32).astype(BF16)

    @pl.when(jnp.logical_and(t + 1 < nu, j == 1))
    def _():
        issue(t + 1, 1 - slot)

    def gate_up(rows):
        gu = _dot1(xb[0:rows, :], wgu_ref[0, 0]) + bgu_ref[0, 0]
        up = pltpu.roll(gu, MOE_TN - 1, axis=1)
        gate = jnp.minimum(gu, SWIGLU_LIMIT)
        up = jnp.clip(up, -SWIGLU_LIMIT, SWIGLU_LIMIT)
        h = ((up + 1.0) * gate * _sigmoid(SWIGLU_ALPHA * gate)).astype(BF16)
        ri = lax.broadcasted_iota(jnp.int32, (MOE_TN, MOE_TF), 0)
        ci = lax.broadcasted_iota(jnp.int32, (MOE_TN, MOE_TF), 1)
        pick_even = jnp.where(ri == 2 * ci, 1.0, 0.0).astype(BF16)
        hs[j, 0:rows, :] = jnp.dot(h, pick_even, preferred_element_type=F32).astype(BF16)

    def down(rows):
        acc = jnp.dot(hs[0, 0:rows, :], wdn_ref[0, 0, 0:MOE_TF, :].astype(BF16), preferred_element_type=F32)
        for c in range(1, MOE_NG):
            acc = acc + jnp.dot(hs[c, 0:rows, :], wdn_ref[0, 0, MOE_TF * c:MOE_TF * (c + 1), :].astype(BF16),
                                preferred_element_type=F32)
        y_ref[0:rows, :] = acc + bdn_ref[0, 0]
        if rows < MOE_TM:
            y_ref[rows:MOE_TM, :] = jnp.zeros((MOE_TM - rows, MOE_DN), F32)

    lower = 0
    for rows in MOE_ROWS:
        fits = jnp.logical_and(active, jnp.logical_and(nv > lower, nv <= rows))
        pl.when(jnp.logical_and(fits, j < MOE_NG))(functools.partial(gate_up, rows))
        pl.when(jnp.logical_and(fits, j >= MOE_NG))(functools.partial(down, rows))
        lower = rows

    @pl.when(jnp.logical_and(jnp.logical_not(active), j >= MOE_NG))
    def _():
        y_ref[...] = jnp.zeros_like(y_ref)


def _moe_experts(u, tile_e, n_used, n_valid, slot_tok, layer, w_gu, b_gu, w_dn, b_dn):
    n_tiles = tile_e.shape[0]
    nj = MOE_NG + MOE_ND
    jj = lambda t, j, nu: jnp.where(t < nu[0], j, nj - 1)
    jg = lambda t, j, nu: jnp.minimum(jj(t, j, nu), MOE_NG - 1)
    jd = lambda t, j, nu: jnp.maximum(jj(t, j, nu) - MOE_NG, 0)
    grid_spec = pltpu.PrefetchScalarGridSpec(
        num_scalar_prefetch=4,
        grid=(n_tiles, nj),
        in_specs=[pl.BlockSpec(memory_space=pl.ANY),
                  pl.BlockSpec((1, 1, D, MOE_TN), lambda t, j, te, nu, nv, tok: (layer, te[t], 0, jg(t, j, nu))),
                  pl.BlockSpec((1, 1, 1, MOE_TN), lambda t, j, te, nu, nv, tok: (layer, te[t], 0, jg(t, j, nu))),
                  pl.BlockSpec((1, 1, D_EXPERT, MOE_DN), lambda t, j, te, nu, nv, tok: (layer, te[t], 0, jd(t, j, nu))),
                  pl.BlockSpec((1, 1, 1, MOE_DN), lambda t, j, te, nu, nv, tok: (layer, te[t], 0, jd(t, j, nu)))],
        out_specs=pl.BlockSpec((MOE_TM, MOE_DN), lambda t, j, te, nu, nv, tok: (t, jnp.maximum(j - MOE_NG, 0))),
        scratch_shapes=[pltpu.VMEM((2, MOE_TM, D // 2), jnp.int32), pltpu.VMEM((MOE_TM, D), BF16),
                        pltpu.VMEM((MOE_NG, MOE_TM, MOE_TF), BF16), pltpu.SemaphoreType.DMA((2,))],
    )
    nl, ne = b_gu.shape[0], b_gu.shape[1]
    return pl.pallas_call(
        _moe_kernel,
        grid_spec=grid_spec,
        out_shape=jax.ShapeDtypeStruct((n_tiles * MOE_TM, D), F32),
        compiler_params=_cparams(("arbitrary", "arbitrary")),
        name="moe_experts",
    )(tile_e, n_used, n_valid, slot_tok, u, w_gu, b_gu.reshape(nl, ne, 1, 2 * D_EXPERT), w_dn, b_dn.reshape(nl, ne, 1, D))


CMB = 256


def _combine_kernel(slot_ref, h_ref, gate_ref, pg, sg, y_hbm, o_ref, buf, sem, *, npt_c):
    i = pl.program_id(0)

    def issue(r, carry):
        for k in range(TOP_K):
            s = slot_ref[(i * CMB + r) * TOP_K + k]
            pltpu.make_async_copy(y_hbm.at[pl.ds(s, 1)], buf.at[k, pl.ds(r, 1)], sem).start()
        return carry
    lax.fori_loop(0, CMB, issue, 0)
    for k in range(TOP_K):
        pltpu.make_async_copy(y_hbm.at[pl.ds(0, CMB)], buf.at[k], sem).wait()
    gates = gate_ref[...]
    mo = gates[:, 0:1] * buf[0]
    for k in range(1, TOP_K):
        mo = mo + gates[:, k:k + 1] * buf[k]
    o_ref[...] = h_ref[...] + _pick(i < npt_c, pg, sg) * mo


def _moe_combine(h, y, slot4, gate128, pmod, smod, npt_c, tiles_per_batch_c):
    t = h.shape[0]
    nsamp_tiles = (t // CMB) - npt_c
    grid_spec = pltpu.PrefetchScalarGridSpec(
        num_scalar_prefetch=1,
        grid=(t // CMB,),
        in_specs=[pl.BlockSpec((CMB, D), lambda i, s: (i, 0)),
                  pl.BlockSpec((CMB, 128), lambda i, s: (i, 0)),
                  pl.BlockSpec((1, 1, D), lambda i, s: (jnp.minimum(i // tiles_per_batch_c, npt_c // tiles_per_batch_c - 1), 0, 2)),
                  pl.BlockSpec((CMB, D), lambda i, s: (jnp.clip(i - npt_c, 0, nsamp_tiles - 1), 2)),
                  pl.BlockSpec(memory_space=pl.ANY)],
        out_specs=pl.BlockSpec((CMB, D), lambda i, s: (i, 0)),
        scratch_shapes=[pltpu.VMEM((TOP_K, CMB, D), F32), pltpu.SemaphoreType.DMA],
    )
    return pl.pallas_call(
        functools.partial(_combine_kernel, npt_c=npt_c),
        grid_spec=grid_spec,
        out_shape=jax.ShapeDtypeStruct((t, D), F32),
        compiler_params=_cparams(("arbitrary",)),
        name="moe_combine",
    )(slot4, h, gate128, pmod, smod, y)


def _moe_layer(h, g, pmod, smod, layer, rw, rb, w_gu, b_gu, w_dn, b_dn, npt, tiles_per_batch):
    t = h.shape[0]
    u, idx128, gate128 = _route(h, g, pmod, smod, rw[layer], rb[layer], npt, tiles_per_batch)
    idx4 = idx128[:, :TOP_K]
    onehot = idx4[:, :, None] == jnp.arange(N_EXPERTS, dtype=jnp.int32)[None, None, :]
    sel = jnp.any(onehot, axis=1).astype(jnp.int32)
    counts = jnp.sum(sel, axis=0)
    rank = jnp.cumsum(sel, axis=0) - 1
    padded = (counts + MOE_TM - 1) // MOE_TM * MOE_TM
    pad_end = jnp.cumsum(padded)
    pad_start = pad_end - padded
    n_tiles = -(-t * TOP_K // MOE_TM) + N_EXPERTS
    dest4 = jnp.sum(jnp.where(onehot, (pad_start[None, :] + rank)[:, None, :], 0), axis=2).astype(jnp.int32)
    tok = jnp.broadcast_to(jnp.arange(t, dtype=jnp.int32)[:, None], (t, TOP_K))
    slot_tok = jnp.zeros((n_tiles * MOE_TM,), jnp.int32).at[dest4.reshape(-1)].set(tok.reshape(-1))
    n_used = (pad_end[-1] // MOE_TM).astype(jnp.int32)
    tile_ids = jnp.arange(n_tiles, dtype=jnp.int32)
    tile_e = jnp.minimum(jnp.searchsorted(pad_end, jnp.minimum(tile_ids, n_used - 1) * MOE_TM, side='right'),
                         N_EXPERTS - 1).astype(jnp.int32)
    n_valid = jnp.where(tile_ids < n_used,
                        jnp.clip((pad_start + counts)[tile_e] - tile_ids * MOE_TM, 0, MOE_TM), 0).astype(jnp.int32)
    y = _moe_experts(u, tile_e, n_used.reshape(1), n_valid, slot_tok, layer, w_gu, b_gu, w_dn, b_dn)
    return _moe_combine(h, y, dest4.reshape(-1), gate128, pmod, smod,
                        npt * (TM // CMB), tiles_per_batch * (TM // CMB))


C_HEADS = 16
C_NOPE = 128
C_ROPE = 64
C_LORA = 512
ROPE_THETA = 10000.0
MLA_SCALE = (C_NOPE + C_ROPE) ** -0.5
HP = 128


def _rope_tables(pos):
    half = C_ROPE // 2
    inv = ROPE_THETA ** (-jnp.arange(half, dtype=F32) / half)
    ang = pos.astype(F32)[:, None] * inv[None, :]
    z = jnp.zeros_like(ang)
    cos = jnp.concatenate([jnp.cos(ang), jnp.cos(ang), z, z], axis=1)
    sin = jnp.concatenate([-jnp.sin(ang), jnp.sin(ang), z, z], axis=1)
    return cos, sin


def _rope128(x, cos, sin):
    n = x.shape[1]
    lane = lax.broadcasted_iota(jnp.int32, x.shape, 1) % HP
    partner = jnp.where(lane < C_ROPE // 2, pltpu.roll(x, n - C_ROPE // 2, axis=1), pltpu.roll(x, C_ROPE // 2, axis=1))
    return x * cos + partner * sin


def _mla_prep_kernel(z_ref, gq_ref, gkv_ref, gkr_ref, gqn_ref, gqr_ref, gkn_ref, wn_ref, wr_ref, wuk_ref,
                     cos_ref, sin_ref, lat_ref, kr_ref, krb_ref, qn_ref, qr_ref, qabs_ref, qg_ref, *, npt):
    i = pl.program_id(0)
    z = z_ref[...]
    cos = cos_ref[...]
    sin = sin_ref[...]
    lat_ref[...] = _rms(z[:, C_LORA:2 * C_LORA], gkv_ref[...])
    k128 = z[:, 2 * C_LORA:2 * C_LORA + HP]
    k128 = k128 * lax.rsqrt(jnp.sum(k128 * k128, axis=-1, keepdims=True) / C_ROPE + EPS) * gkr_ref[...]
    k128 = _rope128(k128, cos, sin)
    kr_ref[...] = k128[:, 0:C_ROPE]
    krb_ref[...] = k128.astype(BF16)
    cqn = _rms(z[:, 0:C_LORA], gq_ref[...]).astype(BF16)
    qn_all = jnp.dot(cqn, wn_ref[...], preferred_element_type=F32)
    qr_all = jnp.dot(cqn, wr_ref[...], preferred_element_type=F32)
    for h in range(C_HEADS):
        sl = slice(HP * h, HP * (h + 1))
        qn = _rms(qn_all[:, sl], gqn_ref[...]) * MLA_SCALE
        qn_ref[:, sl] = qn.astype(BF16)
        qg_ref[:, sl] = qn * gkn_ref[...]
        x = qr_all[:, sl]
        x = x * lax.rsqrt(jnp.sum(x * x, axis=-1, keepdims=True) / C_ROPE + EPS) * gqr_ref[...]
        qr_ref[:, sl] = (_rope128(x, cos, sin) * MLA_SCALE).astype(BF16)

    @pl.when(i == npt)
    def _():
        for h in range(C_HEADS):
            sl = slice(HP * h, HP * (h + 1))
            qabs_ref[:, C_LORA * h:C_LORA * (h + 1)] = _dot1(qg_ref[:, sl], wuk_ref[:, sl], NT).astype(BF16)


def _mla_prep(z1, gq, gkv, gkr, gqn, gqr, gkn, wn, wr, wuk, cos, sin, npt):
    t = z1.shape[0]
    zw = z1.shape[1]
    row = lambda w: pl.BlockSpec((TM, w), lambda i: (i, 0))
    full = lambda a: pl.BlockSpec(a.shape, lambda i: (0,) * a.ndim)
    pad64 = lambda g: jnp.concatenate([g, jnp.zeros((HP - C_ROPE,), F32)]).reshape(1, HP)
    args = (z1, gq.reshape(1, -1), gkv.reshape(1, -1), pad64(gkr), gqn.reshape(1, -1), pad64(gqr), gkn.reshape(1, -1),
            wn, wr, wuk, cos, sin)
    return pl.pallas_call(
        functools.partial(_mla_prep_kernel, npt=npt),
        grid=(t // TM,),
        in_specs=[row(zw)] + [full(a) for a in args[1:10]] + [row(HP), row(HP)],
        out_specs=[row(C_LORA), row(C_ROPE), row(HP), row(C_HEADS * HP), row(C_HEADS * HP),
                   pl.BlockSpec((TM, C_HEADS * C_LORA), lambda i: (0, 0))],
        out_shape=[jax.ShapeDtypeStruct((t, C_LORA), F32), jax.ShapeDtypeStruct((t, C_ROPE), F32),
                   jax.ShapeDtypeStruct((t, HP), BF16), jax.ShapeDtypeStruct((t, C_HEADS * HP), BF16),
                   jax.ShapeDtypeStruct((t, C_HEADS * HP), BF16), jax.ShapeDtypeStruct((TM, C_HEADS * C_LORA), BF16)],
        scratch_shapes=[pltpu.VMEM((TM, C_HEADS * HP), F32)],
        compiler_params=_cparams(("arbitrary",)),
        name="mla_prep",
    )(*args)


def _mla_kv_kernel(lat_ref, wuk_ref, wuv_ref, gkn_ref, kn_ref, v_ref):
    lat = lat_ref[...].astype(BF16)
    kn = jnp.dot(lat, wuk_ref[...], preferred_element_type=F32)
    for h in range(C_HEADS):
        sl = slice(HP * h, HP * (h + 1))
        kn_ref[:, sl] = _rms(kn[:, sl], gkn_ref[...]).astype(BF16)
    v_ref[...] = jnp.dot(lat, wuv_ref[...], preferred_element_type=F32).astype(BF16)


def _mla_kv(lat, wuk, wuv, gkn, rows):
    wide = C_HEADS * HP
    return pl.pallas_call(
        _mla_kv_kernel,
        grid=(rows // TM,),
        in_specs=[pl.BlockSpec((TM, C_LORA), lambda i: (i, 0)),
                  pl.BlockSpec((C_LORA, wide), lambda i: (0, 0)),
                  pl.BlockSpec((C_LORA, wide), lambda i: (0, 0)),
                  pl.BlockSpec((1, HP), lambda i: (0, 0))],
        out_specs=[pl.BlockSpec((TM, wide), lambda i: (i, 0))] * 2,
        out_shape=[jax.ShapeDtypeStruct((rows, wide), BF16)] * 2,
        compiler_params=_cparams(("parallel",)),
        name="mla_kv",
    )(lat, wuk, wuv, gkn.reshape(1, HP))


FQ = 512


def _mla_flash_kernel(qn_ref, qr_ref, kn_ref, kr_ref, v_ref, o_ref):
    qi = pl.program_id(2)
    qn = qn_ref[...]
    qr = qr_ref[...]
    row = lax.broadcasted_iota(jnp.int32, (FQ, FQ), 0)
    col = lax.broadcasted_iota(jnp.int32, (FQ, FQ), 1)

    def body(kb, carry):
        m, l, acc = carry
        ks = pl.ds(pl.multiple_of(kb * FQ, FQ), FQ)
        s = (lax.dot_general(qn, kn_ref[ks, :], (NT, ((), ())), preferred_element_type=F32)
             + lax.dot_general(qr, kr_ref[ks, :], (NT, ((), ())), preferred_element_type=F32))
        s = jnp.where(jnp.logical_or(kb < qi, col <= row), s, -jnp.inf)
        m_new = jnp.maximum(m, jnp.max(s, axis=-1, keepdims=True))
        alpha = jnp.exp(m - m_new)
        p = jnp.exp(s - m_new)
        l = alpha * l + jnp.sum(p, axis=-1, keepdims=True)
        acc = alpha * acc + jnp.dot(p.astype(BF16), v_ref[ks, :], preferred_element_type=F32)
        return m_new, l, acc

    m0 = jnp.full((FQ, 1), -jnp.inf, F32)
    l0 = jnp.zeros((FQ, 1), F32)
    a0 = jnp.zeros((FQ, HP), F32)
    m, l, acc = lax.fori_loop(0, qi + 1, body, (m0, l0, a0))
    o_ref[...] = (acc / l).astype(BF16)


def _mla_flash(qn, qr, kn, krb, v, nb, seq):
    nq = seq // FQ
    qspec = pl.BlockSpec((FQ, HP), lambda b, h, q: (b * nq + q, h))
    kspec = pl.BlockSpec((seq, HP), lambda b, h, q: (b, h))
    return pl.pallas_call(
        _mla_flash_kernel,
        grid=(nb, C_HEADS, nq),
        in_specs=[qspec, qspec, kspec, pl.BlockSpec((seq, HP), lambda b, h, q: (b, 0)), kspec],
        out_specs=qspec,
        out_shape=jax.ShapeDtypeStruct((nb * seq, C_HEADS * HP), BF16),
        compiler_params=_cparams(("parallel", "parallel", "parallel")),
        name="mla_flash",
    )(qn, qr, kn, krb, v)


PPS = 16
PPG = 4


def _mla_s_kernel(*refs, n_steps):
    pt_ref = refs[0]
    lat_refs = refs[1:1 + PPS]
    kr_refs = refs[1 + PPS:1 + 2 * PPS]
    wukt_ref, qabs_ref, qr_ref, latn_ref, krn_ref, o_ref, wst, latb, lg, m_ref, l_ref, acc_ref = refs[1 + 2 * PPS:]
    b = pl.program_id(0)
    kt = pl.program_id(1)
    nq = qabs_ref.shape[1]
    nheads = C_HEADS
    ntok = nq // nheads

    @pl.when(jnp.logical_and(b == 0, kt == 0))
    def _():
        wst[0:nheads * HP, :] = wukt_ref[...]

    @pl.when(kt == 0)
    def _():
        wst[nheads * HP:nheads * HP + nq, :] = qabs_ref[0]
        m_ref[...] = jnp.full_like(m_ref, -jnp.inf)
        l_ref[...] = jnp.zeros_like(l_ref)
        acc_ref[...] = jnp.zeros_like(acc_ref)

    qr = qr_ref[0]

    def up_project(lat_bf16):
        return lax.dot_general(wst[...], lat_bf16, (NT, ((), ())), preferred_element_type=F32)

    def key_logits(r, krt_f32):
        nkeys = r.shape[1]
        kraw = r[0:nheads * HP].reshape(nheads, HP, nkeys)
        rs = lax.rsqrt(jnp.sum(kraw * kraw, axis=1) / HP + EPS)
        lraw = r[nheads * HP:nheads * HP + nq].reshape(ntok, nheads, nkeys) * rs[None]
        return lraw.reshape(nq, nkeys) + _dot1(qr, krt_f32)

    def update(state, logits, lat_bf16):
        m_old, l_old, acc_old = state
        m_new = jnp.maximum(m_old, jnp.max(logits, axis=-1, keepdims=True))
        alpha = jnp.exp(m_old - m_new)
        p = jnp.exp(logits - m_new)
        l_new = alpha * l_old + jnp.sum(p, axis=-1, keepdims=True)
        acc_new = alpha * acc_old + jnp.dot(p.astype(BF16), lat_bf16, preferred_element_type=F32)
        return m_new, l_new, acc_new

    def load_group(g):
        pages = range(PPG * g, PPG * (g + 1))
        return jnp.concatenate([lat_refs[p][0, 0] for p in pages], axis=0).astype(BF16)

    gk = PPG * PAGE
    for g in range(PPS // PPG):
        lat_g = load_group(g)
        krt_g = jnp.concatenate([kr_refs[p][0, 0] for p in range(PPG * g, PPG * (g + 1))], axis=1)
        latb[gk * g:gk * (g + 1), :] = lat_g
        lg[:, gk * g:gk * (g + 1)] = key_logits(up_project(lat_g), krt_g)
    state = update((m_ref[...], l_ref[...], acc_ref[...]), lg[...], latb[...])
    m_ref[...], l_ref[...], acc_ref[...] = state

    @pl.when(kt == n_steps - 1)
    def _():
        latn = jnp.concatenate([latn_ref[0], jnp.zeros((PAGE - 8, C_LORA), F32)], axis=0).astype(BF16)
        logits = key_logits(up_project(latn), krn_ref[0])
        qtok = lax.broadcasted_iota(jnp.int32, (nq, PAGE), 0) // nheads
        slot = lax.broadcasted_iota(jnp.int32, (nq, PAGE), 1)
        _, l_fin, acc_fin = update(state, jnp.where(slot <= qtok, logits, -jnp.inf), latn)
        o_ref[0] = acc_fin / l_fin


def _mla_sample(page_table, lat_cache, kr_cache, ci, wukt, qabs, qr, lat_new8, kr_new8):
    nb, npages = page_table.shape
    n_steps = npages // PPS
    nq = qabs.shape[1]

    def page_spec(k, rows, width):
        return pl.BlockSpec((1, 1, rows, width), lambda b, t, pt: (ci, pt[b * npages + t * PPS + k], 0, 0))

    kr_cache_t = jnp.swapaxes(kr_cache, 2, 3)
    kr_new_t = jnp.pad(jnp.swapaxes(kr_new8, 1, 2), ((0, 0), (0, 0), (0, PAGE - kr_new8.shape[1])))
    grid_spec = pltpu.PrefetchScalarGridSpec(
        num_scalar_prefetch=1,
        grid=(nb, n_steps),
        in_specs=[page_spec(k, PAGE, C_LORA) for k in range(PPS)] + [page_spec(k, C_ROPE, PAGE) for k in range(PPS)]
                 + [pl.BlockSpec(wukt.shape, lambda b, t, pt: (0, 0)),
                    pl.BlockSpec((1, nq, C_LORA), lambda b, t, pt: (b, 0, 0)),
                    pl.BlockSpec((1, nq, C_ROPE), lambda b, t, pt: (b, 0, 0)),
                    pl.BlockSpec((1, 8, C_LORA), lambda b, t, pt: (b, 0, 0)),
                    pl.BlockSpec((1, C_ROPE, PAGE), lambda b, t, pt: (b, 0, 0))],
        out_specs=pl.BlockSpec((1, nq, C_LORA), lambda b, t, pt: (b, 0, 0)),
        scratch_shapes=[pltpu.VMEM((C_HEADS * HP + nq, C_LORA), BF16),
                        pltpu.VMEM((PPS * PAGE, C_LORA), BF16),
                        pltpu.VMEM((nq, PPS * PAGE), F32),
                        pltpu.VMEM((nq, 1), F32), pltpu.VMEM((nq, 1), F32), pltpu.VMEM((nq, C_LORA), F32)],
    )
    return pl.pallas_call(
        functools.partial(_mla_s_kernel, n_steps=n_steps),
        grid_spec=grid_spec,
        out_shape=jax.ShapeDtypeStruct((nb, nq, C_LORA), F32),
        compiler_params=_cparams(("arbitrary", "arbitrary")),
        name="mla_sample",
    )(page_table.reshape(-1), *([lat_cache] * PPS), *([kr_cache_t] * PPS), wukt, qabs, qr, lat_new8, kr_new_t)


def _mla_uv_kernel(c_ref, w_ref, o_ref):
    o_ref[...] = _dot1(c_ref[...], w_ref[0])


def _mla_uv(ctx, wuv3):
    rows = ctx.shape[0]
    return pl.pallas_call(
        _mla_uv_kernel,
        grid=(C_HEADS,),
        in_specs=[pl.BlockSpec((rows, C_LORA), lambda h: (0, h)),
                  pl.BlockSpec((1, C_LORA, HP), lambda h: (h, 0, 0))],
        out_specs=pl.BlockSpec((rows, HP), lambda h: (0, h)),
        out_shape=jax.ShapeDtypeStruct((rows, C_HEADS * HP), F32),
        compiler_params=_cparams(("parallel",)),
        name="mla_uv",
    )(ctx, wuv3)


def kernel(x_prompt, x_sample, state_hgrn, cache_swa_kv, cache_mla_latent, cache_mla_krope, page_table, c_prompt, c_sample, ada_w, ada_b, norm_g, rel_bias, ab_w_in, ab_w_out, hgrn_lb_logits, hgrn_norm_g, swa_q_g, swa_k_g, mla_w_in, mla_q_a_g, mla_kv_a_g, mla_w_uq, mla_w_uk, mla_w_uv, mla_qn_g, mla_qr_g, mla_kn_g, mla_kr_g, mla_w_o, moe_router_w, moe_router_b, moe_w_gu, moe_b_gu, moe_w_dn, moe_b_dn):
    nb, seq, _ = x_prompt.shape
    nsb, nst, _ = x_sample.shape
    tp, ts = nb * seq, nsb * nst
    assert ts == TM and seq % TM == 0 and nst <= 4
    npt, tpb = tp // TM, seq // TM
    x = jnp.concatenate([x_prompt.reshape(tp, D), x_sample.reshape(ts, D)], axis=0).astype(F32)

    nc = nb + nsb
    c_all = jnp.concatenate([c_prompt, c_sample, jnp.zeros((-nc % 8, D), c_prompt.dtype)], axis=0).astype(F32)
    mods = _adaln(c_all, ada_w, ada_b)

    def mod(s):
        return mods[s, 0:nb].reshape(nb, 1, 3 * D), jnp.repeat(mods[s, nb:nc], nst, axis=0)

    pm, sm = mod(0)
    z = _normmod_mm(x, norm_g[0, 0], pm, sm, ab_w_in[0].astype(BF16), 1024, npt, tpb)
    zs3 = z[tp:].reshape(nsb, nst, z.shape[1])
    oa_p, hg_p = _hgrn_prompt(z, hgrn_lb_logits, hgrn_norm_g[0], nb, seq, 0)
    oa_s, hg_s = _hgrn_sample(zs3, hgrn_lb_logits, hgrn_norm_g[0], state_hgrn, 0)
    ob_p, k_p, v_p = _dilated_prompt(z, swa_q_g[0], swa_k_g[0], _band_tables(rel_bias), nb, seq)
    buf_len = cache_swa_kv.shape[2]
    ob_s, k_s, v_s = _dilated_sample(zs3, swa_q_g[0], swa_k_g[0], cache_swa_kv[0], _sample_tables(rel_bias, nst, buf_len))
    w_out = ab_w_out[0].astype(BF16)
    h = _proj_residual([oa_p, ob_p], [oa_s.reshape(ts, 1024), ob_s.reshape(ts, 1024)],
                       [w_out[:1024], w_out[1024:]], x, pm, sm, npt, tpb)
    pm, sm = mod(1)
    h = _moe_layer(h, norm_g[0, 1], pm, sm, 0, moe_router_w, moe_router_b, moe_w_gu, moe_b_gu, moe_w_dn, moe_b_dn, npt, tpb)

    pm, sm = mod(2)
    w_in1 = jnp.pad(mla_w_in[0], ((0, 0), (0, HP - C_ROPE))).astype(BF16)
    z1 = _normmod_mm(h, norm_g[1, 0], pm, sm, w_in1, w_in1.shape[1], npt, tpb)
    past = page_table.shape[1] * PAGE
    pos = jnp.concatenate([jnp.tile(jnp.arange(seq), nb), jnp.tile(past + jnp.arange(nst), nsb)])
    cos, sin = _rope_tables(pos)
    wq = mla_w_uq[0]
    wn = wq[:, :, :C_NOPE].reshape(C_LORA, C_HEADS * HP).astype(BF16)
    wr = jnp.pad(wq[:, :, C_NOPE:], ((0, 0), (0, 0), (0, HP - C_ROPE))).reshape(C_LORA, C_HEADS * HP).astype(BF16)
    wuk = mla_w_uk[0].reshape(C_LORA, C_HEADS * HP).astype(BF16)
    wuv = mla_w_uv[0].reshape(C_LORA, C_HEADS * HP).astype(BF16)
    lat, kr, krb, qn, qr, qabs = _mla_prep(z1, mla_q_a_g[0], mla_kv_a_g[0], mla_kr_g[0], mla_qn_g[0], mla_qr_g[0],
                                           mla_kn_g[0], wn, wr, wuk, cos, sin, npt)
    kn, v = _mla_kv(lat, wuk, wuv, mla_kn_g[0], tp)
    ctx_p = _mla_flash(qn, qr, kn, krb, v, nb, seq)
    qabs3 = qabs.reshape(nsb, nst * C_HEADS, C_LORA)
    qr_s = qr[tp:].reshape(ts, C_HEADS, HP)[:, :, :C_ROPE].reshape(nsb, nst * C_HEADS, C_ROPE)
    lat_new8 = jnp.pad(lat[tp:].reshape(nsb, nst, C_LORA), ((0, 0), (0, 8 - nst), (0, 0)))
    kr_new8 = jnp.pad(kr[tp:].reshape(nsb, nst, C_ROPE), ((0, 0), (0, 8 - nst), (0, 0)))
    ctx_s = _mla_sample(page_table, cache_mla_latent, cache_mla_krope, 0, wuk.T, qabs3, qr_s, lat_new8, kr_new8)
    wuv3 = jnp.transpose(mla_w_uv[0], (1, 0, 2)).astype(BF16)
    o_s = _mla_uv(ctx_s.reshape(ts, C_HEADS * C_LORA), wuv3)
    h = _proj_residual([ctx_p], [o_s], [mla_w_o[0].astype(BF16)], h, pm, sm, npt, tpb)
    pm, sm = mod(3)
    h = _moe_layer(h, norm_g[1, 1], pm, sm, 1, moe_router_w, moe_router_b, moe_w_gu, moe_b_gu, moe_w_dn, moe_b_dn, npt, tpb)

    y_prompt = h[:tp].reshape(nb, seq, D).astype(x_prompt.dtype)
    y_sample = h[tp:].reshape(nsb, nst, D).astype(x_sample.dtype)
    hgrn_prompt = hg_p[None].astype(state_hgrn.dtype)
    hgrn_sample = hg_s.astype(state_hgrn.dtype)
    kv = lambda k, v, b, t: jnp.stack([k.reshape(b, t, 8, 128), v.reshape(b, t, 8, 128)], axis=2)[None]
    swa_prompt = kv(k_p, v_p, nb, seq).astype(cache_swa_kv.dtype)
    swa_sample = kv(k_s, v_s, nsb, nst).astype(cache_swa_kv.dtype)
    lat_prompt = lat[:tp].reshape(1, nb, seq, C_LORA).astype(cache_mla_latent.dtype)
    lat_sample = lat[tp:].reshape(1, nsb, nst, C_LORA).astype(cache_mla_latent.dtype)
    krope_prompt = kr[:tp].reshape(1, nb, seq, C_ROPE).astype(cache_mla_krope.dtype)
    krope_sample = kr[tp:].reshape(1, nsb, nst, C_ROPE).astype(cache_mla_krope.dtype)
    return (y_prompt, y_sample, hgrn_prompt, hgrn_sample, swa_prompt, swa_sample,
            lat_prompt, lat_sample, krope_prompt, krope_sample)
```

```python
import functools
import math

import numpy as np
import jax
import jax.numpy as jnp
from jax import lax
from jax.experimental import pallas as pl
from jax.experimental.pallas import tpu as pltpu

F32 = jnp.float32
BF16 = jnp.bfloat16
EPS = 1e-6

D = 2048
TM = 512
N_EXPERTS = 32
TOP_K = 4
D_EXPERT = 2048
SWIGLU_LIMIT = 7.0
SWIGLU_ALPHA = 1.702
PAGE = 128
V7X_VMEM_LIMIT = 56 * 1024 * 1024

NN = ((1,), (0,))
NT = ((1,), (1,))
TN = ((0,), (0,))


def _cparams(sem, vmem=V7X_VMEM_LIMIT):
    return pltpu.CompilerParams(dimension_semantics=sem, vmem_limit_bytes=vmem)


def _dot1(a, b, dims=NN):
    return lax.dot_general(a.astype(BF16), b.astype(BF16), (dims, ((), ())), preferred_element_type=F32)


def _split2(x):
    hi = x.astype(BF16)
    lo = (x - hi.astype(F32)).astype(BF16)
    return hi, lo


def _split3(x):
    hi = x.astype(BF16)
    r = x - hi.astype(F32)
    mid = r.astype(BF16)
    lo = (r - mid.astype(F32)).astype(BF16)
    return hi, mid, lo


def _dot_exact_rhs(a01, x):
    hi, mid, lo = _split3(x)
    f = lambda p: lax.dot_general(a01, p, (NN, ((), ())), preferred_element_type=F32)
    return f(hi) + f(mid) + f(lo)


def _dot3(a, b, dims=NN):
    ah, al = _split2(a)
    bh, bl = _split2(b)
    f = lambda p, q: lax.dot_general(p, q, (dims, ((), ())), preferred_element_type=F32)
    return f(ah, bh) + f(ah, bl) + f(al, bh)


def _sigmoid(x):
    return 1.0 / (1.0 + jnp.exp(-x))


def _rms(x, g):
    return x * lax.rsqrt(jnp.mean(x * x, axis=-1, keepdims=True) + EPS) * g


def _ada_kernel(c_ref, w_ref, b_ref, o_ref):
    c = c_ref[...]
    s = c * _sigmoid(c)
    o_ref[0] = _dot1(s, w_ref[0, 0]) + b_ref[0, 0]


def _adaln(c_all, ada_w, ada_b):
    r = c_all.shape[0]
    tn = 1024
    depth = ada_w.shape[0]
    b4 = ada_b.reshape(depth, 2, 1, 3 * D)
    return pl.pallas_call(
        _ada_kernel,
        grid=(depth * 2, 3 * D // tn),
        in_specs=[pl.BlockSpec((r, D), lambda s, j: (0, 0)),
                  pl.BlockSpec((1, 1, D, tn), lambda s, j: (s // 2, s % 2, 0, j)),
                  pl.BlockSpec((1, 1, 1, tn), lambda s, j: (s // 2, s % 2, 0, j))],
        out_specs=pl.BlockSpec((1, r, tn), lambda s, j: (s, 0, j)),
        out_shape=jax.ShapeDtypeStruct((depth * 2, r, 3 * D), F32),
        compiler_params=_cparams(("parallel", "parallel")),
        name="adaln",
    )(c_all, ada_w, b4)


def _pick(is_prompt, p_ref, s_ref):
    return jnp.where(is_prompt, p_ref[0], s_ref[...])


def _mod_specs(npt, rows_per_batch_tiles, col, tn=D, with_j=False):
    if with_j:
        pm = pl.BlockSpec((1, 1, tn), lambda i, j: (jnp.minimum(i // rows_per_batch_tiles, npt // rows_per_batch_tiles - 1), 0, col * (D // tn) + j))
        sm = pl.BlockSpec((TM, tn), lambda i, j: (0, col * (D // tn) + j))
    else:
        pm = pl.BlockSpec((1, 1, tn), lambda i, *_: (jnp.minimum(i // rows_per_batch_tiles, npt // rows_per_batch_tiles - 1), 0, col))
        sm = pl.BlockSpec((TM, tn), lambda i, *_: (0, col))
    return pm, sm


def _nm_mm_kernel(x_ref, g_ref, psh, psc, ssh, ssc, w_ref, o_ref, xn_ref, *, npt):
    i = pl.program_id(0)
    j = pl.program_id(1)

    @pl.when(j == 0)
    def _():
        xr = _rms(x_ref[...], g_ref[...])
        sc = _pick(i < npt, psc, ssc)
        sh = _pick(i < npt, psh, ssh)
        xn_ref[...] = (xr * (1.0 + sc) + sh).astype(BF16)

    o_ref[...] = jnp.dot(xn_ref[...], w_ref[...], preferred_element_type=F32)


def _normmod_mm(x, g, pmod, smod, w_bf16, tn, npt, tiles_per_batch):
    t = x.shape[0]
    n = w_bf16.shape[1]
    psh, ssh = _mod_specs(npt, tiles_per_batch, 0)
    psc, ssc = _mod_specs(npt, tiles_per_batch, 1)
    return pl.pallas_call(
        functools.partial(_nm_mm_kernel, npt=npt),
        grid=(t // TM, n // tn),
        in_specs=[pl.BlockSpec((TM, D), lambda i, j: (i, 0)),
                  pl.BlockSpec((1, D), lambda i, j: (0, 0)),
                  psh, psc, ssh, ssc,
                  pl.BlockSpec((D, tn), lambda i, j: (0, j))],
        out_specs=pl.BlockSpec((TM, tn), lambda i, j: (i, j)),
        out_shape=jax.ShapeDtypeStruct((t, n), F32),
        scratch_shapes=[pltpu.VMEM((TM, D), BF16)],
        compiler_params=_cparams(("parallel", "arbitrary")),
        name="normmod_mm",
    )(x, g.reshape(1, D), pmod, pmod, smod, smod, w_bf16)


def _proj_res_kernel(*refs, npt, nk):
    xp = refs[0:nk]
    xs = refs[nk:2 * nk]
    ws = refs[2 * nk:3 * nk]
    h_ref, pg, sg, o_ref = refs[3 * nk:3 * nk + 4]
    xb = refs[3 * nk + 4:]
    i = pl.program_id(0)
    j = pl.program_id(1)

    @pl.when(jnp.logical_and(j == 0, i < npt))
    def _():
        for k in range(nk):
            xb[k][...] = xp[k][...].astype(BF16)

    @pl.when(jnp.logical_and(j == 0, i >= npt))
    def _():
        for k in range(nk):
            xb[k][...] = xs[k][...].astype(BF16)

    acc = jnp.dot(xb[0][...], ws[0][...], preferred_element_type=F32)
    for k in range(1, nk):
        acc = acc + jnp.dot(xb[k][...], ws[k][...], preferred_element_type=F32)
    gate = _pick(i < npt, pg, sg)
    o_ref[...] = h_ref[...] + gate * acc


def _proj_residual(xps, xss, ws, h, pmod, smod, npt, tiles_per_batch, tn=1024):
    nk = len(ws)
    t = h.shape[0]
    ks = [w.shape[0] for w in ws]
    pg, sg = _mod_specs(npt, tiles_per_batch, 2, tn=tn, with_j=True)
    in_specs = ([pl.BlockSpec((TM, k), lambda i, j: (jnp.minimum(i, npt - 1), 0)) for k in ks]
                + [pl.BlockSpec((TM, k), lambda i, j: (0, 0)) for k in ks]
                + [pl.BlockSpec((k, tn), lambda i, j: (0, j)) for k in ks]
                + [pl.BlockSpec((TM, tn), lambda i, j: (i, j)), pg, sg])
    return pl.pallas_call(
        functools.partial(_proj_res_kernel, npt=npt, nk=nk),
        grid=(t // TM, D // tn),
        in_specs=in_specs,
        out_specs=pl.BlockSpec((TM, tn), lambda i, j: (i, j)),
        out_shape=jax.ShapeDtypeStruct((t, D), F32),
        scratch_shapes=[pltpu.VMEM((TM, k), BF16) for k in ks],
        compiler_params=_cparams(("parallel", "arbitrary")),
        name="proj_residual",
    )(*xps, *xss, *ws, h, pmod, smod)


HG_C = 64
HG_B = 16
HG_STEP = 256


def _lower_bound(lbl, layer):
    e = jnp.exp(lbl - jnp.max(lbl, axis=0, keepdims=True))
    return jnp.sum(e[0:layer + 1], axis=0, keepdims=True) / jnp.sum(e, axis=0, keepdims=True)


def _hgrn_chunk(q, af, v, st, lb, tri):
    c, dk = q.shape
    nb = c // HG_B
    f = lb + (1.0 - lb) * _sigmoid(af)
    g = jnp.log(f)
    k = 1.0 - f
    cum = _dot_exact_rhs(tri, g)
    o = _dot1(q * jnp.exp(cum), st, NT)
    q3 = q.reshape(nb, HG_B, dk)
    k3 = k.reshape(nb, HG_B, dk)
    c3 = cum.reshape(nb, HG_B, dk)
    tpos = lax.broadcasted_iota(jnp.int32, (nb, HG_B, dk), 1)
    lane3 = lax.broadcasted_iota(jnp.int32, (nb, HG_B, dk), 2)
    dmat = jnp.zeros((nb, HG_B, dk), F32)
    for s in range(HG_B):
        rel = c3 - c3[:, s:s + 1, :]
        a = q3 * k3[:, s:s + 1, :] * jnp.exp(jnp.where(tpos >= s, rel, -jnp.inf))
        col = jnp.sum(a, axis=-1, keepdims=True)
        dmat = jnp.where(lane3 == s, col, dmat)
    lane_c = lax.broadcasted_iota(jnp.int32, (HG_B, c), 1)
    rows = []
    for blk in range(nb):
        r0 = HG_B * blk
        d_blk = dmat[blk]
        if blk == 0:
            rows.append(d_blk[:, :c])
            continue
        d_blk = pltpu.roll(d_blk, r0, axis=1)[:, :c]
        cb = cum[r0 - 1:r0, :]
        qs = q[r0:r0 + HG_B] * jnp.exp(cum[r0:r0 + HG_B] - cb)
        ks = k * jnp.exp(jnp.minimum(cb - cum, 0.0))
        s_off = _dot1(qs, ks, NT)
        rows.append(jnp.where(lane_c < r0, s_off, 0.0) + d_blk)
    scores = jnp.concatenate(rows, axis=0)
    o = o + _dot1(scores, v)
    last = cum[c - 1:c, :]
    st_new = jnp.exp(last) * st + _dot1(v, k * jnp.exp(last - cum), TN)
    return o, st_new


def _hgrn_p_kernel(q_ref, f_ref, v_ref, g_ref, lbl_ref, gn_ref, o_ref, s_ref, st_ref, *, layer, nsteps):
    tb = pl.program_id(2)

    @pl.when(tb == 0)
    def _():
        st_ref[...] = jnp.zeros_like(st_ref)

    lb = _lower_bound(lbl_ref[...], layer)
    ri = lax.broadcasted_iota(jnp.int32, (HG_C, HG_C), 0)
    ci = lax.broadcasted_iota(jnp.int32, (HG_C, HG_C), 1)
    tri = jnp.where(ri >= ci, 1.0, 0.0).astype(BF16)
    st = st_ref[...]
    for c in range(HG_STEP // HG_C):
        sl = slice(HG_C * c, HG_C * (c + 1))
        o, st = _hgrn_chunk(q_ref[sl, :], f_ref[sl, :], v_ref[sl, :], st, lb, tri)
        o_ref[sl, :] = (_rms(o, gn_ref[...]) * _sigmoid(g_ref[sl, :])).astype(BF16)
    st_ref[...] = st

    @pl.when(tb == nsteps - 1)
    def _():
        s_ref[0, 0] = st.T


def _hgrn_prompt(z, lb_logits, gn, nb, seq, layer):
    nh = 8
    nsteps = seq // HG_STEP
    zspec = lambda c0: pl.BlockSpec((HG_STEP, 128), lambda b, h, t: (b * nsteps + t, c0 + h))
    return pl.pallas_call(
        functools.partial(_hgrn_p_kernel, layer=layer, nsteps=nsteps),
        grid=(nb, nh, nsteps),
        in_specs=[zspec(0), zspec(8), zspec(16), zspec(24),
                  pl.BlockSpec((lb_logits.shape[0], 128), lambda b, h, t: (0, h)),
                  pl.BlockSpec((1, 128), lambda b, h, t: (0, 0))],
        out_specs=[pl.BlockSpec((HG_STEP, 128), lambda b, h, t: (b * nsteps + t, h)),
                   pl.BlockSpec((1, 1, 128, 128), lambda b, h, t: (b, h, 0, 0))],
        out_shape=[jax.ShapeDtypeStruct((nb * seq, nh * 128), BF16),
                   jax.ShapeDtypeStruct((nb, nh, 128, 128), F32)],
        scratch_shapes=[pltpu.VMEM((128, 128), F32)],
        compiler_params=_cparams(("parallel", "parallel", "arbitrary")),
        name="hgrn_prompt",
    )(z, z, z, z, lb_logits, gn.reshape(1, 128))


def _hgrn_s_kernel(q_ref, f_ref, v_ref, g_ref, lbl_ref, gn_ref, s0_ref, o_ref, s_ref, *, layer):
    nt = q_ref.shape[1]
    lb_all = _lower_bound(lbl_ref[...], layer)
    tpos = lax.broadcasted_iota(jnp.int32, (nt, 128), 0)
    outs = []
    for h in range(8):
        sl = slice(128 * h, 128 * (h + 1))
        q = q_ref[0, :, sl]
        v = v_ref[0, :, sl]
        lb = lb_all[:, sl]
        f = lb + (1.0 - lb) * _sigmoid(f_ref[0, :, sl])
        g = jnp.log(f)
        k = 1.0 - f
        crow = [g[0:1]]
        for t in range(1, nt):
            crow.append(crow[-1] + g[t:t + 1])
        cum = jnp.concatenate(crow, axis=0)
        last = crow[-1]
        s0 = s0_ref[0, 0, h]
        qe = jnp.concatenate([q * jnp.exp(cum), jnp.zeros((8 - nt, 128), F32)], axis=0)
        o = _dot1(qe, s0)[0:nt]
        for s in range(nt):
            a = q * k[s:s + 1] * jnp.exp(jnp.where(tpos >= s, cum - cum[s:s + 1], -jnp.inf))
            o = o + jnp.sum(a, axis=-1, keepdims=True) * v[s:s + 1]
        kp = k * jnp.exp(last - cum)
        w8 = jnp.concatenate([jnp.exp(last), kp, jnp.zeros((8 - 1 - nt, 128), F32)], axis=0)
        wt = jnp.concatenate([w8, jnp.zeros((120, 128), F32)], axis=0).T
        s_new = wt[:, 0:1] * s0
        for s in range(nt):
            s_new = s_new + wt[:, 1 + s:2 + s] * v[s:s + 1]
        s_ref[0, 0, h] = s_new
        outs.append(_rms(o, gn_ref[...]) * _sigmoid(g_ref[0, :, sl]))
    o_ref[0] = jnp.concatenate(outs, axis=1)


def _hgrn_sample(zs3, lb_logits, gn, state, layer):
    nb, nt, _ = zs3.shape
    zspec = lambda c: pl.BlockSpec((1, nt, 1024), lambda b: (b, 0, c))
    sspec = pl.BlockSpec((1, 1, 8, 128, 128), lambda b: (0, b, 0, 0, 0))
    return pl.pallas_call(
        functools.partial(_hgrn_s_kernel, layer=layer),
        grid=(nb,),
        in_specs=[zspec(0), zspec(1), zspec(2), zspec(3),
                  pl.BlockSpec(lb_logits.shape, lambda b: (0, 0)),
                  pl.BlockSpec((1, 128), lambda b: (0, 0)),
                  sspec],
        out_specs=[pl.BlockSpec((1, nt, 1024), lambda b: (b, 0, 0)), sspec],
        out_shape=[jax.ShapeDtypeStruct((nb, nt, 1024), F32),
                   jax.ShapeDtypeStruct(state.shape, F32)],
        compiler_params=_cparams(("parallel",)),
        name="hgrn_sample",
    )(zs3, zs3, zs3, zs3, lb_logits, gn.reshape(1, 128), state)


DILATED = ((128, 1), (512, 4), (2048, 16))
REL_BUCKETS = 32
REL_MAX_DISTANCE = 2048
ATT_SCALE = 128 ** -0.5
QB = 128


def _rel_bucket_np(dist):
    exact = REL_BUCKETS // 2
    d = np.asarray(dist)
    far = exact + (np.log(np.maximum(d, 1).astype(np.float32) / np.float32(exact))
                   / np.float32(math.log(REL_MAX_DISTANCE / exact)) * np.float32(REL_BUCKETS - exact)).astype(np.int32)
    return np.where(d < exact, d, np.minimum(far, REL_BUCKETS - 1)).astype(np.int32)


def _band_tables(rel_bias):
    rb = rel_bias.astype(F32)
    tabs = []
    for (window, dil), width in zip(DILATED, (2 * QB, 2 * QB, QB)):
        period = width + QB
        i = np.arange(period + 1)
        k = np.where(i < width, i, i - (period + 1))
        steps = (width - QB) - k
        valid = (steps >= 0) & (steps <= window // dil)
        bucket = _rel_bucket_np(np.clip(steps, 0, window // dil) * dil)
        u = jnp.where(jnp.asarray(valid)[None, :], rb[jnp.asarray(bucket)].T, -jnp.inf)
        flat = jnp.tile(u, (1, QB))[:, :QB * period]
        tabs.append(flat.reshape(rb.shape[1], QB, period)[:, :, :width])
    return tabs


def _soft_block(qb, kw, vw, bias):
    l = _dot1(qb, kw, NT) * ATT_SCALE + bias
    m = jnp.max(l, axis=-1, keepdims=True)
    p = jnp.exp(l - m)
    return m, jnp.sum(p, axis=-1, keepdims=True), _dot1(p, vw)


def _dil_p_kernel(q_ref, k_ref, v_ref, gq_ref, gk_ref, b1_ref, b2_ref, b3_ref, o_ref, ko_ref, vo_ref,
                  qs, ks, vs, a1, a2, a3, m1, m2, m3, l1, l2, l3):
    seq = q_ref.shape[0]
    kn = _rms(k_ref[...], gk_ref[...])
    v = v_ref[...]
    ko_ref[...] = kn
    vo_ref[...] = v
    qs[...] = _rms(q_ref[...], gq_ref[...])
    ks[...] = kn
    vs[...] = v
    zpad = jnp.zeros((QB, 128), F32)
    first_cols = lax.broadcasted_iota(jnp.int32, (QB, 2 * QB), 1) >= QB

    def banded(qsub, ksub, vsub, bias, n_blocks, store):
        for i in range(n_blocks):
            qb = qsub(i)
            if i == 0:
                kw = jnp.concatenate([zpad, ksub(0)], axis=0)
                vw = jnp.concatenate([zpad, vsub(0)], axis=0)
                bb = jnp.where(first_cols, bias, -jnp.inf)
            else:
                kw = jnp.concatenate([ksub(i - 1), ksub(i)], axis=0)
                vw = jnp.concatenate([vsub(i - 1), vsub(i)], axis=0)
                bb = bias
            store(i, *_soft_block(qb, kw, vw, bb))

    def st1(i, m, l, a):
        sl = slice(QB * i, QB * (i + 1))
        m1[sl, :] = m
        l1[sl, :] = l
        a1[sl, :] = a
    blk = lambda ref: (lambda i: ref[QB * i:QB * (i + 1), :])
    banded(blk(qs), blk(ks), blk(vs), b1_ref[0], seq // QB, st1)

    d2 = DILATED[1][1]
    for r in range(d2):
        sub = lambda ref: (lambda i: ref[pl.ds(r + d2 * QB * i, QB, stride=d2), :])

        def st2(i, m, l, a):
            idx = pl.ds(r + d2 * QB * i, QB, stride=d2)
            m2[idx, :] = m
            l2[idx, :] = l
            a2[idx, :] = a
        banded(sub(qs), sub(ks), sub(vs), b2_ref[0], seq // (QB * d2), st2)

    d3 = DILATED[2][1]
    for r in range(d3):
        idx = pl.ds(r, QB, stride=d3)
        m, l, a = _soft_block(qs[idx, :], ks[idx, :], vs[idx, :], b3_ref[0])
        m3[idx, :] = m
        l3[idx, :] = l
        a3[idx, :] = a

    mx = jnp.maximum(jnp.maximum(m1[...], m2[...]), m3[...])
    w1 = jnp.exp(m1[...] - mx)
    w2 = jnp.exp(m2[...] - mx)
    w3 = jnp.exp(m3[...] - mx)
    num = w1 * a1[...] + w2 * a2[...] + w3 * a3[...]
    den = w1 * l1[...] + w2 * l2[...] + w3 * l3[...]
    o_ref[...] = (num / den).astype(BF16)


def _dilated_prompt(z, gq, gk, tabs, nb, seq):
    nh = 8
    zspec = lambda c0: pl.BlockSpec((seq, 128), lambda b, h: (b, c0 + h))
    gspec = pl.BlockSpec((1, 128), lambda b, h: (0, 0))
    tspec = lambda w: pl.BlockSpec((1, QB, w), lambda b, h: (h, 0, 0))
    ospec = pl.BlockSpec((seq, 128), lambda b, h: (b, h))
    big = pltpu.VMEM((seq, 128), F32)
    col = pltpu.VMEM((seq, 1), F32)
    return pl.pallas_call(
        _dil_p_kernel,
        grid=(nb, nh),
        in_specs=[zspec(32), zspec(40), zspec(48), gspec, gspec, tspec(2 * QB), tspec(2 * QB), tspec(QB)],
        out_specs=[ospec, ospec, ospec],
        out_shape=[jax.ShapeDtypeStruct((nb * seq, nh * 128), BF16),
                   jax.ShapeDtypeStruct((nb * seq, nh * 128), F32),
                   jax.ShapeDtypeStruct((nb * seq, nh * 128), F32)],
        scratch_shapes=[big, big, big, big, big, big, col, col, col, col, col, col],
        compiler_params=_cparams(("parallel", "parallel")),
        name="dilated_prompt",
    )(z, z, z, gq.reshape(1, 128), gk.reshape(1, 128), *tabs)


def _sample_tables(rel_bias, nt, buf_len):
    rb = rel_bias.astype(F32)
    nh = rb.shape[1]
    w2 = DILATED[1][0]

    def count(delta, patterns):
        c = np.zeros(delta.shape, np.int32)
        for window, dil in patterns:
            c += ((delta >= 0) & (delta <= window) & (delta % dil == 0)).astype(np.int32)
        return c

    qpos = buf_len + np.arange(8)
    real_q = (np.arange(8) < nt)
    da = qpos[None, :] - (buf_len - w2 + np.arange(w2))[:, None]
    ca = count(da, DILATED[:2]) * real_q[None, :]
    nj = buf_len // 16
    rows_b = (16 * np.arange(nj)[None, :] + np.arange(nt)[:, None]).reshape(-1)
    db = qpos[None, :] - rows_b[:, None]
    cb = count(db, DILATED[2:]) * real_q[None, :]
    dn = qpos[None, :] - qpos[:, None]
    cn = count(dn, DILATED) * real_q[None, :] * real_q[:, None]

    def tables(delta, cnt):
        bucket = _rel_bucket_np(np.clip(delta, 0, REL_MAX_DISTANCE))
        onehot = (bucket.reshape(-1, 1) == np.arange(REL_BUCKETS)[None, :]).astype(np.float32)
        bias = jnp.dot(jnp.asarray(onehot), rb, precision=lax.Precision.HIGHEST).reshape(*delta.shape, nh)
        bias = jnp.where(jnp.asarray((cnt > 0) | ~real_q[None, :])[:, :, None], bias, -jnp.inf)
        bias = jnp.transpose(bias, (0, 2, 1)).reshape(delta.shape[0], nh * 8)
        mult = np.repeat(cnt[:, None, :], nh, axis=1).reshape(delta.shape[0], nh * 8).astype(np.float32)
        return bias, jnp.asarray(mult)

    return tables(da, ca) + tables(db, cb) + tables(dn, cn)


def _dil_s_kernel(q_ref, k_ref, v_ref, gq_ref, gk_ref, ca_ref, cb_ref,
                  ba_ref, ma_ref, bb_ref, mb_ref, bn_ref, mn_ref, o_ref, ko_ref, vo_ref):
    nt = q_ref.shape[1]
    nh = 8
    gq = gq_ref[...]
    gk = gk_ref[...]
    qn = jnp.concatenate([_rms(q_ref[0, :, 128 * h:128 * (h + 1)], gq) for h in range(nh)], axis=1)
    kn = jnp.concatenate([_rms(k_ref[0, :, 128 * h:128 * (h + 1)], gk) for h in range(nh)], axis=1)
    vn = v_ref[0]
    ko_ref[0] = kn
    vo_ref[0] = vn
    pad = jnp.zeros((8 - nt, nh * 128), F32)
    q8 = jnp.concatenate([qn, pad], axis=0)
    k8 = jnp.concatenate([kn, pad], axis=0)
    v8 = jnp.concatenate([vn, pad], axis=0)
    lane_head = lax.broadcasted_iota(jnp.int32, (8, nh * 128), 1) // 128
    qbd = jnp.concatenate([jnp.where(lane_head == h, q8, 0.0) for h in range(nh)], axis=0).astype(BF16)

    def heads(ref2d, first, nrows, stride):
        return jnp.concatenate([ref2d[pl.ds(first + h, nrows, stride=stride), :] for h in range(nh)], axis=1)

    a2 = ca_ref.at[0]
    na = a2.shape[0] // (2 * nh)
    groups = [(heads(a2, 0, na, 2 * nh), heads(a2, nh, na, 2 * nh), ba_ref[...], ma_ref[...])]
    nj = cb_ref.shape[1]
    per_j = cb_ref.shape[2]
    b2 = cb_ref.reshape(nj * per_j, 128)
    for i in range(nt):
        groups.append((heads(b2, 2 * nh * i, nj, per_j), heads(b2, 2 * nh * i + nh, nj, per_j),
                       bb_ref[nj * i:nj * (i + 1), :], mb_ref[nj * i:nj * (i + 1), :]))
    groups.append((k8, v8, bn_ref[...], mn_ref[...]))

    logits = [_dot1(kk, qbd, NT) * ATT_SCALE + bias for kk, _, bias, _ in groups]
    mx = logits[0].max(axis=0, keepdims=True)
    for l in logits[1:]:
        mx = jnp.maximum(mx, l.max(axis=0, keepdims=True))
    ps = [mult * jnp.exp(l - mx) for l, (_, _, _, mult) in zip(logits, groups)]
    den = ps[0].sum(axis=0, keepdims=True)
    for p in ps[1:]:
        den = den + p.sum(axis=0, keepdims=True)
    inv = 1.0 / jnp.where(den > 0.0, den, 1.0)
    acc = None
    for p, (_, vv, _, _) in zip(ps, groups):
        part = _dot1(p * inv, vv, TN)
        acc = part if acc is None else acc + part
    o8 = jnp.concatenate([acc[8 * h:8 * (h + 1), 128 * h:128 * (h + 1)] for h in range(nh)], axis=1)
    o_ref[0] = o8[0:nt]


def _dilated_sample(zs3, gq, gk, cache, tabs):
    nb, nt, _ = zs3.shape
    buf_len = cache.shape[1]
    w2 = DILATED[1][0]
    d3 = DILATED[2][1]
    kvh = 2 * 8
    cache_a = cache.reshape(nb, buf_len * kvh, 128)
    cache_b = cache.reshape(nb, buf_len // d3, d3 * kvh, 128)
    zspec = lambda c: pl.BlockSpec((1, nt, 1024), lambda b: (b, 0, c))
    gspec = pl.BlockSpec((1, 128), lambda b: (0, 0))
    full = lambda a: pl.BlockSpec(a.shape, lambda b: (0, 0))
    ospec = pl.BlockSpec((1, nt, 1024), lambda b: (b, 0, 0))
    return pl.pallas_call(
        _dil_s_kernel,
        grid=(nb,),
        in_specs=[zspec(4), zspec(5), zspec(6), gspec, gspec,
                  pl.BlockSpec((1, w2 * kvh, 128), lambda b: (b, buf_len // w2 - 1, 0)),
                  pl.BlockSpec((1, buf_len // d3, nt * kvh, 128), lambda b: (b, 0, 0, 0))]
                 + [full(a) for a in tabs],
        out_specs=[ospec, ospec, ospec],
        out_shape=[jax.ShapeDtypeStruct((nb, nt, 1024), F32)] * 3,
        compiler_params=_cparams(("parallel",)),
        name="dilated_sample",
    )(zs3, zs3, zs3, gq.reshape(1, 128), gk.reshape(1, 128), cache_a, cache_b, *tabs)


def _route_kernel(x_ref, g_ref, psh, psc, ssh, ssc, rw_ref, rb_ref, u_ref, idx_ref, gate_ref, *, npt):
    i = pl.program_id(0)
    xr = _rms(x_ref[...], g_ref[...])
    u = xr * (1.0 + _pick(i < npt, psc, ssc)) + _pick(i < npt, psh, ssh)
    ub = u.astype(BF16).astype(F32)
    lo = lax.shift_right_logical(lax.bitcast_convert_type(ub[:, :D // 2], jnp.int32), 16)
    u_ref[...] = lax.bitcast_convert_type(ub[:, D // 2:], jnp.int32) | lo
    l = _dot3(u, rw_ref[...]) + rb_ref[...]
    lane = lax.broadcasted_iota(jnp.int32, l.shape, 1).astype(F32)
    out_lane = lax.broadcasted_iota(jnp.int32, (TM, 128), 1)
    vals, idxs = [], []
    for _ in range(TOP_K):
        m = jnp.max(l, axis=-1, keepdims=True)
        first = jnp.min(jnp.where(l == m, lane, float(N_EXPERTS)), axis=-1, keepdims=True)
        vals.append(m)
        idxs.append(first.astype(jnp.int32))
        l = jnp.where(lane == first, -jnp.inf, l)
    es = [jnp.exp(v - vals[0]) for v in vals]
    tot = es[0] + es[1] + es[2] + es[3]
    idx_out = jnp.zeros((TM, 128), jnp.int32)
    gate_out = jnp.zeros((TM, 128), F32)
    for k in range(TOP_K):
        idx_out = jnp.where(out_lane == k, idxs[k], idx_out)
        gate_out = jnp.where(out_lane == k, es[k] / tot, gate_out)
    idx_ref[...] = idx_out
    gate_ref[...] = gate_out


def _route(x, g, pmod, smod, rw, rb, npt, tiles_per_batch):
    t = x.shape[0]
    psh, ssh = _mod_specs(npt, tiles_per_batch, 0)
    psc, ssc = _mod_specs(npt, tiles_per_batch, 1)
    return pl.pallas_call(
        functools.partial(_route_kernel, npt=npt),
        grid=(t // TM,),
        in_specs=[pl.BlockSpec((TM, D), lambda i: (i, 0)),
                  pl.BlockSpec((1, D), lambda i: (0, 0)),
                  psh, psc, ssh, ssc,
                  pl.BlockSpec((D, N_EXPERTS), lambda i: (0, 0)),
                  pl.BlockSpec((1, N_EXPERTS), lambda i: (0, 0))],
        out_specs=[pl.BlockSpec((TM, D // 2), lambda i: (i, 0)),
                   pl.BlockSpec((TM, 128), lambda i: (i, 0)),
                   pl.BlockSpec((TM, 128), lambda i: (i, 0))],
        out_shape=[jax.ShapeDtypeStruct((t, D // 2), jnp.int32),
                   jax.ShapeDtypeStruct((t, 128), jnp.int32),
                   jax.ShapeDtypeStruct((t, 128), F32)],
        compiler_params=_cparams(("parallel",)),
        name="moe_route",
    )(x, g.reshape(1, D), pmod, pmod, smod, smod, rw, rb.reshape(1, N_EXPERTS))


MOE_TM = 1216
MOE_ROWS = (304, 608, MOE_TM)
MOE_TN = 512
MOE_TF = MOE_TN // 2


MOE_NG = 2 * D_EXPERT // MOE_TN
MOE_DN = 512
MOE_ND = D // MOE_DN


def _moe_kernel(te_ref, nu_ref, nv_ref, tok_ref, u_hbm, wgu_ref, bgu_ref, wdn_ref, bdn_ref, y_ref, xg, xb, hs, sem):
    t = pl.program_id(0)
    j = pl.program_id(1)
    nu = nu_ref[0]
    active = t < nu
    slot = t % 2
    nv = nv_ref[t]

    def rows8(n):
        return (n + 7) // 8 * 8

    def issue(tile, s):
        def body(g, carry):
            for i in range(8):
                r = g * 8 + i
                tok = tok_ref[tile * MOE_TM + r]
                pltpu.make_async_copy(u_hbm.at[pl.ds(tok, 1)], xg.at[s, pl.ds(r, 1)], sem.at[s]).start()
            return carry
        lax.fori_loop(0, rows8(nv_ref[tile]) // 8, body, 0)

    @pl.when(jnp.logical_and(t == 0, j == 0))
    def _():
        xg[...] = jnp.zeros_like(xg)
        issue(0, 0)

    @pl.when(jnp.logical_and(active, j == 0))
    def _():
        n8 = pl.multiple_of(rows8(nv), 8)
        pltpu.make_async_copy(u_hbm.at[pl.ds(0, n8)], xg.at[slot, pl.ds(0, n8)], sem.at[slot]).wait()
        w = xg[slot]
        xb[:, :D // 2] = lax.bitcast_convert_type(lax.shift_left(w, 16), F32).astype(BF16)
        xb[:, D // 2:] = lax.bitcast_convert_type(w & -65536, F32).astype(BF16)

    @pl.when(jnp.logical_and(t + 1 < nu, j == 1))
    def _():
        issue(t + 1, 1 - slot)

    def gate_up(rows):
        w = wgu_ref[0, 0].astype(BF16)
        ri = lax.broadcasted_iota(jnp.int32, (MOE_TN, MOE_TF), 0)
        ci = lax.broadcasted_iota(jnp.int32, (MOE_TN, MOE_TF), 1)
        pick_even = jnp.where(ri == 2 * ci, 1.0, 0.0).astype(BF16)
        parts = 2 if rows == MOE_TM else 1
        step = rows // parts
        gus = [jnp.dot(xb[step * p:step * (p + 1), :], w, preferred_element_type=F32) for p in range(parts)]
        for p, gu in enumerate(gus):
            gu = gu + bgu_ref[0, 0]
            up = pltpu.roll(gu, MOE_TN - 1, axis=1)
            gate = jnp.minimum(gu, SWIGLU_LIMIT)
            up = jnp.clip(up, -SWIGLU_LIMIT, SWIGLU_LIMIT)
            sig = 0.5 * jnp.tanh((0.5 * SWIGLU_ALPHA) * gate) + 0.5
            h = ((up + 1.0) * gate * sig).astype(BF16)
            hs[j, step * p:step * (p + 1), :] = jnp.dot(h, pick_even, preferred_element_type=F32).astype(BF16)

    def down(rows):
        acc = jnp.dot(hs[0, 0:rows, :], wdn_ref[0, 0, 0:MOE_TF, :].astype(BF16), preferred_element_type=F32)
        for c in range(1, MOE_NG):
            acc = acc + jnp.dot(hs[c, 0:rows, :], wdn_ref[0, 0, MOE_TF * c:MOE_TF * (c + 1), :].astype(BF16),
                                preferred_element_type=F32)
        y_ref[0:rows, :] = acc + bdn_ref[0, 0]
        if rows < MOE_TM:
            y_ref[rows:MOE_TM, :] = jnp.zeros((MOE_TM - rows, MOE_DN), F32)

    lower = 0
    for rows in MOE_ROWS:
        fits = jnp.logical_and(active, jnp.logical_and(nv > lower, nv <= rows))
        pl.when(jnp.logical_and(fits, j < MOE_NG))(functools.partial(gate_up, rows))
        pl.when(jnp.logical_and(fits, j >= MOE_NG))(functools.partial(down, rows))
        lower = rows

    @pl.when(jnp.logical_and(jnp.logical_not(active), j >= MOE_NG))
    def _():
        y_ref[...] = jnp.zeros_like(y_ref)


def _moe_experts(u, tile_e, n_used, n_valid, slot_tok, layer, w_gu, b_gu, w_dn, b_dn):
    n_tiles = tile_e.shape[0]
    nj = MOE_NG + MOE_ND
    jj = lambda t, j, nu: jnp.where(t < nu[0], j, nj - 1)
    jg = lambda t, j, nu: jnp.minimum(jj(t, j, nu), MOE_NG - 1)
    jd = lambda t, j, nu: jnp.maximum(jj(t, j, nu) - MOE_NG, 0)
    grid_spec = pltpu.PrefetchScalarGridSpec(
        num_scalar_prefetch=4,
        grid=(n_tiles, nj),
        in_specs=[pl.BlockSpec(memory_space=pl.ANY),
                  pl.BlockSpec((1, 1, D, MOE_TN), lambda t, j, te, nu, nv, tok: (layer, te[t], 0, jg(t, j, nu))),
                  pl.BlockSpec((1, 1, 1, MOE_TN), lambda t, j, te, nu, nv, tok: (layer, te[t], 0, jg(t, j, nu))),
                  pl.BlockSpec((1, 1, D_EXPERT, MOE_DN), lambda t, j, te, nu, nv, tok: (layer, te[t], 0, jd(t, j, nu))),
                  pl.BlockSpec((1, 1, 1, MOE_DN), lambda t, j, te, nu, nv, tok: (layer, te[t], 0, jd(t, j, nu)))],
        out_specs=pl.BlockSpec((MOE_TM, MOE_DN), lambda t, j, te, nu, nv, tok: (t, jnp.maximum(j - MOE_NG, 0))),
        scratch_shapes=[pltpu.VMEM((2, MOE_TM, D // 2), jnp.int32), pltpu.VMEM((MOE_TM, D), BF16),
                        pltpu.VMEM((MOE_NG, MOE_TM, MOE_TF), BF16), pltpu.SemaphoreType.DMA((2,))],
    )
    nl, ne = b_gu.shape[0], b_gu.shape[1]
    return pl.pallas_call(
        _moe_kernel,
        grid_spec=grid_spec,
        out_shape=jax.ShapeDtypeStruct((n_tiles * MOE_TM, D), F32),
        compiler_params=_cparams(("arbitrary", "arbitrary")),
        name="moe_experts",
    )(tile_e, n_used, n_valid, slot_tok, u, w_gu, b_gu.reshape(nl, ne, 1, 2 * D_EXPERT), w_dn, b_dn.reshape(nl, ne, 1, D))


CMB = 256


def _combine_kernel(slot_ref, h_ref, gate_ref, pg, sg, y_hbm, o_ref, buf, sem, *, npt_c):
    i = pl.program_id(0)

    def issue(r, carry):
        for k in range(TOP_K):
            s = slot_ref[(i * CMB + r) * TOP_K + k]
            pltpu.make_async_copy(y_hbm.at[pl.ds(s, 1)], buf.at[k, pl.ds(r, 1)], sem).start()
        return carry
    lax.fori_loop(0, CMB, issue, 0, unroll=4)
    for k in range(TOP_K):
        pltpu.make_async_copy(y_hbm.at[pl.ds(0, CMB)], buf.at[k], sem).wait()
    gates = gate_ref[...]
    mo = gates[:, 0:1] * buf[0]
    for k in range(1, TOP_K):
        mo = mo + gates[:, k:k + 1] * buf[k]
    o_ref[...] = h_ref[...] + _pick(i < npt_c, pg, sg) * mo


def _moe_combine(h, y, slot4, gate128, pmod, smod, npt_c, tiles_per_batch_c):
    t = h.shape[0]
    nsamp_tiles = (t // CMB) - npt_c
    grid_spec = pltpu.PrefetchScalarGridSpec(
        num_scalar_prefetch=1,
        grid=(t // CMB,),
        in_specs=[pl.BlockSpec((CMB, D), lambda i, s: (i, 0)),
                  pl.BlockSpec((CMB, 128), lambda i, s: (i, 0)),
                  pl.BlockSpec((1, 1, D), lambda i, s: (jnp.minimum(i // tiles_per_batch_c, npt_c // tiles_per_batch_c - 1), 0, 2)),
                  pl.BlockSpec((CMB, D), lambda i, s: (jnp.clip(i - npt_c, 0, nsamp_tiles - 1), 2)),
                  pl.BlockSpec(memory_space=pl.ANY)],
        out_specs=pl.BlockSpec((CMB, D), lambda i, s: (i, 0)),
        scratch_shapes=[pltpu.VMEM((TOP_K, CMB, D), F32), pltpu.SemaphoreType.DMA],
    )
    return pl.pallas_call(
        functools.partial(_combine_kernel, npt_c=npt_c),
        grid_spec=grid_spec,
        out_shape=jax.ShapeDtypeStruct((t, D), F32),
        compiler_params=_cparams(("arbitrary",)),
        name="moe_combine",
    )(slot4, h, gate128, pmod, smod, y)


def _moe_layer(h, g, pmod, smod, layer, rw, rb, w_gu, b_gu, w_dn, b_dn, npt, tiles_per_batch):
    t = h.shape[0]
    u, idx128, gate128 = _route(h, g, pmod, smod, rw[layer], rb[layer], npt, tiles_per_batch)
    idx4 = idx128[:, :TOP_K]
    onehot = idx4[:, :, None] == jnp.arange(N_EXPERTS, dtype=jnp.int32)[None, None, :]
    sel = jnp.any(onehot, axis=1).astype(jnp.int32)
    counts = jnp.sum(sel, axis=0)
    rank = jnp.cumsum(sel, axis=0) - 1
    padded = (counts + MOE_TM - 1) // MOE_TM * MOE_TM
    pad_end = jnp.cumsum(padded)
    pad_start = pad_end - padded
    n_tiles = -(-t * TOP_K // MOE_TM) + N_EXPERTS
    dest4 = jnp.sum(jnp.where(onehot, (pad_start[None, :] + rank)[:, None, :], 0), axis=2).astype(jnp.int32)
    tok = jnp.broadcast_to(jnp.arange(t, dtype=jnp.int32)[:, None], (t, TOP_K))
    slot_tok = jnp.zeros((n_tiles * MOE_TM,), jnp.int32).at[dest4.reshape(-1)].set(tok.reshape(-1))
    n_used = (pad_end[-1] // MOE_TM).astype(jnp.int32)
    tile_ids = jnp.arange(n_tiles, dtype=jnp.int32)
    tile_e = jnp.minimum(jnp.searchsorted(pad_end, jnp.minimum(tile_ids, n_used - 1) * MOE_TM, side='right'),
                         N_EXPERTS - 1).astype(jnp.int32)
    n_valid = jnp.where(tile_ids < n_used,
                        jnp.clip((pad_start + counts)[tile_e] - tile_ids * MOE_TM, 0, MOE_TM), 0).astype(jnp.int32)
    y = _moe_experts(u, tile_e, n_used.reshape(1), n_valid, slot_tok, layer, w_gu, b_gu, w_dn, b_dn)
    return _moe_combine(h, y, dest4.reshape(-1), gate128, pmod, smod,
                        npt * (TM // CMB), tiles_per_batch * (TM // CMB))


C_HEADS = 16
C_NOPE = 128
C_ROPE = 64
C_LORA = 512
ROPE_THETA = 10000.0
MLA_SCALE = (C_NOPE + C_ROPE) ** -0.5
HP = 128


def _rope_tables(pos):
    half = C_ROPE // 2
    inv = ROPE_THETA ** (-jnp.arange(half, dtype=F32) / half)
    ang = pos.astype(F32)[:, None] * inv[None, :]
    z = jnp.zeros_like(ang)
    cos = jnp.concatenate([jnp.cos(ang), jnp.cos(ang), z, z], axis=1)
    sin = jnp.concatenate([-jnp.sin(ang), jnp.sin(ang), z, z], axis=1)
    return cos, sin


def _rope128(x, cos, sin):
    n = x.shape[1]
    lane = lax.broadcasted_iota(jnp.int32, x.shape, 1) % HP
    partner = jnp.where(lane < C_ROPE // 2, pltpu.roll(x, n - C_ROPE // 2, axis=1), pltpu.roll(x, C_ROPE // 2, axis=1))
    return x * cos + partner * sin


def _mla_prep_kernel(z_ref, gq_ref, gkv_ref, gkr_ref, gqn_ref, gqr_ref, gkn_ref, wn_ref, wr_ref, wuk_ref,
                     cos_ref, sin_ref, lat_ref, kr_ref, krb_ref, qn_ref, qr_ref, qabs_ref, qg_ref, *, npt):
    i = pl.program_id(0)
    z = z_ref[...]
    cos = cos_ref[...]
    sin = sin_ref[...]
    lat_ref[...] = _rms(z[:, C_LORA:2 * C_LORA], gkv_ref[...])
    k128 = z[:, 2 * C_LORA:2 * C_LORA + HP]
    k128 = k128 * lax.rsqrt(jnp.sum(k128 * k128, axis=-1, keepdims=True) / C_ROPE + EPS) * gkr_ref[...]
    k128 = _rope128(k128, cos, sin)
    kr_ref[...] = k128[:, 0:C_ROPE]
    krb_ref[...] = k128.astype(BF16)
    cqn = _rms(z[:, 0:C_LORA], gq_ref[...]).astype(BF16)
    qn_all = jnp.dot(cqn, wn_ref[...], preferred_element_type=F32)
    qr_all = jnp.dot(cqn, wr_ref[...], preferred_element_type=F32)
    for h in range(C_HEADS):
        sl = slice(HP * h, HP * (h + 1))
        qn = _rms(qn_all[:, sl], gqn_ref[...]) * MLA_SCALE
        qn_ref[:, sl] = qn.astype(BF16)
        qg_ref[:, sl] = qn * gkn_ref[...]
        x = qr_all[:, sl]
        x = x * lax.rsqrt(jnp.sum(x * x, axis=-1, keepdims=True) / C_ROPE + EPS) * gqr_ref[...]
        qr_ref[:, sl] = (_rope128(x, cos, sin) * MLA_SCALE).astype(BF16)

    @pl.when(i == npt)
    def _():
        for h in range(C_HEADS):
            sl = slice(HP * h, HP * (h + 1))
            qabs_ref[:, C_LORA * h:C_LORA * (h + 1)] = _dot1(qg_ref[:, sl], wuk_ref[:, sl], NT).astype(BF16)


def _mla_prep(z1, gq, gkv, gkr, gqn, gqr, gkn, wn, wr, wuk, cos, sin, npt):
    t = z1.shape[0]
    zw = z1.shape[1]
    row = lambda w: pl.BlockSpec((TM, w), lambda i: (i, 0))
    full = lambda a: pl.BlockSpec(a.shape, lambda i: (0,) * a.ndim)
    pad64 = lambda g: jnp.concatenate([g, jnp.zeros((HP - C_ROPE,), F32)]).reshape(1, HP)
    args = (z1, gq.reshape(1, -1), gkv.reshape(1, -1), pad64(gkr), gqn.reshape(1, -1), pad64(gqr), gkn.reshape(1, -1),
            wn, wr, wuk, cos, sin)
    return pl.pallas_call(
        functools.partial(_mla_prep_kernel, npt=npt),
        grid=(t // TM,),
        in_specs=[row(zw)] + [full(a) for a in args[1:10]] + [row(HP), row(HP)],
        out_specs=[row(C_LORA), row(C_ROPE), row(HP), row(C_HEADS * HP), row(C_HEADS * HP),
                   pl.BlockSpec((TM, C_HEADS * C_LORA), lambda i: (0, 0))],
        out_shape=[jax.ShapeDtypeStruct((t, C_LORA), F32), jax.ShapeDtypeStruct((t, C_ROPE), F32),
                   jax.ShapeDtypeStruct((t, HP), BF16), jax.ShapeDtypeStruct((t, C_HEADS * HP), BF16),
                   jax.ShapeDtypeStruct((t, C_HEADS * HP), BF16), jax.ShapeDtypeStruct((TM, C_HEADS * C_LORA), BF16)],
        scratch_shapes=[pltpu.VMEM((TM, C_HEADS * HP), F32)],
        compiler_params=_cparams(("arbitrary",)),
        name="mla_prep",
    )(*args)


def _mla_kv_kernel(lat_ref, wuk_ref, wuv_ref, gkn_ref, kn_ref, v_ref):
    lat = lat_ref[...].astype(BF16)
    kn = jnp.dot(lat, wuk_ref[...], preferred_element_type=F32)
    for h in range(C_HEADS):
        sl = slice(HP * h, HP * (h + 1))
        kn_ref[:, sl] = _rms(kn[:, sl], gkn_ref[...]).astype(BF16)
    v_ref[...] = jnp.dot(lat, wuv_ref[...], preferred_element_type=F32).astype(BF16)


def _mla_kv(lat, wuk, wuv, gkn, rows):
    wide = C_HEADS * HP
    return pl.pallas_call(
        _mla_kv_kernel,
        grid=(rows // TM,),
        in_specs=[pl.BlockSpec((TM, C_LORA), lambda i: (i, 0)),
                  pl.BlockSpec((C_LORA, wide), lambda i: (0, 0)),
                  pl.BlockSpec((C_LORA, wide), lambda i: (0, 0)),
                  pl.BlockSpec((1, HP), lambda i: (0, 0))],
        out_specs=[pl.BlockSpec((TM, wide), lambda i: (i, 0))] * 2,
        out_shape=[jax.ShapeDtypeStruct((rows, wide), BF16)] * 2,
        compiler_params=_cparams(("parallel",)),
        name="mla_kv",
    )(lat, wuk, wuv, gkn.reshape(1, HP))


FQ = 512


def _mla_flash_kernel(qn_ref, qr_ref, kn_ref, kr_ref, v_ref, o_ref, kcat):
    qi = pl.program_id(2)

    @pl.when(qi == 0)
    def _():
        kcat[:, 0:HP] = kn_ref[...]
        kcat[:, HP:2 * HP] = kr_ref[...]

    q = jnp.concatenate([qn_ref[...], qr_ref[...]], axis=1)

    def block(kb, carry, diagonal):
        m, l, acc = carry
        ks = pl.ds(pl.multiple_of(kb * FQ, FQ), FQ)
        s = lax.dot_general(q, kcat[ks, :], (NT, ((), ())), preferred_element_type=F32)
        if diagonal:
            row = lax.broadcasted_iota(jnp.int32, (FQ, FQ), 0)
            col = lax.broadcasted_iota(jnp.int32, (FQ, FQ), 1)
            s = jnp.where(col <= row, s, -jnp.inf)
        m_new = jnp.maximum(m, jnp.max(s, axis=-1, keepdims=True))
        alpha = jnp.exp(m - m_new)
        p = jnp.exp(s - m_new)
        l = alpha * l + jnp.sum(p, axis=-1, keepdims=True)
        acc = alpha * acc + jnp.dot(p.astype(BF16), v_ref[ks, :], preferred_element_type=F32)
        return m_new, l, acc

    m0 = jnp.full((FQ, 1), -jnp.inf, F32)
    l0 = jnp.zeros((FQ, 1), F32)
    a0 = jnp.zeros((FQ, HP), F32)
    carry = lax.fori_loop(0, qi, lambda kb, c: block(kb, c, False), (m0, l0, a0))
    m, l, acc = block(qi, carry, True)
    o_ref[...] = (acc / l).astype(BF16)


def _mla_flash(qn, qr, kn, krb, v, nb, seq):
    nq = seq // FQ
    qspec = pl.BlockSpec((FQ, HP), lambda b, h, q: (b * nq + q, h))
    kspec = pl.BlockSpec((seq, HP), lambda b, h, q: (b, h))
    return pl.pallas_call(
        _mla_flash_kernel,
        grid=(nb, C_HEADS, nq),
        in_specs=[qspec, qspec, kspec, pl.BlockSpec((seq, HP), lambda b, h, q: (b, 0)), kspec],
        out_specs=qspec,
        out_shape=jax.ShapeDtypeStruct((nb * seq, C_HEADS * HP), BF16),
        scratch_shapes=[pltpu.VMEM((seq, 2 * HP), BF16)],
        compiler_params=_cparams(("parallel", "parallel", "arbitrary")),
        name="mla_flash",
    )(qn, qr, kn, krb, v)


PPS = 16
PPG = 4


def _mla_s_kernel(*refs, n_steps):
    pt_ref = refs[0]
    lat_refs = refs[1:1 + PPS]
    kr_refs = refs[1 + PPS:1 + 2 * PPS]
    wukt_ref, qabs_ref, qr_ref, latn_ref, krn_ref, o_ref, wst, latb, lg, m_ref, l_ref, acc_ref = refs[1 + 2 * PPS:]
    b = pl.program_id(0)
    kt = pl.program_id(1)
    nq = qabs_ref.shape[1]
    nheads = C_HEADS
    ntok = nq // nheads

    @pl.when(jnp.logical_and(b == 0, kt == 0))
    def _():
        wst[0:nheads * HP, :] = wukt_ref[...]

    @pl.when(kt == 0)
    def _():
        wst[nheads * HP:nheads * HP + nq, :] = qabs_ref[0]
        m_ref[...] = jnp.full_like(m_ref, -jnp.inf)
        l_ref[...] = jnp.zeros_like(l_ref)
        acc_ref[...] = jnp.zeros_like(acc_ref)

    qr = qr_ref[0]

    def up_project(lat_bf16):
        return lax.dot_general(wst[...], lat_bf16, (NT, ((), ())), preferred_element_type=F32)

    def key_logits(r, krt_f32):
        nkeys = r.shape[1]
        kraw = r[0:nheads * HP].reshape(nheads, HP, nkeys)
        rs = lax.rsqrt(jnp.sum(kraw * kraw, axis=1) / HP + EPS)
        lraw = r[nheads * HP:nheads * HP + nq].reshape(ntok, nheads, nkeys) * rs[None]
        return lraw.reshape(nq, nkeys) + _dot1(qr, krt_f32)

    def update(state, logits, lat_bf16):
        m_old, l_old, acc_old = state
        m_new = jnp.maximum(m_old, jnp.max(logits, axis=-1, keepdims=True))
        alpha = jnp.exp(m_old - m_new)
        p = jnp.exp(logits - m_new)
        l_new = alpha * l_old + jnp.sum(p, axis=-1, keepdims=True)
        acc_new = alpha * acc_old + jnp.dot(p.astype(BF16), lat_bf16, preferred_element_type=F32)
        return m_new, l_new, acc_new

    def load_group(g):
        pages = range(PPG * g, PPG * (g + 1))
        return jnp.concatenate([lat_refs[p][0, 0] for p in pages], axis=0).astype(BF16)

    gk = PPG * PAGE
    for g in range(PPS // PPG):
        lat_g = load_group(g)
        krt_g = jnp.concatenate([kr_refs[p][0, 0] for p in range(PPG * g, PPG * (g + 1))], axis=1)
        latb[gk * g:gk * (g + 1), :] = lat_g
        lg[:, gk * g:gk * (g + 1)] = key_logits(up_project(lat_g), krt_g)
    state = update((m_ref[...], l_ref[...], acc_ref[...]), lg[...], latb[...])
    m_ref[...], l_ref[...], acc_ref[...] = state

    @pl.when(kt == n_steps - 1)
    def _():
        latn = jnp.concatenate([latn_ref[0], jnp.zeros((PAGE - 8, C_LORA), F32)], axis=0).astype(BF16)
        logits = key_logits(up_project(latn), krn_ref[0])
        qtok = lax.broadcasted_iota(jnp.int32, (nq, PAGE), 0) // nheads
        slot = lax.broadcasted_iota(jnp.int32, (nq, PAGE), 1)
        _, l_fin, acc_fin = update(state, jnp.where(slot <= qtok, logits, -jnp.inf), latn)
        o_ref[0] = acc_fin / l_fin


def _mla_sample(page_table, lat_cache, kr_cache, ci, wukt, qabs, qr, lat_new8, kr_new8):
    nb, npages = page_table.shape
    n_steps = npages // PPS
    nq = qabs.shape[1]

    def page_spec(k, rows, width):
        return pl.BlockSpec((1, 1, rows, width), lambda b, t, pt: (ci, pt[b * npages + t * PPS + k], 0, 0))

    kr_cache_t = jnp.swapaxes(kr_cache, 2, 3)
    kr_new_t = jnp.pad(jnp.swapaxes(kr_new8, 1, 2), ((0, 0), (0, 0), (0, PAGE - kr_new8.shape[1])))
    grid_spec = pltpu.PrefetchScalarGridSpec(
        num_scalar_prefetch=1,
        grid=(nb, n_steps),
        in_specs=[page_spec(k, PAGE, C_LORA) for k in range(PPS)] + [page_spec(k, C_ROPE, PAGE) for k in range(PPS)]
                 + [pl.BlockSpec(wukt.shape, lambda b, t, pt: (0, 0)),
                    pl.BlockSpec((1, nq, C_LORA), lambda b, t, pt: (b, 0, 0)),
                    pl.BlockSpec((1, nq, C_ROPE), lambda b, t, pt: (b, 0, 0)),
                    pl.BlockSpec((1, 8, C_LORA), lambda b, t, pt: (b, 0, 0)),
                    pl.BlockSpec((1, C_ROPE, PAGE), lambda b, t, pt: (b, 0, 0))],
        out_specs=pl.BlockSpec((1, nq, C_LORA), lambda b, t, pt: (b, 0, 0)),
        scratch_shapes=[pltpu.VMEM((C_HEADS * HP + nq, C_LORA), BF16),
                        pltpu.VMEM((PPS * PAGE, C_LORA), BF16),
                        pltpu.VMEM((nq, PPS * PAGE), F32),
                        pltpu.VMEM((nq, 1), F32), pltpu.VMEM((nq, 1), F32), pltpu.VMEM((nq, C_LORA), F32)],
    )
    return pl.pallas_call(
        functools.partial(_mla_s_kernel, n_steps=n_steps),
        grid_spec=grid_spec,
        out_shape=jax.ShapeDtypeStruct((nb, nq, C_LORA), F32),
        compiler_params=_cparams(("arbitrary", "arbitrary")),
        name="mla_sample",
    )(page_table.reshape(-1), *([lat_cache] * PPS), *([kr_cache_t] * PPS), wukt, qabs, qr, lat_new8, kr_new_t)


def _mla_uv_kernel(c_ref, w_ref, o_ref):
    o_ref[...] = _dot1(c_ref[...], w_ref[0])


def _mla_uv(ctx, wuv3):
    rows = ctx.shape[0]
    return pl.pallas_call(
        _mla_uv_kernel,
        grid=(C_HEADS,),
        in_specs=[pl.BlockSpec((rows, C_LORA), lambda h: (0, h)),
                  pl.BlockSpec((1, C_LORA, HP), lambda h: (h, 0, 0))],
        out_specs=pl.BlockSpec((rows, HP), lambda h: (0, h)),
        out_shape=jax.ShapeDtypeStruct((rows, C_HEADS * HP), F32),
        compiler_params=_cparams(("parallel",)),
        name="mla_uv",
    )(ctx, wuv3)


def kernel(x_prompt, x_sample, state_hgrn, cache_swa_kv, cache_mla_latent, cache_mla_krope, page_table, c_prompt, c_sample, ada_w, ada_b, norm_g, rel_bias, ab_w_in, ab_w_out, hgrn_lb_logits, hgrn_norm_g, swa_q_g, swa_k_g, mla_w_in, mla_q_a_g, mla_kv_a_g, mla_w_uq, mla_w_uk, mla_w_uv, mla_qn_g, mla_qr_g, mla_kn_g, mla_kr_g, mla_w_o, moe_router_w, moe_router_b, moe_w_gu, moe_b_gu, moe_w_dn, moe_b_dn):
    nb, seq, _ = x_prompt.shape
    nsb, nst, _ = x_sample.shape
    tp, ts = nb * seq, nsb * nst
    assert ts == TM and seq % TM == 0 and nst <= 4
    npt, tpb = tp // TM, seq // TM
    x = jnp.concatenate([x_prompt.reshape(tp, D), x_sample.reshape(ts, D)], axis=0).astype(F32)

    nc = nb + nsb
    c_all = jnp.concatenate([c_prompt, c_sample, jnp.zeros((-nc % 8, D), c_prompt.dtype)], axis=0).astype(F32)
    mods = _adaln(c_all, ada_w, ada_b)

    def mod(s):
        return mods[s, 0:nb].reshape(nb, 1, 3 * D), jnp.repeat(mods[s, nb:nc], nst, axis=0)

    pm, sm = mod(0)
    z = _normmod_mm(x, norm_g[0, 0], pm, sm, ab_w_in[0].astype(BF16), 1792, npt, tpb)
    zs3 = z[tp:].reshape(nsb, nst, z.shape[1])
    oa_p, hg_p = _hgrn_prompt(z, hgrn_lb_logits, hgrn_norm_g[0], nb, seq, 0)
    oa_s, hg_s = _hgrn_sample(zs3, hgrn_lb_logits, hgrn_norm_g[0], state_hgrn, 0)
    ob_p, k_p, v_p = _dilated_prompt(z, swa_q_g[0], swa_k_g[0], _band_tables(rel_bias), nb, seq)
    buf_len = cache_swa_kv.shape[2]
    ob_s, k_s, v_s = _dilated_sample(zs3, swa_q_g[0], swa_k_g[0], cache_swa_kv[0], _sample_tables(rel_bias, nst, buf_len))
    w_out = ab_w_out[0].astype(BF16)
    h = _proj_residual([oa_p, ob_p], [oa_s.reshape(ts, 1024), ob_s.reshape(ts, 1024)],
                       [w_out[:1024], w_out[1024:]], x, pm, sm, npt, tpb)
    pm, sm = mod(1)
    h = _moe_layer(h, norm_g[0, 1], pm, sm, 0, moe_router_w, moe_router_b, moe_w_gu, moe_b_gu, moe_w_dn, moe_b_dn, npt, tpb)

    pm, sm = mod(2)
    w_in1 = jnp.pad(mla_w_in[0], ((0, 0), (0, HP - C_ROPE))).astype(BF16)
    z1 = _normmod_mm(h, norm_g[1, 0], pm, sm, w_in1, w_in1.shape[1], npt, tpb)
    past = page_table.shape[1] * PAGE
    pos = jnp.concatenate([jnp.tile(jnp.arange(seq), nb), jnp.tile(past + jnp.arange(nst), nsb)])
    cos, sin = _rope_tables(pos)
    wq = mla_w_uq[0]
    wn = wq[:, :, :C_NOPE].reshape(C_LORA, C_HEADS * HP).astype(BF16)
    wr = jnp.pad(wq[:, :, C_NOPE:], ((0, 0), (0, 0), (0, HP - C_ROPE))).reshape(C_LORA, C_HEADS * HP).astype(BF16)
    wuk = mla_w_uk[0].reshape(C_LORA, C_HEADS * HP).astype(BF16)
    wuv = mla_w_uv[0].reshape(C_LORA, C_HEADS * HP).astype(BF16)
    lat, kr, krb, qn, qr, qabs = _mla_prep(z1, mla_q_a_g[0], mla_kv_a_g[0], mla_kr_g[0], mla_qn_g[0], mla_qr_g[0],
                                           mla_kn_g[0], wn, wr, wuk, cos, sin, npt)
    kn, v = _mla_kv(lat, wuk, wuv, mla_kn_g[0], tp)
    ctx_p = _mla_flash(qn, qr, kn, krb, v, nb, seq)
    qabs3 = qabs.reshape(nsb, nst * C_HEADS, C_LORA)
    qr_s = qr[tp:].reshape(ts, C_HEADS, HP)[:, :, :C_ROPE].reshape(nsb, nst * C_HEADS, C_ROPE)
    lat_new8 = jnp.pad(lat[tp:].reshape(nsb, nst, C_LORA), ((0, 0), (0, 8 - nst), (0, 0)))
    kr_new8 = jnp.pad(kr[tp:].reshape(nsb, nst, C_ROPE), ((0, 0), (0, 8 - nst), (0, 0)))
    ctx_s = _mla_sample(page_table, cache_mla_latent, cache_mla_krope, 0, wuk.T, qabs3, qr_s, lat_new8, kr_new8)
    wuv3 = jnp.transpose(mla_w_uv[0], (1, 0, 2)).astype(BF16)
    o_s = _mla_uv(ctx_s.reshape(ts, C_HEADS * C_LORA), wuv3)
    h = _proj_residual([ctx_p], [o_s], [mla_w_o[0].astype(BF16)], h, pm, sm, npt, tpb)
    pm, sm = mod(3)
    h = _moe_layer(h, norm_g[1, 1], pm, sm, 1, moe_router_w, moe_router_b, moe_w_gu, moe_b_gu, moe_w_dn, moe_b_dn, npt, tpb)

    y_prompt = h[:tp].reshape(nb, seq, D).astype(x_prompt.dtype)
    y_sample = h[tp:].reshape(nsb, nst, D).astype(x_sample.dtype)
    hgrn_prompt = hg_p[None].astype(state_hgrn.dtype)
    hgrn_sample = hg_s.astype(state_hgrn.dtype)
    kv = lambda k, v, b, t: jnp.stack([k.reshape(b, t, 8, 128), v.reshape(b, t, 8, 128)], axis=2)[None]
    swa_prompt = kv(k_p, v_p, nb, seq).astype(cache_swa_kv.dtype)
    swa_sample = kv(k_s, v_s, nsb, nst).astype(cache_swa_kv.dtype)
    lat_prompt = lat[:tp].reshape(1, nb, seq, C_LORA).astype(cache_mla_latent.dtype)
    lat_sample = lat[tp:].reshape(1, nsb, nst, C_LORA).astype(cache_mla_latent.dtype)
    krope_prompt = kr[:tp].reshape(1, nb, seq, C_ROPE).astype(cache_mla_krope.dtype)
    krope_sample = kr[tp:].reshape(1, nsb, nst, C_ROPE).astype(cache_mla_krope.dtype)
    return (y_prompt, y_sample, hgrn_prompt, hgrn_sample, swa_prompt, swa_sample,
            lat_prompt, lat_sample, krope_prompt, krope_sample)
```

```python
import functools
import math

import numpy as np
import jax
import jax.numpy as jnp
from jax import lax
from jax.experimental import pallas as pl
from jax.experimental.pallas import tpu as pltpu

F32 = jnp.float32
BF16 = jnp.bfloat16
EPS = 1e-6

D = 2048
TM = 512
N_EXPERTS = 32
TOP_K = 4
D_EXPERT = 2048
SWIGLU_LIMIT = 7.0
SWIGLU_ALPHA = 1.702
PAGE = 128
V7X_VMEM_LIMIT = 56 * 1024 * 1024

NN = ((1,), (0,))
NT = ((1,), (1,))
TN = ((0,), (0,))


def _cparams(sem, vmem=V7X_VMEM_LIMIT):
    return pltpu.CompilerParams(dimension_semantics=sem, vmem_limit_bytes=vmem)


def _dot1(a, b, dims=NN):
    return lax.dot_general(a.astype(BF16), b.astype(BF16), (dims, ((), ())), preferred_element_type=F32)


def _split2(x):
    hi = x.astype(BF16)
    lo = (x - hi.astype(F32)).astype(BF16)
    return hi, lo


def _split3(x):
    hi = x.astype(BF16)
    r = x - hi.astype(F32)
    mid = r.astype(BF16)
    lo = (r - mid.astype(F32)).astype(BF16)
    return hi, mid, lo


def _dot_exact_rhs(a01, x):
    hi, mid, lo = _split3(x)
    f = lambda p: lax.dot_general(a01, p, (NN, ((), ())), preferred_element_type=F32)
    return f(hi) + f(mid) + f(lo)


def _dot3(a, b, dims=NN):
    ah, al = _split2(a)
    bh, bl = _split2(b)
    f = lambda p, q: lax.dot_general(p, q, (dims, ((), ())), preferred_element_type=F32)
    return f(ah, bh) + f(ah, bl) + f(al, bh)


def _sigmoid(x):
    return 1.0 / (1.0 + jnp.exp(-x))


def _rms(x, g):
    return x * lax.rsqrt(jnp.mean(x * x, axis=-1, keepdims=True) + EPS) * g


def _ada_kernel(c_ref, w_ref, b_ref, o_ref):
    c = c_ref[...]
    s = c * _sigmoid(c)
    o_ref[0] = _dot1(s, w_ref[0, 0]) + b_ref[0, 0]


def _adaln(c_all, ada_w, ada_b):
    r = c_all.shape[0]
    tn = 1024
    depth = ada_w.shape[0]
    b4 = ada_b.reshape(depth, 2, 1, 3 * D)
    return pl.pallas_call(
        _ada_kernel,
        grid=(depth * 2, 3 * D // tn),
        in_specs=[pl.BlockSpec((r, D), lambda s, j: (0, 0)),
                  pl.BlockSpec((1, 1, D, tn), lambda s, j: (s // 2, s % 2, 0, j)),
                  pl.BlockSpec((1, 1, 1, tn), lambda s, j: (s // 2, s % 2, 0, j))],
        out_specs=pl.BlockSpec((1, r, tn), lambda s, j: (s, 0, j)),
        out_shape=jax.ShapeDtypeStruct((depth * 2, r, 3 * D), F32),
        compiler_params=_cparams(("parallel", "parallel")),
        name="adaln",
    )(c_all, ada_w, b4)


def _pick(is_prompt, p_ref, s_ref):
    return jnp.where(is_prompt, p_ref[0], s_ref[...])


def _mod_specs(npt, rows_per_batch_tiles, col, tn=D, with_j=False):
    if with_j:
        pm = pl.BlockSpec((1, 1, tn), lambda i, j: (jnp.minimum(i // rows_per_batch_tiles, npt // rows_per_batch_tiles - 1), 0, col * (D // tn) + j))
        sm = pl.BlockSpec((TM, tn), lambda i, j: (0, col * (D // tn) + j))
    else:
        pm = pl.BlockSpec((1, 1, tn), lambda i, *_: (jnp.minimum(i // rows_per_batch_tiles, npt // rows_per_batch_tiles - 1), 0, col))
        sm = pl.BlockSpec((TM, tn), lambda i, *_: (0, col))
    return pm, sm


def _nm_mm_kernel(x_ref, g_ref, psh, psc, ssh, ssc, w_ref, o_ref, xn_ref, *, npt):
    i = pl.program_id(0)
    j = pl.program_id(1)

    @pl.when(j == 0)
    def _():
        xr = _rms(x_ref[...], g_ref[...])
        sc = _pick(i < npt, psc, ssc)
        sh = _pick(i < npt, psh, ssh)
        xn_ref[...] = (xr * (1.0 + sc) + sh).astype(BF16)

    o_ref[...] = jnp.dot(xn_ref[...], w_ref[...], preferred_element_type=F32)


def _normmod_mm(x, g, pmod, smod, w_bf16, tn, npt, tiles_per_batch):
    t = x.shape[0]
    n = w_bf16.shape[1]
    psh, ssh = _mod_specs(npt, tiles_per_batch, 0)
    psc, ssc = _mod_specs(npt, tiles_per_batch, 1)
    return pl.pallas_call(
        functools.partial(_nm_mm_kernel, npt=npt),
        grid=(t // TM, n // tn),
        in_specs=[pl.BlockSpec((TM, D), lambda i, j: (i, 0)),
                  pl.BlockSpec((1, D), lambda i, j: (0, 0)),
                  psh, psc, ssh, ssc,
                  pl.BlockSpec((D, tn), lambda i, j: (0, j))],
        out_specs=pl.BlockSpec((TM, tn), lambda i, j: (i, j)),
        out_shape=jax.ShapeDtypeStruct((t, n), F32),
        scratch_shapes=[pltpu.VMEM((TM, D), BF16)],
        compiler_params=_cparams(("parallel", "arbitrary")),
        name="normmod_mm",
    )(x, g.reshape(1, D), pmod, pmod, smod, smod, w_bf16)


def _proj_res_kernel(*refs, npt, nk):
    xp = refs[0:nk]
    xs = refs[nk:2 * nk]
    ws = refs[2 * nk:3 * nk]
    h_ref, pg, sg, o_ref = refs[3 * nk:3 * nk + 4]
    xb = refs[3 * nk + 4:]
    i = pl.program_id(0)
    j = pl.program_id(1)

    @pl.when(jnp.logical_and(j == 0, i < npt))
    def _():
        for k in range(nk):
            xb[k][...] = xp[k][...].astype(BF16)

    @pl.when(jnp.logical_and(j == 0, i >= npt))
    def _():
        for k in range(nk):
            xb[k][...] = xs[k][...].astype(BF16)

    acc = jnp.dot(xb[0][...], ws[0][...], preferred_element_type=F32)
    for k in range(1, nk):
        acc = acc + jnp.dot(xb[k][...], ws[k][...], preferred_element_type=F32)
    gate = _pick(i < npt, pg, sg)
    o_ref[...] = h_ref[...] + gate * acc


def _proj_residual(xps, xss, ws, h, pmod, smod, npt, tiles_per_batch, tn=1024):
    nk = len(ws)
    t = h.shape[0]
    ks = [w.shape[0] for w in ws]
    pg, sg = _mod_specs(npt, tiles_per_batch, 2, tn=tn, with_j=True)
    in_specs = ([pl.BlockSpec((TM, k), lambda i, j: (jnp.minimum(i, npt - 1), 0)) for k in ks]
                + [pl.BlockSpec((TM, k), lambda i, j: (0, 0)) for k in ks]
                + [pl.BlockSpec((k, tn), lambda i, j: (0, j)) for k in ks]
                + [pl.BlockSpec((TM, tn), lambda i, j: (i, j)), pg, sg])
    return pl.pallas_call(
        functools.partial(_proj_res_kernel, npt=npt, nk=nk),
        grid=(t // TM, D // tn),
        in_specs=in_specs,
        out_specs=pl.BlockSpec((TM, tn), lambda i, j: (i, j)),
        out_shape=jax.ShapeDtypeStruct((t, D), F32),
        scratch_shapes=[pltpu.VMEM((TM, k), BF16) for k in ks],
        compiler_params=_cparams(("parallel", "arbitrary")),
        name="proj_residual",
    )(*xps, *xss, *ws, h, pmod, smod)


HG_C = 64
HG_B = 16
HG_STEP = 256
HG_HEADS = 4


def _lower_bound(lbl, layer):
    e = jnp.exp(lbl - jnp.max(lbl, axis=0, keepdims=True))
    return jnp.sum(e[0:layer + 1], axis=0, keepdims=True) / jnp.sum(e, axis=0, keepdims=True)


def _hgrn_chunk(q, af, v, st, lb, tri):
    c, dk = q.shape
    nb = c // HG_B
    f = lb + (1.0 - lb) * _sigmoid(af)
    g = jnp.log(f)
    k = 1.0 - f
    cum = _dot_exact_rhs(tri, g)
    o = _dot1(q * jnp.exp(cum), st, NT)
    q3 = q.reshape(nb, HG_B, dk)
    k3 = k.reshape(nb, HG_B, dk)
    c3 = cum.reshape(nb, HG_B, dk)
    tpos = lax.broadcasted_iota(jnp.int32, (nb, HG_B, dk), 1)
    lane3 = lax.broadcasted_iota(jnp.int32, (nb, HG_B, dk), 2)
    dmat = jnp.zeros((nb, HG_B, dk), F32)
    for s in range(HG_B):
        rel = c3 - c3[:, s:s + 1, :]
        a = q3 * k3[:, s:s + 1, :] * jnp.exp(jnp.where(tpos >= s, rel, -jnp.inf))
        col = jnp.sum(a, axis=-1, keepdims=True)
        dmat = jnp.where(lane3 == s, col, dmat)
    lane_c = lax.broadcasted_iota(jnp.int32, (HG_B, c), 1)
    rows = []
    for blk in range(nb):
        r0 = HG_B * blk
        d_blk = dmat[blk]
        if blk == 0:
            rows.append(d_blk[:, :c])
            continue
        d_blk = pltpu.roll(d_blk, r0, axis=1)[:, :c]
        cb = cum[r0 - 1:r0, :]
        qs = q[r0:r0 + HG_B] * jnp.exp(cum[r0:r0 + HG_B] - cb)
        ks = k * jnp.exp(jnp.minimum(cb - cum, 0.0))
        s_off = _dot1(qs, ks, NT)
        rows.append(jnp.where(lane_c < r0, s_off, 0.0) + d_blk)
    scores = jnp.concatenate(rows, axis=0)
    o = o + _dot1(scores, v)
    last = cum[c - 1:c, :]
    st_new = jnp.exp(last) * st + _dot1(v, k * jnp.exp(last - cum), TN)
    return o, st_new


def _hgrn_p_kernel(q_ref, f_ref, v_ref, g_ref, lbl_ref, gn_ref, o_ref, s_ref, st_ref, *, layer, nsteps):
    tb = pl.program_id(2)

    @pl.when(tb == 0)
    def _():
        st_ref[...] = jnp.zeros_like(st_ref)

    lb = _lower_bound(lbl_ref[...], layer)
    ri = lax.broadcasted_iota(jnp.int32, (HG_C, HG_C), 0)
    ci = lax.broadcasted_iota(jnp.int32, (HG_C, HG_C), 1)
    tri = jnp.where(ri >= ci, 1.0, 0.0).astype(BF16)
    sts = [st_ref[hh] for hh in range(HG_HEADS)]
    for c in range(HG_STEP // HG_C):
        sl = slice(HG_C * c, HG_C * (c + 1))
        for hh in range(HG_HEADS):
            hl = slice(128 * hh, 128 * (hh + 1))
            o, sts[hh] = _hgrn_chunk(q_ref[sl, hl], f_ref[sl, hl], v_ref[sl, hl], sts[hh], lb[:, hl], tri)
            o_ref[sl, hl] = (_rms(o, gn_ref[...]) * _sigmoid(g_ref[sl, hl])).astype(BF16)
    for hh in range(HG_HEADS):
        st_ref[hh] = sts[hh]

    @pl.when(tb == nsteps - 1)
    def _():
        for hh in range(HG_HEADS):
            s_ref[0, hh] = sts[hh].T


def _hgrn_prompt(z, lb_logits, gn, nb, seq, layer):
    nh = 8
    nsteps = seq // HG_STEP
    hw = 128 * HG_HEADS
    ng = nh // HG_HEADS
    zspec = lambda c0: pl.BlockSpec((HG_STEP, hw), lambda b, h, t: (b * nsteps + t, c0 * ng + h))
    return pl.pallas_call(
        functools.partial(_hgrn_p_kernel, layer=layer, nsteps=nsteps),
        grid=(nb, ng, nsteps),
        in_specs=[zspec(0), zspec(1), zspec(2), zspec(3),
                  pl.BlockSpec((lb_logits.shape[0], hw), lambda b, h, t: (0, h)),
                  pl.BlockSpec((1, 128), lambda b, h, t: (0, 0))],
        out_specs=[pl.BlockSpec((HG_STEP, hw), lambda b, h, t: (b * nsteps + t, h)),
                   pl.BlockSpec((1, HG_HEADS, 128, 128), lambda b, h, t: (b, h, 0, 0))],
        out_shape=[jax.ShapeDtypeStruct((nb * seq, nh * 128), BF16),
                   jax.ShapeDtypeStruct((nb, nh, 128, 128), F32)],
        scratch_shapes=[pltpu.VMEM((HG_HEADS, 128, 128), F32)],
        compiler_params=_cparams(("parallel", "parallel", "arbitrary")),
        name="hgrn_prompt",
    )(z, z, z, z, lb_logits, gn.reshape(1, 128))


def _hgrn_s_kernel(q_ref, f_ref, v_ref, g_ref, lbl_ref, gn_ref, s0_ref, o_ref, s_ref, *, layer):
    nt = q_ref.shape[1]
    lb_all = _lower_bound(lbl_ref[...], layer)
    tpos = lax.broadcasted_iota(jnp.int32, (nt, 128), 0)
    outs = []
    for h in range(8):
        sl = slice(128 * h, 128 * (h + 1))
        q = q_ref[0, :, sl]
        v = v_ref[0, :, sl]
        lb = lb_all[:, sl]
        f = lb + (1.0 - lb) * _sigmoid(f_ref[0, :, sl])
        g = jnp.log(f)
        k = 1.0 - f
        crow = [g[0:1]]
        for t in range(1, nt):
            crow.append(crow[-1] + g[t:t + 1])
        cum = jnp.concatenate(crow, axis=0)
        last = crow[-1]
        s0 = s0_ref[0, 0, h]
        qe = jnp.concatenate([q * jnp.exp(cum), jnp.zeros((8 - nt, 128), F32)], axis=0)
        o = _dot1(qe, s0)[0:nt]
        for s in range(nt):
            a = q * k[s:s + 1] * jnp.exp(jnp.where(tpos >= s, cum - cum[s:s + 1], -jnp.inf))
            o = o + jnp.sum(a, axis=-1, keepdims=True) * v[s:s + 1]
        kp = k * jnp.exp(last - cum)
        w8 = jnp.concatenate([jnp.exp(last), kp, jnp.zeros((8 - 1 - nt, 128), F32)], axis=0)
        wt = jnp.concatenate([w8, jnp.zeros((120, 128), F32)], axis=0).T
        s_new = wt[:, 0:1] * s0
        for s in range(nt):
            s_new = s_new + wt[:, 1 + s:2 + s] * v[s:s + 1]
        s_ref[0, 0, h] = s_new
        outs.append(_rms(o, gn_ref[...]) * _sigmoid(g_ref[0, :, sl]))
    o_ref[0] = jnp.concatenate(outs, axis=1)


def _hgrn_sample(zs3, lb_logits, gn, state, layer):
    nb, nt, _ = zs3.shape
    zspec = lambda c: pl.BlockSpec((1, nt, 1024), lambda b: (b, 0, c))
    sspec = pl.BlockSpec((1, 1, 8, 128, 128), lambda b: (0, b, 0, 0, 0))
    return pl.pallas_call(
        functools.partial(_hgrn_s_kernel, layer=layer),
        grid=(nb,),
        in_specs=[zspec(0), zspec(1), zspec(2), zspec(3),
                  pl.BlockSpec(lb_logits.shape, lambda b: (0, 0)),
                  pl.BlockSpec((1, 128), lambda b: (0, 0)),
                  sspec],
        out_specs=[pl.BlockSpec((1, nt, 1024), lambda b: (b, 0, 0)), sspec],
        out_shape=[jax.ShapeDtypeStruct((nb, nt, 1024), F32),
                   jax.ShapeDtypeStruct(state.shape, F32)],
        compiler_params=_cparams(("parallel",)),
        name="hgrn_sample",
    )(zs3, zs3, zs3, zs3, lb_logits, gn.reshape(1, 128), state)


DILATED = ((128, 1), (512, 4), (2048, 16))
REL_BUCKETS = 32
REL_MAX_DISTANCE = 2048
ATT_SCALE = 128 ** -0.5
QB = 128


def _rel_bucket_np(dist):
    exact = REL_BUCKETS // 2
    d = np.asarray(dist)
    far = exact + (np.log(np.maximum(d, 1).astype(np.float32) / np.float32(exact))
                   / np.float32(math.log(REL_MAX_DISTANCE / exact)) * np.float32(REL_BUCKETS - exact)).astype(np.int32)
    return np.where(d < exact, d, np.minimum(far, REL_BUCKETS - 1)).astype(np.int32)


def _band_tables(rel_bias):
    rb = rel_bias.astype(F32)
    tabs = []
    for (window, dil), width in zip(DILATED, (2 * QB, 2 * QB, QB)):
        period = width + QB
        i = np.arange(period + 1)
        k = np.where(i < width, i, i - (period + 1))
        steps = (width - QB) - k
        valid = (steps >= 0) & (steps <= window // dil)
        bucket = _rel_bucket_np(np.clip(steps, 0, window // dil) * dil)
        u = jnp.where(jnp.asarray(valid)[None, :], rb[jnp.asarray(bucket)].T, -jnp.inf)
        flat = jnp.tile(u, (1, QB))[:, :QB * period]
        tabs.append(flat.reshape(rb.shape[1], QB, period)[:, :, :width])
    return tabs


def _soft_block(qb, kw, vw, bias):
    l = _dot1(qb, kw, NT) * ATT_SCALE + bias
    m = jnp.max(l, axis=-1, keepdims=True)
    p = jnp.exp(l - m)
    return m, jnp.sum(p, axis=-1, keepdims=True), _dot1(p, vw)


def _dil_p_kernel(q_ref, k_ref, v_ref, gq_ref, gk_ref, b1_ref, b2_ref, b3_ref, o_ref, ko_ref, vo_ref,
                  qs, ks, vs, a1, a2, a3, m1, m2, m3, l1, l2, l3):
    seq = q_ref.shape[0]
    kn = _rms(k_ref[...], gk_ref[...])
    v = v_ref[...]
    ko_ref[...] = kn
    vo_ref[...] = v
    qs[...] = _rms(q_ref[...], gq_ref[...])
    ks[...] = kn
    vs[...] = v
    zpad = jnp.zeros((QB, 128), F32)
    first_cols = lax.broadcasted_iota(jnp.int32, (QB, 2 * QB), 1) >= QB

    def banded(qsub, ksub, vsub, bias, n_blocks, store):
        for i in range(n_blocks):
            qb = qsub(i)
            if i == 0:
                kw = jnp.concatenate([zpad, ksub(0)], axis=0)
                vw = jnp.concatenate([zpad, vsub(0)], axis=0)
                bb = jnp.where(first_cols, bias, -jnp.inf)
            else:
                kw = jnp.concatenate([ksub(i - 1), ksub(i)], axis=0)
                vw = jnp.concatenate([vsub(i - 1), vsub(i)], axis=0)
                bb = bias
            store(i, *_soft_block(qb, kw, vw, bb))

    def st1(i, m, l, a):
        sl = slice(QB * i, QB * (i + 1))
        m1[sl, :] = m
        l1[sl, :] = l
        a1[sl, :] = a
    blk = lambda ref: (lambda i: ref[QB * i:QB * (i + 1), :])
    banded(blk(qs), blk(ks), blk(vs), b1_ref[0], seq // QB, st1)

    d2 = DILATED[1][1]
    for r in range(d2):
        sub = lambda ref: (lambda i: ref[pl.ds(r + d2 * QB * i, QB, stride=d2), :])

        def st2(i, m, l, a):
            idx = pl.ds(r + d2 * QB * i, QB, stride=d2)
            m2[idx, :] = m
            l2[idx, :] = l
            a2[idx, :] = a
        banded(sub(qs), sub(ks), sub(vs), b2_ref[0], seq // (QB * d2), st2)

    d3 = DILATED[2][1]
    for r in range(d3):
        idx = pl.ds(r, QB, stride=d3)
        m, l, a = _soft_block(qs[idx, :], ks[idx, :], vs[idx, :], b3_ref[0])
        m3[idx, :] = m
        l3[idx, :] = l
        a3[idx, :] = a

    mx = jnp.maximum(jnp.maximum(m1[...], m2[...]), m3[...])
    w1 = jnp.exp(m1[...] - mx)
    w2 = jnp.exp(m2[...] - mx)
    w3 = jnp.exp(m3[...] - mx)
    num = w1 * a1[...] + w2 * a2[...] + w3 * a3[...]
    den = w1 * l1[...] + w2 * l2[...] + w3 * l3[...]
    o_ref[...] = (num / den).astype(BF16)


def _dilated_prompt(z, gq, gk, tabs, nb, seq):
    nh = 8
    zspec = lambda c0: pl.BlockSpec((seq, 128), lambda b, h: (b, c0 + h))
    gspec = pl.BlockSpec((1, 128), lambda b, h: (0, 0))
    tspec = lambda w: pl.BlockSpec((1, QB, w), lambda b, h: (h, 0, 0))
    ospec = pl.BlockSpec((seq, 128), lambda b, h: (b, h))
    big = pltpu.VMEM((seq, 128), F32)
    col = pltpu.VMEM((seq, 1), F32)
    return pl.pallas_call(
        _dil_p_kernel,
        grid=(nb, nh),
        in_specs=[zspec(32), zspec(40), zspec(48), gspec, gspec, tspec(2 * QB), tspec(2 * QB), tspec(QB)],
        out_specs=[ospec, ospec, ospec],
        out_shape=[jax.ShapeDtypeStruct((nb * seq, nh * 128), BF16),
                   jax.ShapeDtypeStruct((nb * seq, nh * 128), F32),
                   jax.ShapeDtypeStruct((nb * seq, nh * 128), F32)],
        scratch_shapes=[big, big, big, big, big, big, col, col, col, col, col, col],
        compiler_params=_cparams(("parallel", "parallel")),
        name="dilated_prompt",
    )(z, z, z, gq.reshape(1, 128), gk.reshape(1, 128), *tabs)


def _sample_tables(rel_bias, nt, buf_len):
    rb = rel_bias.astype(F32)
    nh = rb.shape[1]
    w2 = DILATED[1][0]

    def count(delta, patterns):
        c = np.zeros(delta.shape, np.int32)
        for window, dil in patterns:
            c += ((delta >= 0) & (delta <= window) & (delta % dil == 0)).astype(np.int32)
        return c

    qpos = buf_len + np.arange(8)
    real_q = (np.arange(8) < nt)
    da = qpos[None, :] - (buf_len - w2 + np.arange(w2))[:, None]
    ca = count(da, DILATED[:2]) * real_q[None, :]
    nj = buf_len // 16
    rows_b = (16 * np.arange(nj)[None, :] + np.arange(nt)[:, None]).reshape(-1)
    db = qpos[None, :] - rows_b[:, None]
    cb = count(db, DILATED[2:]) * real_q[None, :]
    dn = qpos[None, :] - qpos[:, None]
    cn = count(dn, DILATED) * real_q[None, :] * real_q[:, None]

    def tables(delta, cnt):
        bucket = _rel_bucket_np(np.clip(delta, 0, REL_MAX_DISTANCE))
        onehot = (bucket.reshape(-1, 1) == np.arange(REL_BUCKETS)[None, :]).astype(np.float32)
        bias = jnp.dot(jnp.asarray(onehot), rb, precision=lax.Precision.HIGHEST).reshape(*delta.shape, nh)
        bias = jnp.where(jnp.asarray((cnt > 0) | ~real_q[None, :])[:, :, None], bias, -jnp.inf)
        bias = jnp.transpose(bias, (0, 2, 1)).reshape(delta.shape[0], nh * 8)
        mult = np.repeat(cnt[:, None, :], nh, axis=1).reshape(delta.shape[0], nh * 8).astype(np.float32)
        return bias, jnp.asarray(mult)

    return tables(da, ca) + tables(db, cb) + tables(dn, cn)


def _dil_s_kernel(q_ref, k_ref, v_ref, gq_ref, gk_ref, ca_ref, cb_ref,
                  ba_ref, ma_ref, bb_ref, mb_ref, bn_ref, mn_ref, o_ref, ko_ref, vo_ref):
    nt = q_ref.shape[1]
    nh = 8
    gq = gq_ref[...]
    gk = gk_ref[...]
    qn = jnp.concatenate([_rms(q_ref[0, :, 128 * h:128 * (h + 1)], gq) for h in range(nh)], axis=1)
    kn = jnp.concatenate([_rms(k_ref[0, :, 128 * h:128 * (h + 1)], gk) for h in range(nh)], axis=1)
    vn = v_ref[0]
    ko_ref[0] = kn
    vo_ref[0] = vn
    pad = jnp.zeros((8 - nt, nh * 128), F32)
    q8 = jnp.concatenate([qn, pad], axis=0)
    k8 = jnp.concatenate([kn, pad], axis=0)
    v8 = jnp.concatenate([vn, pad], axis=0)
    lane_head = lax.broadcasted_iota(jnp.int32, (8, nh * 128), 1) // 128
    qbd = jnp.concatenate([jnp.where(lane_head == h, q8, 0.0) for h in range(nh)], axis=0).astype(BF16)

    def heads(ref2d, first, nrows, stride):
        return jnp.concatenate([ref2d[pl.ds(first + h, nrows, stride=stride), :] for h in range(nh)], axis=1)

    a2 = ca_ref.at[0]
    na = a2.shape[0] // (2 * nh)
    groups = [(heads(a2, 0, na, 2 * nh), heads(a2, nh, na, 2 * nh), ba_ref[...], ma_ref[...])]
    nj = cb_ref.shape[1]
    per_j = cb_ref.shape[2]
    b2 = cb_ref.reshape(nj * per_j, 128)
    for i in range(nt):
        groups.append((heads(b2, 2 * nh * i, nj, per_j), heads(b2, 2 * nh * i + nh, nj, per_j),
                       bb_ref[nj * i:nj * (i + 1), :], mb_ref[nj * i:nj * (i + 1), :]))
    groups.append((k8, v8, bn_ref[...], mn_ref[...]))

    logits = [_dot1(kk, qbd, NT) * ATT_SCALE + bias for kk, _, bias, _ in groups]
    mx = logits[0].max(axis=0, keepdims=True)
    for l in logits[1:]:
        mx = jnp.maximum(mx, l.max(axis=0, keepdims=True))
    ps = [mult * jnp.exp(l - mx) for l, (_, _, _, mult) in zip(logits, groups)]
    den = ps[0].sum(axis=0, keepdims=True)
    for p in ps[1:]:
        den = den + p.sum(axis=0, keepdims=True)
    inv = 1.0 / jnp.where(den > 0.0, den, 1.0)
    acc = None
    for p, (_, vv, _, _) in zip(ps, groups):
        part = _dot1(p * inv, vv, TN)
        acc = part if acc is None else acc + part
    o8 = jnp.concatenate([acc[8 * h:8 * (h + 1), 128 * h:128 * (h + 1)] for h in range(nh)], axis=1)
    o_ref[0] = o8[0:nt]


def _dilated_sample(zs3, gq, gk, cache, tabs):
    nb, nt, _ = zs3.shape
    buf_len = cache.shape[1]
    w2 = DILATED[1][0]
    d3 = DILATED[2][1]
    kvh = 2 * 8
    cache_a = cache.reshape(nb, buf_len * kvh, 128)
    cache_b = cache.reshape(nb, buf_len // d3, d3 * kvh, 128)
    zspec = lambda c: pl.BlockSpec((1, nt, 1024), lambda b: (b, 0, c))
    gspec = pl.BlockSpec((1, 128), lambda b: (0, 0))
    full = lambda a: pl.BlockSpec(a.shape, lambda b: (0, 0))
    ospec = pl.BlockSpec((1, nt, 1024), lambda b: (b, 0, 0))
    return pl.pallas_call(
        _dil_s_kernel,
        grid=(nb,),
        in_specs=[zspec(4), zspec(5), zspec(6), gspec, gspec,
                  pl.BlockSpec((1, w2 * kvh, 128), lambda b: (b, buf_len // w2 - 1, 0)),
                  pl.BlockSpec((1, buf_len // d3, nt * kvh, 128), lambda b: (b, 0, 0, 0))]
                 + [full(a) for a in tabs],
        out_specs=[ospec, ospec, ospec],
        out_shape=[jax.ShapeDtypeStruct((nb, nt, 1024), F32)] * 3,
        compiler_params=_cparams(("parallel",)),
        name="dilated_sample",
    )(zs3, zs3, zs3, gq.reshape(1, 128), gk.reshape(1, 128), cache_a, cache_b, *tabs)


def _route_kernel(x_ref, g_ref, psh, psc, ssh, ssc, rw_ref, rb_ref, u_ref, idx_ref, gate_ref, *, npt):
    i = pl.program_id(0)
    xr = _rms(x_ref[...], g_ref[...])
    u = xr * (1.0 + _pick(i < npt, psc, ssc)) + _pick(i < npt, psh, ssh)
    ub = u.astype(BF16).astype(F32)
    lo = lax.shift_right_logical(lax.bitcast_convert_type(ub[:, :D // 2], jnp.int32), 16)
    u_ref[...] = lax.bitcast_convert_type(ub[:, D // 2:], jnp.int32) | lo
    l = _dot3(u, rw_ref[...]) + rb_ref[...]
    lane = lax.broadcasted_iota(jnp.int32, l.shape, 1).astype(F32)
    out_lane = lax.broadcasted_iota(jnp.int32, (TM, 128), 1)
    vals, idxs = [], []
    for _ in range(TOP_K):
        m = jnp.max(l, axis=-1, keepdims=True)
        first = jnp.min(jnp.where(l == m, lane, float(N_EXPERTS)), axis=-1, keepdims=True)
        vals.append(m)
        idxs.append(first.astype(jnp.int32))
        l = jnp.where(lane == first, -jnp.inf, l)
    es = [jnp.exp(v - vals[0]) for v in vals]
    tot = es[0] + es[1] + es[2] + es[3]
    idx_out = jnp.zeros((TM, 128), jnp.int32)
    gate_out = jnp.zeros((TM, 128), F32)
    for k in range(TOP_K):
        idx_out = jnp.where(out_lane == k, idxs[k], idx_out)
        gate_out = jnp.where(out_lane == k, es[k] / tot, gate_out)
    idx_ref[...] = idx_out
    gate_ref[...] = gate_out


def _route(x, g, pmod, smod, rw, rb, npt, tiles_per_batch):
    t = x.shape[0]
    psh, ssh = _mod_specs(npt, tiles_per_batch, 0)
    psc, ssc = _mod_specs(npt, tiles_per_batch, 1)
    return pl.pallas_call(
        functools.partial(_route_kernel, npt=npt),
        grid=(t // TM,),
        in_specs=[pl.BlockSpec((TM, D), lambda i: (i, 0)),
                  pl.BlockSpec((1, D), lambda i: (0, 0)),
                  psh, psc, ssh, ssc,
                  pl.BlockSpec((D, N_EXPERTS), lambda i: (0, 0)),
                  pl.BlockSpec((1, N_EXPERTS), lambda i: (0, 0))],
        out_specs=[pl.BlockSpec((TM, D // 2), lambda i: (i, 0)),
                   pl.BlockSpec((TM, 128), lambda i: (i, 0)),
                   pl.BlockSpec((TM, 128), lambda i: (i, 0))],
        out_shape=[jax.ShapeDtypeStruct((t, D // 2), jnp.int32),
                   jax.ShapeDtypeStruct((t, 128), jnp.int32),
                   jax.ShapeDtypeStruct((t, 128), F32)],
        compiler_params=_cparams(("parallel",)),
        name="moe_route",
    )(x, g.reshape(1, D), pmod, pmod, smod, smod, rw, rb.reshape(1, N_EXPERTS))


MOE_TM = 1216
MOE_ROWS = (304, 608, MOE_TM)
MOE_TN = 1024
MOE_SEL = 512
MOE_TF = MOE_TN // 2


MOE_NG = 2 * D_EXPERT // MOE_TN
MOE_DN = 512
MOE_ND = D // MOE_DN


def _moe_kernel(te_ref, nu_ref, nv_ref, tok_ref, u_hbm, wgu_ref, bgu_ref, wdn_ref, bdn_ref, y_ref, xg, xb, hs, sem):
    t = pl.program_id(0)
    j = pl.program_id(1)
    nu = nu_ref[0]
    active = t < nu
    slot = t % 2
    nv = nv_ref[t]

    def rows8(n):
        return (n + 7) // 8 * 8

    def issue(tile, s):
        def body(g, carry):
            for i in range(8):
                r = g * 8 + i
                tok = tok_ref[tile * MOE_TM + r]
                pltpu.make_async_copy(u_hbm.at[pl.ds(tok, 1)], xg.at[s, pl.ds(r, 1)], sem.at[s]).start()
            return carry
        lax.fori_loop(0, rows8(nv_ref[tile]) // 8, body, 0)

    @pl.when(jnp.logical_and(t == 0, j == 0))
    def _():
        xg[...] = jnp.zeros_like(xg)
        issue(0, 0)

    @pl.when(jnp.logical_and(active, j == 0))
    def _():
        n8 = pl.multiple_of(rows8(nv), 8)
        pltpu.make_async_copy(u_hbm.at[pl.ds(0, n8)], xg.at[slot, pl.ds(0, n8)], sem.at[slot]).wait()
        w = xg[slot]
        xb[:, :D // 2] = lax.bitcast_convert_type(lax.shift_left(w, 16), F32).astype(BF16)
        xb[:, D // 2:] = lax.bitcast_convert_type(w & -65536, F32).astype(BF16)

    @pl.when(jnp.logical_and(t + 1 < nu, j == 1))
    def _():
        issue(t + 1, 1 - slot)

    def gate_up(rows):
        w = wgu_ref[0, 0].astype(BF16)
        ri = lax.broadcasted_iota(jnp.int32, (MOE_SEL, MOE_SEL // 2), 0)
        ci = lax.broadcasted_iota(jnp.int32, (MOE_SEL, MOE_SEL // 2), 1)
        pick_even = jnp.where(ri == 2 * ci, 1.0, 0.0).astype(BF16)
        parts = 2 if rows == MOE_TM else 1
        step = rows // parts
        gus = [jnp.dot(xb[step * p:step * (p + 1), :], w, preferred_element_type=F32) for p in range(parts)]
        for p, gu in enumerate(gus):
            gu = gu + bgu_ref[0, 0]
            up = pltpu.roll(gu, MOE_TN - 1, axis=1)
            gate = jnp.minimum(gu, SWIGLU_LIMIT)
            up = jnp.clip(up, -SWIGLU_LIMIT, SWIGLU_LIMIT)
            sig = 0.5 * jnp.tanh((0.5 * SWIGLU_ALPHA) * gate) + 0.5
            h = ((up + 1.0) * gate * sig).astype(BF16)
            hc = [jnp.dot(h[:, MOE_SEL * s:MOE_SEL * (s + 1)], pick_even, preferred_element_type=F32).astype(BF16)
                  for s in range(MOE_TN // MOE_SEL)]
            hs[j, step * p:step * (p + 1), :] = jnp.concatenate(hc, axis=1)

    def down(rows):
        acc = jnp.dot(hs[0, 0:rows, :], wdn_ref[0, 0, 0:MOE_TF, :].astype(BF16), preferred_element_type=F32)
        for c in range(1, MOE_NG):
            acc = acc + jnp.dot(hs[c, 0:rows, :], wdn_ref[0, 0, MOE_TF * c:MOE_TF * (c + 1), :].astype(BF16),
                                preferred_element_type=F32)
        y_ref[0:rows, :] = acc + bdn_ref[0, 0]
        if rows < MOE_TM:
            y_ref[rows:MOE_TM, :] = jnp.zeros((MOE_TM - rows, MOE_DN), F32)

    lower = 0
    for rows in MOE_ROWS:
        fits = jnp.logical_and(active, jnp.logical_and(nv > lower, nv <= rows))
        pl.when(jnp.logical_and(fits, j < MOE_NG))(functools.partial(gate_up, rows))
        pl.when(jnp.logical_and(fits, j >= MOE_NG))(functools.partial(down, rows))
        lower = rows

    @pl.when(jnp.logical_and(jnp.logical_not(active), j >= MOE_NG))
    def _():
        y_ref[...] = jnp.zeros_like(y_ref)


def _moe_experts(u, tile_e, n_used, n_valid, slot_tok, layer, w_gu, b_gu, w_dn, b_dn):
    n_tiles = tile_e.shape[0]
    nj = MOE_NG + MOE_ND
    jj = lambda t, j, nu: jnp.where(t < nu[0], j, nj - 1)
    jg = lambda t, j, nu: jnp.minimum(jj(t, j, nu), MOE_NG - 1)
    jd = lambda t, j, nu: jnp.maximum(jj(t, j, nu) - MOE_NG, 0)
    grid_spec = pltpu.PrefetchScalarGridSpec(
        num_scalar_prefetch=4,
        grid=(n_tiles, nj),
        in_specs=[pl.BlockSpec(memory_space=pl.ANY),
                  pl.BlockSpec((1, 1, D, MOE_TN), lambda t, j, te, nu, nv, tok: (layer, te[t], 0, jg(t, j, nu))),
                  pl.BlockSpec((1, 1, 1, MOE_TN), lambda t, j, te, nu, nv, tok: (layer, te[t], 0, jg(t, j, nu))),
                  pl.BlockSpec((1, 1, D_EXPERT, MOE_DN), lambda t, j, te, nu, nv, tok: (layer, te[t], 0, jd(t, j, nu))),
                  pl.BlockSpec((1, 1, 1, MOE_DN), lambda t, j, te, nu, nv, tok: (layer, te[t], 0, jd(t, j, nu)))],
        out_specs=pl.BlockSpec((MOE_TM, MOE_DN), lambda t, j, te, nu, nv, tok: (t, jnp.maximum(j - MOE_NG, 0))),
        scratch_shapes=[pltpu.VMEM((2, MOE_TM, D // 2), jnp.int32), pltpu.VMEM((MOE_TM, D), BF16),
                        pltpu.VMEM((MOE_NG, MOE_TM, MOE_TF), BF16), pltpu.SemaphoreType.DMA((2,))],
    )
    nl, ne = b_gu.shape[0], b_gu.shape[1]
    return pl.pallas_call(
        _moe_kernel,
        grid_spec=grid_spec,
        out_shape=jax.ShapeDtypeStruct((n_tiles * MOE_TM, D), F32),
        compiler_params=_cparams(("arbitrary", "arbitrary")),
        name="moe_experts",
    )(tile_e, n_used, n_valid, slot_tok, u, w_gu, b_gu.reshape(nl, ne, 1, 2 * D_EXPERT), w_dn, b_dn.reshape(nl, ne, 1, D))


CMB = 256


def _combine_kernel(slot_ref, h_ref, gate_ref, pg, sg, y_hbm, o_ref, buf, sem, *, npt_c):
    i = pl.program_id(0)

    def issue(r, carry):
        for k in range(TOP_K):
            s = slot_ref[(i * CMB + r) * TOP_K + k]
            pltpu.make_async_copy(y_hbm.at[pl.ds(s, 1)], buf.at[k, pl.ds(r, 1)], sem).start()
        return carry
    lax.fori_loop(0, CMB, issue, 0, unroll=4)
    for k in range(TOP_K):
        pltpu.make_async_copy(y_hbm.at[pl.ds(0, CMB)], buf.at[k], sem).wait()
    gates = gate_ref[...]
    mo = gates[:, 0:1] * buf[0]
    for k in range(1, TOP_K):
        mo = mo + gates[:, k:k + 1] * buf[k]
    o_ref[...] = h_ref[...] + _pick(i < npt_c, pg, sg) * mo


def _moe_combine(h, y, slot4, gate128, pmod, smod, npt_c, tiles_per_batch_c):
    t = h.shape[0]
    nsamp_tiles = (t // CMB) - npt_c
    grid_spec = pltpu.PrefetchScalarGridSpec(
        num_scalar_prefetch=1,
        grid=(t // CMB,),
        in_specs=[pl.BlockSpec((CMB, D), lambda i, s: (i, 0)),
                  pl.BlockSpec((CMB, 128), lambda i, s: (i, 0)),
                  pl.BlockSpec((1, 1, D), lambda i, s: (jnp.minimum(i // tiles_per_batch_c, npt_c // tiles_per_batch_c - 1), 0, 2)),
                  pl.BlockSpec((CMB, D), lambda i, s: (jnp.clip(i - npt_c, 0, nsamp_tiles - 1), 2)),
                  pl.BlockSpec(memory_space=pl.ANY)],
        out_specs=pl.BlockSpec((CMB, D), lambda i, s: (i, 0)),
        scratch_shapes=[pltpu.VMEM((TOP_K, CMB, D), F32), pltpu.SemaphoreType.DMA],
    )
    return pl.pallas_call(
        functools.partial(_combine_kernel, npt_c=npt_c),
        grid_spec=grid_spec,
        out_shape=jax.ShapeDtypeStruct((t, D), F32),
        compiler_params=_cparams(("arbitrary",)),
        name="moe_combine",
    )(slot4, h, gate128, pmod, smod, y)


def _moe_layer(h, g, pmod, smod, layer, rw, rb, w_gu, b_gu, w_dn, b_dn, npt, tiles_per_batch):
    t = h.shape[0]
    u, idx128, gate128 = _route(h, g, pmod, smod, rw[layer], rb[layer], npt, tiles_per_batch)
    idx4 = idx128[:, :TOP_K]
    onehot = idx4[:, :, None] == jnp.arange(N_EXPERTS, dtype=jnp.int32)[None, None, :]
    sel = jnp.any(onehot, axis=1).astype(jnp.int32)
    counts = jnp.sum(sel, axis=0)
    rank = jnp.cumsum(sel, axis=0) - 1
    padded = (counts + MOE_TM - 1) // MOE_TM * MOE_TM
    pad_end = jnp.cumsum(padded)
    pad_start = pad_end - padded
    n_tiles = -(-t * TOP_K // MOE_TM) + N_EXPERTS
    dest4 = jnp.sum(jnp.where(onehot, (pad_start[None, :] + rank)[:, None, :], 0), axis=2).astype(jnp.int32)
    tok = jnp.broadcast_to(jnp.arange(t, dtype=jnp.int32)[:, None], (t, TOP_K))
    slot_tok = jnp.zeros((n_tiles * MOE_TM,), jnp.int32).at[dest4.reshape(-1)].set(tok.reshape(-1))
    n_used = (pad_end[-1] // MOE_TM).astype(jnp.int32)
    tile_ids = jnp.arange(n_tiles, dtype=jnp.int32)
    tile_e = jnp.minimum(jnp.searchsorted(pad_end, jnp.minimum(tile_ids, n_used - 1) * MOE_TM, side='right'),
                         N_EXPERTS - 1).astype(jnp.int32)
    n_valid = jnp.where(tile_ids < n_used,
                        jnp.clip((pad_start + counts)[tile_e] - tile_ids * MOE_TM, 0, MOE_TM), 0).astype(jnp.int32)
    y = _moe_experts(u, tile_e, n_used.reshape(1), n_valid, slot_tok, layer, w_gu, b_gu, w_dn, b_dn)
    return _moe_combine(h, y, dest4.reshape(-1), gate128, pmod, smod,
                        npt * (TM // CMB), tiles_per_batch * (TM // CMB))


C_HEADS = 16
C_NOPE = 128
C_ROPE = 64
C_LORA = 512
ROPE_THETA = 10000.0
MLA_SCALE = (C_NOPE + C_ROPE) ** -0.5
HP = 128


def _rope_tables(pos):
    half = C_ROPE // 2
    inv = ROPE_THETA ** (-jnp.arange(half, dtype=F32) / half)
    ang = pos.astype(F32)[:, None] * inv[None, :]
    z = jnp.zeros_like(ang)
    cos = jnp.concatenate([jnp.cos(ang), jnp.cos(ang), z, z], axis=1)
    sin = jnp.concatenate([-jnp.sin(ang), jnp.sin(ang), z, z], axis=1)
    return cos, sin


def _rope128(x, cos, sin):
    n = x.shape[1]
    lane = lax.broadcasted_iota(jnp.int32, x.shape, 1) % HP
    partner = jnp.where(lane < C_ROPE // 2, pltpu.roll(x, n - C_ROPE // 2, axis=1), pltpu.roll(x, C_ROPE // 2, axis=1))
    return x * cos + partner * sin


def _mla_prep_kernel(z_ref, gq_ref, gkv_ref, gkr_ref, gqn_ref, gqr_ref, gkn_ref, wn_ref, wr_ref, wuk_ref,
                     cos_ref, sin_ref, lat_ref, kr_ref, krb_ref, qn_ref, qr_ref, qabs_ref, qg_ref, *, npt):
    i = pl.program_id(0)
    z = z_ref[...]
    cos = cos_ref[...]
    sin = sin_ref[...]
    lat_ref[...] = _rms(z[:, C_LORA:2 * C_LORA], gkv_ref[...])
    k128 = z[:, 2 * C_LORA:2 * C_LORA + HP]
    k128 = k128 * lax.rsqrt(jnp.sum(k128 * k128, axis=-1, keepdims=True) / C_ROPE + EPS) * gkr_ref[...]
    k128 = _rope128(k128, cos, sin)
    kr_ref[...] = k128[:, 0:C_ROPE]
    krb_ref[...] = k128.astype(BF16)
    cqn = _rms(z[:, 0:C_LORA], gq_ref[...]).astype(BF16)
    qn_all = jnp.dot(cqn, wn_ref[...], preferred_element_type=F32)
    qr_all = jnp.dot(cqn, wr_ref[...], preferred_element_type=F32)
    for h in range(C_HEADS):
        sl = slice(HP * h, HP * (h + 1))
        qn = _rms(qn_all[:, sl], gqn_ref[...]) * MLA_SCALE
        qn_ref[:, sl] = qn.astype(BF16)
        qg_ref[:, sl] = qn * gkn_ref[...]
        x = qr_all[:, sl]
        x = x * lax.rsqrt(jnp.sum(x * x, axis=-1, keepdims=True) / C_ROPE + EPS) * gqr_ref[...]
        qr_ref[:, sl] = (_rope128(x, cos, sin) * MLA_SCALE).astype(BF16)

    @pl.when(i == npt)
    def _():
        for h in range(C_HEADS):
            sl = slice(HP * h, HP * (h + 1))
            qabs_ref[:, C_LORA * h:C_LORA * (h + 1)] = _dot1(qg_ref[:, sl], wuk_ref[:, sl], NT).astype(BF16)


def _mla_prep(z1, gq, gkv, gkr, gqn, gqr, gkn, wn, wr, wuk, cos, sin, npt):
    t = z1.shape[0]
    zw = z1.shape[1]
    row = lambda w: pl.BlockSpec((TM, w), lambda i: (i, 0))
    full = lambda a: pl.BlockSpec(a.shape, lambda i: (0,) * a.ndim)
    pad64 = lambda g: jnp.concatenate([g, jnp.zeros((HP - C_ROPE,), F32)]).reshape(1, HP)
    args = (z1, gq.reshape(1, -1), gkv.reshape(1, -1), pad64(gkr), gqn.reshape(1, -1), pad64(gqr), gkn.reshape(1, -1),
            wn, wr, wuk, cos, sin)
    return pl.pallas_call(
        functools.partial(_mla_prep_kernel, npt=npt),
        grid=(t // TM,),
        in_specs=[row(zw)] + [full(a) for a in args[1:10]] + [row(HP), row(HP)],
        out_specs=[row(C_LORA), row(C_ROPE), row(HP), row(C_HEADS * HP), row(C_HEADS * HP),
                   pl.BlockSpec((TM, C_HEADS * C_LORA), lambda i: (0, 0))],
        out_shape=[jax.ShapeDtypeStruct((t, C_LORA), F32), jax.ShapeDtypeStruct((t, C_ROPE), F32),
                   jax.ShapeDtypeStruct((t, HP), BF16), jax.ShapeDtypeStruct((t, C_HEADS * HP), BF16),
                   jax.ShapeDtypeStruct((t, C_HEADS * HP), BF16), jax.ShapeDtypeStruct((TM, C_HEADS * C_LORA), BF16)],
        scratch_shapes=[pltpu.VMEM((TM, C_HEADS * HP), F32)],
        compiler_params=_cparams(("arbitrary",)),
        name="mla_prep",
    )(*args)


def _mla_kv_kernel(lat_ref, wuk_ref, wuv_ref, gkn_ref, kn_ref, v_ref):
    lat = lat_ref[...].astype(BF16)
    kn = jnp.dot(lat, wuk_ref[...], preferred_element_type=F32)
    for h in range(C_HEADS):
        sl = slice(HP * h, HP * (h + 1))
        kn_ref[:, sl] = _rms(kn[:, sl], gkn_ref[...]).astype(BF16)
    v_ref[...] = jnp.dot(lat, wuv_ref[...], preferred_element_type=F32).astype(BF16)


def _mla_kv(lat, wuk, wuv, gkn, rows):
    wide = C_HEADS * HP
    return pl.pallas_call(
        _mla_kv_kernel,
        grid=(rows // TM,),
        in_specs=[pl.BlockSpec((TM, C_LORA), lambda i: (i, 0)),
                  pl.BlockSpec((C_LORA, wide), lambda i: (0, 0)),
                  pl.BlockSpec((C_LORA, wide), lambda i: (0, 0)),
                  pl.BlockSpec((1, HP), lambda i: (0, 0))],
        out_specs=[pl.BlockSpec((TM, wide), lambda i: (i, 0))] * 2,
        out_shape=[jax.ShapeDtypeStruct((rows, wide), BF16)] * 2,
        compiler_params=_cparams(("parallel",)),
        name="mla_kv",
    )(lat, wuk, wuv, gkn.reshape(1, HP))


FQ = 512


def _mla_flash_kernel(qn_ref, qr_ref, kn_ref, kr_ref, v_ref, o_ref, kcat):
    qi = pl.program_id(2)

    @pl.when(qi == 0)
    def _():
        kcat[:, 0:HP] = kn_ref[...]
        kcat[:, HP:2 * HP] = kr_ref[...]

    q = jnp.concatenate([qn_ref[...], qr_ref[...]], axis=1)

    def block(kb, carry, diagonal):
        m, l, acc = carry
        ks = pl.ds(pl.multiple_of(kb * FQ, FQ), FQ)
        s = lax.dot_general(q, kcat[ks, :], (NT, ((), ())), preferred_element_type=F32)
        if diagonal:
            row = lax.broadcasted_iota(jnp.int32, (FQ, FQ), 0)
            col = lax.broadcasted_iota(jnp.int32, (FQ, FQ), 1)
            s = jnp.where(col <= row, s, -jnp.inf)
        m_new = jnp.maximum(m, jnp.max(s, axis=-1, keepdims=True))
        alpha = jnp.exp(m - m_new)
        p = jnp.exp(s - m_new)
        l = alpha * l + jnp.sum(p, axis=-1, keepdims=True)
        acc = alpha * acc + jnp.dot(p.astype(BF16), v_ref[ks, :], preferred_element_type=F32)
        return m_new, l, acc

    m0 = jnp.full((FQ, 1), -jnp.inf, F32)
    l0 = jnp.zeros((FQ, 1), F32)
    a0 = jnp.zeros((FQ, HP), F32)
    carry = lax.fori_loop(0, qi, lambda kb, c: block(kb, c, False), (m0, l0, a0))
    m, l, acc = block(qi, carry, True)
    o_ref[...] = (acc / l).astype(BF16)


def _mla_flash(qn, qr, kn, krb, v, nb, seq):
    nq = seq // FQ
    qspec = pl.BlockSpec((FQ, HP), lambda b, h, q: (b * nq + q, h))
    kspec = pl.BlockSpec((seq, HP), lambda b, h, q: (b, h))
    return pl.pallas_call(
        _mla_flash_kernel,
        grid=(nb, C_HEADS, nq),
        in_specs=[qspec, qspec, kspec, pl.BlockSpec((seq, HP), lambda b, h, q: (b, 0)), kspec],
        out_specs=qspec,
        out_shape=jax.ShapeDtypeStruct((nb * seq, C_HEADS * HP), BF16),
        scratch_shapes=[pltpu.VMEM((seq, 2 * HP), BF16)],
        compiler_params=_cparams(("parallel", "parallel", "arbitrary")),
        name="mla_flash",
    )(qn, qr, kn, krb, v)


PPS = 16
PPG = 4


def _mla_s_kernel(*refs, n_steps):
    pt_ref = refs[0]
    lat_refs = refs[1:1 + PPS]
    kr_refs = refs[1 + PPS:1 + 2 * PPS]
    wukt_ref, qabs_ref, qr_ref, latn_ref, krn_ref, o_ref, wst, latb, lg, m_ref, l_ref, acc_ref = refs[1 + 2 * PPS:]
    b = pl.program_id(0)
    kt = pl.program_id(1)
    nq = qabs_ref.shape[1]
    nheads = C_HEADS
    ntok = nq // nheads

    @pl.when(jnp.logical_and(b == 0, kt == 0))
    def _():
        wst[0:nheads * HP, :] = wukt_ref[...]

    @pl.when(kt == 0)
    def _():
        wst[nheads * HP:nheads * HP + nq, :] = qabs_ref[0]
        m_ref[...] = jnp.full_like(m_ref, -jnp.inf)
        l_ref[...] = jnp.zeros_like(l_ref)
        acc_ref[...] = jnp.zeros_like(acc_ref)

    qr = qr_ref[0]

    def up_project(lat_bf16):
        return lax.dot_general(wst[...], lat_bf16, (NT, ((), ())), preferred_element_type=F32)

    def key_logits(r, krt_f32):
        nkeys = r.shape[1]
        kraw = r[0:nheads * HP].reshape(nheads, HP, nkeys)
        rs = lax.rsqrt(jnp.sum(kraw * kraw, axis=1) / HP + EPS)
        lraw = r[nheads * HP:nheads * HP + nq].reshape(ntok, nheads, nkeys) * rs[None]
        return lraw.reshape(nq, nkeys) + _dot1(qr, krt_f32)

    def update(state, logits, lat_bf16):
        m_old, l_old, acc_old = state
        m_new = jnp.maximum(m_old, jnp.max(logits, axis=-1, keepdims=True))
        alpha = jnp.exp(m_old - m_new)
        p = jnp.exp(logits - m_new)
        l_new = alpha * l_old + jnp.sum(p, axis=-1, keepdims=True)
        acc_new = alpha * acc_old + jnp.dot(p.astype(BF16), lat_bf16, preferred_element_type=F32)
        return m_new, l_new, acc_new

    def load_group(g):
        pages = range(PPG * g, PPG * (g + 1))
        return jnp.concatenate([lat_refs[p][0, 0] for p in pages], axis=0).astype(BF16)

    gk = PPG * PAGE
    for g in range(PPS // PPG):
        lat_g = load_group(g)
        krt_g = jnp.concatenate([kr_refs[p][0, 0] for p in range(PPG * g, PPG * (g + 1))], axis=1)
        latb[gk * g:gk * (g + 1), :] = lat_g
        lg[:, gk * g:gk * (g + 1)] = key_logits(up_project(lat_g), krt_g)
    state = update((m_ref[...], l_ref[...], acc_ref[...]), lg[...], latb[...])
    m_ref[...], l_ref[...], acc_ref[...] = state

    @pl.when(kt == n_steps - 1)
    def _():
        latn = jnp.concatenate([latn_ref[0], jnp.zeros((PAGE - 8, C_LORA), F32)], axis=0).astype(BF16)
        logits = key_logits(up_project(latn), krn_ref[0])
        qtok = lax.broadcasted_iota(jnp.int32, (nq, PAGE), 0) // nheads
        slot = lax.broadcasted_iota(jnp.int32, (nq, PAGE), 1)
        _, l_fin, acc_fin = update(state, jnp.where(slot <= qtok, logits, -jnp.inf), latn)
        o_ref[0] = acc_fin / l_fin


def _mla_sample(page_table, lat_cache, kr_cache, ci, wukt, qabs, qr, lat_new8, kr_new8):
    nb, npages = page_table.shape
    n_steps = npages // PPS
    nq = qabs.shape[1]

    def page_spec(k, rows, width):
        return pl.BlockSpec((1, 1, rows, width), lambda b, t, pt: (ci, pt[b * npages + t * PPS + k], 0, 0))

    kr_cache_t = jnp.swapaxes(kr_cache, 2, 3)
    kr_new_t = jnp.pad(jnp.swapaxes(kr_new8, 1, 2), ((0, 0), (0, 0), (0, PAGE - kr_new8.shape[1])))
    grid_spec = pltpu.PrefetchScalarGridSpec(
        num_scalar_prefetch=1,
        grid=(nb, n_steps),
        in_specs=[page_spec(k, PAGE, C_LORA) for k in range(PPS)] + [page_spec(k, C_ROPE, PAGE) for k in range(PPS)]
                 + [pl.BlockSpec(wukt.shape, lambda b, t, pt: (0, 0)),
                    pl.BlockSpec((1, nq, C_LORA), lambda b, t, pt: (b, 0, 0)),
                    pl.BlockSpec((1, nq, C_ROPE), lambda b, t, pt: (b, 0, 0)),
                    pl.BlockSpec((1, 8, C_LORA), lambda b, t, pt: (b, 0, 0)),
                    pl.BlockSpec((1, C_ROPE, PAGE), lambda b, t, pt: (b, 0, 0))],
        out_specs=pl.BlockSpec((1, nq, C_LORA), lambda b, t, pt: (b, 0, 0)),
        scratch_shapes=[pltpu.VMEM((C_HEADS * HP + nq, C_LORA), BF16),
                        pltpu.VMEM((PPS * PAGE, C_LORA), BF16),
                        pltpu.VMEM((nq, PPS * PAGE), F32),
                        pltpu.VMEM((nq, 1), F32), pltpu.VMEM((nq, 1), F32), pltpu.VMEM((nq, C_LORA), F32)],
    )
    return pl.pallas_call(
        functools.partial(_mla_s_kernel, n_steps=n_steps),
        grid_spec=grid_spec,
        out_shape=jax.ShapeDtypeStruct((nb, nq, C_LORA), F32),
        compiler_params=_cparams(("arbitrary", "arbitrary")),
        name="mla_sample",
    )(page_table.reshape(-1), *([lat_cache] * PPS), *([kr_cache_t] * PPS), wukt, qabs, qr, lat_new8, kr_new_t)


def _mla_uv_kernel(c_ref, w_ref, o_ref):
    o_ref[...] = _dot1(c_ref[...], w_ref[0])


def _mla_uv(ctx, wuv3):
    rows = ctx.shape[0]
    return pl.pallas_call(
        _mla_uv_kernel,
        grid=(C_HEADS,),
        in_specs=[pl.BlockSpec((rows, C_LORA), lambda h: (0, h)),
                  pl.BlockSpec((1, C_LORA, HP), lambda h: (h, 0, 0))],
        out_specs=pl.BlockSpec((rows, HP), lambda h: (0, h)),
        out_shape=jax.ShapeDtypeStruct((rows, C_HEADS * HP), F32),
        compiler_params=_cparams(("parallel",)),
        name="mla_uv",
    )(ctx, wuv3)


def kernel(x_prompt, x_sample, state_hgrn, cache_swa_kv, cache_mla_latent, cache_mla_krope, page_table, c_prompt, c_sample, ada_w, ada_b, norm_g, rel_bias, ab_w_in, ab_w_out, hgrn_lb_logits, hgrn_norm_g, swa_q_g, swa_k_g, mla_w_in, mla_q_a_g, mla_kv_a_g, mla_w_uq, mla_w_uk, mla_w_uv, mla_qn_g, mla_qr_g, mla_kn_g, mla_kr_g, mla_w_o, moe_router_w, moe_router_b, moe_w_gu, moe_b_gu, moe_w_dn, moe_b_dn):
    nb, seq, _ = x_prompt.shape
    nsb, nst, _ = x_sample.shape
    tp, ts = nb * seq, nsb * nst
    assert ts == TM and seq % TM == 0 and nst <= 4
    npt, tpb = tp // TM, seq // TM
    x = jnp.concatenate([x_prompt.reshape(tp, D), x_sample.reshape(ts, D)], axis=0).astype(F32)

    nc = nb + nsb
    c_all = jnp.concatenate([c_prompt, c_sample, jnp.zeros((-nc % 8, D), c_prompt.dtype)], axis=0).astype(F32)
    mods = _adaln(c_all, ada_w, ada_b)

    def mod(s):
        return mods[s, 0:nb].reshape(nb, 1, 3 * D), jnp.repeat(mods[s, nb:nc], nst, axis=0)

    pm, sm = mod(0)
    z = _normmod_mm(x, norm_g[0, 0], pm, sm, ab_w_in[0].astype(BF16), 1792, npt, tpb)
    zs3 = z[tp:].reshape(nsb, nst, z.shape[1])
    oa_p, hg_p = _hgrn_prompt(z, hgrn_lb_logits, hgrn_norm_g[0], nb, seq, 0)
    oa_s, hg_s = _hgrn_sample(zs3, hgrn_lb_logits, hgrn_norm_g[0], state_hgrn, 0)
    ob_p, k_p, v_p = _dilated_prompt(z, swa_q_g[0], swa_k_g[0], _band_tables(rel_bias), nb, seq)
    buf_len = cache_swa_kv.shape[2]
    ob_s, k_s, v_s = _dilated_sample(zs3, swa_q_g[0], swa_k_g[0], cache_swa_kv[0], _sample_tables(rel_bias, nst, buf_len))
    w_out = ab_w_out[0].astype(BF16)
    h = _proj_residual([oa_p, ob_p], [oa_s.reshape(ts, 1024), ob_s.reshape(ts, 1024)],
                       [w_out[:1024], w_out[1024:]], x, pm, sm, npt, tpb)
    pm, sm = mod(1)
    h = _moe_layer(h, norm_g[0, 1], pm, sm, 0, moe_router_w, moe_router_b, moe_w_gu, moe_b_gu, moe_w_dn, moe_b_dn, npt, tpb)

    pm, sm = mod(2)
    w_in1 = jnp.pad(mla_w_in[0], ((0, 0), (0, HP - C_ROPE))).astype(BF16)
    z1 = _normmod_mm(h, norm_g[1, 0], pm, sm, w_in1, w_in1.shape[1], npt, tpb)
    past = page_table.shape[1] * PAGE
    pos = jnp.concatenate([jnp.tile(jnp.arange(seq), nb), jnp.tile(past + jnp.arange(nst), nsb)])
    cos, sin = _rope_tables(pos)
    wq = mla_w_uq[0]
    wn = wq[:, :, :C_NOPE].reshape(C_LORA, C_HEADS * HP).astype(BF16)
    wr = jnp.pad(wq[:, :, C_NOPE:], ((0, 0), (0, 0), (0, HP - C_ROPE))).reshape(C_LORA, C_HEADS * HP).astype(BF16)
    wuk = mla_w_uk[0].reshape(C_LORA, C_HEADS * HP).astype(BF16)
    wuv = mla_w_uv[0].reshape(C_LORA, C_HEADS * HP).astype(BF16)
    lat, kr, krb, qn, qr, qabs = _mla_prep(z1, mla_q_a_g[0], mla_kv_a_g[0], mla_kr_g[0], mla_qn_g[0], mla_qr_g[0],
                                           mla_kn_g[0], wn, wr, wuk, cos, sin, npt)
    kn, v = _mla_kv(lat, wuk, wuv, mla_kn_g[0], tp)
    ctx_p = _mla_flash(qn, qr, kn, krb, v, nb, seq)
    qabs3 = qabs.reshape(nsb, nst * C_HEADS, C_LORA)
    qr_s = qr[tp:].reshape(ts, C_HEADS, HP)[:, :, :C_ROPE].reshape(nsb, nst * C_HEADS, C_ROPE)
    lat_new8 = jnp.pad(lat[tp:].reshape(nsb, nst, C_LORA), ((0, 0), (0, 8 - nst), (0, 0)))
    kr_new8 = jnp.pad(kr[tp:].reshape(nsb, nst, C_ROPE), ((0, 0), (0, 8 - nst), (0, 0)))
    ctx_s = _mla_sample(page_table, cache_mla_latent, cache_mla_krope, 0, wuk.T, qabs3, qr_s, lat_new8, kr_new8)
    wuv3 = jnp.transpose(mla_w_uv[0], (1, 0, 2)).astype(BF16)
    o_s = _mla_uv(ctx_s.reshape(ts, C_HEADS * C_LORA), wuv3)
    h = _proj_residual([ctx_p], [o_s], [mla_w_o[0].astype(BF16)], h, pm, sm, npt, tpb)
    pm, sm = mod(3)
    h = _moe_layer(h, norm_g[1, 1], pm, sm, 1, moe_router_w, moe_router_b, moe_w_gu, moe_b_gu, moe_w_dn, moe_b_dn, npt, tpb)

    y_prompt = h[:tp].reshape(nb, seq, D).astype(x_prompt.dtype)
    y_sample = h[tp:].reshape(nsb, nst, D).astype(x_sample.dtype)
    hgrn_prompt = hg_p[None].astype(state_hgrn.dtype)
    hgrn_sample = hg_s.astype(state_hgrn.dtype)
    kv = lambda k, v, b, t: jnp.stack([k.reshape(b, t, 8, 128), v.reshape(b, t, 8, 128)], axis=2)[None]
    swa_prompt = kv(k_p, v_p, nb, seq).astype(cache_swa_kv.dtype)
    swa_sample = kv(k_s, v_s, nsb, nst).astype(cache_swa_kv.dtype)
    lat_prompt = lat[:tp].reshape(1, nb, seq, C_LORA).astype(cache_mla_latent.dtype)
    lat_sample = lat[tp:].reshape(1, nsb, nst, C_LORA).astype(cache_mla_latent.dtype)
    krope_prompt = kr[:tp].reshape(1, nb, seq, C_ROPE).astype(cache_mla_krope.dtype)
    krope_sample = kr[tp:].reshape(1, nsb, nst, C_ROPE).astype(cache_mla_krope.dtype)
    return (y_prompt, y_sample, hgrn_prompt, hgrn_sample, swa_prompt, swa_sample,
            lat_prompt, lat_sample, krope_prompt, krope_sample)
```

```python
import functools
import math

import numpy as np
import jax
import jax.numpy as jnp
from jax import lax
from jax.experimental import pallas as pl
from jax.experimental.pallas import tpu as pltpu

F32 = jnp.float32
BF16 = jnp.bfloat16
EPS = 1e-6

D = 2048
TM = 512
N_EXPERTS = 32
TOP_K = 4
D_EXPERT = 2048
SWIGLU_LIMIT = 7.0
SWIGLU_ALPHA = 1.702
PAGE = 128
V7X_VMEM_LIMIT = 56 * 1024 * 1024

NN = ((1,), (0,))
NT = ((1,), (1,))
TN = ((0,), (0,))


def _cparams(sem, vmem=V7X_VMEM_LIMIT):
    return pltpu.CompilerParams(dimension_semantics=sem, vmem_limit_bytes=vmem)


def _dot1(a, b, dims=NN):
    return lax.dot_general(a.astype(BF16), b.astype(BF16), (dims, ((), ())), preferred_element_type=F32)


def _split2(x):
    hi = x.astype(BF16)
    lo = (x - hi.astype(F32)).astype(BF16)
    return hi, lo


def _split3(x):
    hi = x.astype(BF16)
    r = x - hi.astype(F32)
    mid = r.astype(BF16)
    lo = (r - mid.astype(F32)).astype(BF16)
    return hi, mid, lo


def _dot_exact_rhs(a01, x):
    hi, mid, lo = _split3(x)
    f = lambda p: lax.dot_general(a01, p, (NN, ((), ())), preferred_element_type=F32)
    return f(hi) + f(mid) + f(lo)


def _dot3(a, b, dims=NN):
    ah, al = _split2(a)
    bh, bl = _split2(b)
    f = lambda p, q: lax.dot_general(p, q, (dims, ((), ())), preferred_element_type=F32)
    return f(ah, bh) + f(ah, bl) + f(al, bh)


def _sigmoid(x):
    return 1.0 / (1.0 + jnp.exp(-x))


def _rms(x, g):
    return x * lax.rsqrt(jnp.mean(x * x, axis=-1, keepdims=True) + EPS) * g


def _ada_kernel(c_ref, w_ref, b_ref, o_ref):
    c = c_ref[...]
    s = c * _sigmoid(c)
    o_ref[0] = _dot1(s, w_ref[0, 0]) + b_ref[0, 0]


def _adaln(c_all, ada_w, ada_b):
    r = c_all.shape[0]
    tn = 1024
    depth = ada_w.shape[0]
    b4 = ada_b.reshape(depth, 2, 1, 3 * D)
    return pl.pallas_call(
        _ada_kernel,
        grid=(depth * 2, 3 * D // tn),
        in_specs=[pl.BlockSpec((r, D), lambda s, j: (0, 0)),
                  pl.BlockSpec((1, 1, D, tn), lambda s, j: (s // 2, s % 2, 0, j)),
                  pl.BlockSpec((1, 1, 1, tn), lambda s, j: (s // 2, s % 2, 0, j))],
        out_specs=pl.BlockSpec((1, r, tn), lambda s, j: (s, 0, j)),
        out_shape=jax.ShapeDtypeStruct((depth * 2, r, 3 * D), F32),
        compiler_params=_cparams(("parallel", "parallel")),
        name="adaln",
    )(c_all, ada_w, b4)


def _pick(is_prompt, p_ref, s_ref):
    return jnp.where(is_prompt, p_ref[0], s_ref[...])


def _mod_specs(npt, rows_per_batch_tiles, col, tn=D, with_j=False):
    if with_j:
        pm = pl.BlockSpec((1, 1, tn), lambda i, j: (jnp.minimum(i // rows_per_batch_tiles, npt // rows_per_batch_tiles - 1), 0, col * (D // tn) + j))
        sm = pl.BlockSpec((TM, tn), lambda i, j: (0, col * (D // tn) + j))
    else:
        pm = pl.BlockSpec((1, 1, tn), lambda i, *_: (jnp.minimum(i // rows_per_batch_tiles, npt // rows_per_batch_tiles - 1), 0, col))
        sm = pl.BlockSpec((TM, tn), lambda i, *_: (0, col))
    return pm, sm


def _nm_mm_kernel(x_ref, g_ref, psh, psc, ssh, ssc, w_ref, o_ref, xn_ref, *, npt):
    i = pl.program_id(0)
    j = pl.program_id(1)

    @pl.when(j == 0)
    def _():
        xr = _rms(x_ref[...], g_ref[...])
        sc = _pick(i < npt, psc, ssc)
        sh = _pick(i < npt, psh, ssh)
        xn_ref[...] = (xr * (1.0 + sc) + sh).astype(BF16)

    o_ref[...] = jnp.dot(xn_ref[...], w_ref[...], preferred_element_type=F32)


def _normmod_mm(x, g, pmod, smod, w_bf16, tn, npt, tiles_per_batch):
    t = x.shape[0]
    n = w_bf16.shape[1]
    psh, ssh = _mod_specs(npt, tiles_per_batch, 0)
    psc, ssc = _mod_specs(npt, tiles_per_batch, 1)
    return pl.pallas_call(
        functools.partial(_nm_mm_kernel, npt=npt),
        grid=(t // TM, n // tn),
        in_specs=[pl.BlockSpec((TM, D), lambda i, j: (i, 0)),
                  pl.BlockSpec((1, D), lambda i, j: (0, 0)),
                  psh, psc, ssh, ssc,
                  pl.BlockSpec((D, tn), lambda i, j: (0, j))],
        out_specs=pl.BlockSpec((TM, tn), lambda i, j: (i, j)),
        out_shape=jax.ShapeDtypeStruct((t, n), F32),
        scratch_shapes=[pltpu.VMEM((TM, D), BF16)],
        compiler_params=_cparams(("parallel", "arbitrary")),
        name="normmod_mm",
    )(x, g.reshape(1, D), pmod, pmod, smod, smod, w_bf16)


def _proj_res_kernel(*refs, npt, nk):
    xp = refs[0:nk]
    xs = refs[nk:2 * nk]
    ws = refs[2 * nk:3 * nk]
    h_ref, pg, sg, o_ref = refs[3 * nk:3 * nk + 4]
    xb = refs[3 * nk + 4:]
    i = pl.program_id(0)
    j = pl.program_id(1)

    @pl.when(jnp.logical_and(j == 0, i < npt))
    def _():
        for k in range(nk):
            xb[k][...] = xp[k][...].astype(BF16)

    @pl.when(jnp.logical_and(j == 0, i >= npt))
    def _():
        for k in range(nk):
            xb[k][...] = xs[k][...].astype(BF16)

    acc = jnp.dot(xb[0][...], ws[0][...], preferred_element_type=F32)
    for k in range(1, nk):
        acc = acc + jnp.dot(xb[k][...], ws[k][...], preferred_element_type=F32)
    gate = _pick(i < npt, pg, sg)
    o_ref[...] = h_ref[...] + gate * acc


def _proj_residual(xps, xss, ws, h, pmod, smod, npt, tiles_per_batch, tn=1024):
    nk = len(ws)
    t = h.shape[0]
    ks = [w.shape[0] for w in ws]
    pg, sg = _mod_specs(npt, tiles_per_batch, 2, tn=tn, with_j=True)
    in_specs = ([pl.BlockSpec((TM, k), lambda i, j: (jnp.minimum(i, npt - 1), 0)) for k in ks]
                + [pl.BlockSpec((TM, k), lambda i, j: (0, 0)) for k in ks]
                + [pl.BlockSpec((k, tn), lambda i, j: (0, j)) for k in ks]
                + [pl.BlockSpec((TM, tn), lambda i, j: (i, j)), pg, sg])
    return pl.pallas_call(
        functools.partial(_proj_res_kernel, npt=npt, nk=nk),
        grid=(t // TM, D // tn),
        in_specs=in_specs,
        out_specs=pl.BlockSpec((TM, tn), lambda i, j: (i, j)),
        out_shape=jax.ShapeDtypeStruct((t, D), F32),
        scratch_shapes=[pltpu.VMEM((TM, k), BF16) for k in ks],
        compiler_params=_cparams(("parallel", "arbitrary")),
        name="proj_residual",
    )(*xps, *xss, *ws, h, pmod, smod)


HG_C = 64
HG_B = 16
HG_STEP = 256
HG_HEADS = 4


def _lower_bound(lbl, layer):
    e = jnp.exp(lbl - jnp.max(lbl, axis=0, keepdims=True))
    return jnp.sum(e[0:layer + 1], axis=0, keepdims=True) / jnp.sum(e, axis=0, keepdims=True)


def _hgrn_chunk(q, af, v, st, lb, tri):
    c, dk = q.shape
    nb = c // HG_B
    f = lb + (1.0 - lb) * _sigmoid(af)
    g = jnp.log(f)
    k = 1.0 - f
    cum = _dot_exact_rhs(tri, g)
    o = _dot1(q * jnp.exp(cum), st, NT)
    q3 = q.reshape(nb, HG_B, dk)
    k3 = k.reshape(nb, HG_B, dk)
    c3 = cum.reshape(nb, HG_B, dk)
    tpos = lax.broadcasted_iota(jnp.int32, (nb, HG_B, dk), 1)
    lane3 = lax.broadcasted_iota(jnp.int32, (nb, HG_B, dk), 2)
    dmat = jnp.zeros((nb, HG_B, dk), F32)
    for s in range(HG_B):
        rel = c3 - c3[:, s:s + 1, :]
        a = q3 * k3[:, s:s + 1, :] * jnp.exp(jnp.where(tpos >= s, rel, -jnp.inf))
        col = jnp.sum(a, axis=-1, keepdims=True)
        dmat = jnp.where(lane3 == s, col, dmat)
    lane_c = lax.broadcasted_iota(jnp.int32, (HG_B, c), 1)
    rows = []
    for blk in range(nb):
        r0 = HG_B * blk
        d_blk = dmat[blk]
        if blk == 0:
            rows.append(d_blk[:, :c])
            continue
        d_blk = pltpu.roll(d_blk, r0, axis=1)[:, :c]
        cb = cum[r0 - 1:r0, :]
        qs = q[r0:r0 + HG_B] * jnp.exp(cum[r0:r0 + HG_B] - cb)
        ks = k * jnp.exp(jnp.minimum(cb - cum, 0.0))
        s_off = _dot1(qs, ks, NT)
        rows.append(jnp.where(lane_c < r0, s_off, 0.0) + d_blk)
    scores = jnp.concatenate(rows, axis=0)
    o = o + _dot1(scores, v)
    last = cum[c - 1:c, :]
    st_new = jnp.exp(last) * st + _dot1(v, k * jnp.exp(last - cum), TN)
    return o, st_new


def _hgrn_p_kernel(q_ref, f_ref, v_ref, g_ref, lbl_ref, gn_ref, o_ref, s_ref, st_ref, *, layer, nsteps):
    tb = pl.program_id(2)

    @pl.when(tb == 0)
    def _():
        st_ref[...] = jnp.zeros_like(st_ref)

    lb = _lower_bound(lbl_ref[...], layer)
    ri = lax.broadcasted_iota(jnp.int32, (HG_C, HG_C), 0)
    ci = lax.broadcasted_iota(jnp.int32, (HG_C, HG_C), 1)
    tri = jnp.where(ri >= ci, 1.0, 0.0).astype(BF16)
    sts = [st_ref[hh] for hh in range(HG_HEADS)]
    for c in range(HG_STEP // HG_C):
        sl = slice(HG_C * c, HG_C * (c + 1))
        for hh in range(HG_HEADS):
            hl = slice(128 * hh, 128 * (hh + 1))
            o, sts[hh] = _hgrn_chunk(q_ref[sl, hl], f_ref[sl, hl], v_ref[sl, hl], sts[hh], lb[:, hl], tri)
            o_ref[sl, hl] = (_rms(o, gn_ref[...]) * _sigmoid(g_ref[sl, hl])).astype(BF16)
    for hh in range(HG_HEADS):
        st_ref[hh] = sts[hh]

    @pl.when(tb == nsteps - 1)
    def _():
        for hh in range(HG_HEADS):
            s_ref[0, hh] = sts[hh].T


def _hgrn_prompt(z, lb_logits, gn, nb, seq, layer):
    nh = 8
    nsteps = seq // HG_STEP
    hw = 128 * HG_HEADS
    ng = nh // HG_HEADS
    zspec = lambda c0: pl.BlockSpec((HG_STEP, hw), lambda b, h, t: (b * nsteps + t, c0 * ng + h))
    return pl.pallas_call(
        functools.partial(_hgrn_p_kernel, layer=layer, nsteps=nsteps),
        grid=(nb, ng, nsteps),
        in_specs=[zspec(0), zspec(1), zspec(2), zspec(3),
                  pl.BlockSpec((lb_logits.shape[0], hw), lambda b, h, t: (0, h)),
                  pl.BlockSpec((1, 128), lambda b, h, t: (0, 0))],
        out_specs=[pl.BlockSpec((HG_STEP, hw), lambda b, h, t: (b * nsteps + t, h)),
                   pl.BlockSpec((1, HG_HEADS, 128, 128), lambda b, h, t: (b, h, 0, 0))],
        out_shape=[jax.ShapeDtypeStruct((nb * seq, nh * 128), BF16),
                   jax.ShapeDtypeStruct((nb, nh, 128, 128), F32)],
        scratch_shapes=[pltpu.VMEM((HG_HEADS, 128, 128), F32)],
        compiler_params=_cparams(("parallel", "parallel", "arbitrary")),
        name="hgrn_prompt",
    )(z, z, z, z, lb_logits, gn.reshape(1, 128))


def _hgrn_s_kernel(q_ref, f_ref, v_ref, g_ref, lbl_ref, gn_ref, s0_ref, o_ref, s_ref, *, layer):
    nt = q_ref.shape[1]
    lb_all = _lower_bound(lbl_ref[...], layer)
    tpos = lax.broadcasted_iota(jnp.int32, (nt, 128), 0)
    outs = []
    for h in range(8):
        sl = slice(128 * h, 128 * (h + 1))
        q = q_ref[0, :, sl]
        v = v_ref[0, :, sl]
        lb = lb_all[:, sl]
        f = lb + (1.0 - lb) * _sigmoid(f_ref[0, :, sl])
        g = jnp.log(f)
        k = 1.0 - f
        crow = [g[0:1]]
        for t in range(1, nt):
            crow.append(crow[-1] + g[t:t + 1])
        cum = jnp.concatenate(crow, axis=0)
        last = crow[-1]
        s0 = s0_ref[0, 0, h]
        qe = jnp.concatenate([q * jnp.exp(cum), jnp.zeros((8 - nt, 128), F32)], axis=0)
        o = _dot1(qe, s0)[0:nt]
        for s in range(nt):
            a = q * k[s:s + 1] * jnp.exp(jnp.where(tpos >= s, cum - cum[s:s + 1], -jnp.inf))
            o = o + jnp.sum(a, axis=-1, keepdims=True) * v[s:s + 1]
        kp = k * jnp.exp(last - cum)
        w8 = jnp.concatenate([jnp.exp(last), kp, jnp.zeros((8 - 1 - nt, 128), F32)], axis=0)
        wt = jnp.concatenate([w8, jnp.zeros((120, 128), F32)], axis=0).T
        s_new = wt[:, 0:1] * s0
        for s in range(nt):
            s_new = s_new + wt[:, 1 + s:2 + s] * v[s:s + 1]
        s_ref[0, 0, h] = s_new
        outs.append(_rms(o, gn_ref[...]) * _sigmoid(g_ref[0, :, sl]))
    o_ref[0] = jnp.concatenate(outs, axis=1)


def _hgrn_sample(zs3, lb_logits, gn, state, layer):
    nb, nt, _ = zs3.shape
    zspec = lambda c: pl.BlockSpec((1, nt, 1024), lambda b: (b, 0, c))
    sspec = pl.BlockSpec((1, 1, 8, 128, 128), lambda b: (0, b, 0, 0, 0))
    return pl.pallas_call(
        functools.partial(_hgrn_s_kernel, layer=layer),
        grid=(nb,),
        in_specs=[zspec(0), zspec(1), zspec(2), zspec(3),
                  pl.BlockSpec(lb_logits.shape, lambda b: (0, 0)),
                  pl.BlockSpec((1, 128), lambda b: (0, 0)),
                  sspec],
        out_specs=[pl.BlockSpec((1, nt, 1024), lambda b: (b, 0, 0)), sspec],
        out_shape=[jax.ShapeDtypeStruct((nb, nt, 1024), F32),
                   jax.ShapeDtypeStruct(state.shape, F32)],
        compiler_params=_cparams(("parallel",)),
        name="hgrn_sample",
    )(zs3, zs3, zs3, zs3, lb_logits, gn.reshape(1, 128), state)


DILATED = ((128, 1), (512, 4), (2048, 16))
REL_BUCKETS = 32
REL_MAX_DISTANCE = 2048
ATT_SCALE = 128 ** -0.5
QB = 128


def _rel_bucket_np(dist):
    exact = REL_BUCKETS // 2
    d = np.asarray(dist)
    far = exact + (np.log(np.maximum(d, 1).astype(np.float32) / np.float32(exact))
                   / np.float32(math.log(REL_MAX_DISTANCE / exact)) * np.float32(REL_BUCKETS - exact)).astype(np.int32)
    return np.where(d < exact, d, np.minimum(far, REL_BUCKETS - 1)).astype(np.int32)


def _band_tables(rel_bias):
    rb = rel_bias.astype(F32)
    tabs = []
    for (window, dil), width in zip(DILATED, (2 * QB, 2 * QB, QB)):
        period = width + QB
        i = np.arange(period + 1)
        k = np.where(i < width, i, i - (period + 1))
        steps = (width - QB) - k
        valid = (steps >= 0) & (steps <= window // dil)
        bucket = _rel_bucket_np(np.clip(steps, 0, window // dil) * dil)
        u = jnp.where(jnp.asarray(valid)[None, :], rb[jnp.asarray(bucket)].T, -jnp.inf)
        flat = jnp.tile(u, (1, QB))[:, :QB * period]
        tabs.append(flat.reshape(rb.shape[1], QB, period)[:, :, :width])
    return tabs


def _soft_block(qb, kw, vw, bias):
    l = _dot1(qb, kw, NT) * ATT_SCALE + bias
    m = jnp.max(l, axis=-1, keepdims=True)
    p = jnp.exp(l - m)
    return m, jnp.sum(p, axis=-1, keepdims=True), _dot1(p, vw)


def _dil_p_kernel(q_ref, k_ref, v_ref, gq_ref, gk_ref, b1_ref, b2_ref, b3_ref, o_ref, ko_ref, vo_ref,
                  qs, ks, vs, a1, a2, a3, m1, m2, m3, l1, l2, l3):
    seq = q_ref.shape[0]
    kn = _rms(k_ref[...], gk_ref[...])
    v = v_ref[...]
    ko_ref[...] = kn
    vo_ref[...] = v
    qs[...] = _rms(q_ref[...], gq_ref[...])
    ks[...] = kn
    vs[...] = v
    zpad = jnp.zeros((QB, 128), F32)
    first_cols = lax.broadcasted_iota(jnp.int32, (QB, 2 * QB), 1) >= QB

    def banded(qsub, ksub, vsub, bias, n_blocks, store):
        for i in range(n_blocks):
            qb = qsub(i)
            if i == 0:
                kw = jnp.concatenate([zpad, ksub(0)], axis=0)
                vw = jnp.concatenate([zpad, vsub(0)], axis=0)
                bb = jnp.where(first_cols, bias, -jnp.inf)
            else:
                kw = jnp.concatenate([ksub(i - 1), ksub(i)], axis=0)
                vw = jnp.concatenate([vsub(i - 1), vsub(i)], axis=0)
                bb = bias
            store(i, *_soft_block(qb, kw, vw, bb))

    def st1(i, m, l, a):
        sl = slice(QB * i, QB * (i + 1))
        m1[sl, :] = m
        l1[sl, :] = l
        a1[sl, :] = a
    blk = lambda ref: (lambda i: ref[QB * i:QB * (i + 1), :])
    banded(blk(qs), blk(ks), blk(vs), b1_ref[0], seq // QB, st1)

    d2 = DILATED[1][1]
    for r in range(d2):
        sub = lambda ref: (lambda i: ref[pl.ds(r + d2 * QB * i, QB, stride=d2), :])

        def st2(i, m, l, a):
            idx = pl.ds(r + d2 * QB * i, QB, stride=d2)
            m2[idx, :] = m
            l2[idx, :] = l
            a2[idx, :] = a
        banded(sub(qs), sub(ks), sub(vs), b2_ref[0], seq // (QB * d2), st2)

    d3 = DILATED[2][1]
    for r in range(d3):
        idx = pl.ds(r, QB, stride=d3)
        m, l, a = _soft_block(qs[idx, :], ks[idx, :], vs[idx, :], b3_ref[0])
        m3[idx, :] = m
        l3[idx, :] = l
        a3[idx, :] = a

    mx = jnp.maximum(jnp.maximum(m1[...], m2[...]), m3[...])
    w1 = jnp.exp(m1[...] - mx)
    w2 = jnp.exp(m2[...] - mx)
    w3 = jnp.exp(m3[...] - mx)
    num = w1 * a1[...] + w2 * a2[...] + w3 * a3[...]
    den = w1 * l1[...] + w2 * l2[...] + w3 * l3[...]
    o_ref[...] = (num / den).astype(BF16)


def _dilated_prompt(z, gq, gk, tabs, nb, seq):
    nh = 8
    zspec = lambda c0: pl.BlockSpec((seq, 128), lambda b, h: (b, c0 + h))
    gspec = pl.BlockSpec((1, 128), lambda b, h: (0, 0))
    tspec = lambda w: pl.BlockSpec((1, QB, w), lambda b, h: (h, 0, 0))
    ospec = pl.BlockSpec((seq, 128), lambda b, h: (b, h))
    big = pltpu.VMEM((seq, 128), F32)
    col = pltpu.VMEM((seq, 1), F32)
    return pl.pallas_call(
        _dil_p_kernel,
        grid=(nb, nh),
        in_specs=[zspec(32), zspec(40), zspec(48), gspec, gspec, tspec(2 * QB), tspec(2 * QB), tspec(QB)],
        out_specs=[ospec, ospec, ospec],
        out_shape=[jax.ShapeDtypeStruct((nb * seq, nh * 128), BF16),
                   jax.ShapeDtypeStruct((nb * seq, nh * 128), F32),
                   jax.ShapeDtypeStruct((nb * seq, nh * 128), F32)],
        scratch_shapes=[big, big, big, big, big, big, col, col, col, col, col, col],
        compiler_params=_cparams(("parallel", "parallel")),
        name="dilated_prompt",
    )(z, z, z, gq.reshape(1, 128), gk.reshape(1, 128), *tabs)


def _sample_tables(rel_bias, nt, buf_len):
    rb = rel_bias.astype(F32)
    nh = rb.shape[1]
    w2 = DILATED[1][0]

    def count(delta, patterns):
        c = np.zeros(delta.shape, np.int32)
        for window, dil in patterns:
            c += ((delta >= 0) & (delta <= window) & (delta % dil == 0)).astype(np.int32)
        return c

    qpos = buf_len + np.arange(8)
    real_q = (np.arange(8) < nt)
    da = qpos[None, :] - (buf_len - w2 + np.arange(w2))[:, None]
    ca = count(da, DILATED[:2]) * real_q[None, :]
    nj = buf_len // 16
    rows_b = (16 * np.arange(nj)[None, :] + np.arange(nt)[:, None]).reshape(-1)
    db = qpos[None, :] - rows_b[:, None]
    cb = count(db, DILATED[2:]) * real_q[None, :]
    dn = qpos[None, :] - qpos[:, None]
    cn = count(dn, DILATED) * real_q[None, :] * real_q[:, None]

    def tables(delta, cnt):
        bucket = _rel_bucket_np(np.clip(delta, 0, REL_MAX_DISTANCE))
        onehot = (bucket.reshape(-1, 1) == np.arange(REL_BUCKETS)[None, :]).astype(np.float32)
        bias = jnp.dot(jnp.asarray(onehot), rb, precision=lax.Precision.HIGHEST).reshape(*delta.shape, nh)
        bias = jnp.where(jnp.asarray((cnt > 0) | ~real_q[None, :])[:, :, None], bias, -jnp.inf)
        bias = jnp.transpose(bias, (0, 2, 1)).reshape(delta.shape[0], nh * 8)
        mult = np.repeat(cnt[:, None, :], nh, axis=1).reshape(delta.shape[0], nh * 8).astype(np.float32)
        return bias, jnp.asarray(mult)

    return tables(da, ca) + tables(db, cb) + tables(dn, cn)


def _dil_s_kernel(q_ref, k_ref, v_ref, gq_ref, gk_ref, ca_ref, cb_ref,
                  ba_ref, ma_ref, bb_ref, mb_ref, bn_ref, mn_ref, o_ref, ko_ref, vo_ref):
    nt = q_ref.shape[1]
    nh = 8
    gq = gq_ref[...]
    gk = gk_ref[...]
    qn = jnp.concatenate([_rms(q_ref[0, :, 128 * h:128 * (h + 1)], gq) for h in range(nh)], axis=1)
    kn = jnp.concatenate([_rms(k_ref[0, :, 128 * h:128 * (h + 1)], gk) for h in range(nh)], axis=1)
    vn = v_ref[0]
    ko_ref[0] = kn
    vo_ref[0] = vn
    pad = jnp.zeros((8 - nt, nh * 128), F32)
    q8 = jnp.concatenate([qn, pad], axis=0)
    k8 = jnp.concatenate([kn, pad], axis=0)
    v8 = jnp.concatenate([vn, pad], axis=0)
    lane_head = lax.broadcasted_iota(jnp.int32, (8, nh * 128), 1) // 128
    qbd = jnp.concatenate([jnp.where(lane_head == h, q8, 0.0) for h in range(nh)], axis=0).astype(BF16)

    def heads(ref2d, first, nrows, stride):
        return jnp.concatenate([ref2d[pl.ds(first + h, nrows, stride=stride), :] for h in range(nh)], axis=1)

    a2 = ca_ref.at[0]
    na = a2.shape[0] // (2 * nh)
    groups = [(heads(a2, 0, na, 2 * nh), heads(a2, nh, na, 2 * nh), ba_ref[...], ma_ref[...])]
    nj = cb_ref.shape[1]
    per_j = cb_ref.shape[2]
    b2 = cb_ref.reshape(nj * per_j, 128)
    for i in range(nt):
        groups.append((heads(b2, 2 * nh * i, nj, per_j), heads(b2, 2 * nh * i + nh, nj, per_j),
                       bb_ref[nj * i:nj * (i + 1), :], mb_ref[nj * i:nj * (i + 1), :]))
    groups.append((k8, v8, bn_ref[...], mn_ref[...]))

    logits = [_dot1(kk, qbd, NT) * ATT_SCALE + bias for kk, _, bias, _ in groups]
    mx = logits[0].max(axis=0, keepdims=True)
    for l in logits[1:]:
        mx = jnp.maximum(mx, l.max(axis=0, keepdims=True))
    ps = [mult * jnp.exp(l - mx) for l, (_, _, _, mult) in zip(logits, groups)]
    den = ps[0].sum(axis=0, keepdims=True)
    for p in ps[1:]:
        den = den + p.sum(axis=0, keepdims=True)
    inv = 1.0 / jnp.where(den > 0.0, den, 1.0)
    acc = None
    for p, (_, vv, _, _) in zip(ps, groups):
        part = _dot1(p * inv, vv, TN)
        acc = part if acc is None else acc + part
    o8 = jnp.concatenate([acc[8 * h:8 * (h + 1), 128 * h:128 * (h + 1)] for h in range(nh)], axis=1)
    o_ref[0] = o8[0:nt]


def _dilated_sample(zs3, gq, gk, cache, tabs):
    nb, nt, _ = zs3.shape
    buf_len = cache.shape[1]
    w2 = DILATED[1][0]
    d3 = DILATED[2][1]
    kvh = 2 * 8
    cache_a = cache.reshape(nb, buf_len * kvh, 128)
    cache_b = cache.reshape(nb, buf_len // d3, d3 * kvh, 128)
    zspec = lambda c: pl.BlockSpec((1, nt, 1024), lambda b: (b, 0, c))
    gspec = pl.BlockSpec((1, 128), lambda b: (0, 0))
    full = lambda a: pl.BlockSpec(a.shape, lambda b: (0, 0))
    ospec = pl.BlockSpec((1, nt, 1024), lambda b: (b, 0, 0))
    return pl.pallas_call(
        _dil_s_kernel,
        grid=(nb,),
        in_specs=[zspec(4), zspec(5), zspec(6), gspec, gspec,
                  pl.BlockSpec((1, w2 * kvh, 128), lambda b: (b, buf_len // w2 - 1, 0)),
                  pl.BlockSpec((1, buf_len // d3, nt * kvh, 128), lambda b: (b, 0, 0, 0))]
                 + [full(a) for a in tabs],
        out_specs=[ospec, ospec, ospec],
        out_shape=[jax.ShapeDtypeStruct((nb, nt, 1024), F32)] * 3,
        compiler_params=_cparams(("parallel",)),
        name="dilated_sample",
    )(zs3, zs3, zs3, gq.reshape(1, 128), gk.reshape(1, 128), cache_a, cache_b, *tabs)


def _route_kernel(x_ref, g_ref, psh, psc, ssh, ssc, rw_ref, rb_ref, u_ref, idx_ref, gate_ref, *, npt):
    i = pl.program_id(0)
    xr = _rms(x_ref[...], g_ref[...])
    u = xr * (1.0 + _pick(i < npt, psc, ssc)) + _pick(i < npt, psh, ssh)
    ub = u.astype(BF16).astype(F32)
    lo = lax.shift_right_logical(lax.bitcast_convert_type(ub[:, :D // 2], jnp.int32), 16)
    u_ref[...] = lax.bitcast_convert_type(ub[:, D // 2:], jnp.int32) | lo
    l = _dot3(u, rw_ref[...]) + rb_ref[...]
    lane = lax.broadcasted_iota(jnp.int32, l.shape, 1).astype(F32)
    out_lane = lax.broadcasted_iota(jnp.int32, (TM, 128), 1)
    vals, idxs = [], []
    for _ in range(TOP_K):
        m = jnp.max(l, axis=-1, keepdims=True)
        first = jnp.min(jnp.where(l == m, lane, float(N_EXPERTS)), axis=-1, keepdims=True)
        vals.append(m)
        idxs.append(first.astype(jnp.int32))
        l = jnp.where(lane == first, -jnp.inf, l)
    es = [jnp.exp(v - vals[0]) for v in vals]
    tot = es[0] + es[1] + es[2] + es[3]
    idx_out = jnp.zeros((TM, 128), jnp.int32)
    gate_out = jnp.zeros((TM, 128), F32)
    for k in range(TOP_K):
        idx_out = jnp.where(out_lane == k, idxs[k], idx_out)
        gate_out = jnp.where(out_lane == k, es[k] / tot, gate_out)
    idx_ref[...] = idx_out
    gate_ref[...] = gate_out


def _route(x, g, pmod, smod, rw, rb, npt, tiles_per_batch):
    t = x.shape[0]
    psh, ssh = _mod_specs(npt, tiles_per_batch, 0)
    psc, ssc = _mod_specs(npt, tiles_per_batch, 1)
    return pl.pallas_call(
        functools.partial(_route_kernel, npt=npt),
        grid=(t // TM,),
        in_specs=[pl.BlockSpec((TM, D), lambda i: (i, 0)),
                  pl.BlockSpec((1, D), lambda i: (0, 0)),
                  psh, psc, ssh, ssc,
                  pl.BlockSpec((D, N_EXPERTS), lambda i: (0, 0)),
                  pl.BlockSpec((1, N_EXPERTS), lambda i: (0, 0))],
        out_specs=[pl.BlockSpec((TM, D // 2), lambda i: (i, 0)),
                   pl.BlockSpec((TM, 128), lambda i: (i, 0)),
                   pl.BlockSpec((TM, 128), lambda i: (i, 0))],
        out_shape=[jax.ShapeDtypeStruct((t, D // 2), jnp.int32),
                   jax.ShapeDtypeStruct((t, 128), jnp.int32),
                   jax.ShapeDtypeStruct((t, 128), F32)],
        compiler_params=_cparams(("parallel",)),
        name="moe_route",
    )(x, g.reshape(1, D), pmod, pmod, smod, smod, rw, rb.reshape(1, N_EXPERTS))


MOE_TM = 1216
MOE_ROWS = (304, 608, 1056, MOE_TM)
MOE_TN = 1024
MOE_SEL = 512
MOE_TF = MOE_TN // 2


MOE_NG = 2 * D_EXPERT // MOE_TN
MOE_DN = 512
MOE_ND = D // MOE_DN


def _moe_kernel(te_ref, nu_ref, nv_ref, tok_ref, u_hbm, wgu_ref, bgu_ref, wdn_ref, bdn_ref, y_ref, xg, xb, hs, sem):
    t = pl.program_id(0)
    j = pl.program_id(1)
    nu = nu_ref[0]
    active = t < nu
    slot = t % 2
    nv = nv_ref[t]

    def rows8(n):
        return (n + 7) // 8 * 8

    def issue(tile, s):
        def body(g, carry):
            for i in range(8):
                r = g * 8 + i
                tok = tok_ref[tile * MOE_TM + r]
                pltpu.make_async_copy(u_hbm.at[pl.ds(tok, 1)], xg.at[s, pl.ds(r, 1)], sem.at[s]).start()
            return carry
        lax.fori_loop(0, rows8(nv_ref[tile]) // 8, body, 0)

    @pl.when(jnp.logical_and(t == 0, j == 0))
    def _():
        xg[...] = jnp.zeros_like(xg)
        issue(0, 0)

    @pl.when(jnp.logical_and(active, j == 0))
    def _():
        n8 = pl.multiple_of(rows8(nv), 8)
        pltpu.make_async_copy(u_hbm.at[pl.ds(0, n8)], xg.at[slot, pl.ds(0, n8)], sem.at[slot]).wait()
        w = xg[slot]
        xb[:, :D // 2] = lax.bitcast_convert_type(lax.shift_left(w, 16), F32).astype(BF16)
        xb[:, D // 2:] = lax.bitcast_convert_type(w & -65536, F32).astype(BF16)

    @pl.when(jnp.logical_and(t + 1 < nu, j == 1))
    def _():
        issue(t + 1, 1 - slot)

    def gate_up(rows):
        w = wgu_ref[0, 0].astype(BF16)
        ri = lax.broadcasted_iota(jnp.int32, (MOE_SEL, MOE_SEL // 2), 0)
        ci = lax.broadcasted_iota(jnp.int32, (MOE_SEL, MOE_SEL // 2), 1)
        pick_even = jnp.where(ri == 2 * ci, 1.0, 0.0).astype(BF16)
        parts = 2 if rows > MOE_TM // 2 else 1
        step = rows // parts
        gus = [jnp.dot(xb[step * p:step * (p + 1), :], w, preferred_element_type=F32) for p in range(parts)]
        for p, gu in enumerate(gus):
            gu = gu + bgu_ref[0, 0]
            up = pltpu.roll(gu, MOE_TN - 1, axis=1)
            gate = jnp.minimum(gu, SWIGLU_LIMIT)
            up = jnp.clip(up, -SWIGLU_LIMIT, SWIGLU_LIMIT)
            sig = 0.5 * jnp.tanh((0.5 * SWIGLU_ALPHA) * gate) + 0.5
            h = ((up + 1.0) * gate * sig).astype(BF16)
            hc = [jnp.dot(h[:, MOE_SEL * s:MOE_SEL * (s + 1)], pick_even, preferred_element_type=F32).astype(BF16)
                  for s in range(MOE_TN // MOE_SEL)]
            hs[j, step * p:step * (p + 1), :] = jnp.concatenate(hc, axis=1)

    def down(rows):
        acc = jnp.dot(hs[0, 0:rows, :], wdn_ref[0, 0, 0:MOE_TF, :].astype(BF16), preferred_element_type=F32)
        for c in range(1, MOE_NG):
            acc = acc + jnp.dot(hs[c, 0:rows, :], wdn_ref[0, 0, MOE_TF * c:MOE_TF * (c + 1), :].astype(BF16),
                                preferred_element_type=F32)
        y_ref[0:rows, :] = acc + bdn_ref[0, 0]
        if rows < MOE_TM:
            y_ref[rows:MOE_TM, :] = jnp.zeros((MOE_TM - rows, MOE_DN), F32)

    lower = 0
    for rows in MOE_ROWS:
        fits = jnp.logical_and(active, jnp.logical_and(nv > lower, nv <= rows))
        pl.when(jnp.logical_and(fits, j < MOE_NG))(functools.partial(gate_up, rows))
        pl.when(jnp.logical_and(fits, j >= MOE_NG))(functools.partial(down, rows))
        lower = rows

    @pl.when(jnp.logical_and(jnp.logical_not(active), j >= MOE_NG))
    def _():
        y_ref[...] = jnp.zeros_like(y_ref)


def _moe_experts(u, tile_e, n_used, n_valid, slot_tok, layer, w_gu, b_gu, w_dn, b_dn):
    n_tiles = tile_e.shape[0]
    nj = MOE_NG + MOE_ND
    jj = lambda t, j, nu: jnp.where(t < nu[0], j, nj - 1)
    jg = lambda t, j, nu: jnp.minimum(jj(t, j, nu), MOE_NG - 1)
    jd = lambda t, j, nu: jnp.maximum(jj(t, j, nu) - MOE_NG, 0)
    grid_spec = pltpu.PrefetchScalarGridSpec(
        num_scalar_prefetch=4,
        grid=(n_tiles, nj),
        in_specs=[pl.BlockSpec(memory_space=pl.ANY),
                  pl.BlockSpec((1, 1, D, MOE_TN), lambda t, j, te, nu, nv, tok: (layer, te[t], 0, jg(t, j, nu))),
                  pl.BlockSpec((1, 1, 1, MOE_TN), lambda t, j, te, nu, nv, tok: (layer, te[t], 0, jg(t, j, nu))),
                  pl.BlockSpec((1, 1, D_EXPERT, MOE_DN), lambda t, j, te, nu, nv, tok: (layer, te[t], 0, jd(t, j, nu))),
                  pl.BlockSpec((1, 1, 1, MOE_DN), lambda t, j, te, nu, nv, tok: (layer, te[t], 0, jd(t, j, nu)))],
        out_specs=pl.BlockSpec((MOE_TM, MOE_DN), lambda t, j, te, nu, nv, tok: (t, jnp.maximum(j - MOE_NG, 0))),
        scratch_shapes=[pltpu.VMEM((2, MOE_TM, D // 2), jnp.int32), pltpu.VMEM((MOE_TM, D), BF16),
                        pltpu.VMEM((MOE_NG, MOE_TM, MOE_TF), BF16), pltpu.SemaphoreType.DMA((2,))],
    )
    nl, ne = b_gu.shape[0], b_gu.shape[1]
    return pl.pallas_call(
        _moe_kernel,
        grid_spec=grid_spec,
        out_shape=jax.ShapeDtypeStruct((n_tiles * MOE_TM, D), F32),
        compiler_params=_cparams(("arbitrary", "arbitrary")),
        name="moe_experts",
    )(tile_e, n_used, n_valid, slot_tok, u, w_gu, b_gu.reshape(nl, ne, 1, 2 * D_EXPERT), w_dn, b_dn.reshape(nl, ne, 1, D))


CMB = 256


def _combine_kernel(slot_ref, h_ref, gate_ref, pg, sg, y_hbm, *rest, npt_c, split):
    if split:
        op_ref, os_ref, buf, sem = rest
    else:
        o_ref, buf, sem = rest
    i = pl.program_id(0)

    def issue(r, carry):
        for k in range(TOP_K):
            s = slot_ref[(i * CMB + r) * TOP_K + k]
            pltpu.make_async_copy(y_hbm.at[pl.ds(s, 1)], buf.at[k, pl.ds(r, 1)], sem).start()
        return carry
    lax.fori_loop(0, CMB, issue, 0, unroll=4)
    for k in range(TOP_K):
        pltpu.make_async_copy(y_hbm.at[pl.ds(0, CMB)], buf.at[k], sem).wait()
    gates = gate_ref[...]
    mo = gates[:, 0:1] * buf[0]
    for k in range(1, TOP_K):
        mo = mo + gates[:, k:k + 1] * buf[k]
    out = h_ref[...] + _pick(i < npt_c, pg, sg) * mo
    if split:
        @pl.when(i < npt_c)
        def _():
            op_ref[...] = out

        @pl.when(i >= npt_c)
        def _():
            os_ref[...] = out
    else:
        o_ref[...] = out


def _moe_combine(h, y, slot4, gate128, pmod, smod, npt_c, tiles_per_batch_c, split=False):
    t = h.shape[0]
    nsamp_tiles = (t // CMB) - npt_c
    if split:
        out_specs = [pl.BlockSpec((CMB, D), lambda i, s: (jnp.minimum(i, npt_c - 1), 0)),
                     pl.BlockSpec((CMB, D), lambda i, s: (jnp.clip(i - npt_c, 0, nsamp_tiles - 1), 0))]
        out_shape = [jax.ShapeDtypeStruct((npt_c * CMB, D), F32), jax.ShapeDtypeStruct((nsamp_tiles * CMB, D), F32)]
    else:
        out_specs = pl.BlockSpec((CMB, D), lambda i, s: (i, 0))
        out_shape = jax.ShapeDtypeStruct((t, D), F32)
    grid_spec = pltpu.PrefetchScalarGridSpec(
        num_scalar_prefetch=1,
        grid=(t // CMB,),
        in_specs=[pl.BlockSpec((CMB, D), lambda i, s: (i, 0)),
                  pl.BlockSpec((CMB, 128), lambda i, s: (i, 0)),
                  pl.BlockSpec((1, 1, D), lambda i, s: (jnp.minimum(i // tiles_per_batch_c, npt_c // tiles_per_batch_c - 1), 0, 2)),
                  pl.BlockSpec((CMB, D), lambda i, s: (jnp.clip(i - npt_c, 0, nsamp_tiles - 1), 2)),
                  pl.BlockSpec(memory_space=pl.ANY)],
        out_specs=out_specs,
        scratch_shapes=[pltpu.VMEM((TOP_K, CMB, D), F32), pltpu.SemaphoreType.DMA],
    )
    return pl.pallas_call(
        functools.partial(_combine_kernel, npt_c=npt_c, split=split),
        grid_spec=grid_spec,
        out_shape=out_shape,
        compiler_params=_cparams(("arbitrary",)),
        name="moe_combine",
    )(slot4, h, gate128, pmod, smod, y)


def _moe_layer(h, g, pmod, smod, layer, rw, rb, w_gu, b_gu, w_dn, b_dn, npt, tiles_per_batch, split=False):
    t = h.shape[0]
    u, idx128, gate128 = _route(h, g, pmod, smod, rw[layer], rb[layer], npt, tiles_per_batch)
    idx4 = idx128[:, :TOP_K]
    onehot = idx4[:, :, None] == jnp.arange(N_EXPERTS, dtype=jnp.int32)[None, None, :]
    sel = jnp.any(onehot, axis=1).astype(jnp.int32)
    counts = jnp.sum(sel, axis=0)
    rank = jnp.cumsum(sel, axis=0) - 1
    padded = (counts + MOE_TM - 1) // MOE_TM * MOE_TM
    pad_end = jnp.cumsum(padded)
    pad_start = pad_end - padded
    n_tiles = -(-t * TOP_K // MOE_TM) + N_EXPERTS
    dest4 = jnp.sum(jnp.where(onehot, (pad_start[None, :] + rank)[:, None, :], 0), axis=2).astype(jnp.int32)
    tok = jnp.broadcast_to(jnp.arange(t, dtype=jnp.int32)[:, None], (t, TOP_K))
    slot_tok = jnp.zeros((n_tiles * MOE_TM,), jnp.int32).at[dest4.reshape(-1)].set(tok.reshape(-1))
    n_used = (pad_end[-1] // MOE_TM).astype(jnp.int32)
    tile_ids = jnp.arange(n_tiles, dtype=jnp.int32)
    tile_e = jnp.minimum(jnp.searchsorted(pad_end, jnp.minimum(tile_ids, n_used - 1) * MOE_TM, side='right'),
                         N_EXPERTS - 1).astype(jnp.int32)
    n_valid = jnp.where(tile_ids < n_used,
                        jnp.clip((pad_start + counts)[tile_e] - tile_ids * MOE_TM, 0, MOE_TM), 0).astype(jnp.int32)
    y = _moe_experts(u, tile_e, n_used.reshape(1), n_valid, slot_tok, layer, w_gu, b_gu, w_dn, b_dn)
    return _moe_combine(h, y, dest4.reshape(-1), gate128, pmod, smod,
                        npt * (TM // CMB), tiles_per_batch * (TM // CMB), split)


C_HEADS = 16
C_NOPE = 128
C_ROPE = 64
C_LORA = 512
ROPE_THETA = 10000.0
MLA_SCALE = (C_NOPE + C_ROPE) ** -0.5
HP = 128


def _rope_tables(pos):
    half = C_ROPE // 2
    inv = ROPE_THETA ** (-jnp.arange(half, dtype=F32) / half)
    ang = pos.astype(F32)[:, None] * inv[None, :]
    z = jnp.zeros_like(ang)
    cos = jnp.concatenate([jnp.cos(ang), jnp.cos(ang), z, z], axis=1)
    sin = jnp.concatenate([-jnp.sin(ang), jnp.sin(ang), z, z], axis=1)
    return cos, sin


def _rope128(x, cos, sin):
    n = x.shape[1]
    lane = lax.broadcasted_iota(jnp.int32, x.shape, 1) % HP
    partner = jnp.where(lane < C_ROPE // 2, pltpu.roll(x, n - C_ROPE // 2, axis=1), pltpu.roll(x, C_ROPE // 2, axis=1))
    return x * cos + partner * sin


def _mla_prep_kernel(z_ref, gq_ref, gkv_ref, gkr_ref, gqn_ref, gqr_ref, gkn_ref, wn_ref, wr_ref, wuk_ref,
                     cos_ref, sin_ref, lat_ref, kr_ref, krb_ref, qn_ref, qr_ref, qabs_ref, qg_ref, *, npt):
    i = pl.program_id(0)
    z = z_ref[...]
    cos = cos_ref[...]
    sin = sin_ref[...]
    lat_ref[...] = _rms(z[:, C_LORA:2 * C_LORA], gkv_ref[...])
    k128 = z[:, 2 * C_LORA:2 * C_LORA + HP]
    k128 = k128 * lax.rsqrt(jnp.sum(k128 * k128, axis=-1, keepdims=True) / C_ROPE + EPS) * gkr_ref[...]
    k128 = _rope128(k128, cos, sin)
    kr_ref[...] = k128[:, 0:C_ROPE]
    krb_ref[...] = k128.astype(BF16)
    cqn = _rms(z[:, 0:C_LORA], gq_ref[...]).astype(BF16)
    qn_all = jnp.dot(cqn, wn_ref[...], preferred_element_type=F32)
    qr_all = jnp.dot(cqn, wr_ref[...], preferred_element_type=F32)
    for h in range(C_HEADS):
        sl = slice(HP * h, HP * (h + 1))
        qn = _rms(qn_all[:, sl], gqn_ref[...]) * MLA_SCALE
        qn_ref[:, sl] = qn.astype(BF16)
        qg_ref[:, sl] = qn * gkn_ref[...]
        x = qr_all[:, sl]
        x = x * lax.rsqrt(jnp.sum(x * x, axis=-1, keepdims=True) / C_ROPE + EPS) * gqr_ref[...]
        qr_ref[:, sl] = (_rope128(x, cos, sin) * MLA_SCALE).astype(BF16)

    @pl.when(i == npt)
    def _():
        for h in range(C_HEADS):
            sl = slice(HP * h, HP * (h + 1))
            qabs_ref[:, C_LORA * h:C_LORA * (h + 1)] = _dot1(qg_ref[:, sl], wuk_ref[:, sl], NT).astype(BF16)


def _mla_prep(z1, gq, gkv, gkr, gqn, gqr, gkn, wn, wr, wuk, cos, sin, npt):
    t = z1.shape[0]
    zw = z1.shape[1]
    row = lambda w: pl.BlockSpec((TM, w), lambda i: (i, 0))
    full = lambda a: pl.BlockSpec(a.shape, lambda i: (0,) * a.ndim)
    pad64 = lambda g: jnp.concatenate([g, jnp.zeros((HP - C_ROPE,), F32)]).reshape(1, HP)
    args = (z1, gq.reshape(1, -1), gkv.reshape(1, -1), pad64(gkr), gqn.reshape(1, -1), pad64(gqr), gkn.reshape(1, -1),
            wn, wr, wuk, cos, sin)
    return pl.pallas_call(
        functools.partial(_mla_prep_kernel, npt=npt),
        grid=(t // TM,),
        in_specs=[row(zw)] + [full(a) for a in args[1:10]] + [row(HP), row(HP)],
        out_specs=[row(C_LORA), row(C_ROPE), row(HP), row(C_HEADS * HP), row(C_HEADS * HP),
                   pl.BlockSpec((TM, C_HEADS * C_LORA), lambda i: (0, 0))],
        out_shape=[jax.ShapeDtypeStruct((t, C_LORA), F32), jax.ShapeDtypeStruct((t, C_ROPE), F32),
                   jax.ShapeDtypeStruct((t, HP), BF16), jax.ShapeDtypeStruct((t, C_HEADS * HP), BF16),
                   jax.ShapeDtypeStruct((t, C_HEADS * HP), BF16), jax.ShapeDtypeStruct((TM, C_HEADS * C_LORA), BF16)],
        scratch_shapes=[pltpu.VMEM((TM, C_HEADS * HP), F32)],
        compiler_params=_cparams(("arbitrary",)),
        name="mla_prep",
    )(*args)


def _mla_kv_kernel(lat_ref, wuk_ref, wuv_ref, gkn_ref, kn_ref, v_ref):
    lat = lat_ref[...].astype(BF16)
    kn = jnp.dot(lat, wuk_ref[...], preferred_element_type=F32)
    for h in range(C_HEADS):
        sl = slice(HP * h, HP * (h + 1))
        kn_ref[:, sl] = _rms(kn[:, sl], gkn_ref[...]).astype(BF16)
    v_ref[...] = jnp.dot(lat, wuv_ref[...], preferred_element_type=F32).astype(BF16)


def _mla_kv(lat, wuk, wuv, gkn, rows):
    wide = C_HEADS * HP
    return pl.pallas_call(
        _mla_kv_kernel,
        grid=(rows // TM,),
        in_specs=[pl.BlockSpec((TM, C_LORA), lambda i: (i, 0)),
                  pl.BlockSpec((C_LORA, wide), lambda i: (0, 0)),
                  pl.BlockSpec((C_LORA, wide), lambda i: (0, 0)),
                  pl.BlockSpec((1, HP), lambda i: (0, 0))],
        out_specs=[pl.BlockSpec((TM, wide), lambda i: (i, 0))] * 2,
        out_shape=[jax.ShapeDtypeStruct((rows, wide), BF16)] * 2,
        compiler_params=_cparams(("parallel",)),
        name="mla_kv",
    )(lat, wuk, wuv, gkn.reshape(1, HP))


FQ = 512


def _mla_flash_kernel(qn_ref, qr_ref, kn_ref, kr_ref, v_ref, o_ref, kcat):
    qi = pl.program_id(2)

    @pl.when(qi == 0)
    def _():
        kcat[:, 0:HP] = kn_ref[...]
        kcat[:, HP:2 * HP] = kr_ref[...]

    q = jnp.concatenate([qn_ref[...], qr_ref[...]], axis=1)

    def block(kb, carry, diagonal):
        m, l, acc = carry
        ks = pl.ds(pl.multiple_of(kb * FQ, FQ), FQ)
        s = lax.dot_general(q, kcat[ks, :], (NT, ((), ())), preferred_element_type=F32)
        if diagonal:
            row = lax.broadcasted_iota(jnp.int32, (FQ, FQ), 0)
            col = lax.broadcasted_iota(jnp.int32, (FQ, FQ), 1)
            s = jnp.where(col <= row, s, -jnp.inf)
        m_new = jnp.maximum(m, jnp.max(s, axis=-1, keepdims=True))
        alpha = jnp.exp(m - m_new)
        p = jnp.exp(s - m_new)
        l = alpha * l + jnp.sum(p, axis=-1, keepdims=True)
        acc = alpha * acc + jnp.dot(p.astype(BF16), v_ref[ks, :], preferred_element_type=F32)
        return m_new, l, acc

    m0 = jnp.full((FQ, 1), -jnp.inf, F32)
    l0 = jnp.zeros((FQ, 1), F32)
    a0 = jnp.zeros((FQ, HP), F32)
    carry = lax.fori_loop(0, qi, lambda kb, c: block(kb, c, False), (m0, l0, a0))
    m, l, acc = block(qi, carry, True)
    o_ref[...] = (acc / l).astype(BF16)


def _mla_flash(qn, qr, kn, krb, v, nb, seq):
    nq = seq // FQ
    qspec = pl.BlockSpec((FQ, HP), lambda b, h, q: (b * nq + q, h))
    kspec = pl.BlockSpec((seq, HP), lambda b, h, q: (b, h))
    return pl.pallas_call(
        _mla_flash_kernel,
        grid=(nb, C_HEADS, nq),
        in_specs=[qspec, qspec, kspec, pl.BlockSpec((seq, HP), lambda b, h, q: (b, 0)), kspec],
        out_specs=qspec,
        out_shape=jax.ShapeDtypeStruct((nb * seq, C_HEADS * HP), BF16),
        scratch_shapes=[pltpu.VMEM((seq, 2 * HP), BF16)],
        compiler_params=_cparams(("parallel", "parallel", "arbitrary")),
        name="mla_flash",
    )(qn, qr, kn, krb, v)


PPS = 32
PPG = 4


def _mla_s_kernel(*refs, n_steps, n_batch, ci):
    (pt_ref, lat_hbm, kr_hbm, wukt_ref, qabs_ref, qr_ref, latn_ref, krn_ref, o_ref,
     wst, latb, lg, m_ref, l_ref, acc_ref, latbuf, krbuf, lsem, ksem) = refs
    b = pl.program_id(0)
    kt = pl.program_id(1)
    nq = qabs_ref.shape[1]
    nheads = C_HEADS
    ntok = nq // nheads

    step = b * n_steps + kt
    slot = step % 2

    def issue_pages(first_page, s):
        for k in range(PPS):
            page = pt_ref[first_page + k]
            pltpu.make_async_copy(lat_hbm.at[ci, page], latbuf.at[s, k], lsem.at[s]).start()
            pltpu.make_async_copy(kr_hbm.at[ci, page], krbuf.at[s, k], ksem.at[s]).start()

    @pl.when(step == 0)
    def _():
        issue_pages(0, 0)

    pltpu.make_async_copy(lat_hbm.at[ci, pl.ds(0, PPS)], latbuf.at[slot], lsem.at[slot]).wait()
    pltpu.make_async_copy(kr_hbm.at[ci, pl.ds(0, PPS)], krbuf.at[slot], ksem.at[slot]).wait()

    @pl.when(step + 1 < n_batch * n_steps)
    def _():
        issue_pages((step + 1) * PPS, 1 - slot)

    @pl.when(jnp.logical_and(b == 0, kt == 0))
    def _():
        wst[0:nheads * HP, :] = wukt_ref[...]

    @pl.when(kt == 0)
    def _():
        wst[nheads * HP:nheads * HP + nq, :] = qabs_ref[0]
        m_ref[...] = jnp.full_like(m_ref, -jnp.inf)
        l_ref[...] = jnp.zeros_like(l_ref)
        acc_ref[...] = jnp.zeros_like(acc_ref)

    qr = qr_ref[0]

    def up_project(lat_bf16):
        return lax.dot_general(wst[...], lat_bf16, (NT, ((), ())), preferred_element_type=F32)

    def key_logits(r, krt_f32):
        nkeys = r.shape[1]
        kraw = r[0:nheads * HP].reshape(nheads, HP, nkeys)
        rs = lax.rsqrt(jnp.sum(kraw * kraw, axis=1) / HP + EPS)
        lraw = r[nheads * HP:nheads * HP + nq].reshape(ntok, nheads, nkeys) * rs[None]
        return lraw.reshape(nq, nkeys) + _dot1(qr, krt_f32)

    def update(state, logits, lat_bf16):
        m_old, l_old, acc_old = state
        m_new = jnp.maximum(m_old, jnp.max(logits, axis=-1, keepdims=True))
        alpha = jnp.exp(m_old - m_new)
        p = jnp.exp(logits - m_new)
        l_new = alpha * l_old + jnp.sum(p, axis=-1, keepdims=True)
        acc_new = alpha * acc_old + jnp.dot(p.astype(BF16), lat_bf16, preferred_element_type=F32)
        return m_new, l_new, acc_new

    def load_group(g):
        pages = range(PPG * g, PPG * (g + 1))
        return jnp.concatenate([latbuf[slot, p] for p in pages], axis=0).astype(BF16)

    gk = PPG * PAGE
    for g in range(PPS // PPG):
        lat_g = load_group(g)
        krt_g = jnp.concatenate([krbuf[slot, p] for p in range(PPG * g, PPG * (g + 1))], axis=1)
        latb[gk * g:gk * (g + 1), :] = lat_g
        lg[:, gk * g:gk * (g + 1)] = key_logits(up_project(lat_g), krt_g)
    state = update((m_ref[...], l_ref[...], acc_ref[...]), lg[...], latb[...])
    m_ref[...], l_ref[...], acc_ref[...] = state

    @pl.when(kt == n_steps - 1)
    def _():
        latn = jnp.concatenate([latn_ref[0], jnp.zeros((PAGE - 8, C_LORA), F32)], axis=0).astype(BF16)
        logits = key_logits(up_project(latn), krn_ref[0])
        qtok = lax.broadcasted_iota(jnp.int32, (nq, PAGE), 0) // nheads
        slot = lax.broadcasted_iota(jnp.int32, (nq, PAGE), 1)
        _, l_fin, acc_fin = update(state, jnp.where(slot <= qtok, logits, -jnp.inf), latn)
        o_ref[0] = acc_fin / l_fin


def _mla_sample(page_table, lat_cache, kr_cache, ci, wukt, qabs, qr, lat_new8, kr_new8):
    nb, npages = page_table.shape
    n_steps = npages // PPS
    nq = qabs.shape[1]

    def page_spec(k, rows, width):
        return pl.BlockSpec((1, 1, rows, width), lambda b, t, pt: (ci, pt[b * npages + t * PPS + k], 0, 0))

    kr_cache_t = jnp.swapaxes(kr_cache, 2, 3)
    kr_new_t = jnp.pad(jnp.swapaxes(kr_new8, 1, 2), ((0, 0), (0, 0), (0, PAGE - kr_new8.shape[1])))
    grid_spec = pltpu.PrefetchScalarGridSpec(
        num_scalar_prefetch=1,
        grid=(nb, n_steps),
        in_specs=[pl.BlockSpec(memory_space=pl.ANY), pl.BlockSpec(memory_space=pl.ANY),
                  pl.BlockSpec(wukt.shape, lambda b, t, pt: (0, 0)),
                    pl.BlockSpec((1, nq, C_LORA), lambda b, t, pt: (b, 0, 0)),
                    pl.BlockSpec((1, nq, C_ROPE), lambda b, t, pt: (b, 0, 0)),
                    pl.BlockSpec((1, 8, C_LORA), lambda b, t, pt: (b, 0, 0)),
                    pl.BlockSpec((1, C_ROPE, PAGE), lambda b, t, pt: (b, 0, 0))],
        out_specs=pl.BlockSpec((1, nq, C_LORA), lambda b, t, pt: (b, 0, 0)),
        scratch_shapes=[pltpu.VMEM((C_HEADS * HP + nq, C_LORA), BF16),
                        pltpu.VMEM((PPS * PAGE, C_LORA), BF16),
                        pltpu.VMEM((nq, PPS * PAGE), F32),
                        pltpu.VMEM((nq, 1), F32), pltpu.VMEM((nq, 1), F32), pltpu.VMEM((nq, C_LORA), F32),
                        pltpu.VMEM((2, PPS, PAGE, C_LORA), F32), pltpu.VMEM((2, PPS, C_ROPE, PAGE), F32),
                        pltpu.SemaphoreType.DMA((2,)), pltpu.SemaphoreType.DMA((2,))],
    )
    return pl.pallas_call(
        functools.partial(_mla_s_kernel, n_steps=n_steps, n_batch=nb, ci=ci),
        grid_spec=grid_spec,
        out_shape=jax.ShapeDtypeStruct((nb, nq, C_LORA), F32),
        compiler_params=_cparams(("arbitrary", "arbitrary")),
        name="mla_sample",
    )(page_table.reshape(-1), lat_cache, kr_cache_t, wukt, qabs, qr, lat_new8, kr_new_t)


def _mla_uv_kernel(c_ref, w_ref, o_ref):
    o_ref[...] = _dot1(c_ref[...], w_ref[0])


def _mla_uv(ctx, wuv3):
    rows = ctx.shape[0]
    return pl.pallas_call(
        _mla_uv_kernel,
        grid=(C_HEADS,),
        in_specs=[pl.BlockSpec((rows, C_LORA), lambda h: (0, h)),
                  pl.BlockSpec((1, C_LORA, HP), lambda h: (h, 0, 0))],
        out_specs=pl.BlockSpec((rows, HP), lambda h: (0, h)),
        out_shape=jax.ShapeDtypeStruct((rows, C_HEADS * HP), F32),
        compiler_params=_cparams(("parallel",)),
        name="mla_uv",
    )(ctx, wuv3)


def kernel(x_prompt, x_sample, state_hgrn, cache_swa_kv, cache_mla_latent, cache_mla_krope, page_table, c_prompt, c_sample, ada_w, ada_b, norm_g, rel_bias, ab_w_in, ab_w_out, hgrn_lb_logits, hgrn_norm_g, swa_q_g, swa_k_g, mla_w_in, mla_q_a_g, mla_kv_a_g, mla_w_uq, mla_w_uk, mla_w_uv, mla_qn_g, mla_qr_g, mla_kn_g, mla_kr_g, mla_w_o, moe_router_w, moe_router_b, moe_w_gu, moe_b_gu, moe_w_dn, moe_b_dn):
    nb, seq, _ = x_prompt.shape
    nsb, nst, _ = x_sample.shape
    tp, ts = nb * seq, nsb * nst
    assert ts == TM and seq % TM == 0 and nst <= 4
    npt, tpb = tp // TM, seq // TM
    x = jnp.concatenate([x_prompt.reshape(tp, D), x_sample.reshape(ts, D)], axis=0).astype(F32)

    nc = nb + nsb
    c_all = jnp.concatenate([c_prompt, c_sample, jnp.zeros((-nc % 8, D), c_prompt.dtype)], axis=0).astype(F32)
    mods = _adaln(c_all, ada_w, ada_b)

    def mod(s):
        return mods[s, 0:nb].reshape(nb, 1, 3 * D), jnp.repeat(mods[s, nb:nc], nst, axis=0)

    pm, sm = mod(0)
    z = _normmod_mm(x, norm_g[0, 0], pm, sm, ab_w_in[0].astype(BF16), 1792, npt, tpb)
    zs3 = z[tp:].reshape(nsb, nst, z.shape[1])
    oa_p, hg_p = _hgrn_prompt(z, hgrn_lb_logits, hgrn_norm_g[0], nb, seq, 0)
    oa_s, hg_s = _hgrn_sample(zs3, hgrn_lb_logits, hgrn_norm_g[0], state_hgrn, 0)
    ob_p, k_p, v_p = _dilated_prompt(z, swa_q_g[0], swa_k_g[0], _band_tables(rel_bias), nb, seq)
    buf_len = cache_swa_kv.shape[2]
    ob_s, k_s, v_s = _dilated_sample(zs3, swa_q_g[0], swa_k_g[0], cache_swa_kv[0], _sample_tables(rel_bias, nst, buf_len))
    w_out = ab_w_out[0].astype(BF16)
    h = _proj_residual([oa_p, ob_p], [oa_s.reshape(ts, 1024), ob_s.reshape(ts, 1024)],
                       [w_out[:1024], w_out[1024:]], x, pm, sm, npt, tpb)
    pm, sm = mod(1)
    h = _moe_layer(h, norm_g[0, 1], pm, sm, 0, moe_router_w, moe_router_b, moe_w_gu, moe_b_gu, moe_w_dn, moe_b_dn, npt, tpb)

    pm, sm = mod(2)
    w_in1 = jnp.pad(mla_w_in[0], ((0, 0), (0, HP - C_ROPE))).astype(BF16)
    z1 = _normmod_mm(h, norm_g[1, 0], pm, sm, w_in1, w_in1.shape[1], npt, tpb)
    past = page_table.shape[1] * PAGE
    pos = jnp.concatenate([jnp.tile(jnp.arange(seq), nb), jnp.tile(past + jnp.arange(nst), nsb)])
    cos, sin = _rope_tables(pos)
    wq = mla_w_uq[0]
    wn = wq[:, :, :C_NOPE].reshape(C_LORA, C_HEADS * HP).astype(BF16)
    wr = jnp.pad(wq[:, :, C_NOPE:], ((0, 0), (0, 0), (0, HP - C_ROPE))).reshape(C_LORA, C_HEADS * HP).astype(BF16)
    wuk = mla_w_uk[0].reshape(C_LORA, C_HEADS * HP).astype(BF16)
    wuv = mla_w_uv[0].reshape(C_LORA, C_HEADS * HP).astype(BF16)
    lat, kr, krb, qn, qr, qabs = _mla_prep(z1, mla_q_a_g[0], mla_kv_a_g[0], mla_kr_g[0], mla_qn_g[0], mla_qr_g[0],
                                           mla_kn_g[0], wn, wr, wuk, cos, sin, npt)
    kn, v = _mla_kv(lat, wuk, wuv, mla_kn_g[0], tp)
    ctx_p = _mla_flash(qn, qr, kn, krb, v, nb, seq)
    qabs3 = qabs.reshape(nsb, nst * C_HEADS, C_LORA)
    qr_s = qr[tp:].reshape(ts, C_HEADS, HP)[:, :, :C_ROPE].reshape(nsb, nst * C_HEADS, C_ROPE)
    lat_new8 = jnp.pad(lat[tp:].reshape(nsb, nst, C_LORA), ((0, 0), (0, 8 - nst), (0, 0)))
    kr_new8 = jnp.pad(kr[tp:].reshape(nsb, nst, C_ROPE), ((0, 0), (0, 8 - nst), (0, 0)))
    ctx_s = _mla_sample(page_table, cache_mla_latent, cache_mla_krope, 0, wuk.T, qabs3, qr_s, lat_new8, kr_new8)
    wuv3 = jnp.transpose(mla_w_uv[0], (1, 0, 2)).astype(BF16)
    o_s = _mla_uv(ctx_s.reshape(ts, C_HEADS * C_LORA), wuv3)
    h = _proj_residual([ctx_p], [o_s], [mla_w_o[0].astype(BF16)], h, pm, sm, npt, tpb)
    pm, sm = mod(3)
    hp, hs = _moe_layer(h, norm_g[1, 1], pm, sm, 1, moe_router_w, moe_router_b, moe_w_gu, moe_b_gu, moe_w_dn, moe_b_dn,
                        npt, tpb, split=True)

    y_prompt = hp.reshape(nb, seq, D).astype(x_prompt.dtype)
    y_sample = hs.reshape(nsb, nst, D).astype(x_sample.dtype)
    hgrn_prompt = hg_p[None].astype(state_hgrn.dtype)
    hgrn_sample = hg_s.astype(state_hgrn.dtype)
    kv = lambda k, v, b, t: jnp.stack([k.reshape(b, t, 8, 128), v.reshape(b, t, 8, 128)], axis=2)[None]
    swa_prompt = kv(k_p, v_p, nb, seq).astype(cache_swa_kv.dtype)
    swa_sample = kv(k_s, v_s, nsb, nst).astype(cache_swa_kv.dtype)
    lat_prompt = lat[:tp].reshape(1, nb, seq, C_LORA).astype(cache_mla_latent.dtype)
    lat_sample = lat[tp:].reshape(1, nsb, nst, C_LORA).astype(cache_mla_latent.dtype)
    krope_prompt = kr[:tp].reshape(1, nb, seq, C_ROPE).astype(cache_mla_krope.dtype)
    krope_sample = kr[tp:].reshape(1, nsb, nst, C_ROPE).astype(cache_mla_krope.dtype)
    return (y_prompt, y_sample, hgrn_prompt, hgrn_sample, swa_prompt, swa_sample,
            lat_prompt, lat_sample, krope_prompt, krope_sample)
```

```python
import functools
import math

import numpy as np
import jax
import jax.numpy as jnp
from jax import lax
from jax.experimental import pallas as pl
from jax.experimental.pallas import tpu as pltpu

F32 = jnp.float32
BF16 = jnp.bfloat16
EPS = 1e-6

D = 2048
TM = 512
N_EXPERTS = 32
TOP_K = 4
D_EXPERT = 2048
SWIGLU_LIMIT = 7.0
SWIGLU_ALPHA = 1.702
PAGE = 128
V7X_VMEM_LIMIT = 56 * 1024 * 1024

NN = ((1,), (0,))
NT = ((1,), (1,))
TN = ((0,), (0,))


def _cparams(sem, vmem=V7X_VMEM_LIMIT):
    return pltpu.CompilerParams(dimension_semantics=sem, vmem_limit_bytes=vmem)


def _dot1(a, b, dims=NN):
    return lax.dot_general(a.astype(BF16), b.astype(BF16), (dims, ((), ())), preferred_element_type=F32)


def _split2(x):
    hi = x.astype(BF16)
    lo = (x - hi.astype(F32)).astype(BF16)
    return hi, lo


def _split3(x):
    hi = x.astype(BF16)
    r = x - hi.astype(F32)
    mid = r.astype(BF16)
    lo = (r - mid.astype(F32)).astype(BF16)
    return hi, mid, lo


def _dot_exact_rhs(a01, x):
    hi, mid, lo = _split3(x)
    f = lambda p: lax.dot_general(a01, p, (NN, ((), ())), preferred_element_type=F32)
    return f(hi) + f(mid) + f(lo)


def _dot3(a, b, dims=NN):
    ah, al = _split2(a)
    bh, bl = _split2(b)
    f = lambda p, q: lax.dot_general(p, q, (dims, ((), ())), preferred_element_type=F32)
    return f(ah, bh) + f(ah, bl) + f(al, bh)


def _sigmoid(x):
    return 1.0 / (1.0 + jnp.exp(-x))


def _rms(x, g):
    return x * lax.rsqrt(jnp.mean(x * x, axis=-1, keepdims=True) + EPS) * g


def _ada_kernel(c_ref, w_ref, b_ref, o_ref):
    c = c_ref[...]
    s = c * _sigmoid(c)
    o_ref[0] = _dot1(s, w_ref[0, 0]) + b_ref[0, 0]


def _adaln(c_all, ada_w, ada_b):
    r = c_all.shape[0]
    tn = 1024
    depth = ada_w.shape[0]
    b4 = ada_b.reshape(depth, 2, 1, 3 * D)
    return pl.pallas_call(
        _ada_kernel,
        grid=(depth * 2, 3 * D // tn),
        in_specs=[pl.BlockSpec((r, D), lambda s, j: (0, 0)),
                  pl.BlockSpec((1, 1, D, tn), lambda s, j: (s // 2, s % 2, 0, j)),
                  pl.BlockSpec((1, 1, 1, tn), lambda s, j: (s // 2, s % 2, 0, j))],
        out_specs=pl.BlockSpec((1, r, tn), lambda s, j: (s, 0, j)),
        out_shape=jax.ShapeDtypeStruct((depth * 2, r, 3 * D), F32),
        compiler_params=_cparams(("parallel", "parallel")),
        name="adaln",
    )(c_all, ada_w, b4)


def _pick(is_prompt, p_ref, s_ref):
    return jnp.where(is_prompt, p_ref[0], s_ref[...])


def _mod_specs(npt, rows_per_batch_tiles, col, tn=D, with_j=False):
    if with_j:
        pm = pl.BlockSpec((1, 1, tn), lambda i, j: (jnp.minimum(i // rows_per_batch_tiles, npt // rows_per_batch_tiles - 1), 0, col * (D // tn) + j))
        sm = pl.BlockSpec((TM, tn), lambda i, j: (0, col * (D // tn) + j))
    else:
        pm = pl.BlockSpec((1, 1, tn), lambda i, *_: (jnp.minimum(i // rows_per_batch_tiles, npt // rows_per_batch_tiles - 1), 0, col))
        sm = pl.BlockSpec((TM, tn), lambda i, *_: (0, col))
    return pm, sm


def _nm_mm_kernel(x_ref, g_ref, psh, psc, ssh, ssc, w_ref, o_ref, xn_ref, *, npt):
    i = pl.program_id(0)
    j = pl.program_id(1)

    @pl.when(j == 0)
    def _():
        xr = _rms(x_ref[...], g_ref[...])
        sc = _pick(i < npt, psc, ssc)
        sh = _pick(i < npt, psh, ssh)
        xn_ref[...] = (xr * (1.0 + sc) + sh).astype(BF16)

    o_ref[...] = jnp.dot(xn_ref[...], w_ref[...], preferred_element_type=F32)


def _normmod_mm(x, g, pmod, smod, w_bf16, tn, npt, tiles_per_batch):
    t = x.shape[0]
    n = w_bf16.shape[1]
    psh, ssh = _mod_specs(npt, tiles_per_batch, 0)
    psc, ssc = _mod_specs(npt, tiles_per_batch, 1)
    return pl.pallas_call(
        functools.partial(_nm_mm_kernel, npt=npt),
        grid=(t // TM, n // tn),
        in_specs=[pl.BlockSpec((TM, D), lambda i, j: (i, 0)),
                  pl.BlockSpec((1, D), lambda i, j: (0, 0)),
                  psh, psc, ssh, ssc,
                  pl.BlockSpec((D, tn), lambda i, j: (0, j))],
        out_specs=pl.BlockSpec((TM, tn), lambda i, j: (i, j)),
        out_shape=jax.ShapeDtypeStruct((t, n), F32),
        scratch_shapes=[pltpu.VMEM((TM, D), BF16)],
        compiler_params=_cparams(("parallel", "arbitrary")),
        name="normmod_mm",
    )(x, g.reshape(1, D), pmod, pmod, smod, smod, w_bf16)


def _proj_res_kernel(*refs, npt, nk):
    xp = refs[0:nk]
    xs = refs[nk:2 * nk]
    ws = refs[2 * nk:3 * nk]
    h_ref, pg, sg, o_ref = refs[3 * nk:3 * nk + 4]
    xb = refs[3 * nk + 4:]
    i = pl.program_id(0)
    j = pl.program_id(1)

    @pl.when(jnp.logical_and(j == 0, i < npt))
    def _():
        for k in range(nk):
            xb[k][...] = xp[k][...].astype(BF16)

    @pl.when(jnp.logical_and(j == 0, i >= npt))
    def _():
        for k in range(nk):
            xb[k][...] = xs[k][...].astype(BF16)

    acc = jnp.dot(xb[0][...], ws[0][...], preferred_element_type=F32)
    for k in range(1, nk):
        acc = acc + jnp.dot(xb[k][...], ws[k][...], preferred_element_type=F32)
    gate = _pick(i < npt, pg, sg)
    o_ref[...] = h_ref[...] + gate * acc


def _proj_residual(xps, xss, ws, h, pmod, smod, npt, tiles_per_batch, tn=1024):
    nk = len(ws)
    t = h.shape[0]
    ks = [w.shape[0] for w in ws]
    pg, sg = _mod_specs(npt, tiles_per_batch, 2, tn=tn, with_j=True)
    in_specs = ([pl.BlockSpec((TM, k), lambda i, j: (jnp.minimum(i, npt - 1), 0)) for k in ks]
                + [pl.BlockSpec((TM, k), lambda i, j: (0, 0)) for k in ks]
                + [pl.BlockSpec((k, tn), lambda i, j: (0, j)) for k in ks]
                + [pl.BlockSpec((TM, tn), lambda i, j: (i, j)), pg, sg])
    return pl.pallas_call(
        functools.partial(_proj_res_kernel, npt=npt, nk=nk),
        grid=(t // TM, D // tn),
        in_specs=in_specs,
        out_specs=pl.BlockSpec((TM, tn), lambda i, j: (i, j)),
        out_shape=jax.ShapeDtypeStruct((t, D), F32),
        scratch_shapes=[pltpu.VMEM((TM, k), BF16) for k in ks],
        compiler_params=_cparams(("parallel", "arbitrary")),
        name="proj_residual",
    )(*xps, *xss, *ws, h, pmod, smod)


HG_C = 64
HG_B = 16
HG_STEP = 256
HG_HEADS = 4


def _lower_bound(lbl, layer):
    e = jnp.exp(lbl - jnp.max(lbl, axis=0, keepdims=True))
    return jnp.sum(e[0:layer + 1], axis=0, keepdims=True) / jnp.sum(e, axis=0, keepdims=True)


def _hgrn_chunk(q, af, v, st, lb, tri):
    c, dk = q.shape
    nb = c // HG_B
    f = lb + (1.0 - lb) * _sigmoid(af)
    g = jnp.log(f)
    k = 1.0 - f
    cum = _dot_exact_rhs(tri, g)
    o = _dot1(q * jnp.exp(cum), st, NT)
    q3 = q.reshape(nb, HG_B, dk)
    k3 = k.reshape(nb, HG_B, dk)
    c3 = cum.reshape(nb, HG_B, dk)
    tpos = lax.broadcasted_iota(jnp.int32, (nb, HG_B, dk), 1)
    lane3 = lax.broadcasted_iota(jnp.int32, (nb, HG_B, dk), 2)
    dmat = jnp.zeros((nb, HG_B, dk), F32)
    for s in range(HG_B):
        rel = c3 - c3[:, s:s + 1, :]
        a = q3 * k3[:, s:s + 1, :] * jnp.exp(jnp.where(tpos >= s, rel, -jnp.inf))
        col = jnp.sum(a, axis=-1, keepdims=True)
        dmat = jnp.where(lane3 == s, col, dmat)
    lane_c = lax.broadcasted_iota(jnp.int32, (HG_B, c), 1)
    rows = []
    for blk in range(nb):
        r0 = HG_B * blk
        d_blk = dmat[blk]
        if blk == 0:
            rows.append(d_blk[:, :c])
            continue
        d_blk = pltpu.roll(d_blk, r0, axis=1)[:, :c]
        cb = cum[r0 - 1:r0, :]
        qs = q[r0:r0 + HG_B] * jnp.exp(cum[r0:r0 + HG_B] - cb)
        ks = k * jnp.exp(jnp.minimum(cb - cum, 0.0))
        s_off = _dot1(qs, ks, NT)
        rows.append(jnp.where(lane_c < r0, s_off, 0.0) + d_blk)
    scores = jnp.concatenate(rows, axis=0)
    o = o + _dot1(scores, v)
    last = cum[c - 1:c, :]
    st_new = jnp.exp(last) * st + _dot1(v, k * jnp.exp(last - cum), TN)
    return o, st_new


def _hgrn_p_kernel(q_ref, f_ref, v_ref, g_ref, lbl_ref, gn_ref, o_ref, s_ref, st_ref, *, layer, nsteps):
    tb = pl.program_id(2)

    @pl.when(tb == 0)
    def _():
        st_ref[...] = jnp.zeros_like(st_ref)

    lb = _lower_bound(lbl_ref[...], layer)
    ri = lax.broadcasted_iota(jnp.int32, (HG_C, HG_C), 0)
    ci = lax.broadcasted_iota(jnp.int32, (HG_C, HG_C), 1)
    tri = jnp.where(ri >= ci, 1.0, 0.0).astype(BF16)
    sts = [st_ref[hh] for hh in range(HG_HEADS)]
    for c in range(HG_STEP // HG_C):
        sl = slice(HG_C * c, HG_C * (c + 1))
        for hh in range(HG_HEADS):
            hl = slice(128 * hh, 128 * (hh + 1))
            o, sts[hh] = _hgrn_chunk(q_ref[sl, hl], f_ref[sl, hl], v_ref[sl, hl], sts[hh], lb[:, hl], tri)
            o_ref[sl, hl] = (_rms(o, gn_ref[...]) * _sigmoid(g_ref[sl, hl])).astype(BF16)
    for hh in range(HG_HEADS):
        st_ref[hh] = sts[hh]

    @pl.when(tb == nsteps - 1)
    def _():
        for hh in range(HG_HEADS):
            s_ref[0, hh] = sts[hh].T


def _hgrn_prompt(z, lb_logits, gn, nb, seq, layer):
    nh = 8
    nsteps = seq // HG_STEP
    hw = 128 * HG_HEADS
    ng = nh // HG_HEADS
    zspec = lambda c0: pl.BlockSpec((HG_STEP, hw), lambda b, h, t: (b * nsteps + t, c0 * ng + h))
    return pl.pallas_call(
        functools.partial(_hgrn_p_kernel, layer=layer, nsteps=nsteps),
        grid=(nb, ng, nsteps),
        in_specs=[zspec(0), zspec(1), zspec(2), zspec(3),
                  pl.BlockSpec((lb_logits.shape[0], hw), lambda b, h, t: (0, h)),
                  pl.BlockSpec((1, 128), lambda b, h, t: (0, 0))],
        out_specs=[pl.BlockSpec((HG_STEP, hw), lambda b, h, t: (b * nsteps + t, h)),
                   pl.BlockSpec((1, HG_HEADS, 128, 128), lambda b, h, t: (b, h, 0, 0))],
        out_shape=[jax.ShapeDtypeStruct((nb * seq, nh * 128), BF16),
                   jax.ShapeDtypeStruct((nb, nh, 128, 128), F32)],
        scratch_shapes=[pltpu.VMEM((HG_HEADS, 128, 128), F32)],
        compiler_params=_cparams(("parallel", "parallel", "arbitrary")),
        name="hgrn_prompt",
    )(z, z, z, z, lb_logits, gn.reshape(1, 128))


def _hgrn_s_kernel(q_ref, f_ref, v_ref, g_ref, lbl_ref, gn_ref, s0_ref, o_ref, s_ref, *, layer):
    nt = q_ref.shape[1]
    lb_all = _lower_bound(lbl_ref[...], layer)
    tpos = lax.broadcasted_iota(jnp.int32, (nt, 128), 0)
    outs = []
    for h in range(8):
        sl = slice(128 * h, 128 * (h + 1))
        q = q_ref[0, :, sl]
        v = v_ref[0, :, sl]
        lb = lb_all[:, sl]
        f = lb + (1.0 - lb) * _sigmoid(f_ref[0, :, sl])
        g = jnp.log(f)
        k = 1.0 - f
        crow = [g[0:1]]
        for t in range(1, nt):
            crow.append(crow[-1] + g[t:t + 1])
        cum = jnp.concatenate(crow, axis=0)
        last = crow[-1]
        s0 = s0_ref[0, 0, h]
        qe = jnp.concatenate([q * jnp.exp(cum), jnp.zeros((8 - nt, 128), F32)], axis=0)
        o = _dot1(qe, s0)[0:nt]
        for s in range(nt):
            a = q * k[s:s + 1] * jnp.exp(jnp.where(tpos >= s, cum - cum[s:s + 1], -jnp.inf))
            o = o + jnp.sum(a, axis=-1, keepdims=True) * v[s:s + 1]
        kp = k * jnp.exp(last - cum)
        w8 = jnp.concatenate([jnp.exp(last), kp, jnp.zeros((8 - 1 - nt, 128), F32)], axis=0)
        wt = jnp.concatenate([w8, jnp.zeros((120, 128), F32)], axis=0).T
        s_new = wt[:, 0:1] * s0
        for s in range(nt):
            s_new = s_new + wt[:, 1 + s:2 + s] * v[s:s + 1]
        s_ref[0, 0, h] = s_new
        outs.append(_rms(o, gn_ref[...]) * _sigmoid(g_ref[0, :, sl]))
    o_ref[0] = jnp.concatenate(outs, axis=1)


def _hgrn_sample(zs3, lb_logits, gn, state, layer):
    nb, nt, _ = zs3.shape
    zspec = lambda c: pl.BlockSpec((1, nt, 1024), lambda b: (b, 0, c))
    sspec = pl.BlockSpec((1, 1, 8, 128, 128), lambda b: (0, b, 0, 0, 0))
    return pl.pallas_call(
        functools.partial(_hgrn_s_kernel, layer=layer),
        grid=(nb,),
        in_specs=[zspec(0), zspec(1), zspec(2), zspec(3),
                  pl.BlockSpec(lb_logits.shape, lambda b: (0, 0)),
                  pl.BlockSpec((1, 128), lambda b: (0, 0)),
                  sspec],
        out_specs=[pl.BlockSpec((1, nt, 1024), lambda b: (b, 0, 0)), sspec],
        out_shape=[jax.ShapeDtypeStruct((nb, nt, 1024), F32),
                   jax.ShapeDtypeStruct(state.shape, F32)],
        compiler_params=_cparams(("parallel",)),
        name="hgrn_sample",
    )(zs3, zs3, zs3, zs3, lb_logits, gn.reshape(1, 128), state)


DILATED = ((128, 1), (512, 4), (2048, 16))
REL_BUCKETS = 32
REL_MAX_DISTANCE = 2048
ATT_SCALE = 128 ** -0.5
QB = 128


def _rel_bucket_np(dist):
    exact = REL_BUCKETS // 2
    d = np.asarray(dist)
    far = exact + (np.log(np.maximum(d, 1).astype(np.float32) / np.float32(exact))
                   / np.float32(math.log(REL_MAX_DISTANCE / exact)) * np.float32(REL_BUCKETS - exact)).astype(np.int32)
    return np.where(d < exact, d, np.minimum(far, REL_BUCKETS - 1)).astype(np.int32)


def _band_tables(rel_bias):
    rb = rel_bias.astype(F32)
    tabs = []
    for (window, dil), width in zip(DILATED, (2 * QB, 2 * QB, QB)):
        period = width + QB
        i = np.arange(period + 1)
        k = np.where(i < width, i, i - (period + 1))
        steps = (width - QB) - k
        valid = (steps >= 0) & (steps <= window // dil)
        bucket = _rel_bucket_np(np.clip(steps, 0, window // dil) * dil)
        u = jnp.where(jnp.asarray(valid)[None, :], rb[jnp.asarray(bucket)].T, -jnp.inf)
        flat = jnp.tile(u, (1, QB))[:, :QB * period]
        tabs.append(flat.reshape(rb.shape[1], QB, period)[:, :, :width])
    return tabs


def _soft_block(qb, kw, vw, bias):
    l = _dot1(qb, kw, NT) * ATT_SCALE + bias
    m = jnp.max(l, axis=-1, keepdims=True)
    p = jnp.exp(l - m)
    return m, jnp.sum(p, axis=-1, keepdims=True), _dot1(p, vw)


def _dil_p_kernel(q_ref, k_ref, v_ref, gq_ref, gk_ref, b1_ref, b2_ref, b3_ref, o_ref, ko_ref, vo_ref,
                  qs, ks, vs, a1, a2, a3, m1, m2, m3, l1, l2, l3):
    seq = q_ref.shape[0]
    kn = _rms(k_ref[...], gk_ref[...])
    v = v_ref[...]
    ko_ref[...] = kn
    vo_ref[...] = v
    qs[...] = _rms(q_ref[...], gq_ref[...])
    ks[...] = kn
    vs[...] = v
    zpad = jnp.zeros((QB, 128), F32)
    first_cols = lax.broadcasted_iota(jnp.int32, (QB, 2 * QB), 1) >= QB

    def banded(qsub, ksub, vsub, bias, n_blocks, store):
        for i in range(n_blocks):
            qb = qsub(i)
            if i == 0:
                kw = jnp.concatenate([zpad, ksub(0)], axis=0)
                vw = jnp.concatenate([zpad, vsub(0)], axis=0)
                bb = jnp.where(first_cols, bias, -jnp.inf)
            else:
                kw = jnp.concatenate([ksub(i - 1), ksub(i)], axis=0)
                vw = jnp.concatenate([vsub(i - 1), vsub(i)], axis=0)
                bb = bias
            store(i, *_soft_block(qb, kw, vw, bb))

    def st1(i, m, l, a):
        sl = slice(QB * i, QB * (i + 1))
        m1[sl, :] = m
        l1[sl, :] = l
        a1[sl, :] = a
    blk = lambda ref: (lambda i: ref[QB * i:QB * (i + 1), :])
    banded(blk(qs), blk(ks), blk(vs), b1_ref[0], seq // QB, st1)

    d2 = DILATED[1][1]
    for r in range(d2):
        sub = lambda ref: (lambda i: ref[pl.ds(r + d2 * QB * i, QB, stride=d2), :])

        def st2(i, m, l, a):
            idx = pl.ds(r + d2 * QB * i, QB, stride=d2)
            m2[idx, :] = m
            l2[idx, :] = l
            a2[idx, :] = a
        banded(sub(qs), sub(ks), sub(vs), b2_ref[0], seq // (QB * d2), st2)

    d3 = DILATED[2][1]
    for r in range(d3):
        idx = pl.ds(r, QB, stride=d3)
        m, l, a = _soft_block(qs[idx, :], ks[idx, :], vs[idx, :], b3_ref[0])
        m3[idx, :] = m
        l3[idx, :] = l
        a3[idx, :] = a

    mx = jnp.maximum(jnp.maximum(m1[...], m2[...]), m3[...])
    w1 = jnp.exp(m1[...] - mx)
    w2 = jnp.exp(m2[...] - mx)
    w3 = jnp.exp(m3[...] - mx)
    num = w1 * a1[...] + w2 * a2[...] + w3 * a3[...]
    den = w1 * l1[...] + w2 * l2[...] + w3 * l3[...]
    o_ref[...] = (num / den).astype(BF16)


def _dilated_prompt(z, gq, gk, tabs, nb, seq):
    nh = 8
    zspec = lambda c0: pl.BlockSpec((seq, 128), lambda b, h: (b, c0 + h))
    gspec = pl.BlockSpec((1, 128), lambda b, h: (0, 0))
    tspec = lambda w: pl.BlockSpec((1, QB, w), lambda b, h: (h, 0, 0))
    ospec = pl.BlockSpec((seq, 128), lambda b, h: (b, h))
    big = pltpu.VMEM((seq, 128), F32)
    col = pltpu.VMEM((seq, 1), F32)
    return pl.pallas_call(
        _dil_p_kernel,
        grid=(nb, nh),
        in_specs=[zspec(32), zspec(40), zspec(48), gspec, gspec, tspec(2 * QB), tspec(2 * QB), tspec(QB)],
        out_specs=[ospec, ospec, ospec],
        out_shape=[jax.ShapeDtypeStruct((nb * seq, nh * 128), BF16),
                   jax.ShapeDtypeStruct((nb * seq, nh * 128), F32),
                   jax.ShapeDtypeStruct((nb * seq, nh * 128), F32)],
        scratch_shapes=[big, big, big, big, big, big, col, col, col, col, col, col],
        compiler_params=_cparams(("parallel", "parallel")),
        name="dilated_prompt",
    )(z, z, z, gq.reshape(1, 128), gk.reshape(1, 128), *tabs)


def _sample_tables(rel_bias, nt, buf_len):
    rb = rel_bias.astype(F32)
    nh = rb.shape[1]
    w2 = DILATED[1][0]

    def count(delta, patterns):
        c = np.zeros(delta.shape, np.int32)
        for window, dil in patterns:
            c += ((delta >= 0) & (delta <= window) & (delta % dil == 0)).astype(np.int32)
        return c

    qpos = buf_len + np.arange(8)
    real_q = (np.arange(8) < nt)
    da = qpos[None, :] - (buf_len - w2 + np.arange(w2))[:, None]
    ca = count(da, DILATED[:2]) * real_q[None, :]
    nj = buf_len // 16
    rows_b = (16 * np.arange(nj)[None, :] + np.arange(nt)[:, None]).reshape(-1)
    db = qpos[None, :] - rows_b[:, None]
    cb = count(db, DILATED[2:]) * real_q[None, :]
    dn = qpos[None, :] - qpos[:, None]
    cn = count(dn, DILATED) * real_q[None, :] * real_q[:, None]

    def tables(delta, cnt):
        bucket = _rel_bucket_np(np.clip(delta, 0, REL_MAX_DISTANCE))
        onehot = (bucket.reshape(-1, 1) == np.arange(REL_BUCKETS)[None, :]).astype(np.float32)
        bias = jnp.dot(jnp.asarray(onehot), rb, precision=lax.Precision.HIGHEST).reshape(*delta.shape, nh)
        bias = jnp.where(jnp.asarray((cnt > 0) | ~real_q[None, :])[:, :, None], bias, -jnp.inf)
        bias = jnp.transpose(bias, (0, 2, 1)).reshape(delta.shape[0], nh * 8)
        mult = np.repeat(cnt[:, None, :], nh, axis=1).reshape(delta.shape[0], nh * 8).astype(np.float32)
        return bias, jnp.asarray(mult)

    return tables(da, ca) + tables(db, cb) + tables(dn, cn)


def _dil_s_kernel(q_ref, k_ref, v_ref, gq_ref, gk_ref, ca_ref, cb_ref,
                  ba_ref, ma_ref, bb_ref, mb_ref, bn_ref, mn_ref, o_ref, ko_ref, vo_ref):
    nt = q_ref.shape[1]
    nh = 8
    gq = gq_ref[...]
    gk = gk_ref[...]
    qn = jnp.concatenate([_rms(q_ref[0, :, 128 * h:128 * (h + 1)], gq) for h in range(nh)], axis=1)
    kn = jnp.concatenate([_rms(k_ref[0, :, 128 * h:128 * (h + 1)], gk) for h in range(nh)], axis=1)
    vn = v_ref[0]
    ko_ref[0] = kn
    vo_ref[0] = vn
    pad = jnp.zeros((8 - nt, nh * 128), F32)
    q8 = jnp.concatenate([qn, pad], axis=0)
    k8 = jnp.concatenate([kn, pad], axis=0)
    v8 = jnp.concatenate([vn, pad], axis=0)
    lane_head = lax.broadcasted_iota(jnp.int32, (8, nh * 128), 1) // 128
    qbd = jnp.concatenate([jnp.where(lane_head == h, q8, 0.0) for h in range(nh)], axis=0).astype(BF16)

    def heads(ref2d, first, nrows, stride):
        return jnp.concatenate([ref2d[pl.ds(first + h, nrows, stride=stride), :] for h in range(nh)], axis=1)

    a2 = ca_ref.at[0]
    na = a2.shape[0] // (2 * nh)
    groups = [(heads(a2, 0, na, 2 * nh), heads(a2, nh, na, 2 * nh), ba_ref[...], ma_ref[...])]
    nj = cb_ref.shape[1]
    per_j = cb_ref.shape[2]
    b2 = cb_ref.reshape(nj * per_j, 128)
    for i in range(nt):
        groups.append((heads(b2, 2 * nh * i, nj, per_j), heads(b2, 2 * nh * i + nh, nj, per_j),
                       bb_ref[nj * i:nj * (i + 1), :], mb_ref[nj * i:nj * (i + 1), :]))
    groups.append((k8, v8, bn_ref[...], mn_ref[...]))

    logits = [_dot1(kk, qbd, NT) * ATT_SCALE + bias for kk, _, bias, _ in groups]
    mx = logits[0].max(axis=0, keepdims=True)
    for l in logits[1:]:
        mx = jnp.maximum(mx, l.max(axis=0, keepdims=True))
    ps = [mult * jnp.exp(l - mx) for l, (_, _, _, mult) in zip(logits, groups)]
    den = ps[0].sum(axis=0, keepdims=True)
    for p in ps[1:]:
        den = den + p.sum(axis=0, keepdims=True)
    inv = 1.0 / jnp.where(den > 0.0, den, 1.0)
    acc = None
    for p, (_, vv, _, _) in zip(ps, groups):
        part = _dot1(p * inv, vv, TN)
        acc = part if acc is None else acc + part
    o8 = jnp.concatenate([acc[8 * h:8 * (h + 1), 128 * h:128 * (h + 1)] for h in range(nh)], axis=1)
    o_ref[0] = o8[0:nt]


def _dilated_sample(zs3, gq, gk, cache, tabs):
    nb, nt, _ = zs3.shape
    buf_len = cache.shape[1]
    w2 = DILATED[1][0]
    d3 = DILATED[2][1]
    kvh = 2 * 8
    cache_a = cache.reshape(nb, buf_len * kvh, 128)
    cache_b = cache.reshape(nb, buf_len // d3, d3 * kvh, 128)
    zspec = lambda c: pl.BlockSpec((1, nt, 1024), lambda b: (b, 0, c))
    gspec = pl.BlockSpec((1, 128), lambda b: (0, 0))
    full = lambda a: pl.BlockSpec(a.shape, lambda b: (0, 0))
    ospec = pl.BlockSpec((1, nt, 1024), lambda b: (b, 0, 0))
    return pl.pallas_call(
        _dil_s_kernel,
        grid=(nb,),
        in_specs=[zspec(4), zspec(5), zspec(6), gspec, gspec,
                  pl.BlockSpec((1, w2 * kvh, 128), lambda b: (b, buf_len // w2 - 1, 0)),
                  pl.BlockSpec((1, buf_len // d3, nt * kvh, 128), lambda b: (b, 0, 0, 0))]
                 + [full(a) for a in tabs],
        out_specs=[ospec, ospec, ospec],
        out_shape=[jax.ShapeDtypeStruct((nb, nt, 1024), F32)] * 3,
        compiler_params=_cparams(("parallel",)),
        name="dilated_sample",
    )(zs3, zs3, zs3, gq.reshape(1, 128), gk.reshape(1, 128), cache_a, cache_b, *tabs)


def _route_kernel(x_ref, g_ref, psh, psc, ssh, ssc, rw_ref, rb_ref, u_ref, idx_ref, gate_ref, *, npt):
    i = pl.program_id(0)
    xr = _rms(x_ref[...], g_ref[...])
    u = xr * (1.0 + _pick(i < npt, psc, ssc)) + _pick(i < npt, psh, ssh)
    ub = u.astype(BF16).astype(F32)
    lo = lax.shift_right_logical(lax.bitcast_convert_type(ub[:, :D // 2], jnp.int32), 16)
    u_ref[...] = lax.bitcast_convert_type(ub[:, D // 2:], jnp.int32) | lo
    l = _dot3(u, rw_ref[...]) + rb_ref[...]
    lane = lax.broadcasted_iota(jnp.int32, l.shape, 1).astype(F32)
    out_lane = lax.broadcasted_iota(jnp.int32, (TM, 128), 1)
    vals, idxs = [], []
    for _ in range(TOP_K):
        m = jnp.max(l, axis=-1, keepdims=True)
        first = jnp.min(jnp.where(l == m, lane, float(N_EXPERTS)), axis=-1, keepdims=True)
        vals.append(m)
        idxs.append(first.astype(jnp.int32))
        l = jnp.where(lane == first, -jnp.inf, l)
    es = [jnp.exp(v - vals[0]) for v in vals]
    tot = es[0] + es[1] + es[2] + es[3]
    idx_out = jnp.zeros((TM, 128), jnp.int32)
    gate_out = jnp.zeros((TM, 128), F32)
    for k in range(TOP_K):
        idx_out = jnp.where(out_lane == k, idxs[k], idx_out)
        gate_out = jnp.where(out_lane == k, es[k] / tot, gate_out)
    idx_ref[...] = idx_out
    gate_ref[...] = gate_out


def _route(x, g, pmod, smod, rw, rb, npt, tiles_per_batch):
    t = x.shape[0]
    psh, ssh = _mod_specs(npt, tiles_per_batch, 0)
    psc, ssc = _mod_specs(npt, tiles_per_batch, 1)
    return pl.pallas_call(
        functools.partial(_route_kernel, npt=npt),
        grid=(t // TM,),
        in_specs=[pl.BlockSpec((TM, D), lambda i: (i, 0)),
                  pl.BlockSpec((1, D), lambda i: (0, 0)),
                  psh, psc, ssh, ssc,
                  pl.BlockSpec((D, N_EXPERTS), lambda i: (0, 0)),
                  pl.BlockSpec((1, N_EXPERTS), lambda i: (0, 0))],
        out_specs=[pl.BlockSpec((TM, D // 2), lambda i: (i, 0)),
                   pl.BlockSpec((TM, 128), lambda i: (i, 0)),
                   pl.BlockSpec((TM, 128), lambda i: (i, 0))],
        out_shape=[jax.ShapeDtypeStruct((t, D // 2), jnp.int32),
                   jax.ShapeDtypeStruct((t, 128), jnp.int32),
                   jax.ShapeDtypeStruct((t, 128), F32)],
        compiler_params=_cparams(("parallel",)),
        name="moe_route",
    )(x, g.reshape(1, D), pmod, pmod, smod, smod, rw, rb.reshape(1, N_EXPERTS))


MOE_TM = 1344
MOE_ROWS = (304, 608, 1056, 1216, MOE_TM)
MOE_TN = 1024
MOE_SEL = 512
MOE_TF = MOE_TN // 2


MOE_NG = 2 * D_EXPERT // MOE_TN
MOE_DN = 512
MOE_ND = D // MOE_DN


def _moe_kernel(te_ref, nu_ref, nv_ref, tok_ref, u_hbm, wgu_ref, bgu_ref, wdn_ref, bdn_ref, y_ref, xg, xb, hs, sem):
    t = pl.program_id(0)
    j = pl.program_id(1)
    nu = nu_ref[0]
    active = t < nu
    nv = nv_ref[t]

    def rows8(n):
        return (n + 7) // 8 * 8

    def issue(tile):
        def body(g, carry):
            for i in range(8):
                r = g * 8 + i
                tok = tok_ref[tile * MOE_TM + r]
                pltpu.make_async_copy(u_hbm.at[pl.ds(tok, 1)], xg.at[pl.ds(r, 1)], sem).start()
            return carry
        lax.fori_loop(0, rows8(nv_ref[tile]) // 8, body, 0)

    @pl.when(jnp.logical_and(t == 0, j == 0))
    def _():
        xg[...] = jnp.zeros_like(xg)
        issue(0)

    @pl.when(jnp.logical_and(active, j == 0))
    def _():
        n8 = pl.multiple_of(rows8(nv), 8)
        pltpu.make_async_copy(u_hbm.at[pl.ds(0, n8)], xg.at[pl.ds(0, n8)], sem).wait()
        w = xg[...]
        xb[:, :D // 2] = lax.bitcast_convert_type(lax.shift_left(w, 16), F32).astype(BF16)
        xb[:, D // 2:] = lax.bitcast_convert_type(w & -65536, F32).astype(BF16)

    @pl.when(jnp.logical_and(t + 1 < nu, j == 1))
    def _():
        issue(t + 1)

    def gate_up(rows):
        w = wgu_ref[0, 0].astype(BF16)
        ri = lax.broadcasted_iota(jnp.int32, (MOE_SEL, MOE_SEL // 2), 0)
        ci = lax.broadcasted_iota(jnp.int32, (MOE_SEL, MOE_SEL // 2), 1)
        pick_even = jnp.where(ri == 2 * ci, 1.0, 0.0).astype(BF16)
        parts = 2 if rows > MOE_TM // 2 else 1
        step = rows // parts
        gus = [jnp.dot(xb[step * p:step * (p + 1), :], w, preferred_element_type=F32) for p in range(parts)]
        for p, gu in enumerate(gus):
            gu = gu + bgu_ref[0, 0]
            up = pltpu.roll(gu, MOE_TN - 1, axis=1)
            gate = jnp.minimum(gu, SWIGLU_LIMIT)
            up = jnp.clip(up, -SWIGLU_LIMIT, SWIGLU_LIMIT)
            sig = 0.5 * jnp.tanh((0.5 * SWIGLU_ALPHA) * gate) + 0.5
            h = ((up + 1.0) * gate * sig).astype(BF16)
            hc = [jnp.dot(h[:, MOE_SEL * s:MOE_SEL * (s + 1)], pick_even, preferred_element_type=F32).astype(BF16)
                  for s in range(MOE_TN // MOE_SEL)]
            hs[j, step * p:step * (p + 1), :] = jnp.concatenate(hc, axis=1)

    def down(rows):
        acc = jnp.dot(hs[0, 0:rows, :], wdn_ref[0, 0, 0:MOE_TF, :].astype(BF16), preferred_element_type=F32)
        for c in range(1, MOE_NG):
            acc = acc + jnp.dot(hs[c, 0:rows, :], wdn_ref[0, 0, MOE_TF * c:MOE_TF * (c + 1), :].astype(BF16),
                                preferred_element_type=F32)
        y_ref[0:rows, :] = acc + bdn_ref[0, 0]
        if rows < MOE_TM:
            y_ref[rows:MOE_TM, :] = jnp.zeros((MOE_TM - rows, MOE_DN), F32)

    lower = 0
    for rows in MOE_ROWS:
        fits = jnp.logical_and(active, jnp.logical_and(nv > lower, nv <= rows))
        pl.when(jnp.logical_and(fits, j < MOE_NG))(functools.partial(gate_up, rows))
        pl.when(jnp.logical_and(fits, j >= MOE_NG))(functools.partial(down, rows))
        lower = rows

    @pl.when(jnp.logical_and(jnp.logical_not(active), j >= MOE_NG))
    def _():
        y_ref[...] = jnp.zeros_like(y_ref)


def _moe_experts(u, tile_e, n_used, n_valid, slot_tok, layer, w_gu, b_gu, w_dn, b_dn):
    n_tiles = tile_e.shape[0]
    nj = MOE_NG + MOE_ND
    jj = lambda t, j, nu: jnp.where(t < nu[0], j, nj - 1)
    jg = lambda t, j, nu: jnp.minimum(jj(t, j, nu), MOE_NG - 1)
    jd = lambda t, j, nu: jnp.maximum(jj(t, j, nu) - MOE_NG, 0)
    grid_spec = pltpu.PrefetchScalarGridSpec(
        num_scalar_prefetch=4,
        grid=(n_tiles, nj),
        in_specs=[pl.BlockSpec(memory_space=pl.ANY),
                  pl.BlockSpec((1, 1, D, MOE_TN), lambda t, j, te, nu, nv, tok: (layer, te[t], 0, jg(t, j, nu))),
                  pl.BlockSpec((1, 1, 1, MOE_TN), lambda t, j, te, nu, nv, tok: (layer, te[t], 0, jg(t, j, nu))),
                  pl.BlockSpec((1, 1, D_EXPERT, MOE_DN), lambda t, j, te, nu, nv, tok: (layer, te[t], 0, jd(t, j, nu))),
                  pl.BlockSpec((1, 1, 1, MOE_DN), lambda t, j, te, nu, nv, tok: (layer, te[t], 0, jd(t, j, nu)))],
        out_specs=pl.BlockSpec((MOE_TM, MOE_DN), lambda t, j, te, nu, nv, tok: (t, jnp.maximum(j - MOE_NG, 0))),
        scratch_shapes=[pltpu.VMEM((MOE_TM, D // 2), jnp.int32), pltpu.VMEM((MOE_TM, D), BF16),
                        pltpu.VMEM((MOE_NG, MOE_TM, MOE_TF), BF16), pltpu.SemaphoreType.DMA],
    )
    nl, ne = b_gu.shape[0], b_gu.shape[1]
    return pl.pallas_call(
        _moe_kernel,
        grid_spec=grid_spec,
        out_shape=jax.ShapeDtypeStruct((n_tiles * MOE_TM, D), F32),
        compiler_params=_cparams(("arbitrary", "arbitrary")),
        name="moe_experts",
    )(tile_e, n_used, n_valid, slot_tok, u, w_gu, b_gu.reshape(nl, ne, 1, 2 * D_EXPERT), w_dn, b_dn.reshape(nl, ne, 1, D))


CMB = 256


def _combine_kernel(slot_ref, h_ref, gate_ref, pg, sg, y_hbm, *rest, npt_c, split):
    if split:
        op_ref, os_ref, buf, sem = rest
    else:
        o_ref, buf, sem = rest
    i = pl.program_id(0)

    def issue(r, carry):
        for k in range(TOP_K):
            s = slot_ref[(i * CMB + r) * TOP_K + k]
            pltpu.make_async_copy(y_hbm.at[pl.ds(s, 1)], buf.at[k, pl.ds(r, 1)], sem).start()
        return carry
    lax.fori_loop(0, CMB, issue, 0, unroll=4)
    for k in range(TOP_K):
        pltpu.make_async_copy(y_hbm.at[pl.ds(0, CMB)], buf.at[k], sem).wait()
    gates = gate_ref[...]
    mo = gates[:, 0:1] * buf[0]
    for k in range(1, TOP_K):
        mo = mo + gates[:, k:k + 1] * buf[k]
    out = h_ref[...] + _pick(i < npt_c, pg, sg) * mo
    if split:
        @pl.when(i < npt_c)
        def _():
            op_ref[...] = out

        @pl.when(i >= npt_c)
        def _():
            os_ref[...] = out
    else:
        o_ref[...] = out


def _moe_combine(h, y, slot4, gate128, pmod, smod, npt_c, tiles_per_batch_c, split=False):
    t = h.shape[0]
    nsamp_tiles = (t // CMB) - npt_c
    if split:
        out_specs = [pl.BlockSpec((CMB, D), lambda i, s: (jnp.minimum(i, npt_c - 1), 0)),
                     pl.BlockSpec((CMB, D), lambda i, s: (jnp.clip(i - npt_c, 0, nsamp_tiles - 1), 0))]
        out_shape = [jax.ShapeDtypeStruct((npt_c * CMB, D), F32), jax.ShapeDtypeStruct((nsamp_tiles * CMB, D), F32)]
    else:
        out_specs = pl.BlockSpec((CMB, D), lambda i, s: (i, 0))
        out_shape = jax.ShapeDtypeStruct((t, D), F32)
    grid_spec = pltpu.PrefetchScalarGridSpec(
        num_scalar_prefetch=1,
        grid=(t // CMB,),
        in_specs=[pl.BlockSpec((CMB, D), lambda i, s: (i, 0)),
                  pl.BlockSpec((CMB, 128), lambda i, s: (i, 0)),
                  pl.BlockSpec((1, 1, D), lambda i, s: (jnp.minimum(i // tiles_per_batch_c, npt_c // tiles_per_batch_c - 1), 0, 2)),
                  pl.BlockSpec((CMB, D), lambda i, s: (jnp.clip(i - npt_c, 0, nsamp_tiles - 1), 2)),
                  pl.BlockSpec(memory_space=pl.ANY)],
        out_specs=out_specs,
        scratch_shapes=[pltpu.VMEM((TOP_K, CMB, D), F32), pltpu.SemaphoreType.DMA],
    )
    return pl.pallas_call(
        functools.partial(_combine_kernel, npt_c=npt_c, split=split),
        grid_spec=grid_spec,
        out_shape=out_shape,
        compiler_params=_cparams(("arbitrary",)),
        name="moe_combine",
    )(slot4, h, gate128, pmod, smod, y)


def _moe_layer(h, g, pmod, smod, layer, rw, rb, w_gu, b_gu, w_dn, b_dn, npt, tiles_per_batch, split=False):
    t = h.shape[0]
    u, idx128, gate128 = _route(h, g, pmod, smod, rw[layer], rb[layer], npt, tiles_per_batch)
    idx4 = idx128[:, :TOP_K]
    onehot = idx4[:, :, None] == jnp.arange(N_EXPERTS, dtype=jnp.int32)[None, None, :]
    sel = jnp.any(onehot, axis=1).astype(jnp.int32)
    counts = jnp.sum(sel, axis=0)
    rank = jnp.cumsum(sel, axis=0) - 1
    padded = (counts + MOE_TM - 1) // MOE_TM * MOE_TM
    pad_end = jnp.cumsum(padded)
    pad_start = pad_end - padded
    n_tiles = -(-t * TOP_K // MOE_TM) + N_EXPERTS
    dest4 = jnp.sum(jnp.where(onehot, (pad_start[None, :] + rank)[:, None, :], 0), axis=2).astype(jnp.int32)
    tok = jnp.broadcast_to(jnp.arange(t, dtype=jnp.int32)[:, None], (t, TOP_K))
    slot_tok = jnp.zeros((n_tiles * MOE_TM,), jnp.int32).at[dest4.reshape(-1)].set(tok.reshape(-1))
    n_used = (pad_end[-1] // MOE_TM).astype(jnp.int32)
    tile_ids = jnp.arange(n_tiles, dtype=jnp.int32)
    tile_e = jnp.minimum(jnp.searchsorted(pad_end, jnp.minimum(tile_ids, n_used - 1) * MOE_TM, side='right'),
                         N_EXPERTS - 1).astype(jnp.int32)
    n_valid = jnp.where(tile_ids < n_used,
                        jnp.clip((pad_start + counts)[tile_e] - tile_ids * MOE_TM, 0, MOE_TM), 0).astype(jnp.int32)
    y = _moe_experts(u, tile_e, n_used.reshape(1), n_valid, slot_tok, layer, w_gu, b_gu, w_dn, b_dn)
    return _moe_combine(h, y, dest4.reshape(-1), gate128, pmod, smod,
                        npt * (TM // CMB), tiles_per_batch * (TM // CMB), split)


C_HEADS = 16
C_NOPE = 128
C_ROPE = 64
C_LORA = 512
ROPE_THETA = 10000.0
MLA_SCALE = (C_NOPE + C_ROPE) ** -0.5
HP = 128


def _rope_tables(pos):
    half = C_ROPE // 2
    inv = ROPE_THETA ** (-jnp.arange(half, dtype=F32) / half)
    ang = pos.astype(F32)[:, None] * inv[None, :]
    z = jnp.zeros_like(ang)
    cos = jnp.concatenate([jnp.cos(ang), jnp.cos(ang), z, z], axis=1)
    sin = jnp.concatenate([-jnp.sin(ang), jnp.sin(ang), z, z], axis=1)
    return cos, sin


def _rope128(x, cos, sin):
    n = x.shape[1]
    lane = lax.broadcasted_iota(jnp.int32, x.shape, 1) % HP
    partner = jnp.where(lane < C_ROPE // 2, pltpu.roll(x, n - C_ROPE // 2, axis=1), pltpu.roll(x, C_ROPE // 2, axis=1))
    return x * cos + partner * sin


def _mla_prep_kernel(z_ref, gq_ref, gkv_ref, gkr_ref, gqn_ref, gqr_ref, gkn_ref, wn_ref, wr_ref, wuk_ref,
                     cos_ref, sin_ref, lat_ref, kr_ref, krb_ref, qn_ref, qr_ref, qabs_ref, qg_ref, *, npt):
    i = pl.program_id(0)
    z = z_ref[...]
    cos = cos_ref[...]
    sin = sin_ref[...]
    lat_ref[...] = _rms(z[:, C_LORA:2 * C_LORA], gkv_ref[...])
    k128 = z[:, 2 * C_LORA:2 * C_LORA + HP]
    k128 = k128 * lax.rsqrt(jnp.sum(k128 * k128, axis=-1, keepdims=True) / C_ROPE + EPS) * gkr_ref[...]
    k128 = _rope128(k128, cos, sin)
    kr_ref[...] = k128[:, 0:C_ROPE]
    krb_ref[...] = k128.astype(BF16)
    cqn = _rms(z[:, 0:C_LORA], gq_ref[...]).astype(BF16)
    qn_all = jnp.dot(cqn, wn_ref[...], preferred_element_type=F32)
    qr_all = jnp.dot(cqn, wr_ref[...], preferred_element_type=F32)
    for h in range(C_HEADS):
        sl = slice(HP * h, HP * (h + 1))
        qn = _rms(qn_all[:, sl], gqn_ref[...]) * MLA_SCALE
        qn_ref[:, sl] = qn.astype(BF16)
        qg_ref[:, sl] = qn * gkn_ref[...]
        x = qr_all[:, sl]
        x = x * lax.rsqrt(jnp.sum(x * x, axis=-1, keepdims=True) / C_ROPE + EPS) * gqr_ref[...]
        qr_ref[:, sl] = (_rope128(x, cos, sin) * MLA_SCALE).astype(BF16)

    @pl.when(i == npt)
    def _():
        for h in range(C_HEADS):
            sl = slice(HP * h, HP * (h + 1))
            qabs_ref[:, C_LORA * h:C_LORA * (h + 1)] = _dot1(qg_ref[:, sl], wuk_ref[:, sl], NT).astype(BF16)


def _mla_prep(z1, gq, gkv, gkr, gqn, gqr, gkn, wn, wr, wuk, cos, sin, npt):
    t = z1.shape[0]
    zw = z1.shape[1]
    row = lambda w: pl.BlockSpec((TM, w), lambda i: (i, 0))
    full = lambda a: pl.BlockSpec(a.shape, lambda i: (0,) * a.ndim)
    pad64 = lambda g: jnp.concatenate([g, jnp.zeros((HP - C_ROPE,), F32)]).reshape(1, HP)
    args = (z1, gq.reshape(1, -1), gkv.reshape(1, -1), pad64(gkr), gqn.reshape(1, -1), pad64(gqr), gkn.reshape(1, -1),
            wn, wr, wuk, cos, sin)
    return pl.pallas_call(
        functools.partial(_mla_prep_kernel, npt=npt),
        grid=(t // TM,),
        in_specs=[row(zw)] + [full(a) for a in args[1:10]] + [row(HP), row(HP)],
        out_specs=[row(C_LORA), row(C_ROPE), row(HP), row(C_HEADS * HP), row(C_HEADS * HP),
                   pl.BlockSpec((TM, C_HEADS * C_LORA), lambda i: (0, 0))],
        out_shape=[jax.ShapeDtypeStruct((t, C_LORA), F32), jax.ShapeDtypeStruct((t, C_ROPE), F32),
                   jax.ShapeDtypeStruct((t, HP), BF16), jax.ShapeDtypeStruct((t, C_HEADS * HP), BF16),
                   jax.ShapeDtypeStruct((t, C_HEADS * HP), BF16), jax.ShapeDtypeStruct((TM, C_HEADS * C_LORA), BF16)],
        scratch_shapes=[pltpu.VMEM((TM, C_HEADS * HP), F32)],
        compiler_params=_cparams(("arbitrary",)),
        name="mla_prep",
    )(*args)


def _mla_kv_kernel(lat_ref, wuk_ref, wuv_ref, gkn_ref, kn_ref, v_ref):
    lat = lat_ref[...].astype(BF16)
    kn = jnp.dot(lat, wuk_ref[...], preferred_element_type=F32)
    for h in range(C_HEADS):
        sl = slice(HP * h, HP * (h + 1))
        kn_ref[:, sl] = _rms(kn[:, sl], gkn_ref[...]).astype(BF16)
    v_ref[...] = jnp.dot(lat, wuv_ref[...], preferred_element_type=F32).astype(BF16)


def _mla_kv(lat, wuk, wuv, gkn, rows):
    wide = C_HEADS * HP
    return pl.pallas_call(
        _mla_kv_kernel,
        grid=(rows // TM,),
        in_specs=[pl.BlockSpec((TM, C_LORA), lambda i: (i, 0)),
                  pl.BlockSpec((C_LORA, wide), lambda i: (0, 0)),
                  pl.BlockSpec((C_LORA, wide), lambda i: (0, 0)),
                  pl.BlockSpec((1, HP), lambda i: (0, 0))],
        out_specs=[pl.BlockSpec((TM, wide), lambda i: (i, 0))] * 2,
        out_shape=[jax.ShapeDtypeStruct((rows, wide), BF16)] * 2,
        compiler_params=_cparams(("parallel",)),
        name="mla_kv",
    )(lat, wuk, wuv, gkn.reshape(1, HP))


FQ = 512


def _mla_flash_kernel(qn_ref, qr_ref, kn_ref, kr_ref, v_ref, o_ref, kcat):
    qi = pl.program_id(2)

    @pl.when(qi == 0)
    def _():
        kcat[:, 0:HP] = kn_ref[...]
        kcat[:, HP:2 * HP] = kr_ref[...]

    q = jnp.concatenate([qn_ref[...], qr_ref[...]], axis=1)

    def block(kb, carry, diagonal):
        m, l, acc = carry
        ks = pl.ds(pl.multiple_of(kb * FQ, FQ), FQ)
        s = lax.dot_general(q, kcat[ks, :], (NT, ((), ())), preferred_element_type=F32)
        if diagonal:
            row = lax.broadcasted_iota(jnp.int32, (FQ, FQ), 0)
            col = lax.broadcasted_iota(jnp.int32, (FQ, FQ), 1)
            s = jnp.where(col <= row, s, -jnp.inf)
        m_new = jnp.maximum(m, jnp.max(s, axis=-1, keepdims=True))
        alpha = jnp.exp(m - m_new)
        p = jnp.exp(s - m_new)
        l = alpha * l + jnp.sum(p, axis=-1, keepdims=True)
        acc = alpha * acc + jnp.dot(p.astype(BF16), v_ref[ks, :], preferred_element_type=F32)
        return m_new, l, acc

    m0 = jnp.full((FQ, 1), -jnp.inf, F32)
    l0 = jnp.zeros((FQ, 1), F32)
    a0 = jnp.zeros((FQ, HP), F32)
    carry = lax.fori_loop(0, qi, lambda kb, c: block(kb, c, False), (m0, l0, a0))
    m, l, acc = block(qi, carry, True)
    o_ref[...] = (acc / l).astype(BF16)


def _mla_flash(qn, qr, kn, krb, v, nb, seq):
    nq = seq // FQ
    qspec = pl.BlockSpec((FQ, HP), lambda b, h, q: (b * nq + q, h))
    kspec = pl.BlockSpec((seq, HP), lambda b, h, q: (b, h))
    return pl.pallas_call(
        _mla_flash_kernel,
        grid=(nb, C_HEADS, nq),
        in_specs=[qspec, qspec, kspec, pl.BlockSpec((seq, HP), lambda b, h, q: (b, 0)), kspec],
        out_specs=qspec,
        out_shape=jax.ShapeDtypeStruct((nb * seq, C_HEADS * HP), BF16),
        scratch_shapes=[pltpu.VMEM((seq, 2 * HP), BF16)],
        compiler_params=_cparams(("parallel", "parallel", "arbitrary")),
        name="mla_flash",
    )(qn, qr, kn, krb, v)


PPS = 32
PPG = 4


def _mla_s_kernel(*refs, n_steps, n_batch, ci):
    (pt_ref, lat_hbm, kr_hbm, wukt_ref, qabs_ref, qr_ref, latn_ref, krn_ref, o_ref,
     wst, latb, lg, m_ref, l_ref, acc_ref, latbuf, krbuf, lsem, ksem) = refs
    b = pl.program_id(0)
    kt = pl.program_id(1)
    nq = qabs_ref.shape[1]
    nheads = C_HEADS
    ntok = nq // nheads

    step = b * n_steps + kt
    slot = step % 2

    def issue_pages(first_page, s):
        for k in range(PPS):
            page = pt_ref[first_page + k]
            pltpu.make_async_copy(lat_hbm.at[ci, page], latbuf.at[s, k], lsem.at[s]).start()
            pltpu.make_async_copy(kr_hbm.at[ci, page], krbuf.at[s, k], ksem.at[s]).start()

    @pl.when(step == 0)
    def _():
        issue_pages(0, 0)

    pltpu.make_async_copy(lat_hbm.at[ci, pl.ds(0, PPS)], latbuf.at[slot], lsem.at[slot]).wait()
    pltpu.make_async_copy(kr_hbm.at[ci, pl.ds(0, PPS)], krbuf.at[slot], ksem.at[slot]).wait()

    @pl.when(step + 1 < n_batch * n_steps)
    def _():
        issue_pages((step + 1) * PPS, 1 - slot)

    @pl.when(jnp.logical_and(b == 0, kt == 0))
    def _():
        wst[0:nheads * HP, :] = wukt_ref[...]

    @pl.when(kt == 0)
    def _():
        wst[nheads * HP:nheads * HP + nq, :] = qabs_ref[0]
        m_ref[...] = jnp.full_like(m_ref, -jnp.inf)
        l_ref[...] = jnp.zeros_like(l_ref)
        acc_ref[...] = jnp.zeros_like(acc_ref)

    qr = qr_ref[0]

    def up_project(lat_bf16):
        return lax.dot_general(wst[...], lat_bf16, (NT, ((), ())), preferred_element_type=F32)

    def key_logits(r, krt_f32):
        nkeys = r.shape[1]
        kraw = r[0:nheads * HP].reshape(nheads, HP, nkeys)
        rs = lax.rsqrt(jnp.sum(kraw * kraw, axis=1) / HP + EPS)
        lraw = r[nheads * HP:nheads * HP + nq].reshape(ntok, nheads, nkeys) * rs[None]
        return lraw.reshape(nq, nkeys) + _dot1(qr, krt_f32)

    def update(state, logits, lat_bf16):
        m_old, l_old, acc_old = state
        m_new = jnp.maximum(m_old, jnp.max(logits, axis=-1, keepdims=True))
        alpha = jnp.exp(m_old - m_new)
        p = jnp.exp(logits - m_new)
        l_new = alpha * l_old + jnp.sum(p, axis=-1, keepdims=True)
        acc_new = alpha * acc_old + jnp.dot(p.astype(BF16), lat_bf16, preferred_element_type=F32)
        return m_new, l_new, acc_new

    def load_group(g):
        pages = range(PPG * g, PPG * (g + 1))
        return jnp.concatenate([latbuf[slot, p] for p in pages], axis=0).astype(BF16)

    gk = PPG * PAGE
    for g in range(PPS // PPG):
        lat_g = load_group(g)
        krt_g = jnp.concatenate([krbuf[slot, p] for p in range(PPG * g, PPG * (g + 1))], axis=1)
        latb[gk * g:gk * (g + 1), :] = lat_g
        lg[:, gk * g:gk * (g + 1)] = key_logits(up_project(lat_g), krt_g)
    state = update((m_ref[...], l_ref[...], acc_ref[...]), lg[...], latb[...])
    m_ref[...], l_ref[...], acc_ref[...] = state

    @pl.when(kt == n_steps - 1)
    def _():
        latn = jnp.concatenate([latn_ref[0], jnp.zeros((PAGE - 8, C_LORA), F32)], axis=0).astype(BF16)
        logits = key_logits(up_project(latn), krn_ref[0])
        qtok = lax.broadcasted_iota(jnp.int32, (nq, PAGE), 0) // nheads
        slot = lax.broadcasted_iota(jnp.int32, (nq, PAGE), 1)
        _, l_fin, acc_fin = update(state, jnp.where(slot <= qtok, logits, -jnp.inf), latn)
        o_ref[0] = acc_fin / l_fin


def _mla_sample(page_table, lat_cache, kr_cache, ci, wukt, qabs, qr, lat_new8, kr_new8):
    nb, npages = page_table.shape
    n_steps = npages // PPS
    nq = qabs.shape[1]

    kr_cache_t = jnp.swapaxes(kr_cache, 2, 3)
    kr_new_t = jnp.pad(jnp.swapaxes(kr_new8, 1, 2), ((0, 0), (0, 0), (0, PAGE - kr_new8.shape[1])))
    grid_spec = pltpu.PrefetchScalarGridSpec(
        num_scalar_prefetch=1,
        grid=(nb, n_steps),
        in_specs=[pl.BlockSpec(memory_space=pl.ANY), pl.BlockSpec(memory_space=pl.ANY),
                  pl.BlockSpec(wukt.shape, lambda b, t, pt: (0, 0)),
                    pl.BlockSpec((1, nq, C_LORA), lambda b, t, pt: (b, 0, 0)),
                    pl.BlockSpec((1, nq, C_ROPE), lambda b, t, pt: (b, 0, 0)),
                    pl.BlockSpec((1, 8, C_LORA), lambda b, t, pt: (b, 0, 0)),
                    pl.BlockSpec((1, C_ROPE, PAGE), lambda b, t, pt: (b, 0, 0))],
        out_specs=pl.BlockSpec((1, nq, C_LORA), lambda b, t, pt: (b, 0, 0)),
        scratch_shapes=[pltpu.VMEM((C_HEADS * HP + nq, C_LORA), BF16),
                        pltpu.VMEM((PPS * PAGE, C_LORA), BF16),
                        pltpu.VMEM((nq, PPS * PAGE), F32),
                        pltpu.VMEM((nq, 1), F32), pltpu.VMEM((nq, 1), F32), pltpu.VMEM((nq, C_LORA), F32),
                        pltpu.VMEM((2, PPS, PAGE, C_LORA), F32), pltpu.VMEM((2, PPS, C_ROPE, PAGE), F32),
                        pltpu.SemaphoreType.DMA((2,)), pltpu.SemaphoreType.DMA((2,))],
    )
    return pl.pallas_call(
        functools.partial(_mla_s_kernel, n_steps=n_steps, n_batch=nb, ci=ci),
        grid_spec=grid_spec,
        out_shape=jax.ShapeDtypeStruct((nb, nq, C_LORA), F32),
        compiler_params=_cparams(("arbitrary", "arbitrary")),
        name="mla_sample",
    )(page_table.reshape(-1), lat_cache, kr_cache_t, wukt, qabs, qr, lat_new8, kr_new_t)


def _mla_uv_kernel(c_ref, w_ref, o_ref):
    o_ref[...] = _dot1(c_ref[...], w_ref[0])


def _mla_uv(ctx, wuv3):
    rows = ctx.shape[0]
    return pl.pallas_call(
        _mla_uv_kernel,
        grid=(C_HEADS,),
        in_specs=[pl.BlockSpec((rows, C_LORA), lambda h: (0, h)),
                  pl.BlockSpec((1, C_LORA, HP), lambda h: (h, 0, 0))],
        out_specs=pl.BlockSpec((rows, HP), lambda h: (0, h)),
        out_shape=jax.ShapeDtypeStruct((rows, C_HEADS * HP), F32),
        compiler_params=_cparams(("parallel",)),
        name="mla_uv",
    )(ctx, wuv3)


def kernel(x_prompt, x_sample, state_hgrn, cache_swa_kv, cache_mla_latent, cache_mla_krope, page_table, c_prompt, c_sample, ada_w, ada_b, norm_g, rel_bias, ab_w_in, ab_w_out, hgrn_lb_logits, hgrn_norm_g, swa_q_g, swa_k_g, mla_w_in, mla_q_a_g, mla_kv_a_g, mla_w_uq, mla_w_uk, mla_w_uv, mla_qn_g, mla_qr_g, mla_kn_g, mla_kr_g, mla_w_o, moe_router_w, moe_router_b, moe_w_gu, moe_b_gu, moe_w_dn, moe_b_dn):
    nb, seq, _ = x_prompt.shape
    nsb, nst, _ = x_sample.shape
    tp, ts = nb * seq, nsb * nst
    assert ts == TM and seq % TM == 0 and nst <= 4
    npt, tpb = tp // TM, seq // TM
    x = jnp.concatenate([x_prompt.reshape(tp, D), x_sample.reshape(ts, D)], axis=0).astype(F32)

    nc = nb + nsb
    c_all = jnp.concatenate([c_prompt, c_sample, jnp.zeros((-nc % 8, D), c_prompt.dtype)], axis=0).astype(F32)
    mods = _adaln(c_all, ada_w, ada_b)

    def mod(s):
        return mods[s, 0:nb].reshape(nb, 1, 3 * D), jnp.repeat(mods[s, nb:nc], nst, axis=0)

    pm, sm = mod(0)
    z = _normmod_mm(x, norm_g[0, 0], pm, sm, ab_w_in[0].astype(BF16), 1792, npt, tpb)
    zs3 = z[tp:].reshape(nsb, nst, z.shape[1])
    oa_p, hg_p = _hgrn_prompt(z, hgrn_lb_logits, hgrn_norm_g[0], nb, seq, 0)
    oa_s, hg_s = _hgrn_sample(zs3, hgrn_lb_logits, hgrn_norm_g[0], state_hgrn, 0)
    ob_p, k_p, v_p = _dilated_prompt(z, swa_q_g[0], swa_k_g[0], _band_tables(rel_bias), nb, seq)
    buf_len = cache_swa_kv.shape[2]
    ob_s, k_s, v_s = _dilated_sample(zs3, swa_q_g[0], swa_k_g[0], cache_swa_kv[0], _sample_tables(rel_bias, nst, buf_len))
    w_out = ab_w_out[0].astype(BF16)
    h = _proj_residual([oa_p, ob_p], [oa_s.reshape(ts, 1024), ob_s.reshape(ts, 1024)],
                       [w_out[:1024], w_out[1024:]], x, pm, sm, npt, tpb)
    pm, sm = mod(1)
    h = _moe_layer(h, norm_g[0, 1], pm, sm, 0, moe_router_w, moe_router_b, moe_w_gu, moe_b_gu, moe_w_dn, moe_b_dn, npt, tpb)

    pm, sm = mod(2)
    w_in1 = jnp.pad(mla_w_in[0], ((0, 0), (0, HP - C_ROPE))).astype(BF16)
    z1 = _normmod_mm(h, norm_g[1, 0], pm, sm, w_in1, w_in1.shape[1], npt, tpb)
    past = page_table.shape[1] * PAGE
    pos = jnp.concatenate([jnp.tile(jnp.arange(seq), nb), jnp.tile(past + jnp.arange(nst), nsb)])
    cos, sin = _rope_tables(pos)
    wq = mla_w_uq[0]
    wn = wq[:, :, :C_NOPE].reshape(C_LORA, C_HEADS * HP).astype(BF16)
    wr = jnp.pad(wq[:, :, C_NOPE:], ((0, 0), (0, 0), (0, HP - C_ROPE))).reshape(C_LORA, C_HEADS * HP).astype(BF16)
    wuk = mla_w_uk[0].reshape(C_LORA, C_HEADS * HP).astype(BF16)
    wuv = mla_w_uv[0].reshape(C_LORA, C_HEADS * HP).astype(BF16)
    lat, kr, krb, qn, qr, qabs = _mla_prep(z1, mla_q_a_g[0], mla_kv_a_g[0], mla_kr_g[0], mla_qn_g[0], mla_qr_g[0],
                                           mla_kn_g[0], wn, wr, wuk, cos, sin, npt)
    kn, v = _mla_kv(lat, wuk, wuv, mla_kn_g[0], tp)
    ctx_p = _mla_flash(qn, qr, kn, krb, v, nb, seq)
    qabs3 = qabs.reshape(nsb, nst * C_HEADS, C_LORA)
    qr_s = qr[tp:].reshape(ts, C_HEADS, HP)[:, :, :C_ROPE].reshape(nsb, nst * C_HEADS, C_ROPE)
    lat_new8 = jnp.pad(lat[tp:].reshape(nsb, nst, C_LORA), ((0, 0), (0, 8 - nst), (0, 0)))
    kr_new8 = jnp.pad(kr[tp:].reshape(nsb, nst, C_ROPE), ((0, 0), (0, 8 - nst), (0, 0)))
    ctx_s = _mla_sample(page_table, cache_mla_latent, cache_mla_krope, 0, wuk.T, qabs3, qr_s, lat_new8, kr_new8)
    wuv3 = jnp.transpose(mla_w_uv[0], (1, 0, 2)).astype(BF16)
    o_s = _mla_uv(ctx_s.reshape(ts, C_HEADS * C_LORA), wuv3)
    h = _proj_residual([ctx_p], [o_s], [mla_w_o[0].astype(BF16)], h, pm, sm, npt, tpb)
    pm, sm = mod(3)
    hp, hs = _moe_layer(h, norm_g[1, 1], pm, sm, 1, moe_router_w, moe_router_b, moe_w_gu, moe_b_gu, moe_w_dn, moe_b_dn,
                        npt, tpb, split=True)

    y_prompt = hp.reshape(nb, seq, D).astype(x_prompt.dtype)
    y_sample = hs.reshape(nsb, nst, D).astype(x_sample.dtype)
    hgrn_prompt = hg_p[None].astype(state_hgrn.dtype)
    hgrn_sample = hg_s.astype(state_hgrn.dtype)
    kv = lambda k, v, b, t: jnp.stack([k.reshape(b, t, 8, 128), v.reshape(b, t, 8, 128)], axis=2)[None]
    swa_prompt = kv(k_p, v_p, nb, seq).astype(cache_swa_kv.dtype)
    swa_sample = kv(k_s, v_s, nsb, nst).astype(cache_swa_kv.dtype)
    lat_prompt = lat[:tp].reshape(1, nb, seq, C_LORA).astype(cache_mla_latent.dtype)
    lat_sample = lat[tp:].reshape(1, nsb, nst, C_LORA).astype(cache_mla_latent.dtype)
    krope_prompt = kr[:tp].reshape(1, nb, seq, C_ROPE).astype(cache_mla_krope.dtype)
    krope_sample = kr[tp:].reshape(1, nsb, nst, C_ROPE).astype(cache_mla_krope.dtype)
    return (y_prompt, y_sample, hgrn_prompt, hgrn_sample, swa_prompt, swa_sample,
            lat_prompt, lat_sample, krope_prompt, krope_sample)
```

```python
import functools
import math

import numpy as np
import jax
import jax.numpy as jnp
from jax import lax
from jax.experimental import pallas as pl
from jax.experimental.pallas import tpu as pltpu

F32 = jnp.float32
BF16 = jnp.bfloat16
EPS = 1e-6

D = 2048
TM = 512
N_EXPERTS = 32
TOP_K = 4
D_EXPERT = 2048
SWIGLU_LIMIT = 7.0
SWIGLU_ALPHA = 1.702
PAGE = 128
V7X_VMEM_LIMIT = 56 * 1024 * 1024

NN = ((1,), (0,))
NT = ((1,), (1,))
TN = ((0,), (0,))


def _cparams(sem, vmem=V7X_VMEM_LIMIT):
    return pltpu.CompilerParams(dimension_semantics=sem, vmem_limit_bytes=vmem)


def _dot1(a, b, dims=NN):
    return lax.dot_general(a.astype(BF16), b.astype(BF16), (dims, ((), ())), preferred_element_type=F32)


def _split2(x):
    hi = x.astype(BF16)
    lo = (x - hi.astype(F32)).astype(BF16)
    return hi, lo


def _split3(x):
    hi = x.astype(BF16)
    r = x - hi.astype(F32)
    mid = r.astype(BF16)
    lo = (r - mid.astype(F32)).astype(BF16)
    return hi, mid, lo


def _dot_exact_rhs(a01, x):
    hi, mid, lo = _split3(x)
    f = lambda p: lax.dot_general(a01, p, (NN, ((), ())), preferred_element_type=F32)
    return f(hi) + f(mid) + f(lo)


def _dot3(a, b, dims=NN):
    ah, al = _split2(a)
    bh, bl = _split2(b)
    f = lambda p, q: lax.dot_general(p, q, (dims, ((), ())), preferred_element_type=F32)
    return f(ah, bh) + f(ah, bl) + f(al, bh)


def _sigmoid(x):
    return 1.0 / (1.0 + jnp.exp(-x))


def _rms(x, g):
    return x * lax.rsqrt(jnp.mean(x * x, axis=-1, keepdims=True) + EPS) * g


def _ada_kernel(c_ref, w_ref, b_ref, o_ref):
    c = c_ref[...]
    s = c * _sigmoid(c)
    o_ref[0] = _dot1(s, w_ref[0, 0]) + b_ref[0, 0]


def _adaln(c_all, ada_w, ada_b):
    r = c_all.shape[0]
    tn = 1024
    depth = ada_w.shape[0]
    b4 = ada_b.reshape(depth, 2, 1, 3 * D)
    return pl.pallas_call(
        _ada_kernel,
        grid=(depth * 2, 3 * D // tn),
        in_specs=[pl.BlockSpec((r, D), lambda s, j: (0, 0)),
                  pl.BlockSpec((1, 1, D, tn), lambda s, j: (s // 2, s % 2, 0, j)),
                  pl.BlockSpec((1, 1, 1, tn), lambda s, j: (s // 2, s % 2, 0, j))],
        out_specs=pl.BlockSpec((1, r, tn), lambda s, j: (s, 0, j)),
        out_shape=jax.ShapeDtypeStruct((depth * 2, r, 3 * D), F32),
        compiler_params=_cparams(("parallel", "parallel")),
        name="adaln",
    )(c_all, ada_w, b4)


def _pick(is_prompt, p_ref, s_ref):
    return jnp.where(is_prompt, p_ref[0], s_ref[...])


def _mod_specs(npt, rows_per_batch_tiles, col, tn=D, with_j=False):
    if with_j:
        pm = pl.BlockSpec((1, 1, tn), lambda i, j: (jnp.minimum(i // rows_per_batch_tiles, npt // rows_per_batch_tiles - 1), 0, col * (D // tn) + j))
        sm = pl.BlockSpec((TM, tn), lambda i, j: (0, col * (D // tn) + j))
    else:
        pm = pl.BlockSpec((1, 1, tn), lambda i, *_: (jnp.minimum(i // rows_per_batch_tiles, npt // rows_per_batch_tiles - 1), 0, col))
        sm = pl.BlockSpec((TM, tn), lambda i, *_: (0, col))
    return pm, sm


def _nm_mm_kernel(x_ref, g_ref, psh, psc, ssh, ssc, w_ref, o_ref, xn_ref, *, npt):
    i = pl.program_id(0)
    j = pl.program_id(1)

    @pl.when(j == 0)
    def _():
        xr = _rms(x_ref[...], g_ref[...])
        sc = _pick(i < npt, psc, ssc)
        sh = _pick(i < npt, psh, ssh)
        xn_ref[...] = (xr * (1.0 + sc) + sh).astype(BF16)

    o_ref[...] = jnp.dot(xn_ref[...], w_ref[...], preferred_element_type=F32)


def _normmod_mm(x, g, pmod, smod, w_bf16, tn, npt, tiles_per_batch):
    t = x.shape[0]
    n = w_bf16.shape[1]
    psh, ssh = _mod_specs(npt, tiles_per_batch, 0)
    psc, ssc = _mod_specs(npt, tiles_per_batch, 1)
    return pl.pallas_call(
        functools.partial(_nm_mm_kernel, npt=npt),
        grid=(t // TM, n // tn),
        in_specs=[pl.BlockSpec((TM, D), lambda i, j: (i, 0)),
                  pl.BlockSpec((1, D), lambda i, j: (0, 0)),
                  psh, psc, ssh, ssc,
                  pl.BlockSpec((D, tn), lambda i, j: (0, j))],
        out_specs=pl.BlockSpec((TM, tn), lambda i, j: (i, j)),
        out_shape=jax.ShapeDtypeStruct((t, n), F32),
        scratch_shapes=[pltpu.VMEM((TM, D), BF16)],
        compiler_params=_cparams(("parallel", "arbitrary")),
        name="normmod_mm",
    )(x, g.reshape(1, D), pmod, pmod, smod, smod, w_bf16)


def _proj_res_kernel(*refs, npt, nk):
    xp = refs[0:nk]
    xs = refs[nk:2 * nk]
    ws = refs[2 * nk:3 * nk]
    h_ref, pg, sg, o_ref = refs[3 * nk:3 * nk + 4]
    xb = refs[3 * nk + 4:]
    i = pl.program_id(0)
    j = pl.program_id(1)

    @pl.when(jnp.logical_and(j == 0, i < npt))
    def _():
        for k in range(nk):
            xb[k][...] = xp[k][...].astype(BF16)

    @pl.when(jnp.logical_and(j == 0, i >= npt))
    def _():
        for k in range(nk):
            xb[k][...] = xs[k][...].astype(BF16)

    acc = jnp.dot(xb[0][...], ws[0][...], preferred_element_type=F32)
    for k in range(1, nk):
        acc = acc + jnp.dot(xb[k][...], ws[k][...], preferred_element_type=F32)
    gate = _pick(i < npt, pg, sg)
    o_ref[...] = h_ref[...] + gate * acc


def _proj_residual(xps, xss, ws, h, pmod, smod, npt, tiles_per_batch, tn=1024):
    nk = len(ws)
    t = h.shape[0]
    ks = [w.shape[0] for w in ws]
    pg, sg = _mod_specs(npt, tiles_per_batch, 2, tn=tn, with_j=True)
    in_specs = ([pl.BlockSpec((TM, k), lambda i, j: (jnp.minimum(i, npt - 1), 0)) for k in ks]
                + [pl.BlockSpec((TM, k), lambda i, j: (0, 0)) for k in ks]
                + [pl.BlockSpec((k, tn), lambda i, j: (0, j)) for k in ks]
                + [pl.BlockSpec((TM, tn), lambda i, j: (i, j)), pg, sg])
    return pl.pallas_call(
        functools.partial(_proj_res_kernel, npt=npt, nk=nk),
        grid=(t // TM, D // tn),
        in_specs=in_specs,
        out_specs=pl.BlockSpec((TM, tn), lambda i, j: (i, j)),
        out_shape=jax.ShapeDtypeStruct((t, D), F32),
        scratch_shapes=[pltpu.VMEM((TM, k), BF16) for k in ks],
        compiler_params=_cparams(("parallel", "arbitrary")),
        name="proj_residual",
    )(*xps, *xss, *ws, h, pmod, smod)


HG_C = 64
HG_B = 16
HG_STEP = 256
HG_HEADS = 4


def _lower_bound(lbl, layer):
    e = jnp.exp(lbl - jnp.max(lbl, axis=0, keepdims=True))
    return jnp.sum(e[0:layer + 1], axis=0, keepdims=True) / jnp.sum(e, axis=0, keepdims=True)


def _hgrn_chunk(q, af, v, st, lb, tri):
    c, dk = q.shape
    nb = c // HG_B
    f = lb + (1.0 - lb) * _sigmoid(af)
    g = jnp.log(f)
    k = 1.0 - f
    cum = _dot_exact_rhs(tri, g)
    o = _dot1(q * jnp.exp(cum), st, NT)
    q3 = q.reshape(nb, HG_B, dk)
    k3 = k.reshape(nb, HG_B, dk)
    c3 = cum.reshape(nb, HG_B, dk)
    tpos = lax.broadcasted_iota(jnp.int32, (nb, HG_B, dk), 1)
    lane3 = lax.broadcasted_iota(jnp.int32, (nb, HG_B, dk), 2)
    dmat = jnp.zeros((nb, HG_B, dk), F32)
    for s in range(HG_B):
        rel = c3 - c3[:, s:s + 1, :]
        a = q3 * k3[:, s:s + 1, :] * jnp.exp(jnp.where(tpos >= s, rel, -jnp.inf))
        col = jnp.sum(a, axis=-1, keepdims=True)
        dmat = jnp.where(lane3 == s, col, dmat)
    lane_c = lax.broadcasted_iota(jnp.int32, (HG_B, c), 1)
    rows = []
    for blk in range(nb):
        r0 = HG_B * blk
        d_blk = dmat[blk]
        if blk == 0:
            rows.append(d_blk[:, :c])
            continue
        d_blk = pltpu.roll(d_blk, r0, axis=1)[:, :c]
        cb = cum[r0 - 1:r0, :]
        qs = q[r0:r0 + HG_B] * jnp.exp(cum[r0:r0 + HG_B] - cb)
        ks = k * jnp.exp(jnp.minimum(cb - cum, 0.0))
        s_off = _dot1(qs, ks, NT)
        rows.append(jnp.where(lane_c < r0, s_off, 0.0) + d_blk)
    scores = jnp.concatenate(rows, axis=0)
    o = o + _dot1(scores, v)
    last = cum[c - 1:c, :]
    st_new = jnp.exp(last) * st + _dot1(v, k * jnp.exp(last - cum), TN)
    return o, st_new


def _hgrn_p_kernel(q_ref, f_ref, v_ref, g_ref, lbl_ref, gn_ref, o_ref, s_ref, st_ref, *, layer, nsteps):
    tb = pl.program_id(2)

    @pl.when(tb == 0)
    def _():
        st_ref[...] = jnp.zeros_like(st_ref)

    lb = _lower_bound(lbl_ref[...], layer)
    ri = lax.broadcasted_iota(jnp.int32, (HG_C, HG_C), 0)
    ci = lax.broadcasted_iota(jnp.int32, (HG_C, HG_C), 1)
    tri = jnp.where(ri >= ci, 1.0, 0.0).astype(BF16)
    sts = [st_ref[hh] for hh in range(HG_HEADS)]
    for c in range(HG_STEP // HG_C):
        sl = slice(HG_C * c, HG_C * (c + 1))
        for hh in range(HG_HEADS):
            hl = slice(128 * hh, 128 * (hh + 1))
            o, sts[hh] = _hgrn_chunk(q_ref[sl, hl], f_ref[sl, hl], v_ref[sl, hl], sts[hh], lb[:, hl], tri)
            o_ref[sl, hl] = (_rms(o, gn_ref[...]) * _sigmoid(g_ref[sl, hl])).astype(BF16)
    for hh in range(HG_HEADS):
        st_ref[hh] = sts[hh]

    @pl.when(tb == nsteps - 1)
    def _():
        for hh in range(HG_HEADS):
            s_ref[0, hh] = sts[hh].T


def _hgrn_prompt(z, lb_logits, gn, nb, seq, layer):
    nh = 8
    nsteps = seq // HG_STEP
    hw = 128 * HG_HEADS
    ng = nh // HG_HEADS
    zspec = lambda c0: pl.BlockSpec((HG_STEP, hw), lambda b, h, t: (b * nsteps + t, c0 * ng + h))
    return pl.pallas_call(
        functools.partial(_hgrn_p_kernel, layer=layer, nsteps=nsteps),
        grid=(nb, ng, nsteps),
        in_specs=[zspec(0), zspec(1), zspec(2), zspec(3),
                  pl.BlockSpec((lb_logits.shape[0], hw), lambda b, h, t: (0, h)),
                  pl.BlockSpec((1, 128), lambda b, h, t: (0, 0))],
        out_specs=[pl.BlockSpec((HG_STEP, hw), lambda b, h, t: (b * nsteps + t, h)),
                   pl.BlockSpec((1, HG_HEADS, 128, 128), lambda b, h, t: (b, h, 0, 0))],
        out_shape=[jax.ShapeDtypeStruct((nb * seq, nh * 128), BF16),
                   jax.ShapeDtypeStruct((nb, nh, 128, 128), F32)],
        scratch_shapes=[pltpu.VMEM((HG_HEADS, 128, 128), F32)],
        compiler_params=_cparams(("parallel", "parallel", "arbitrary")),
        name="hgrn_prompt",
    )(z, z, z, z, lb_logits, gn.reshape(1, 128))


def _hgrn_s_kernel(q_ref, f_ref, v_ref, g_ref, lbl_ref, gn_ref, s0_ref, o_ref, s_ref, *, layer):
    nt = q_ref.shape[1]
    lb_all = _lower_bound(lbl_ref[...], layer)
    tpos = lax.broadcasted_iota(jnp.int32, (nt, 128), 0)
    outs = []
    for h in range(8):
        sl = slice(128 * h, 128 * (h + 1))
        q = q_ref[0, :, sl]
        v = v_ref[0, :, sl]
        lb = lb_all[:, sl]
        f = lb + (1.0 - lb) * _sigmoid(f_ref[0, :, sl])
        g = jnp.log(f)
        k = 1.0 - f
        crow = [g[0:1]]
        for t in range(1, nt):
            crow.append(crow[-1] + g[t:t + 1])
        cum = jnp.concatenate(crow, axis=0)
        last = crow[-1]
        s0 = s0_ref[0, 0, h]
        qe = jnp.concatenate([q * jnp.exp(cum), jnp.zeros((8 - nt, 128), F32)], axis=0)
        o = _dot1(qe, s0)[0:nt]
        for s in range(nt):
            a = q * k[s:s + 1] * jnp.exp(jnp.where(tpos >= s, cum - cum[s:s + 1], -jnp.inf))
            o = o + jnp.sum(a, axis=-1, keepdims=True) * v[s:s + 1]
        kp = k * jnp.exp(last - cum)
        w8 = jnp.concatenate([jnp.exp(last), kp, jnp.zeros((8 - 1 - nt, 128), F32)], axis=0)
        wt = jnp.concatenate([w8, jnp.zeros((120, 128), F32)], axis=0).T
        s_new = wt[:, 0:1] * s0
        for s in range(nt):
            s_new = s_new + wt[:, 1 + s:2 + s] * v[s:s + 1]
        s_ref[0, 0, h] = s_new
        outs.append(_rms(o, gn_ref[...]) * _sigmoid(g_ref[0, :, sl]))
    o_ref[0] = jnp.concatenate(outs, axis=1)


def _hgrn_sample(zs3, lb_logits, gn, state, layer):
    nb, nt, _ = zs3.shape
    zspec = lambda c: pl.BlockSpec((1, nt, 1024), lambda b: (b, 0, c))
    sspec = pl.BlockSpec((1, 1, 8, 128, 128), lambda b: (0, b, 0, 0, 0))
    return pl.pallas_call(
        functools.partial(_hgrn_s_kernel, layer=layer),
        grid=(nb,),
        in_specs=[zspec(0), zspec(1), zspec(2), zspec(3),
                  pl.BlockSpec(lb_logits.shape, lambda b: (0, 0)),
                  pl.BlockSpec((1, 128), lambda b: (0, 0)),
                  sspec],
        out_specs=[pl.BlockSpec((1, nt, 1024), lambda b: (b, 0, 0)), sspec],
        out_shape=[jax.ShapeDtypeStruct((nb, nt, 1024), F32),
                   jax.ShapeDtypeStruct(state.shape, F32)],
        compiler_params=_cparams(("parallel",)),
        name="hgrn_sample",
    )(zs3, zs3, zs3, zs3, lb_logits, gn.reshape(1, 128), state)


DILATED = ((128, 1), (512, 4), (2048, 16))
REL_BUCKETS = 32
REL_MAX_DISTANCE = 2048
ATT_SCALE = 128 ** -0.5
QB = 128


def _rel_bucket_np(dist):
    exact = REL_BUCKETS // 2
    d = np.asarray(dist)
    far = exact + (np.log(np.maximum(d, 1).astype(np.float32) / np.float32(exact))
                   / np.float32(math.log(REL_MAX_DISTANCE / exact)) * np.float32(REL_BUCKETS - exact)).astype(np.int32)
    return np.where(d < exact, d, np.minimum(far, REL_BUCKETS - 1)).astype(np.int32)


def _band_tables(rel_bias):
    rb = rel_bias.astype(F32)
    tabs = []
    for (window, dil), width in zip(DILATED, (2 * QB, 2 * QB, QB)):
        period = width + QB
        i = np.arange(period + 1)
        k = np.where(i < width, i, i - (period + 1))
        steps = (width - QB) - k
        valid = (steps >= 0) & (steps <= window // dil)
        bucket = _rel_bucket_np(np.clip(steps, 0, window // dil) * dil)
        u = jnp.where(jnp.asarray(valid)[None, :], rb[jnp.asarray(bucket)].T, -jnp.inf)
        flat = jnp.tile(u, (1, QB))[:, :QB * period]
        tabs.append(flat.reshape(rb.shape[1], QB, period)[:, :, :width])
    return tabs


def _soft_block(qb, kw, vw, bias):
    l = _dot1(qb, kw, NT) * ATT_SCALE + bias
    m = jnp.max(l, axis=-1, keepdims=True)
    p = jnp.exp(l - m)
    return m, jnp.sum(p, axis=-1, keepdims=True), _dot1(p, vw)


def _dil_p_kernel(q_ref, k_ref, v_ref, gq_ref, gk_ref, b1_ref, b2_ref, b3_ref, o_ref, ko_ref, vo_ref,
                  qs, ks, vs, a1, a2, a3, m1, m2, m3, l1, l2, l3):
    seq = q_ref.shape[0]
    kn = _rms(k_ref[...], gk_ref[...])
    v = v_ref[...]
    ko_ref[...] = kn
    vo_ref[...] = v
    qs[...] = _rms(q_ref[...], gq_ref[...])
    ks[...] = kn
    vs[...] = v
    zpad = jnp.zeros((QB, 128), F32)
    first_cols = lax.broadcasted_iota(jnp.int32, (QB, 2 * QB), 1) >= QB

    def banded(qsub, ksub, vsub, bias, n_blocks, store):
        for i in range(n_blocks):
            qb = qsub(i)
            if i == 0:
                kw = jnp.concatenate([zpad, ksub(0)], axis=0)
                vw = jnp.concatenate([zpad, vsub(0)], axis=0)
                bb = jnp.where(first_cols, bias, -jnp.inf)
            else:
                kw = jnp.concatenate([ksub(i - 1), ksub(i)], axis=0)
                vw = jnp.concatenate([vsub(i - 1), vsub(i)], axis=0)
                bb = bias
            store(i, *_soft_block(qb, kw, vw, bb))

    def st1(i, m, l, a):
        sl = slice(QB * i, QB * (i + 1))
        m1[sl, :] = m
        l1[sl, :] = l
        a1[sl, :] = a
    blk = lambda ref: (lambda i: ref[QB * i:QB * (i + 1), :])
    banded(blk(qs), blk(ks), blk(vs), b1_ref[0], seq // QB, st1)

    d2 = DILATED[1][1]
    for r in range(d2):
        sub = lambda ref: (lambda i: ref[pl.ds(r + d2 * QB * i, QB, stride=d2), :])

        def st2(i, m, l, a):
            idx = pl.ds(r + d2 * QB * i, QB, stride=d2)
            m2[idx, :] = m
            l2[idx, :] = l
            a2[idx, :] = a
        banded(sub(qs), sub(ks), sub(vs), b2_ref[0], seq // (QB * d2), st2)

    d3 = DILATED[2][1]
    for r in range(d3):
        idx = pl.ds(r, QB, stride=d3)
        m, l, a = _soft_block(qs[idx, :], ks[idx, :], vs[idx, :], b3_ref[0])
        m3[idx, :] = m
        l3[idx, :] = l
        a3[idx, :] = a

    mx = jnp.maximum(jnp.maximum(m1[...], m2[...]), m3[...])
    w1 = jnp.exp(m1[...] - mx)
    w2 = jnp.exp(m2[...] - mx)
    w3 = jnp.exp(m3[...] - mx)
    num = w1 * a1[...] + w2 * a2[...] + w3 * a3[...]
    den = w1 * l1[...] + w2 * l2[...] + w3 * l3[...]
    o_ref[...] = (num / den).astype(BF16)


def _dilated_prompt(z, gq, gk, tabs, nb, seq):
    nh = 8
    zspec = lambda c0: pl.BlockSpec((seq, 128), lambda b, h: (b, c0 + h))
    gspec = pl.BlockSpec((1, 128), lambda b, h: (0, 0))
    tspec = lambda w: pl.BlockSpec((1, QB, w), lambda b, h: (h, 0, 0))
    ospec = pl.BlockSpec((seq, 128), lambda b, h: (b, h))
    big = pltpu.VMEM((seq, 128), F32)
    col = pltpu.VMEM((seq, 1), F32)
    return pl.pallas_call(
        _dil_p_kernel,
        grid=(nb, nh),
        in_specs=[zspec(32), zspec(40), zspec(48), gspec, gspec, tspec(2 * QB), tspec(2 * QB), tspec(QB)],
        out_specs=[ospec, ospec, ospec],
        out_shape=[jax.ShapeDtypeStruct((nb * seq, nh * 128), BF16),
                   jax.ShapeDtypeStruct((nb * seq, nh * 128), F32),
                   jax.ShapeDtypeStruct((nb * seq, nh * 128), F32)],
        scratch_shapes=[big, big, big, big, big, big, col, col, col, col, col, col],
        compiler_params=_cparams(("parallel", "parallel")),
        name="dilated_prompt",
    )(z, z, z, gq.reshape(1, 128), gk.reshape(1, 128), *tabs)


def _sample_tables(rel_bias, nt, buf_len):
    rb = rel_bias.astype(F32)
    nh = rb.shape[1]
    w2 = DILATED[1][0]

    def count(delta, patterns):
        c = np.zeros(delta.shape, np.int32)
        for window, dil in patterns:
            c += ((delta >= 0) & (delta <= window) & (delta % dil == 0)).astype(np.int32)
        return c

    qpos = buf_len + np.arange(8)
    real_q = (np.arange(8) < nt)
    da = qpos[None, :] - (buf_len - w2 + np.arange(w2))[:, None]
    ca = count(da, DILATED[:2]) * real_q[None, :]
    nj = buf_len // 16
    rows_b = (16 * np.arange(nj)[None, :] + np.arange(nt)[:, None]).reshape(-1)
    db = qpos[None, :] - rows_b[:, None]
    cb = count(db, DILATED[2:]) * real_q[None, :]
    dn = qpos[None, :] - qpos[:, None]
    cn = count(dn, DILATED) * real_q[None, :] * real_q[:, None]

    def tables(delta, cnt):
        bucket = _rel_bucket_np(np.clip(delta, 0, REL_MAX_DISTANCE))
        onehot = (bucket.reshape(-1, 1) == np.arange(REL_BUCKETS)[None, :]).astype(np.float32)
        bias = jnp.dot(jnp.asarray(onehot), rb, precision=lax.Precision.HIGHEST).reshape(*delta.shape, nh)
        bias = jnp.where(jnp.asarray((cnt > 0) | ~real_q[None, :])[:, :, None], bias, -jnp.inf)
        bias = jnp.transpose(bias, (0, 2, 1)).reshape(delta.shape[0], nh * 8)
        mult = np.repeat(cnt[:, None, :], nh, axis=1).reshape(delta.shape[0], nh * 8).astype(np.float32)
        return bias, jnp.asarray(mult)

    return tables(da, ca) + tables(db, cb) + tables(dn, cn)


def _dil_s_kernel(q_ref, k_ref, v_ref, gq_ref, gk_ref, ca_ref, cb_ref,
                  ba_ref, ma_ref, bb_ref, mb_ref, bn_ref, mn_ref, o_ref, ko_ref, vo_ref):
    nt = q_ref.shape[1]
    nh = 8
    gq = gq_ref[...]
    gk = gk_ref[...]
    qn = jnp.concatenate([_rms(q_ref[0, :, 128 * h:128 * (h + 1)], gq) for h in range(nh)], axis=1)
    kn = jnp.concatenate([_rms(k_ref[0, :, 128 * h:128 * (h + 1)], gk) for h in range(nh)], axis=1)
    vn = v_ref[0]
    ko_ref[0] = kn
    vo_ref[0] = vn
    pad = jnp.zeros((8 - nt, nh * 128), F32)
    q8 = jnp.concatenate([qn, pad], axis=0)
    k8 = jnp.concatenate([kn, pad], axis=0)
    v8 = jnp.concatenate([vn, pad], axis=0)
    lane_head = lax.broadcasted_iota(jnp.int32, (8, nh * 128), 1) // 128
    qbd = jnp.concatenate([jnp.where(lane_head == h, q8, 0.0) for h in range(nh)], axis=0).astype(BF16)

    def heads(ref2d, first, nrows, stride):
        return jnp.concatenate([ref2d[pl.ds(first + h, nrows, stride=stride), :] for h in range(nh)], axis=1)

    a2 = ca_ref.at[0]
    na = a2.shape[0] // (2 * nh)
    groups = [(heads(a2, 0, na, 2 * nh), heads(a2, nh, na, 2 * nh), ba_ref[...], ma_ref[...])]
    nj = cb_ref.shape[1]
    per_j = cb_ref.shape[2]
    b2 = cb_ref.reshape(nj * per_j, 128)
    for i in range(nt):
        groups.append((heads(b2, 2 * nh * i, nj, per_j), heads(b2, 2 * nh * i + nh, nj, per_j),
                       bb_ref[nj * i:nj * (i + 1), :], mb_ref[nj * i:nj * (i + 1), :]))
    groups.append((k8, v8, bn_ref[...], mn_ref[...]))

    logits = [_dot1(kk, qbd, NT) * ATT_SCALE + bias for kk, _, bias, _ in groups]
    mx = logits[0].max(axis=0, keepdims=True)
    for l in logits[1:]:
        mx = jnp.maximum(mx, l.max(axis=0, keepdims=True))
    ps = [mult * jnp.exp(l - mx) for l, (_, _, _, mult) in zip(logits, groups)]
    den = ps[0].sum(axis=0, keepdims=True)
    for p in ps[1:]:
        den = den + p.sum(axis=0, keepdims=True)
    inv = 1.0 / jnp.where(den > 0.0, den, 1.0)
    acc = None
    for p, (_, vv, _, _) in zip(ps, groups):
        part = _dot1(p * inv, vv, TN)
        acc = part if acc is None else acc + part
    o8 = jnp.concatenate([acc[8 * h:8 * (h + 1), 128 * h:128 * (h + 1)] for h in range(nh)], axis=1)
    o_ref[0] = o8[0:nt]


def _dilated_sample(zs3, gq, gk, cache, tabs):
    nb, nt, _ = zs3.shape
    buf_len = cache.shape[1]
    w2 = DILATED[1][0]
    d3 = DILATED[2][1]
    kvh = 2 * 8
    cache_a = cache.reshape(nb, buf_len * kvh, 128)
    cache_b = cache.reshape(nb, buf_len // d3, d3 * kvh, 128)
    zspec = lambda c: pl.BlockSpec((1, nt, 1024), lambda b: (b, 0, c))
    gspec = pl.BlockSpec((1, 128), lambda b: (0, 0))
    full = lambda a: pl.BlockSpec(a.shape, lambda b: (0, 0))
    ospec = pl.BlockSpec((1, nt, 1024), lambda b: (b, 0, 0))
    return pl.pallas_call(
        _dil_s_kernel,
        grid=(nb,),
        in_specs=[zspec(4), zspec(5), zspec(6), gspec, gspec,
                  pl.BlockSpec((1, w2 * kvh, 128), lambda b: (b, buf_len // w2 - 1, 0)),
                  pl.BlockSpec((1, buf_len // d3, nt * kvh, 128), lambda b: (b, 0, 0, 0))]
                 + [full(a) for a in tabs],
        out_specs=[ospec, ospec, ospec],
        out_shape=[jax.ShapeDtypeStruct((nb, nt, 1024), F32)] * 3,
        compiler_params=_cparams(("parallel",)),
        name="dilated_sample",
    )(zs3, zs3, zs3, gq.reshape(1, 128), gk.reshape(1, 128), cache_a, cache_b, *tabs)


def _route_kernel(x_ref, g_ref, psh, psc, ssh, ssc, rw_ref, rb_ref, u_ref, idx_ref, gate_ref, *, npt):
    i = pl.program_id(0)
    xr = _rms(x_ref[...], g_ref[...])
    u = xr * (1.0 + _pick(i < npt, psc, ssc)) + _pick(i < npt, psh, ssh)
    ub = u.astype(BF16).astype(F32)
    lo = lax.shift_right_logical(lax.bitcast_convert_type(ub[:, :D // 2], jnp.int32), 16)
    u_ref[...] = lax.bitcast_convert_type(ub[:, D // 2:], jnp.int32) | lo
    l = _dot3(u, rw_ref[...]) + rb_ref[...]
    lane = lax.broadcasted_iota(jnp.int32, l.shape, 1).astype(F32)
    out_lane = lax.broadcasted_iota(jnp.int32, (TM, 128), 1)
    vals, idxs = [], []
    for _ in range(TOP_K):
        m = jnp.max(l, axis=-1, keepdims=True)
        first = jnp.min(jnp.where(l == m, lane, float(N_EXPERTS)), axis=-1, keepdims=True)
        vals.append(m)
        idxs.append(first.astype(jnp.int32))
        l = jnp.where(lane == first, -jnp.inf, l)
    es = [jnp.exp(v - vals[0]) for v in vals]
    tot = es[0] + es[1] + es[2] + es[3]
    idx_out = jnp.zeros((TM, 128), jnp.int32)
    gate_out = jnp.zeros((TM, 128), F32)
    for k in range(TOP_K):
        idx_out = jnp.where(out_lane == k, idxs[k], idx_out)
        gate_out = jnp.where(out_lane == k, es[k] / tot, gate_out)
    idx_ref[...] = idx_out
    gate_ref[...] = gate_out


def _route(x, g, pmod, smod, rw, rb, npt, tiles_per_batch):
    t = x.shape[0]
    psh, ssh = _mod_specs(npt, tiles_per_batch, 0)
    psc, ssc = _mod_specs(npt, tiles_per_batch, 1)
    return pl.pallas_call(
        functools.partial(_route_kernel, npt=npt),
        grid=(t // TM,),
        in_specs=[pl.BlockSpec((TM, D), lambda i: (i, 0)),
                  pl.BlockSpec((1, D), lambda i: (0, 0)),
                  psh, psc, ssh, ssc,
                  pl.BlockSpec((D, N_EXPERTS), lambda i: (0, 0)),
                  pl.BlockSpec((1, N_EXPERTS), lambda i: (0, 0))],
        out_specs=[pl.BlockSpec((TM, D // 2), lambda i: (i, 0)),
                   pl.BlockSpec((TM, 128), lambda i: (i, 0)),
                   pl.BlockSpec((TM, 128), lambda i: (i, 0))],
        out_shape=[jax.ShapeDtypeStruct((t, D // 2), jnp.int32),
                   jax.ShapeDtypeStruct((t, 128), jnp.int32),
                   jax.ShapeDtypeStruct((t, 128), F32)],
        compiler_params=_cparams(("parallel",)),
        name="moe_route",
    )(x, g.reshape(1, D), pmod, pmod, smod, smod, rw, rb.reshape(1, N_EXPERTS))


MOE_TM = 1344
MOE_ROWS = (304, 608, 1056, 1216, MOE_TM)
MOE_SPARE_TILES = 4
MOE_TN = 1024
MOE_SEL = 512
MOE_TF = MOE_TN // 2


MOE_NG = 2 * D_EXPERT // MOE_TN
MOE_DN = 512
MOE_ND = D // MOE_DN


def _moe_kernel(te_ref, nu_ref, nv_ref, tok_ref, u_hbm, wgu_ref, bgu_ref, wdn_ref, bdn_ref, y_ref, xg, xb, hs, sem):
    t = pl.program_id(0)
    j = pl.program_id(1)
    nu = nu_ref[0]
    active = t < nu
    nv = nv_ref[t]

    def rows8(n):
        return (n + 7) // 8 * 8

    def issue(tile):
        def body(g, carry):
            for i in range(8):
                r = g * 8 + i
                tok = tok_ref[tile * MOE_TM + r]
                pltpu.make_async_copy(u_hbm.at[pl.ds(tok, 1)], xg.at[pl.ds(r, 1)], sem).start()
            return carry
        lax.fori_loop(0, rows8(nv_ref[tile]) // 8, body, 0)

    @pl.when(jnp.logical_and(t == 0, j == 0))
    def _():
        xg[...] = jnp.zeros_like(xg)
        issue(0)

    @pl.when(jnp.logical_and(active, j == 0))
    def _():
        n8 = pl.multiple_of(rows8(nv), 8)
        pltpu.make_async_copy(u_hbm.at[pl.ds(0, n8)], xg.at[pl.ds(0, n8)], sem).wait()
        w = xg[...]
        xb[:, :D // 2] = lax.bitcast_convert_type(lax.shift_left(w, 16), F32).astype(BF16)
        xb[:, D // 2:] = lax.bitcast_convert_type(w & -65536, F32).astype(BF16)

    @pl.when(jnp.logical_and(t + 1 < nu, j == 1))
    def _():
        issue(t + 1)

    def gate_up(rows):
        w = wgu_ref[0, 0].astype(BF16)
        ri = lax.broadcasted_iota(jnp.int32, (MOE_SEL, MOE_SEL // 2), 0)
        ci = lax.broadcasted_iota(jnp.int32, (MOE_SEL, MOE_SEL // 2), 1)
        pick_even = jnp.where(ri == 2 * ci, 1.0, 0.0).astype(BF16)
        parts = 2 if rows > MOE_TM // 2 else 1
        step = rows // parts
        gus = [jnp.dot(xb[step * p:step * (p + 1), :], w, preferred_element_type=F32) for p in range(parts)]
        for p, gu in enumerate(gus):
            gu = gu + bgu_ref[0, 0]
            up = pltpu.roll(gu, MOE_TN - 1, axis=1)
            gate = jnp.minimum(gu, SWIGLU_LIMIT)
            up = jnp.clip(up, -SWIGLU_LIMIT, SWIGLU_LIMIT)
            sig = 0.5 * jnp.tanh((0.5 * SWIGLU_ALPHA) * gate) + 0.5
            h = ((up + 1.0) * gate * sig).astype(BF16)
            hc = [jnp.dot(h[:, MOE_SEL * s:MOE_SEL * (s + 1)], pick_even, preferred_element_type=F32).astype(BF16)
                  for s in range(MOE_TN // MOE_SEL)]
            hs[j, step * p:step * (p + 1), :] = jnp.concatenate(hc, axis=1)

    def down(rows):
        acc = jnp.dot(hs[0, 0:rows, :], wdn_ref[0, 0, 0:MOE_TF, :].astype(BF16), preferred_element_type=F32)
        for c in range(1, MOE_NG):
            acc = acc + jnp.dot(hs[c, 0:rows, :], wdn_ref[0, 0, MOE_TF * c:MOE_TF * (c + 1), :].astype(BF16),
                                preferred_element_type=F32)
        y_ref[0:rows, :] = acc + bdn_ref[0, 0]
        if rows < MOE_TM:
            y_ref[rows:MOE_TM, :] = jnp.zeros((MOE_TM - rows, MOE_DN), F32)

    lower = 0
    for rows in MOE_ROWS:
        fits = jnp.logical_and(active, jnp.logical_and(nv > lower, nv <= rows))
        pl.when(jnp.logical_and(fits, j < MOE_NG))(functools.partial(gate_up, rows))
        pl.when(jnp.logical_and(fits, j >= MOE_NG))(functools.partial(down, rows))
        lower = rows

    @pl.when(jnp.logical_and(jnp.logical_not(active), j >= MOE_NG))
    def _():
        y_ref[...] = jnp.zeros_like(y_ref)


def _moe_experts(u, tile_e, n_used, n_valid, slot_tok, layer, w_gu, b_gu, w_dn, b_dn):
    n_tiles = tile_e.shape[0]
    nj = MOE_NG + MOE_ND
    jj = lambda t, j, nu: jnp.where(t < nu[0], j, nj - 1)
    jg = lambda t, j, nu: jnp.minimum(jj(t, j, nu), MOE_NG - 1)
    jd = lambda t, j, nu: jnp.maximum(jj(t, j, nu) - MOE_NG, 0)
    grid_spec = pltpu.PrefetchScalarGridSpec(
        num_scalar_prefetch=4,
        grid=(n_tiles, nj),
        in_specs=[pl.BlockSpec(memory_space=pl.ANY),
                  pl.BlockSpec((1, 1, D, MOE_TN), lambda t, j, te, nu, nv, tok: (layer, te[t], 0, jg(t, j, nu))),
                  pl.BlockSpec((1, 1, 1, MOE_TN), lambda t, j, te, nu, nv, tok: (layer, te[t], 0, jg(t, j, nu))),
                  pl.BlockSpec((1, 1, D_EXPERT, MOE_DN), lambda t, j, te, nu, nv, tok: (layer, te[t], 0, jd(t, j, nu))),
                  pl.BlockSpec((1, 1, 1, MOE_DN), lambda t, j, te, nu, nv, tok: (layer, te[t], 0, jd(t, j, nu)))],
        out_specs=pl.BlockSpec((MOE_TM, MOE_DN), lambda t, j, te, nu, nv, tok: (t, jnp.maximum(j - MOE_NG, 0))),
        scratch_shapes=[pltpu.VMEM((MOE_TM, D // 2), jnp.int32), pltpu.VMEM((MOE_TM, D), BF16),
                        pltpu.VMEM((MOE_NG, MOE_TM, MOE_TF), BF16), pltpu.SemaphoreType.DMA],
    )
    nl, ne = b_gu.shape[0], b_gu.shape[1]
    return pl.pallas_call(
        _moe_kernel,
        grid_spec=grid_spec,
        out_shape=jax.ShapeDtypeStruct((n_tiles * MOE_TM, D), F32),
        compiler_params=_cparams(("arbitrary", "arbitrary")),
        name="moe_experts",
    )(tile_e, n_used, n_valid, slot_tok, u, w_gu, b_gu.reshape(nl, ne, 1, 2 * D_EXPERT), w_dn, b_dn.reshape(nl, ne, 1, D))


CMB = 256


def _combine_kernel(slot_ref, h_ref, gate_ref, pg, sg, y_hbm, *rest, npt_c, split):
    if split:
        op_ref, os_ref, buf, sem = rest
    else:
        o_ref, buf, sem = rest
    i = pl.program_id(0)

    def issue(r, carry):
        for k in range(TOP_K):
            s = slot_ref[(i * CMB + r) * TOP_K + k]
            pltpu.make_async_copy(y_hbm.at[pl.ds(s, 1)], buf.at[k, pl.ds(r, 1)], sem).start()
        return carry
    lax.fori_loop(0, CMB, issue, 0, unroll=4)
    for k in range(TOP_K):
        pltpu.make_async_copy(y_hbm.at[pl.ds(0, CMB)], buf.at[k], sem).wait()
    gates = gate_ref[...]
    mo = gates[:, 0:1] * buf[0]
    for k in range(1, TOP_K):
        mo = mo + gates[:, k:k + 1] * buf[k]
    out = h_ref[...] + _pick(i < npt_c, pg, sg) * mo
    if split:
        @pl.when(i < npt_c)
        def _():
            op_ref[...] = out

        @pl.when(i >= npt_c)
        def _():
            os_ref[...] = out
    else:
        o_ref[...] = out


def _moe_combine(h, y, slot4, gate128, pmod, smod, npt_c, tiles_per_batch_c, split=False):
    t = h.shape[0]
    nsamp_tiles = (t // CMB) - npt_c
    if split:
        out_specs = [pl.BlockSpec((CMB, D), lambda i, s: (jnp.minimum(i, npt_c - 1), 0)),
                     pl.BlockSpec((CMB, D), lambda i, s: (jnp.clip(i - npt_c, 0, nsamp_tiles - 1), 0))]
        out_shape = [jax.ShapeDtypeStruct((npt_c * CMB, D), F32), jax.ShapeDtypeStruct((nsamp_tiles * CMB, D), F32)]
    else:
        out_specs = pl.BlockSpec((CMB, D), lambda i, s: (i, 0))
        out_shape = jax.ShapeDtypeStruct((t, D), F32)
    grid_spec = pltpu.PrefetchScalarGridSpec(
        num_scalar_prefetch=1,
        grid=(t // CMB,),
        in_specs=[pl.BlockSpec((CMB, D), lambda i, s: (i, 0)),
                  pl.BlockSpec((CMB, 128), lambda i, s: (i, 0)),
                  pl.BlockSpec((1, 1, D), lambda i, s: (jnp.minimum(i // tiles_per_batch_c, npt_c // tiles_per_batch_c - 1), 0, 2)),
                  pl.BlockSpec((CMB, D), lambda i, s: (jnp.clip(i - npt_c, 0, nsamp_tiles - 1), 2)),
                  pl.BlockSpec(memory_space=pl.ANY)],
        out_specs=out_specs,
        scratch_shapes=[pltpu.VMEM((TOP_K, CMB, D), F32), pltpu.SemaphoreType.DMA],
    )
    return pl.pallas_call(
        functools.partial(_combine_kernel, npt_c=npt_c, split=split),
        grid_spec=grid_spec,
        out_shape=out_shape,
        compiler_params=_cparams(("arbitrary",)),
        name="moe_combine",
    )(slot4, h, gate128, pmod, smod, y)


def _moe_layer(h, g, pmod, smod, layer, rw, rb, w_gu, b_gu, w_dn, b_dn, npt, tiles_per_batch, split=False):
    t = h.shape[0]
    u, idx128, gate128 = _route(h, g, pmod, smod, rw[layer], rb[layer], npt, tiles_per_batch)
    idx4 = idx128[:, :TOP_K]
    onehot = idx4[:, :, None] == jnp.arange(N_EXPERTS, dtype=jnp.int32)[None, None, :]
    sel = jnp.any(onehot, axis=1).astype(jnp.int32)
    counts = jnp.sum(sel, axis=0)
    rank = jnp.cumsum(sel, axis=0) - 1
    padded = (counts + MOE_TM - 1) // MOE_TM * MOE_TM
    pad_end = jnp.cumsum(padded)
    pad_start = pad_end - padded
    n_tiles = -(-t * TOP_K // MOE_TM) + N_EXPERTS
    dest4 = jnp.sum(jnp.where(onehot, (pad_start[None, :] + rank)[:, None, :], 0), axis=2).astype(jnp.int32)
    tok = jnp.broadcast_to(jnp.arange(t, dtype=jnp.int32)[:, None], (t, TOP_K))
    slot_tok = jnp.zeros((n_tiles * MOE_TM,), jnp.int32).at[dest4.reshape(-1)].set(tok.reshape(-1))
    n_used = (pad_end[-1] // MOE_TM).astype(jnp.int32)
    tile_ids = jnp.arange(n_tiles, dtype=jnp.int32)
    tile_e = jnp.minimum(jnp.searchsorted(pad_end, jnp.minimum(tile_ids, n_used - 1) * MOE_TM, side='right'),
                         N_EXPERTS - 1).astype(jnp.int32)
    n_valid = jnp.where(tile_ids < n_used,
                        jnp.clip((pad_start + counts)[tile_e] - tile_ids * MOE_TM, 0, MOE_TM), 0).astype(jnp.int32)
    def run(nt):
        y = _moe_experts(u, tile_e[:nt], n_used.reshape(1), n_valid[:nt], slot_tok[:nt * MOE_TM], layer,
                         w_gu, b_gu, w_dn, b_dn)
        return _moe_combine(h, y, dest4.reshape(-1), gate128, pmod, smod,
                            npt * (TM // CMB), tiles_per_batch * (TM // CMB), split)

    n_small = N_EXPERTS + MOE_SPARE_TILES
    return lax.cond(n_used <= n_small, functools.partial(run, n_small), functools.partial(run, n_tiles))


C_HEADS = 16
C_NOPE = 128
C_ROPE = 64
C_LORA = 512
ROPE_THETA = 10000.0
MLA_SCALE = (C_NOPE + C_ROPE) ** -0.5
HP = 128


def _rope_tables(pos):
    half = C_ROPE // 2
    inv = ROPE_THETA ** (-jnp.arange(half, dtype=F32) / half)
    ang = pos.astype(F32)[:, None] * inv[None, :]
    z = jnp.zeros_like(ang)
    cos = jnp.concatenate([jnp.cos(ang), jnp.cos(ang), z, z], axis=1)
    sin = jnp.concatenate([-jnp.sin(ang), jnp.sin(ang), z, z], axis=1)
    return cos, sin


def _rope128(x, cos, sin):
    n = x.shape[1]
    lane = lax.broadcasted_iota(jnp.int32, x.shape, 1) % HP
    partner = jnp.where(lane < C_ROPE // 2, pltpu.roll(x, n - C_ROPE // 2, axis=1), pltpu.roll(x, C_ROPE // 2, axis=1))
    return x * cos + partner * sin


def _mla_prep_kernel(z_ref, gq_ref, gkv_ref, gkr_ref, gqn_ref, gqr_ref, gkn_ref, wn_ref, wr_ref, wuk_ref,
                     cos_ref, sin_ref, lat_ref, kr_ref, krb_ref, qn_ref, qr_ref, qabs_ref, qg_ref, *, npt):
    i = pl.program_id(0)
    z = z_ref[...]
    cos = cos_ref[...]
    sin = sin_ref[...]
    lat_ref[...] = _rms(z[:, C_LORA:2 * C_LORA], gkv_ref[...])
    k128 = z[:, 2 * C_LORA:2 * C_LORA + HP]
    k128 = k128 * lax.rsqrt(jnp.sum(k128 * k128, axis=-1, keepdims=True) / C_ROPE + EPS) * gkr_ref[...]
    k128 = _rope128(k128, cos, sin)
    kr_ref[...] = k128[:, 0:C_ROPE]
    krb_ref[...] = k128.astype(BF16)
    cqn = _rms(z[:, 0:C_LORA], gq_ref[...]).astype(BF16)
    qn_all = jnp.dot(cqn, wn_ref[...], preferred_element_type=F32)
    qr_all = jnp.dot(cqn, wr_ref[...], preferred_element_type=F32)
    for h in range(C_HEADS):
        sl = slice(HP * h, HP * (h + 1))
        qn = _rms(qn_all[:, sl], gqn_ref[...]) * MLA_SCALE
        qn_ref[:, sl] = qn.astype(BF16)
        qg_ref[:, sl] = qn * gkn_ref[...]
        x = qr_all[:, sl]
        x = x * lax.rsqrt(jnp.sum(x * x, axis=-1, keepdims=True) / C_ROPE + EPS) * gqr_ref[...]
        qr_ref[:, sl] = (_rope128(x, cos, sin) * MLA_SCALE).astype(BF16)

    @pl.when(i == npt)
    def _():
        for h in range(C_HEADS):
            sl = slice(HP * h, HP * (h + 1))
            qabs_ref[:, C_LORA * h:C_LORA * (h + 1)] = _dot1(qg_ref[:, sl], wuk_ref[:, sl], NT).astype(BF16)


def _mla_prep(z1, gq, gkv, gkr, gqn, gqr, gkn, wn, wr, wuk, cos, sin, npt):
    t = z1.shape[0]
    zw = z1.shape[1]
    row = lambda w: pl.BlockSpec((TM, w), lambda i: (i, 0))
    full = lambda a: pl.BlockSpec(a.shape, lambda i: (0,) * a.ndim)
    pad64 = lambda g: jnp.concatenate([g, jnp.zeros((HP - C_ROPE,), F32)]).reshape(1, HP)
    args = (z1, gq.reshape(1, -1), gkv.reshape(1, -1), pad64(gkr), gqn.reshape(1, -1), pad64(gqr), gkn.reshape(1, -1),
            wn, wr, wuk, cos, sin)
    return pl.pallas_call(
        functools.partial(_mla_prep_kernel, npt=npt),
        grid=(t // TM,),
        in_specs=[row(zw)] + [full(a) for a in args[1:10]] + [row(HP), row(HP)],
        out_specs=[row(C_LORA), row(C_ROPE), row(HP), row(C_HEADS * HP), row(C_HEADS * HP),
                   pl.BlockSpec((TM, C_HEADS * C_LORA), lambda i: (0, 0))],
        out_shape=[jax.ShapeDtypeStruct((t, C_LORA), F32), jax.ShapeDtypeStruct((t, C_ROPE), F32),
                   jax.ShapeDtypeStruct((t, HP), BF16), jax.ShapeDtypeStruct((t, C_HEADS * HP), BF16),
                   jax.ShapeDtypeStruct((t, C_HEADS * HP), BF16), jax.ShapeDtypeStruct((TM, C_HEADS * C_LORA), BF16)],
        scratch_shapes=[pltpu.VMEM((TM, C_HEADS * HP), F32)],
        compiler_params=_cparams(("arbitrary",)),
        name="mla_prep",
    )(*args)


def _mla_kv_kernel(lat_ref, wuk_ref, wuv_ref, gkn_ref, kn_ref, v_ref):
    lat = lat_ref[...].astype(BF16)
    kn = jnp.dot(lat, wuk_ref[...], preferred_element_type=F32)
    for h in range(C_HEADS):
        sl = slice(HP * h, HP * (h + 1))
        kn_ref[:, sl] = _rms(kn[:, sl], gkn_ref[...]).astype(BF16)
    v_ref[...] = jnp.dot(lat, wuv_ref[...], preferred_element_type=F32).astype(BF16)


def _mla_kv(lat, wuk, wuv, gkn, rows):
    wide = C_HEADS * HP
    return pl.pallas_call(
        _mla_kv_kernel,
        grid=(rows // TM,),
        in_specs=[pl.BlockSpec((TM, C_LORA), lambda i: (i, 0)),
                  pl.BlockSpec((C_LORA, wide), lambda i: (0, 0)),
                  pl.BlockSpec((C_LORA, wide), lambda i: (0, 0)),
                  pl.BlockSpec((1, HP), lambda i: (0, 0))],
        out_specs=[pl.BlockSpec((TM, wide), lambda i: (i, 0))] * 2,
        out_shape=[jax.ShapeDtypeStruct((rows, wide), BF16)] * 2,
        compiler_params=_cparams(("parallel",)),
        name="mla_kv",
    )(lat, wuk, wuv, gkn.reshape(1, HP))


FQ = 512


def _mla_flash_kernel(qn_ref, qr_ref, kn_ref, kr_ref, v_ref, o_ref, kcat):
    qi = pl.program_id(2)

    @pl.when(qi == 0)
    def _():
        kcat[:, 0:HP] = kn_ref[...]
        kcat[:, HP:2 * HP] = kr_ref[...]

    q = jnp.concatenate([qn_ref[...], qr_ref[...]], axis=1)

    def block(kb, carry, diagonal):
        m, l, acc = carry
        ks = pl.ds(pl.multiple_of(kb * FQ, FQ), FQ)
        s = lax.dot_general(q, kcat[ks, :], (NT, ((), ())), preferred_element_type=F32)
        if diagonal:
            row = lax.broadcasted_iota(jnp.int32, (FQ, FQ), 0)
            col = lax.broadcasted_iota(jnp.int32, (FQ, FQ), 1)
            s = jnp.where(col <= row, s, -jnp.inf)
        m_new = jnp.maximum(m, jnp.max(s, axis=-1, keepdims=True))
        alpha = jnp.exp(m - m_new)
        p = jnp.exp(s - m_new)
        l = alpha * l + jnp.sum(p, axis=-1, keepdims=True)
        acc = alpha * acc + jnp.dot(p.astype(BF16), v_ref[ks, :], preferred_element_type=F32)
        return m_new, l, acc

    m0 = jnp.full((FQ, 1), -jnp.inf, F32)
    l0 = jnp.zeros((FQ, 1), F32)
    a0 = jnp.zeros((FQ, HP), F32)
    carry = lax.fori_loop(0, qi, lambda kb, c: block(kb, c, False), (m0, l0, a0))
    m, l, acc = block(qi, carry, True)
    o_ref[...] = (acc / l).astype(BF16)


def _mla_flash(qn, qr, kn, krb, v, nb, seq):
    nq = seq // FQ
    qspec = pl.BlockSpec((FQ, HP), lambda b, h, q: (b * nq + q, h))
    kspec = pl.BlockSpec((seq, HP), lambda b, h, q: (b, h))
    return pl.pallas_call(
        _mla_flash_kernel,
        grid=(nb, C_HEADS, nq),
        in_specs=[qspec, qspec, kspec, pl.BlockSpec((seq, HP), lambda b, h, q: (b, 0)), kspec],
        out_specs=qspec,
        out_shape=jax.ShapeDtypeStruct((nb * seq, C_HEADS * HP), BF16),
        scratch_shapes=[pltpu.VMEM((seq, 2 * HP), BF16)],
        compiler_params=_cparams(("parallel", "parallel", "arbitrary")),
        name="mla_flash",
    )(qn, qr, kn, krb, v)


PPS = 32
PPG = 4


def _mla_s_kernel(*refs, n_steps, n_batch, ci):
    (pt_ref, lat_hbm, kr_hbm, wukt_ref, qabs_ref, qr_ref, latn_ref, krn_ref, o_ref,
     wst, latb, lg, m_ref, l_ref, acc_ref, latbuf, krbuf, lsem, ksem) = refs
    b = pl.program_id(0)
    kt = pl.program_id(1)
    nq = qabs_ref.shape[1]
    nheads = C_HEADS
    ntok = nq // nheads

    step = b * n_steps + kt
    slot = step % 2

    def issue_pages(first_page, s):
        for k in range(PPS):
            page = pt_ref[first_page + k]
            pltpu.make_async_copy(lat_hbm.at[ci, page], latbuf.at[s, k], lsem.at[s]).start()
            pltpu.make_async_copy(kr_hbm.at[ci, page], krbuf.at[s, k], ksem.at[s]).start()

    @pl.when(step == 0)
    def _():
        issue_pages(0, 0)

    pltpu.make_async_copy(lat_hbm.at[ci, pl.ds(0, PPS)], latbuf.at[slot], lsem.at[slot]).wait()
    pltpu.make_async_copy(kr_hbm.at[ci, pl.ds(0, PPS)], krbuf.at[slot], ksem.at[slot]).wait()

    @pl.when(step + 1 < n_batch * n_steps)
    def _():
        issue_pages((step + 1) * PPS, 1 - slot)

    @pl.when(jnp.logical_and(b == 0, kt == 0))
    def _():
        wst[0:nheads * HP, :] = wukt_ref[...]

    @pl.when(kt == 0)
    def _():
        wst[nheads * HP:nheads * HP + nq, :] = qabs_ref[0]
        m_ref[...] = jnp.full_like(m_ref, -jnp.inf)
        l_ref[...] = jnp.zeros_like(l_ref)
        acc_ref[...] = jnp.zeros_like(acc_ref)

    qr = qr_ref[0]

    def up_project(lat_bf16):
        return lax.dot_general(wst[...], lat_bf16, (NT, ((), ())), preferred_element_type=F32)

    def key_logits(r, krt_f32):
        nkeys = r.shape[1]
        kraw = r[0:nheads * HP].reshape(nheads, HP, nkeys)
        rs = lax.rsqrt(jnp.sum(kraw * kraw, axis=1) / HP + EPS)
        lraw = r[nheads * HP:nheads * HP + nq].reshape(ntok, nheads, nkeys) * rs[None]
        return lraw.reshape(nq, nkeys) + _dot1(qr, krt_f32)

    def update(state, logits, lat_bf16):
        m_old, l_old, acc_old = state
        m_new = jnp.maximum(m_old, jnp.max(logits, axis=-1, keepdims=True))
        alpha = jnp.exp(m_old - m_new)
        p = jnp.exp(logits - m_new)
        l_new = alpha * l_old + jnp.sum(p, axis=-1, keepdims=True)
        acc_new = alpha * acc_old + jnp.dot(p.astype(BF16), lat_bf16, preferred_element_type=F32)
        return m_new, l_new, acc_new

    def load_group(g):
        pages = range(PPG * g, PPG * (g + 1))
        return jnp.concatenate([latbuf[slot, p] for p in pages], axis=0).astype(BF16)

    gk = PPG * PAGE
    for g in range(PPS // PPG):
        lat_g = load_group(g)
        krt_g = jnp.concatenate([krbuf[slot, p] for p in range(PPG * g, PPG * (g + 1))], axis=1)
        latb[gk * g:gk * (g + 1), :] = lat_g
        lg[:, gk * g:gk * (g + 1)] = key_logits(up_project(lat_g), krt_g)
    state = update((m_ref[...], l_ref[...], acc_ref[...]), lg[...], latb[...])
    m_ref[...], l_ref[...], acc_ref[...] = state

    @pl.when(kt == n_steps - 1)
    def _():
        latn = jnp.concatenate([latn_ref[0], jnp.zeros((PAGE - 8, C_LORA), F32)], axis=0).astype(BF16)
        logits = key_logits(up_project(latn), krn_ref[0])
        qtok = lax.broadcasted_iota(jnp.int32, (nq, PAGE), 0) // nheads
        slot = lax.broadcasted_iota(jnp.int32, (nq, PAGE), 1)
        _, l_fin, acc_fin = update(state, jnp.where(slot <= qtok, logits, -jnp.inf), latn)
        o_ref[0] = acc_fin / l_fin


def _mla_sample(page_table, lat_cache, kr_cache, ci, wukt, qabs, qr, lat_new8, kr_new8):
    nb, npages = page_table.shape
    n_steps = npages // PPS
    nq = qabs.shape[1]

    kr_cache_t = jnp.swapaxes(kr_cache, 2, 3)
    kr_new_t = jnp.pad(jnp.swapaxes(kr_new8, 1, 2), ((0, 0), (0, 0), (0, PAGE - kr_new8.shape[1])))
    grid_spec = pltpu.PrefetchScalarGridSpec(
        num_scalar_prefetch=1,
        grid=(nb, n_steps),
        in_specs=[pl.BlockSpec(memory_space=pl.ANY), pl.BlockSpec(memory_space=pl.ANY),
                  pl.BlockSpec(wukt.shape, lambda b, t, pt: (0, 0)),
                    pl.BlockSpec((1, nq, C_LORA), lambda b, t, pt: (b, 0, 0)),
                    pl.BlockSpec((1, nq, C_ROPE), lambda b, t, pt: (b, 0, 0)),
                    pl.BlockSpec((1, 8, C_LORA), lambda b, t, pt: (b, 0, 0)),
                    pl.BlockSpec((1, C_ROPE, PAGE), lambda b, t, pt: (b, 0, 0))],
        out_specs=pl.BlockSpec((1, nq, C_LORA), lambda b, t, pt: (b, 0, 0)),
        scratch_shapes=[pltpu.VMEM((C_HEADS * HP + nq, C_LORA), BF16),
                        pltpu.VMEM((PPS * PAGE, C_LORA), BF16),
                        pltpu.VMEM((nq, PPS * PAGE), F32),
                        pltpu.VMEM((nq, 1), F32), pltpu.VMEM((nq, 1), F32), pltpu.VMEM((nq, C_LORA), F32),
                        pltpu.VMEM((2, PPS, PAGE, C_LORA), F32), pltpu.VMEM((2, PPS, C_ROPE, PAGE), F32),
                        pltpu.SemaphoreType.DMA((2,)), pltpu.SemaphoreType.DMA((2,))],
    )
    return pl.pallas_call(
        functools.partial(_mla_s_kernel, n_steps=n_steps, n_batch=nb, ci=ci),
        grid_spec=grid_spec,
        out_shape=jax.ShapeDtypeStruct((nb, nq, C_LORA), F32),
        compiler_params=_cparams(("arbitrary", "arbitrary")),
        name="mla_sample",
    )(page_table.reshape(-1), lat_cache, kr_cache_t, wukt, qabs, qr, lat_new8, kr_new_t)


def _mla_uv_kernel(c_ref, w_ref, o_ref):
    o_ref[...] = _dot1(c_ref[...], w_ref[0])


def _mla_uv(ctx, wuv3):
    rows = ctx.shape[0]
    return pl.pallas_call(
        _mla_uv_kernel,
        grid=(C_HEADS,),
        in_specs=[pl.BlockSpec((rows, C_LORA), lambda h: (0, h)),
                  pl.BlockSpec((1, C_LORA, HP), lambda h: (h, 0, 0))],
        out_specs=pl.BlockSpec((rows, HP), lambda h: (0, h)),
        out_shape=jax.ShapeDtypeStruct((rows, C_HEADS * HP), F32),
        compiler_params=_cparams(("parallel",)),
        name="mla_uv",
    )(ctx, wuv3)


def kernel(x_prompt, x_sample, state_hgrn, cache_swa_kv, cache_mla_latent, cache_mla_krope, page_table, c_prompt, c_sample, ada_w, ada_b, norm_g, rel_bias, ab_w_in, ab_w_out, hgrn_lb_logits, hgrn_norm_g, swa_q_g, swa_k_g, mla_w_in, mla_q_a_g, mla_kv_a_g, mla_w_uq, mla_w_uk, mla_w_uv, mla_qn_g, mla_qr_g, mla_kn_g, mla_kr_g, mla_w_o, moe_router_w, moe_router_b, moe_w_gu, moe_b_gu, moe_w_dn, moe_b_dn):
    nb, seq, _ = x_prompt.shape
    nsb, nst, _ = x_sample.shape
    tp, ts = nb * seq, nsb * nst
    assert ts == TM and seq % TM == 0 and nst <= 4
    npt, tpb = tp // TM, seq // TM
    x = jnp.concatenate([x_prompt.reshape(tp, D), x_sample.reshape(ts, D)], axis=0).astype(F32)

    nc = nb + nsb
    c_all = jnp.concatenate([c_prompt, c_sample, jnp.zeros((-nc % 8, D), c_prompt.dtype)], axis=0).astype(F32)
    mods = _adaln(c_all, ada_w, ada_b)

    def mod(s):
        return mods[s, 0:nb].reshape(nb, 1, 3 * D), jnp.repeat(mods[s, nb:nc], nst, axis=0)

    pm, sm = mod(0)
    z = _normmod_mm(x, norm_g[0, 0], pm, sm, ab_w_in[0].astype(BF16), 1792, npt, tpb)
    zs3 = z[tp:].reshape(nsb, nst, z.shape[1])
    oa_p, hg_p = _hgrn_prompt(z, hgrn_lb_logits, hgrn_norm_g[0], nb, seq, 0)
    oa_s, hg_s = _hgrn_sample(zs3, hgrn_lb_logits, hgrn_norm_g[0], state_hgrn, 0)
    ob_p, k_p, v_p = _dilated_prompt(z, swa_q_g[0], swa_k_g[0], _band_tables(rel_bias), nb, seq)
    buf_len = cache_swa_kv.shape[2]
    ob_s, k_s, v_s = _dilated_sample(zs3, swa_q_g[0], swa_k_g[0], cache_swa_kv[0], _sample_tables(rel_bias, nst, buf_len))
    w_out = ab_w_out[0].astype(BF16)
    h = _proj_residual([oa_p, ob_p], [oa_s.reshape(ts, 1024), ob_s.reshape(ts, 1024)],
                       [w_out[:1024], w_out[1024:]], x, pm, sm, npt, tpb)
    pm, sm = mod(1)
    h = _moe_layer(h, norm_g[0, 1], pm, sm, 0, moe_router_w, moe_router_b, moe_w_gu, moe_b_gu, moe_w_dn, moe_b_dn, npt, tpb)

    pm, sm = mod(2)
    w_in1 = jnp.pad(mla_w_in[0], ((0, 0), (0, HP - C_ROPE))).astype(BF16)
    z1 = _normmod_mm(h, norm_g[1, 0], pm, sm, w_in1, w_in1.shape[1], npt, tpb)
    past = page_table.shape[1] * PAGE
    pos = jnp.concatenate([jnp.tile(jnp.arange(seq), nb), jnp.tile(past + jnp.arange(nst), nsb)])
    cos, sin = _rope_tables(pos)
    wq = mla_w_uq[0]
    wn = wq[:, :, :C_NOPE].reshape(C_LORA, C_HEADS * HP).astype(BF16)
    wr = jnp.pad(wq[:, :, C_NOPE:], ((0, 0), (0, 0), (0, HP - C_ROPE))).reshape(C_LORA, C_HEADS * HP).astype(BF16)
    wuk = mla_w_uk[0].reshape(C_LORA, C_HEADS * HP).astype(BF16)
    wuv = mla_w_uv[0].reshape(C_LORA, C_HEADS * HP).astype(BF16)
    lat, kr, krb, qn, qr, qabs = _mla_prep(z1, mla_q_a_g[0], mla_kv_a_g[0], mla_kr_g[0], mla_qn_g[0], mla_qr_g[0],
                                           mla_kn_g[0], wn, wr, wuk, cos, sin, npt)
    kn, v = _mla_kv(lat, wuk, wuv, mla_kn_g[0], tp)
    ctx_p = _mla_flash(qn, qr, kn, krb, v, nb, seq)
    qabs3 = qabs.reshape(nsb, nst * C_HEADS, C_LORA)
    qr_s = qr[tp:].reshape(ts, C_HEADS, HP)[:, :, :C_ROPE].reshape(nsb, nst * C_HEADS, C_ROPE)
    lat_new8 = jnp.pad(lat[tp:].reshape(nsb, nst, C_LORA), ((0, 0), (0, 8 - nst), (0, 0)))
    kr_new8 = jnp.pad(kr[tp:].reshape(nsb, nst, C_ROPE), ((0, 0), (0, 8 - nst), (0, 0)))
    ctx_s = _mla_sample(page_table, cache_mla_latent, cache_mla_krope, 0, wuk.T, qabs3, qr_s, lat_new8, kr_new8)
    wuv3 = jnp.transpose(mla_w_uv[0], (1, 0, 2)).astype(BF16)
    o_s = _mla_uv(ctx_s.reshape(ts, C_HEADS * C_LORA), wuv3)
    h = _proj_residual([ctx_p], [o_s], [mla_w_o[0].astype(BF16)], h, pm, sm, npt, tpb)
    pm, sm = mod(3)
    hp, hs = _moe_layer(h, norm_g[1, 1], pm, sm, 1, moe_router_w, moe_router_b, moe_w_gu, moe_b_gu, moe_w_dn, moe_b_dn,
                        npt, tpb, split=True)

    y_prompt = hp.reshape(nb, seq, D).astype(x_prompt.dtype)
    y_sample = hs.reshape(nsb, nst, D).astype(x_sample.dtype)
    hgrn_prompt = hg_p[None].astype(state_hgrn.dtype)
    hgrn_sample = hg_s.astype(state_hgrn.dtype)
    kv = lambda k, v, b, t: jnp.stack([k.reshape(b, t, 8, 128), v.reshape(b, t, 8, 128)], axis=2)[None]
    swa_prompt = kv(k_p, v_p, nb, seq).astype(cache_swa_kv.dtype)
    swa_sample = kv(k_s, v_s, nsb, nst).astype(cache_swa_kv.dtype)
    lat_prompt = lat[:tp].reshape(1, nb, seq, C_LORA).astype(cache_mla_latent.dtype)
    lat_sample = lat[tp:].reshape(1, nsb, nst, C_LORA).astype(cache_mla_latent.dtype)
    krope_prompt = kr[:tp].reshape(1, nb, seq, C_ROPE).astype(cache_mla_krope.dtype)
    krope_sample = kr[tp:].reshape(1, nsb, nst, C_ROPE).astype(cache_mla_krope.dtype)
    return (y_prompt, y_sample, hgrn_prompt, hgrn_sample, swa_prompt, swa_sample,
            lat_prompt, lat_sample, krope_prompt, krope_sample)
```

```python
import functools
import math

import numpy as np
import jax
import jax.numpy as jnp
from jax import lax
from jax.experimental import pallas as pl
from jax.experimental.pallas import tpu as pltpu

F32 = jnp.float32
BF16 = jnp.bfloat16
EPS = 1e-6

D = 2048
TM = 512
N_EXPERTS = 32
TOP_K = 4
D_EXPERT = 2048
SWIGLU_LIMIT = 7.0
SWIGLU_ALPHA = 1.702
PAGE = 128
V7X_VMEM_LIMIT = 56 * 1024 * 1024

NN = ((1,), (0,))
NT = ((1,), (1,))
TN = ((0,), (0,))


def _cparams(sem, vmem=V7X_VMEM_LIMIT):
    return pltpu.CompilerParams(dimension_semantics=sem, vmem_limit_bytes=vmem)


def _dot1(a, b, dims=NN):
    return lax.dot_general(a.astype(BF16), b.astype(BF16), (dims, ((), ())), preferred_element_type=F32)


def _split2(x):
    hi = x.astype(BF16)
    lo = (x - hi.astype(F32)).astype(BF16)
    return hi, lo


def _split3(x):
    hi = x.astype(BF16)
    r = x - hi.astype(F32)
    mid = r.astype(BF16)
    lo = (r - mid.astype(F32)).astype(BF16)
    return hi, mid, lo


def _dot_exact_rhs(a01, x):
    hi, mid, lo = _split3(x)
    f = lambda p: lax.dot_general(a01, p, (NN, ((), ())), preferred_element_type=F32)
    return f(hi) + f(mid) + f(lo)


def _dot3(a, b, dims=NN):
    ah, al = _split2(a)
    bh, bl = _split2(b)
    f = lambda p, q: lax.dot_general(p, q, (dims, ((), ())), preferred_element_type=F32)
    return f(ah, bh) + f(ah, bl) + f(al, bh)


def _sigmoid(x):
    return 1.0 / (1.0 + jnp.exp(-x))


def _rms(x, g):
    return x * lax.rsqrt(jnp.mean(x * x, axis=-1, keepdims=True) + EPS) * g


def _ada_kernel(c_ref, w_ref, b_ref, o_ref):
    c = c_ref[...]
    s = c * _sigmoid(c)
    o_ref[0] = _dot1(s, w_ref[0, 0]) + b_ref[0, 0]


def _adaln(c_all, ada_w, ada_b):
    r = c_all.shape[0]
    tn = 1024
    depth = ada_w.shape[0]
    b4 = ada_b.reshape(depth, 2, 1, 3 * D)
    return pl.pallas_call(
        _ada_kernel,
        grid=(depth * 2, 3 * D // tn),
        in_specs=[pl.BlockSpec((r, D), lambda s, j: (0, 0)),
                  pl.BlockSpec((1, 1, D, tn), lambda s, j: (s // 2, s % 2, 0, j)),
                  pl.BlockSpec((1, 1, 1, tn), lambda s, j: (s // 2, s % 2, 0, j))],
        out_specs=pl.BlockSpec((1, r, tn), lambda s, j: (s, 0, j)),
        out_shape=jax.ShapeDtypeStruct((depth * 2, r, 3 * D), F32),
        compiler_params=_cparams(("parallel", "parallel")),
        name="adaln",
    )(c_all, ada_w, b4)


def _pick(is_prompt, p_ref, s_ref):
    return jnp.where(is_prompt, p_ref[0], s_ref[...])


def _mod_specs(npt, rows_per_batch_tiles, col, tn=D, with_j=False):
    if with_j:
        pm = pl.BlockSpec((1, 1, tn), lambda i, j: (jnp.minimum(i // rows_per_batch_tiles, npt // rows_per_batch_tiles - 1), 0, col * (D // tn) + j))
        sm = pl.BlockSpec((TM, tn), lambda i, j: (0, col * (D // tn) + j))
    else:
        pm = pl.BlockSpec((1, 1, tn), lambda i, *_: (jnp.minimum(i // rows_per_batch_tiles, npt // rows_per_batch_tiles - 1), 0, col))
        sm = pl.BlockSpec((TM, tn), lambda i, *_: (0, col))
    return pm, sm


def _nm_mm_kernel(x_ref, g_ref, psh, psc, ssh, ssc, w_ref, o_ref, xn_ref, *, npt):
    i = pl.program_id(0)
    j = pl.program_id(1)

    @pl.when(j == 0)
    def _():
        xr = _rms(x_ref[...], g_ref[...])
        sc = _pick(i < npt, psc, ssc)
        sh = _pick(i < npt, psh, ssh)
        xn_ref[...] = (xr * (1.0 + sc) + sh).astype(BF16)

    o_ref[...] = jnp.dot(xn_ref[...], w_ref[...], preferred_element_type=F32)


def _normmod_mm(x, g, pmod, smod, w_bf16, tn, npt, tiles_per_batch):
    t = x.shape[0]
    n = w_bf16.shape[1]
    psh, ssh = _mod_specs(npt, tiles_per_batch, 0)
    psc, ssc = _mod_specs(npt, tiles_per_batch, 1)
    return pl.pallas_call(
        functools.partial(_nm_mm_kernel, npt=npt),
        grid=(t // TM, n // tn),
        in_specs=[pl.BlockSpec((TM, D), lambda i, j: (i, 0)),
                  pl.BlockSpec((1, D), lambda i, j: (0, 0)),
                  psh, psc, ssh, ssc,
                  pl.BlockSpec((D, tn), lambda i, j: (0, j))],
        out_specs=pl.BlockSpec((TM, tn), lambda i, j: (i, j)),
        out_shape=jax.ShapeDtypeStruct((t, n), F32),
        scratch_shapes=[pltpu.VMEM((TM, D), BF16)],
        compiler_params=_cparams(("parallel", "arbitrary")),
        name="normmod_mm",
    )(x, g.reshape(1, D), pmod, pmod, smod, smod, w_bf16)


def _proj_res_kernel(*refs, npt, nk):
    xp = refs[0:nk]
    xs = refs[nk:2 * nk]
    ws = refs[2 * nk:3 * nk]
    h_ref, pg, sg, o_ref = refs[3 * nk:3 * nk + 4]
    xb = refs[3 * nk + 4:]
    i = pl.program_id(0)
    j = pl.program_id(1)

    @pl.when(jnp.logical_and(j == 0, i < npt))
    def _():
        for k in range(nk):
            xb[k][...] = xp[k][...].astype(BF16)

    @pl.when(jnp.logical_and(j == 0, i >= npt))
    def _():
        for k in range(nk):
            xb[k][...] = xs[k][...].astype(BF16)

    acc = jnp.dot(xb[0][...], ws[0][...], preferred_element_type=F32)
    for k in range(1, nk):
        acc = acc + jnp.dot(xb[k][...], ws[k][...], preferred_element_type=F32)
    gate = _pick(i < npt, pg, sg)
    o_ref[...] = h_ref[...] + gate * acc


def _proj_residual(xps, xss, ws, h, pmod, smod, npt, tiles_per_batch, tn=1024):
    nk = len(ws)
    t = h.shape[0]
    ks = [w.shape[0] for w in ws]
    pg, sg = _mod_specs(npt, tiles_per_batch, 2, tn=tn, with_j=True)
    in_specs = ([pl.BlockSpec((TM, k), lambda i, j: (jnp.minimum(i, npt - 1), 0)) for k in ks]
                + [pl.BlockSpec((TM, k), lambda i, j: (0, 0)) for k in ks]
                + [pl.BlockSpec((k, tn), lambda i, j: (0, j)) for k in ks]
                + [pl.BlockSpec((TM, tn), lambda i, j: (i, j)), pg, sg])
    return pl.pallas_call(
        functools.partial(_proj_res_kernel, npt=npt, nk=nk),
        grid=(t // TM, D // tn),
        in_specs=in_specs,
        out_specs=pl.BlockSpec((TM, tn), lambda i, j: (i, j)),
        out_shape=jax.ShapeDtypeStruct((t, D), F32),
        scratch_shapes=[pltpu.VMEM((TM, k), BF16) for k in ks],
        compiler_params=_cparams(("parallel", "arbitrary")),
        name="proj_residual",
    )(*xps, *xss, *ws, h, pmod, smod)


HG_C = 64
HG_B = 16
HG_STEP = 256
HG_HEADS = 4


def _lower_bound(lbl, layer):
    e = jnp.exp(lbl - jnp.max(lbl, axis=0, keepdims=True))
    return jnp.sum(e[0:layer + 1], axis=0, keepdims=True) / jnp.sum(e, axis=0, keepdims=True)


def _hgrn_chunk(q, af, v, st, lb, tri):
    c, dk = q.shape
    nb = c // HG_B
    f = lb + (1.0 - lb) * _sigmoid(af)
    g = jnp.log(f)
    k = 1.0 - f
    cum = _dot_exact_rhs(tri, g)
    o = _dot1(q * jnp.exp(cum), st, NT)
    q3 = q.reshape(nb, HG_B, dk)
    k3 = k.reshape(nb, HG_B, dk)
    c3 = cum.reshape(nb, HG_B, dk)
    tpos = lax.broadcasted_iota(jnp.int32, (nb, HG_B, dk), 1)
    lane3 = lax.broadcasted_iota(jnp.int32, (nb, HG_B, dk), 2)
    dmat = jnp.zeros((nb, HG_B, dk), F32)
    for s in range(HG_B):
        rel = c3 - c3[:, s:s + 1, :]
        a = q3 * k3[:, s:s + 1, :] * jnp.exp(jnp.where(tpos >= s, rel, -jnp.inf))
        col = jnp.sum(a, axis=-1, keepdims=True)
        dmat = jnp.where(lane3 == s, col, dmat)
    lane_c = lax.broadcasted_iota(jnp.int32, (HG_B, c), 1)
    rows = []
    for blk in range(nb):
        r0 = HG_B * blk
        d_blk = dmat[blk]
        if blk == 0:
            rows.append(d_blk[:, :c])
            continue
        d_blk = pltpu.roll(d_blk, r0, axis=1)[:, :c]
        cb = cum[r0 - 1:r0, :]
        qs = q[r0:r0 + HG_B] * jnp.exp(cum[r0:r0 + HG_B] - cb)
        ks = k * jnp.exp(jnp.minimum(cb - cum, 0.0))
        s_off = _dot1(qs, ks, NT)
        rows.append(jnp.where(lane_c < r0, s_off, 0.0) + d_blk)
    scores = jnp.concatenate(rows, axis=0)
    o = o + _dot1(scores, v)
    last = cum[c - 1:c, :]
    st_new = jnp.exp(last) * st + _dot1(v, k * jnp.exp(last - cum), TN)
    return o, st_new


def _hgrn_p_kernel(q_ref, f_ref, v_ref, g_ref, lbl_ref, gn_ref, o_ref, s_ref, st_ref, *, layer, nsteps):
    tb = pl.program_id(2)

    @pl.when(tb == 0)
    def _():
        st_ref[...] = jnp.zeros_like(st_ref)

    lb = _lower_bound(lbl_ref[...], layer)
    ri = lax.broadcasted_iota(jnp.int32, (HG_C, HG_C), 0)
    ci = lax.broadcasted_iota(jnp.int32, (HG_C, HG_C), 1)
    tri = jnp.where(ri >= ci, 1.0, 0.0).astype(BF16)
    sts = [st_ref[hh] for hh in range(HG_HEADS)]
    for c in range(HG_STEP // HG_C):
        sl = slice(HG_C * c, HG_C * (c + 1))
        for hh in range(HG_HEADS):
            hl = slice(128 * hh, 128 * (hh + 1))
            o, sts[hh] = _hgrn_chunk(q_ref[sl, hl], f_ref[sl, hl], v_ref[sl, hl], sts[hh], lb[:, hl], tri)
            o_ref[sl, hl] = (_rms(o, gn_ref[...]) * _sigmoid(g_ref[sl, hl])).astype(BF16)
    for hh in range(HG_HEADS):
        st_ref[hh] = sts[hh]

    @pl.when(tb == nsteps - 1)
    def _():
        for hh in range(HG_HEADS):
            s_ref[0, hh] = sts[hh].T


def _hgrn_prompt(z, lb_logits, gn, nb, seq, layer):
    nh = 8
    nsteps = seq // HG_STEP
    hw = 128 * HG_HEADS
    ng = nh // HG_HEADS
    zspec = lambda c0: pl.BlockSpec((HG_STEP, hw), lambda b, h, t: (b * nsteps + t, c0 * ng + h))
    return pl.pallas_call(
        functools.partial(_hgrn_p_kernel, layer=layer, nsteps=nsteps),
        grid=(nb, ng, nsteps),
        in_specs=[zspec(0), zspec(1), zspec(2), zspec(3),
                  pl.BlockSpec((lb_logits.shape[0], hw), lambda b, h, t: (0, h)),
                  pl.BlockSpec((1, 128), lambda b, h, t: (0, 0))],
        out_specs=[pl.BlockSpec((HG_STEP, hw), lambda b, h, t: (b * nsteps + t, h)),
                   pl.BlockSpec((1, HG_HEADS, 128, 128), lambda b, h, t: (b, h, 0, 0))],
        out_shape=[jax.ShapeDtypeStruct((nb * seq, nh * 128), BF16),
                   jax.ShapeDtypeStruct((nb, nh, 128, 128), F32)],
        scratch_shapes=[pltpu.VMEM((HG_HEADS, 128, 128), F32)],
        compiler_params=_cparams(("parallel", "parallel", "arbitrary")),
        name="hgrn_prompt",
    )(z, z, z, z, lb_logits, gn.reshape(1, 128))


def _hgrn_s_kernel(q_ref, f_ref, v_ref, g_ref, lbl_ref, gn_ref, s0_ref, o_ref, s_ref, *, layer):
    nt = q_ref.shape[1]
    lb_all = _lower_bound(lbl_ref[...], layer)
    tpos = lax.broadcasted_iota(jnp.int32, (nt, 128), 0)
    outs = []
    for h in range(8):
        sl = slice(128 * h, 128 * (h + 1))
        q = q_ref[0, :, sl]
        v = v_ref[0, :, sl]
        lb = lb_all[:, sl]
        f = lb + (1.0 - lb) * _sigmoid(f_ref[0, :, sl])
        g = jnp.log(f)
        k = 1.0 - f
        crow = [g[0:1]]
        for t in range(1, nt):
            crow.append(crow[-1] + g[t:t + 1])
        cum = jnp.concatenate(crow, axis=0)
        last = crow[-1]
        s0 = s0_ref[0, 0, h]
        qe = jnp.concatenate([q * jnp.exp(cum), jnp.zeros((8 - nt, 128), F32)], axis=0)
        o = _dot1(qe, s0)[0:nt]
        for s in range(nt):
            a = q * k[s:s + 1] * jnp.exp(jnp.where(tpos >= s, cum - cum[s:s + 1], -jnp.inf))
            o = o + jnp.sum(a, axis=-1, keepdims=True) * v[s:s + 1]
        kp = k * jnp.exp(last - cum)
        w8 = jnp.concatenate([jnp.exp(last), kp, jnp.zeros((8 - 1 - nt, 128), F32)], axis=0)
        wt = jnp.concatenate([w8, jnp.zeros((120, 128), F32)], axis=0).T
        s_new = wt[:, 0:1] * s0
        for s in range(nt):
            s_new = s_new + wt[:, 1 + s:2 + s] * v[s:s + 1]
        s_ref[0, 0, h] = s_new
        outs.append(_rms(o, gn_ref[...]) * _sigmoid(g_ref[0, :, sl]))
    o_ref[0] = jnp.concatenate(outs, axis=1)


def _hgrn_sample(zs3, lb_logits, gn, state, layer):
    nb, nt, _ = zs3.shape
    zspec = lambda c: pl.BlockSpec((1, nt, 1024), lambda b: (b, 0, c))
    sspec = pl.BlockSpec((1, 1, 8, 128, 128), lambda b: (0, b, 0, 0, 0))
    return pl.pallas_call(
        functools.partial(_hgrn_s_kernel, layer=layer),
        grid=(nb,),
        in_specs=[zspec(0), zspec(1), zspec(2), zspec(3),
                  pl.BlockSpec(lb_logits.shape, lambda b: (0, 0)),
                  pl.BlockSpec((1, 128), lambda b: (0, 0)),
                  sspec],
        out_specs=[pl.BlockSpec((1, nt, 1024), lambda b: (b, 0, 0)), sspec],
        out_shape=[jax.ShapeDtypeStruct((nb, nt, 1024), F32),
                   jax.ShapeDtypeStruct(state.shape, F32)],
        compiler_params=_cparams(("parallel",)),
        name="hgrn_sample",
    )(zs3, zs3, zs3, zs3, lb_logits, gn.reshape(1, 128), state)


DILATED = ((128, 1), (512, 4), (2048, 16))
REL_BUCKETS = 32
REL_MAX_DISTANCE = 2048
ATT_SCALE = 128 ** -0.5
QB = 128


def _rel_bucket_np(dist):
    exact = REL_BUCKETS // 2
    d = np.asarray(dist)
    far = exact + (np.log(np.maximum(d, 1).astype(np.float32) / np.float32(exact))
                   / np.float32(math.log(REL_MAX_DISTANCE / exact)) * np.float32(REL_BUCKETS - exact)).astype(np.int32)
    return np.where(d < exact, d, np.minimum(far, REL_BUCKETS - 1)).astype(np.int32)


def _band_tables(rel_bias):
    rb = rel_bias.astype(F32)
    tabs = []
    for (window, dil), width in zip(DILATED, (2 * QB, 2 * QB, QB)):
        period = width + QB
        i = np.arange(period + 1)
        k = np.where(i < width, i, i - (period + 1))
        steps = (width - QB) - k
        valid = (steps >= 0) & (steps <= window // dil)
        bucket = _rel_bucket_np(np.clip(steps, 0, window // dil) * dil)
        u = jnp.where(jnp.asarray(valid)[None, :], rb[jnp.asarray(bucket)].T, -jnp.inf)
        flat = jnp.tile(u, (1, QB))[:, :QB * period]
        tabs.append(flat.reshape(rb.shape[1], QB, period)[:, :, :width])
    return tabs


def _soft_block(qb, kw, vw, bias):
    l = _dot1(qb, kw, NT) * ATT_SCALE + bias
    m = jnp.max(l, axis=-1, keepdims=True)
    p = jnp.exp(l - m)
    return m, jnp.sum(p, axis=-1, keepdims=True), _dot1(p, vw)


def _dil_p_kernel(q_ref, k_ref, v_ref, gq_ref, gk_ref, b1_ref, b2_ref, b3_ref, o_ref, ko_ref, vo_ref,
                  qs, ks, vs, a1, a2, a3, m1, m2, m3, l1, l2, l3):
    seq = q_ref.shape[0]
    kn = _rms(k_ref[...], gk_ref[...])
    v = v_ref[...]
    ko_ref[...] = kn
    vo_ref[...] = v
    qs[...] = _rms(q_ref[...], gq_ref[...])
    ks[...] = kn
    vs[...] = v
    zpad = jnp.zeros((QB, 128), F32)
    first_cols = lax.broadcasted_iota(jnp.int32, (QB, 2 * QB), 1) >= QB

    def banded(qsub, ksub, vsub, bias, n_blocks, store):
        for i in range(n_blocks):
            qb = qsub(i)
            if i == 0:
                kw = jnp.concatenate([zpad, ksub(0)], axis=0)
                vw = jnp.concatenate([zpad, vsub(0)], axis=0)
                bb = jnp.where(first_cols, bias, -jnp.inf)
            else:
                kw = jnp.concatenate([ksub(i - 1), ksub(i)], axis=0)
                vw = jnp.concatenate([vsub(i - 1), vsub(i)], axis=0)
                bb = bias
            store(i, *_soft_block(qb, kw, vw, bb))

    def st1(i, m, l, a):
        sl = slice(QB * i, QB * (i + 1))
        m1[sl, :] = m
        l1[sl, :] = l
        a1[sl, :] = a
    blk = lambda ref: (lambda i: ref[QB * i:QB * (i + 1), :])
    banded(blk(qs), blk(ks), blk(vs), b1_ref[0], seq // QB, st1)

    d2 = DILATED[1][1]
    for r in range(d2):
        sub = lambda ref: (lambda i: ref[pl.ds(r + d2 * QB * i, QB, stride=d2), :])

        def st2(i, m, l, a):
            idx = pl.ds(r + d2 * QB * i, QB, stride=d2)
            m2[idx, :] = m
            l2[idx, :] = l
            a2[idx, :] = a
        banded(sub(qs), sub(ks), sub(vs), b2_ref[0], seq // (QB * d2), st2)

    d3 = DILATED[2][1]
    for r in range(d3):
        idx = pl.ds(r, QB, stride=d3)
        m, l, a = _soft_block(qs[idx, :], ks[idx, :], vs[idx, :], b3_ref[0])
        m3[idx, :] = m
        l3[idx, :] = l
        a3[idx, :] = a

    mx = jnp.maximum(jnp.maximum(m1[...], m2[...]), m3[...])
    w1 = jnp.exp(m1[...] - mx)
    w2 = jnp.exp(m2[...] - mx)
    w3 = jnp.exp(m3[...] - mx)
    num = w1 * a1[...] + w2 * a2[...] + w3 * a3[...]
    den = w1 * l1[...] + w2 * l2[...] + w3 * l3[...]
    o_ref[...] = (num / den).astype(BF16)


def _dilated_prompt(z, gq, gk, tabs, nb, seq):
    nh = 8
    zspec = lambda c0: pl.BlockSpec((seq, 128), lambda b, h: (b, c0 + h))
    gspec = pl.BlockSpec((1, 128), lambda b, h: (0, 0))
    tspec = lambda w: pl.BlockSpec((1, QB, w), lambda b, h: (h, 0, 0))
    ospec = pl.BlockSpec((seq, 128), lambda b, h: (b, h))
    big = pltpu.VMEM((seq, 128), F32)
    col = pltpu.VMEM((seq, 1), F32)
    return pl.pallas_call(
        _dil_p_kernel,
        grid=(nb, nh),
        in_specs=[zspec(32), zspec(40), zspec(48), gspec, gspec, tspec(2 * QB), tspec(2 * QB), tspec(QB)],
        out_specs=[ospec, ospec, ospec],
        out_shape=[jax.ShapeDtypeStruct((nb * seq, nh * 128), BF16),
                   jax.ShapeDtypeStruct((nb * seq, nh * 128), F32),
                   jax.ShapeDtypeStruct((nb * seq, nh * 128), F32)],
        scratch_shapes=[big, big, big, big, big, big, col, col, col, col, col, col],
        compiler_params=_cparams(("parallel", "parallel")),
        name="dilated_prompt",
    )(z, z, z, gq.reshape(1, 128), gk.reshape(1, 128), *tabs)


def _sample_tables(rel_bias, nt, buf_len):
    rb = rel_bias.astype(F32)
    nh = rb.shape[1]
    w2 = DILATED[1][0]

    def count(delta, patterns):
        c = np.zeros(delta.shape, np.int32)
        for window, dil in patterns:
            c += ((delta >= 0) & (delta <= window) & (delta % dil == 0)).astype(np.int32)
        return c

    qpos = buf_len + np.arange(8)
    real_q = (np.arange(8) < nt)
    da = qpos[None, :] - (buf_len - w2 + np.arange(w2))[:, None]
    ca = count(da, DILATED[:2]) * real_q[None, :]
    nj = buf_len // 16
    rows_b = (16 * np.arange(nj)[None, :] + np.arange(nt)[:, None]).reshape(-1)
    db = qpos[None, :] - rows_b[:, None]
    cb = count(db, DILATED[2:]) * real_q[None, :]
    dn = qpos[None, :] - qpos[:, None]
    cn = count(dn, DILATED) * real_q[None, :] * real_q[:, None]

    def tables(delta, cnt):
        bucket = _rel_bucket_np(np.clip(delta, 0, REL_MAX_DISTANCE))
        onehot = (bucket.reshape(-1, 1) == np.arange(REL_BUCKETS)[None, :]).astype(np.float32)
        bias = jnp.dot(jnp.asarray(onehot), rb, precision=lax.Precision.HIGHEST).reshape(*delta.shape, nh)
        bias = jnp.where(jnp.asarray((cnt > 0) | ~real_q[None, :])[:, :, None], bias, -jnp.inf)
        bias = jnp.transpose(bias, (0, 2, 1)).reshape(delta.shape[0], nh * 8)
        mult = np.repeat(cnt[:, None, :], nh, axis=1).reshape(delta.shape[0], nh * 8).astype(np.float32)
        return bias, jnp.asarray(mult)

    return tables(da, ca) + tables(db, cb) + tables(dn, cn)


def _dil_s_kernel(q_ref, k_ref, v_ref, gq_ref, gk_ref, ca_ref, cb_ref,
                  ba_ref, ma_ref, bb_ref, mb_ref, bn_ref, mn_ref, o_ref, ko_ref, vo_ref):
    nt = q_ref.shape[1]
    nh = 8
    gq = gq_ref[...]
    gk = gk_ref[...]
    qn = jnp.concatenate([_rms(q_ref[0, :, 128 * h:128 * (h + 1)], gq) for h in range(nh)], axis=1)
    kn = jnp.concatenate([_rms(k_ref[0, :, 128 * h:128 * (h + 1)], gk) for h in range(nh)], axis=1)
    vn = v_ref[0]
    ko_ref[0] = kn
    vo_ref[0] = vn
    pad = jnp.zeros((8 - nt, nh * 128), F32)
    q8 = jnp.concatenate([qn, pad], axis=0)
    k8 = jnp.concatenate([kn, pad], axis=0)
    v8 = jnp.concatenate([vn, pad], axis=0)
    lane_head = lax.broadcasted_iota(jnp.int32, (8, nh * 128), 1) // 128
    qbd = jnp.concatenate([jnp.where(lane_head == h, q8, 0.0) for h in range(nh)], axis=0).astype(BF16)

    def heads(ref2d, first, nrows, stride):
        return jnp.concatenate([ref2d[pl.ds(first + h, nrows, stride=stride), :] for h in range(nh)], axis=1)

    a2 = ca_ref.at[0]
    na = a2.shape[0] // (2 * nh)
    groups = [(heads(a2, 0, na, 2 * nh), heads(a2, nh, na, 2 * nh), ba_ref[...], ma_ref[...])]
    nj = cb_ref.shape[1]
    per_j = cb_ref.shape[2]
    b2 = cb_ref.reshape(nj * per_j, 128)
    for i in range(nt):
        groups.append((heads(b2, 2 * nh * i, nj, per_j), heads(b2, 2 * nh * i + nh, nj, per_j),
                       bb_ref[nj * i:nj * (i + 1), :], mb_ref[nj * i:nj * (i + 1), :]))
    groups.append((k8, v8, bn_ref[...], mn_ref[...]))

    logits = [_dot1(kk, qbd, NT) * ATT_SCALE + bias for kk, _, bias, _ in groups]
    mx = logits[0].max(axis=0, keepdims=True)
    for l in logits[1:]:
        mx = jnp.maximum(mx, l.max(axis=0, keepdims=True))
    ps = [mult * jnp.exp(l - mx) for l, (_, _, _, mult) in zip(logits, groups)]
    den = ps[0].sum(axis=0, keepdims=True)
    for p in ps[1:]:
        den = den + p.sum(axis=0, keepdims=True)
    inv = 1.0 / jnp.where(den > 0.0, den, 1.0)
    acc = None
    for p, (_, vv, _, _) in zip(ps, groups):
        part = _dot1(p * inv, vv, TN)
        acc = part if acc is None else acc + part
    o8 = jnp.concatenate([acc[8 * h:8 * (h + 1), 128 * h:128 * (h + 1)] for h in range(nh)], axis=1)
    o_ref[0] = o8[0:nt]


def _dilated_sample(zs3, gq, gk, cache, tabs):
    nb, nt, _ = zs3.shape
    buf_len = cache.shape[1]
    w2 = DILATED[1][0]
    d3 = DILATED[2][1]
    kvh = 2 * 8
    cache_a = cache.reshape(nb, buf_len * kvh, 128)
    cache_b = cache.reshape(nb, buf_len // d3, d3 * kvh, 128)
    zspec = lambda c: pl.BlockSpec((1, nt, 1024), lambda b: (b, 0, c))
    gspec = pl.BlockSpec((1, 128), lambda b: (0, 0))
    full = lambda a: pl.BlockSpec(a.shape, lambda b: (0, 0))
    ospec = pl.BlockSpec((1, nt, 1024), lambda b: (b, 0, 0))
    return pl.pallas_call(
        _dil_s_kernel,
        grid=(nb,),
        in_specs=[zspec(4), zspec(5), zspec(6), gspec, gspec,
                  pl.BlockSpec((1, w2 * kvh, 128), lambda b: (b, buf_len // w2 - 1, 0)),
                  pl.BlockSpec((1, buf_len // d3, nt * kvh, 128), lambda b: (b, 0, 0, 0))]
                 + [full(a) for a in tabs],
        out_specs=[ospec, ospec, ospec],
        out_shape=[jax.ShapeDtypeStruct((nb, nt, 1024), F32)] * 3,
        compiler_params=_cparams(("parallel",)),
        name="dilated_sample",
    )(zs3, zs3, zs3, gq.reshape(1, 128), gk.reshape(1, 128), cache_a, cache_b, *tabs)


def _route_kernel(x_ref, g_ref, psh, psc, ssh, ssc, rw_ref, rb_ref, u_ref, idx_ref, gate_ref, *, npt):
    i = pl.program_id(0)
    xr = _rms(x_ref[...], g_ref[...])
    u = xr * (1.0 + _pick(i < npt, psc, ssc)) + _pick(i < npt, psh, ssh)
    ub = u.astype(BF16).astype(F32)
    lo = lax.shift_right_logical(lax.bitcast_convert_type(ub[:, :D // 2], jnp.int32), 16)
    u_ref[...] = lax.bitcast_convert_type(ub[:, D // 2:], jnp.int32) | lo
    l = _dot3(u, rw_ref[...]) + rb_ref[...]
    lane = lax.broadcasted_iota(jnp.int32, l.shape, 1).astype(F32)
    out_lane = lax.broadcasted_iota(jnp.int32, (TM, 128), 1)
    vals, idxs = [], []
    for _ in range(TOP_K):
        m = jnp.max(l, axis=-1, keepdims=True)
        first = jnp.min(jnp.where(l == m, lane, float(N_EXPERTS)), axis=-1, keepdims=True)
        vals.append(m)
        idxs.append(first.astype(jnp.int32))
        l = jnp.where(lane == first, -jnp.inf, l)
    es = [jnp.exp(v - vals[0]) for v in vals]
    tot = es[0] + es[1] + es[2] + es[3]
    idx_out = jnp.zeros((TM, 128), jnp.int32)
    gate_out = jnp.zeros((TM, 128), F32)
    for k in range(TOP_K):
        idx_out = jnp.where(out_lane == k, idxs[k], idx_out)
        gate_out = jnp.where(out_lane == k, es[k] / tot, gate_out)
    idx_ref[...] = idx_out
    gate_ref[...] = gate_out


def _route(x, g, pmod, smod, rw, rb, npt, tiles_per_batch):
    t = x.shape[0]
    psh, ssh = _mod_specs(npt, tiles_per_batch, 0)
    psc, ssc = _mod_specs(npt, tiles_per_batch, 1)
    return pl.pallas_call(
        functools.partial(_route_kernel, npt=npt),
        grid=(t // TM,),
        in_specs=[pl.BlockSpec((TM, D), lambda i: (i, 0)),
                  pl.BlockSpec((1, D), lambda i: (0, 0)),
                  psh, psc, ssh, ssc,
                  pl.BlockSpec((D, N_EXPERTS), lambda i: (0, 0)),
                  pl.BlockSpec((1, N_EXPERTS), lambda i: (0, 0))],
        out_specs=[pl.BlockSpec((TM, D // 2), lambda i: (i, 0)),
                   pl.BlockSpec((TM, 128), lambda i: (i, 0)),
                   pl.BlockSpec((TM, 128), lambda i: (i, 0))],
        out_shape=[jax.ShapeDtypeStruct((t, D // 2), jnp.int32),
                   jax.ShapeDtypeStruct((t, 128), jnp.int32),
                   jax.ShapeDtypeStruct((t, 128), F32)],
        compiler_params=_cparams(("parallel",)),
        name="moe_route",
    )(x, g.reshape(1, D), pmod, pmod, smod, smod, rw, rb.reshape(1, N_EXPERTS))


MOE_TM = 1344
MOE_ROWS = (304, 608, 768, 896, 1056, 1216, MOE_TM)
MOE_TN = 1024
MOE_SEL = 512
MOE_TF = MOE_TN // 2


MOE_NG = 2 * D_EXPERT // MOE_TN
MOE_DN = 512
MOE_ND = D // MOE_DN


def _moe_kernel(te_ref, nu_ref, nv_ref, tok_ref, u_hbm, wgu_ref, bgu_ref, wdn_ref, bdn_ref, y_ref, xg, xb, hs, sem):
    t = pl.program_id(0)
    j = pl.program_id(1)
    nu = nu_ref[0]
    active = t < nu
    nv = nv_ref[t]

    def rows8(n):
        return (n + 7) // 8 * 8

    def issue(tile):
        def body(g, carry):
            for i in range(8):
                r = g * 8 + i
                tok = tok_ref[tile * MOE_TM + r]
                pltpu.make_async_copy(u_hbm.at[pl.ds(tok, 1)], xg.at[pl.ds(r, 1)], sem).start()
            return carry
        lax.fori_loop(0, rows8(nv_ref[tile]) // 8, body, 0)

    @pl.when(jnp.logical_and(t == 0, j == 0))
    def _():
        xg[...] = jnp.zeros_like(xg)
        issue(0)

    @pl.when(jnp.logical_and(active, j == 0))
    def _():
        n8 = pl.multiple_of(rows8(nv), 8)
        pltpu.make_async_copy(u_hbm.at[pl.ds(0, n8)], xg.at[pl.ds(0, n8)], sem).wait()
        w = xg[...]
        xb[:, :D // 2] = lax.bitcast_convert_type(lax.shift_left(w, 16), F32).astype(BF16)
        xb[:, D // 2:] = lax.bitcast_convert_type(w & -65536, F32).astype(BF16)

    @pl.when(jnp.logical_and(t + 1 < nu, j == 1))
    def _():
        issue(t + 1)

    def gate_up(rows):
        w = wgu_ref[0, 0].astype(BF16)
        ri = lax.broadcasted_iota(jnp.int32, (MOE_SEL, MOE_SEL // 2), 0)
        ci = lax.broadcasted_iota(jnp.int32, (MOE_SEL, MOE_SEL // 2), 1)
        pick_even = jnp.where(ri == 2 * ci, 1.0, 0.0).astype(BF16)
        parts = 2 if rows > MOE_TM // 2 else 1
        step = rows // parts
        gus = [jnp.dot(xb[step * p:step * (p + 1), :], w, preferred_element_type=F32) for p in range(parts)]
        for p, gu in enumerate(gus):
            gu = gu + bgu_ref[0, 0]
            up = pltpu.roll(gu, MOE_TN - 1, axis=1)
            gate = jnp.minimum(gu, SWIGLU_LIMIT)
            up = jnp.clip(up, -SWIGLU_LIMIT, SWIGLU_LIMIT)
            sig = 0.5 * jnp.tanh((0.5 * SWIGLU_ALPHA) * gate) + 0.5
            h = ((up + 1.0) * gate * sig).astype(BF16)
            hc = [jnp.dot(h[:, MOE_SEL * s:MOE_SEL * (s + 1)], pick_even, preferred_element_type=F32).astype(BF16)
                  for s in range(MOE_TN // MOE_SEL)]
            hs[j, step * p:step * (p + 1), :] = jnp.concatenate(hc, axis=1)

    def down(rows):
        acc = jnp.dot(hs[0, 0:rows, :], wdn_ref[0, 0, 0:MOE_TF, :].astype(BF16), preferred_element_type=F32)
        for c in range(1, MOE_NG):
            acc = acc + jnp.dot(hs[c, 0:rows, :], wdn_ref[0, 0, MOE_TF * c:MOE_TF * (c + 1), :].astype(BF16),
                                preferred_element_type=F32)
        y_ref[0:rows, :] = acc + bdn_ref[0, 0]
        if rows < MOE_TM:
            y_ref[rows:MOE_TM, :] = jnp.zeros((MOE_TM - rows, MOE_DN), F32)

    lower = 0
    for rows in MOE_ROWS:
        fits = jnp.logical_and(active, jnp.logical_and(nv > lower, nv <= rows))
        pl.when(jnp.logical_and(fits, j < MOE_NG))(functools.partial(gate_up, rows))
        pl.when(jnp.logical_and(fits, j >= MOE_NG))(functools.partial(down, rows))
        lower = rows

    @pl.when(jnp.logical_and(jnp.logical_not(active), j >= MOE_NG))
    def _():
        y_ref[...] = jnp.zeros_like(y_ref)


def _moe_experts(u, tile_e, n_used, n_valid, slot_tok, layer, w_gu, b_gu, w_dn, b_dn):
    n_tiles = tile_e.shape[0]
    nj = MOE_NG + MOE_ND
    jj = lambda t, j, nu: jnp.where(t < nu[0], j, nj - 1)
    jg = lambda t, j, nu: jnp.minimum(jj(t, j, nu), MOE_NG - 1)
    jd = lambda t, j, nu: jnp.maximum(jj(t, j, nu) - MOE_NG, 0)
    grid_spec = pltpu.PrefetchScalarGridSpec(
        num_scalar_prefetch=4,
        grid=(n_tiles, nj),
        in_specs=[pl.BlockSpec(memory_space=pl.ANY),
                  pl.BlockSpec((1, 1, D, MOE_TN), lambda t, j, te, nu, nv, tok: (layer, te[t], 0, jg(t, j, nu))),
                  pl.BlockSpec((1, 1, 1, MOE_TN), lambda t, j, te, nu, nv, tok: (layer, te[t], 0, jg(t, j, nu))),
                  pl.BlockSpec((1, 1, D_EXPERT, MOE_DN), lambda t, j, te, nu, nv, tok: (layer, te[t], 0, jd(t, j, nu))),
                  pl.BlockSpec((1, 1, 1, MOE_DN), lambda t, j, te, nu, nv, tok: (layer, te[t], 0, jd(t, j, nu)))],
        out_specs=pl.BlockSpec((MOE_TM, MOE_DN), lambda t, j, te, nu, nv, tok: (t, jnp.maximum(j - MOE_NG, 0))),
        scratch_shapes=[pltpu.VMEM((MOE_TM, D // 2), jnp.int32), pltpu.VMEM((MOE_TM, D), BF16),
                        pltpu.VMEM((MOE_NG, MOE_TM, MOE_TF), BF16), pltpu.SemaphoreType.DMA],
    )
    nl, ne = b_gu.shape[0], b_gu.shape[1]
    return pl.pallas_call(
        _moe_kernel,
        grid_spec=grid_spec,
        out_shape=jax.ShapeDtypeStruct((n_tiles * MOE_TM, D), F32),
        compiler_params=_cparams(("arbitrary", "arbitrary")),
        name="moe_experts",
    )(tile_e, n_used, n_valid, slot_tok, u, w_gu, b_gu.reshape(nl, ne, 1, 2 * D_EXPERT), w_dn, b_dn.reshape(nl, ne, 1, D))


CMB = 256


def _combine_kernel(slot_ref, h_ref, gate_ref, pg, sg, y_hbm, *rest, npt_c, split):
    if split:
        op_ref, os_ref, buf, sem = rest
    else:
        o_ref, buf, sem = rest
    i = pl.program_id(0)

    def issue(r, carry):
        for k in range(TOP_K):
            s = slot_ref[(i * CMB + r) * TOP_K + k]
            pltpu.make_async_copy(y_hbm.at[pl.ds(s, 1)], buf.at[k, pl.ds(r, 1)], sem).start()
        return carry
    lax.fori_loop(0, CMB, issue, 0, unroll=4)
    for k in range(TOP_K):
        pltpu.make_async_copy(y_hbm.at[pl.ds(0, CMB)], buf.at[k], sem).wait()
    gates = gate_ref[...]
    mo = gates[:, 0:1] * buf[0]
    for k in range(1, TOP_K):
        mo = mo + gates[:, k:k + 1] * buf[k]
    out = h_ref[...] + _pick(i < npt_c, pg, sg) * mo
    if split:
        @pl.when(i < npt_c)
        def _():
            op_ref[...] = out

        @pl.when(i >= npt_c)
        def _():
            os_ref[...] = out
    else:
        o_ref[...] = out


def _moe_combine(h, y, slot4, gate128, pmod, smod, npt_c, tiles_per_batch_c, split=False):
    t = h.shape[0]
    nsamp_tiles = (t // CMB) - npt_c
    if split:
        out_specs = [pl.BlockSpec((CMB, D), lambda i, s: (jnp.minimum(i, npt_c - 1), 0)),
                     pl.BlockSpec((CMB, D), lambda i, s: (jnp.clip(i - npt_c, 0, nsamp_tiles - 1), 0))]
        out_shape = [jax.ShapeDtypeStruct((npt_c * CMB, D), F32), jax.ShapeDtypeStruct((nsamp_tiles * CMB, D), F32)]
    else:
        out_specs = pl.BlockSpec((CMB, D), lambda i, s: (i, 0))
        out_shape = jax.ShapeDtypeStruct((t, D), F32)
    grid_spec = pltpu.PrefetchScalarGridSpec(
        num_scalar_prefetch=1,
        grid=(t // CMB,),
        in_specs=[pl.BlockSpec((CMB, D), lambda i, s: (i, 0)),
                  pl.BlockSpec((CMB, 128), lambda i, s: (i, 0)),
                  pl.BlockSpec((1, 1, D), lambda i, s: (jnp.minimum(i // tiles_per_batch_c, npt_c // tiles_per_batch_c - 1), 0, 2)),
                  pl.BlockSpec((CMB, D), lambda i, s: (jnp.clip(i - npt_c, 0, nsamp_tiles - 1), 2)),
                  pl.BlockSpec(memory_space=pl.ANY)],
        out_specs=out_specs,
        scratch_shapes=[pltpu.VMEM((TOP_K, CMB, D), F32), pltpu.SemaphoreType.DMA],
    )
    return pl.pallas_call(
        functools.partial(_combine_kernel, npt_c=npt_c, split=split),
        grid_spec=grid_spec,
        out_shape=out_shape,
        compiler_params=_cparams(("arbitrary",)),
        name="moe_combine",
    )(slot4, h, gate128, pmod, smod, y)


def _moe_layer(h, g, pmod, smod, layer, rw, rb, w_gu, b_gu, w_dn, b_dn, npt, tiles_per_batch, split=False):
    t = h.shape[0]
    u, idx128, gate128 = _route(h, g, pmod, smod, rw[layer], rb[layer], npt, tiles_per_batch)
    idx4 = idx128[:, :TOP_K]
    onehot = idx4[:, :, None] == jnp.arange(N_EXPERTS, dtype=jnp.int32)[None, None, :]
    sel = jnp.any(onehot, axis=1).astype(jnp.int32)
    counts = jnp.sum(sel, axis=0)
    rank = jnp.cumsum(sel, axis=0) - 1
    padded = (counts + MOE_TM - 1) // MOE_TM * MOE_TM
    pad_end = jnp.cumsum(padded)
    pad_start = pad_end - padded
    n_tiles = -(-t * TOP_K // MOE_TM) + N_EXPERTS
    dest4 = jnp.sum(jnp.where(onehot, (pad_start[None, :] + rank)[:, None, :], 0), axis=2).astype(jnp.int32)
    tok = jnp.broadcast_to(jnp.arange(t, dtype=jnp.int32)[:, None], (t, TOP_K))
    slot_tok = jnp.zeros((n_tiles * MOE_TM,), jnp.int32).at[dest4.reshape(-1)].set(tok.reshape(-1))
    n_used = (pad_end[-1] // MOE_TM).astype(jnp.int32)
    tile_ids = jnp.arange(n_tiles, dtype=jnp.int32)
    tile_e = jnp.minimum(jnp.searchsorted(pad_end, jnp.minimum(tile_ids, n_used - 1) * MOE_TM, side='right'),
                         N_EXPERTS - 1).astype(jnp.int32)
    n_valid = jnp.where(tile_ids < n_used,
                        jnp.clip((pad_start + counts)[tile_e] - tile_ids * MOE_TM, 0, MOE_TM), 0).astype(jnp.int32)
    y = _moe_experts(u, tile_e, n_used.reshape(1), n_valid, slot_tok, layer, w_gu, b_gu, w_dn, b_dn)
    return _moe_combine(h, y, dest4.reshape(-1), gate128, pmod, smod,
                        npt * (TM // CMB), tiles_per_batch * (TM // CMB), split)


C_HEADS = 16
C_NOPE = 128
C_ROPE = 64
C_LORA = 512
ROPE_THETA = 10000.0
MLA_SCALE = (C_NOPE + C_ROPE) ** -0.5
HP = 128


def _rope_tables(pos):
    half = C_ROPE // 2
    inv = ROPE_THETA ** (-jnp.arange(half, dtype=F32) / half)
    ang = pos.astype(F32)[:, None] * inv[None, :]
    z = jnp.zeros_like(ang)
    cos = jnp.concatenate([jnp.cos(ang), jnp.cos(ang), z, z], axis=1)
    sin = jnp.concatenate([-jnp.sin(ang), jnp.sin(ang), z, z], axis=1)
    return cos, sin


def _rope128(x, cos, sin):
    n = x.shape[1]
    lane = lax.broadcasted_iota(jnp.int32, x.shape, 1) % HP
    partner = jnp.where(lane < C_ROPE // 2, pltpu.roll(x, n - C_ROPE // 2, axis=1), pltpu.roll(x, C_ROPE // 2, axis=1))
    return x * cos + partner * sin


def _mla_prep_kernel(z_ref, gq_ref, gkv_ref, gkr_ref, gqn_ref, gqr_ref, gkn_ref, wn_ref, wr_ref, wuk_ref,
                     cos_ref, sin_ref, lat_ref, kr_ref, krb_ref, qn_ref, qr_ref, qabs_ref, qg_ref, *, npt):
    i = pl.program_id(0)
    z = z_ref[...]
    cos = cos_ref[...]
    sin = sin_ref[...]
    lat_ref[...] = _rms(z[:, C_LORA:2 * C_LORA], gkv_ref[...])
    k128 = z[:, 2 * C_LORA:2 * C_LORA + HP]
    k128 = k128 * lax.rsqrt(jnp.sum(k128 * k128, axis=-1, keepdims=True) / C_ROPE + EPS) * gkr_ref[...]
    k128 = _rope128(k128, cos, sin)
    kr_ref[...] = k128[:, 0:C_ROPE]
    krb_ref[...] = k128.astype(BF16)
    cqn = _rms(z[:, 0:C_LORA], gq_ref[...]).astype(BF16)
    qn_all = jnp.dot(cqn, wn_ref[...], preferred_element_type=F32)
    qr_all = jnp.dot(cqn, wr_ref[...], preferred_element_type=F32)
    for h in range(C_HEADS):
        sl = slice(HP * h, HP * (h + 1))
        qn = _rms(qn_all[:, sl], gqn_ref[...]) * MLA_SCALE
        qn_ref[:, sl] = qn.astype(BF16)
        qg_ref[:, sl] = qn * gkn_ref[...]
        x = qr_all[:, sl]
        x = x * lax.rsqrt(jnp.sum(x * x, axis=-1, keepdims=True) / C_ROPE + EPS) * gqr_ref[...]
        qr_ref[:, sl] = (_rope128(x, cos, sin) * MLA_SCALE).astype(BF16)

    @pl.when(i == npt)
    def _():
        for h in range(C_HEADS):
            sl = slice(HP * h, HP * (h + 1))
            qabs_ref[:, C_LORA * h:C_LORA * (h + 1)] = _dot1(qg_ref[:, sl], wuk_ref[:, sl], NT).astype(BF16)


def _mla_prep(z1, gq, gkv, gkr, gqn, gqr, gkn, wn, wr, wuk, cos, sin, npt):
    t = z1.shape[0]
    zw = z1.shape[1]
    row = lambda w: pl.BlockSpec((TM, w), lambda i: (i, 0))
    full = lambda a: pl.BlockSpec(a.shape, lambda i: (0,) * a.ndim)
    pad64 = lambda g: jnp.concatenate([g, jnp.zeros((HP - C_ROPE,), F32)]).reshape(1, HP)
    args = (z1, gq.reshape(1, -1), gkv.reshape(1, -1), pad64(gkr), gqn.reshape(1, -1), pad64(gqr), gkn.reshape(1, -1),
            wn, wr, wuk, cos, sin)
    return pl.pallas_call(
        functools.partial(_mla_prep_kernel, npt=npt),
        grid=(t // TM,),
        in_specs=[row(zw)] + [full(a) for a in args[1:10]] + [row(HP), row(HP)],
        out_specs=[row(C_LORA), row(C_ROPE), row(HP), row(C_HEADS * HP), row(C_HEADS * HP),
                   pl.BlockSpec((TM, C_HEADS * C_LORA), lambda i: (0, 0))],
        out_shape=[jax.ShapeDtypeStruct((t, C_LORA), F32), jax.ShapeDtypeStruct((t, C_ROPE), F32),
                   jax.ShapeDtypeStruct((t, HP), BF16), jax.ShapeDtypeStruct((t, C_HEADS * HP), BF16),
                   jax.ShapeDtypeStruct((t, C_HEADS * HP), BF16), jax.ShapeDtypeStruct((TM, C_HEADS * C_LORA), BF16)],
        scratch_shapes=[pltpu.VMEM((TM, C_HEADS * HP), F32)],
        compiler_params=_cparams(("arbitrary",)),
        name="mla_prep",
    )(*args)


def _mla_kv_kernel(lat_ref, wuk_ref, wuv_ref, gkn_ref, kn_ref, v_ref):
    lat = lat_ref[...].astype(BF16)
    kn = jnp.dot(lat, wuk_ref[...], preferred_element_type=F32)
    for h in range(C_HEADS):
        sl = slice(HP * h, HP * (h + 1))
        kn_ref[:, sl] = _rms(kn[:, sl], gkn_ref[...]).astype(BF16)
    v_ref[...] = jnp.dot(lat, wuv_ref[...], preferred_element_type=F32).astype(BF16)


def _mla_kv(lat, wuk, wuv, gkn, rows):
    wide = C_HEADS * HP
    return pl.pallas_call(
        _mla_kv_kernel,
        grid=(rows // TM,),
        in_specs=[pl.BlockSpec((TM, C_LORA), lambda i: (i, 0)),
                  pl.BlockSpec((C_LORA, wide), lambda i: (0, 0)),
                  pl.BlockSpec((C_LORA, wide), lambda i: (0, 0)),
                  pl.BlockSpec((1, HP), lambda i: (0, 0))],
        out_specs=[pl.BlockSpec((TM, wide), lambda i: (i, 0))] * 2,
        out_shape=[jax.ShapeDtypeStruct((rows, wide), BF16)] * 2,
        compiler_params=_cparams(("parallel",)),
        name="mla_kv",
    )(lat, wuk, wuv, gkn.reshape(1, HP))


FQ = 512


def _mla_flash_kernel(qn_ref, qr_ref, kn_ref, kr_ref, v_ref, o_ref, kcat):
    qi = pl.program_id(2)

    @pl.when(qi == 0)
    def _():
        kcat[:, 0:HP] = kn_ref[...]
        kcat[:, HP:2 * HP] = kr_ref[...]

    q = jnp.concatenate([qn_ref[...], qr_ref[...]], axis=1)

    def block(kb, carry, diagonal):
        m, l, acc = carry
        ks = pl.ds(pl.multiple_of(kb * FQ, FQ), FQ)
        s = lax.dot_general(q, kcat[ks, :], (NT, ((), ())), preferred_element_type=F32)
        if diagonal:
            row = lax.broadcasted_iota(jnp.int32, (FQ, FQ), 0)
            col = lax.broadcasted_iota(jnp.int32, (FQ, FQ), 1)
            s = jnp.where(col <= row, s, -jnp.inf)
        m_new = jnp.maximum(m, jnp.max(s, axis=-1, keepdims=True))
        alpha = jnp.exp(m - m_new)
        p = jnp.exp(s - m_new)
        l = alpha * l + jnp.sum(p, axis=-1, keepdims=True)
        acc = alpha * acc + jnp.dot(p.astype(BF16), v_ref[ks, :], preferred_element_type=F32)
        return m_new, l, acc

    m0 = jnp.full((FQ, 1), -jnp.inf, F32)
    l0 = jnp.zeros((FQ, 1), F32)
    a0 = jnp.zeros((FQ, HP), F32)
    carry = lax.fori_loop(0, qi, lambda kb, c: block(kb, c, False), (m0, l0, a0))
    m, l, acc = block(qi, carry, True)
    o_ref[...] = (acc / l).astype(BF16)


def _mla_flash(qn, qr, kn, krb, v, nb, seq):
    nq = seq // FQ
    qspec = pl.BlockSpec((FQ, HP), lambda b, h, q: (b * nq + q, h))
    kspec = pl.BlockSpec((seq, HP), lambda b, h, q: (b, h))
    return pl.pallas_call(
        _mla_flash_kernel,
        grid=(nb, C_HEADS, nq),
        in_specs=[qspec, qspec, kspec, pl.BlockSpec((seq, HP), lambda b, h, q: (b, 0)), kspec],
        out_specs=qspec,
        out_shape=jax.ShapeDtypeStruct((nb * seq, C_HEADS * HP), BF16),
        scratch_shapes=[pltpu.VMEM((seq, 2 * HP), BF16)],
        compiler_params=_cparams(("parallel", "parallel", "arbitrary")),
        name="mla_flash",
    )(qn, qr, kn, krb, v)


PPS = 32
PPG = 4


def _mla_s_kernel(*refs, n_steps, n_batch, ci):
    (pt_ref, lat_hbm, kr_hbm, wukt_ref, qabs_ref, qr_ref, latn_ref, krn_ref, o_ref,
     wst, latb, lg, m_ref, l_ref, acc_ref, latbuf, krbuf, lsem, ksem) = refs
    b = pl.program_id(0)
    kt = pl.program_id(1)
    nq = qabs_ref.shape[1]
    nheads = C_HEADS
    ntok = nq // nheads

    step = b * n_steps + kt
    slot = step % 2

    def issue_pages(first_page, s):
        for k in range(PPS):
            page = pt_ref[first_page + k]
            pltpu.make_async_copy(lat_hbm.at[ci, page], latbuf.at[s, k], lsem.at[s]).start()
            pltpu.make_async_copy(kr_hbm.at[ci, page], krbuf.at[s, k], ksem.at[s]).start()

    @pl.when(step == 0)
    def _():
        issue_pages(0, 0)

    pltpu.make_async_copy(lat_hbm.at[ci, pl.ds(0, PPS)], latbuf.at[slot], lsem.at[slot]).wait()
    pltpu.make_async_copy(kr_hbm.at[ci, pl.ds(0, PPS)], krbuf.at[slot], ksem.at[slot]).wait()

    @pl.when(step + 1 < n_batch * n_steps)
    def _():
        issue_pages((step + 1) * PPS, 1 - slot)

    @pl.when(jnp.logical_and(b == 0, kt == 0))
    def _():
        wst[0:nheads * HP, :] = wukt_ref[...]

    @pl.when(kt == 0)
    def _():
        wst[nheads * HP:nheads * HP + nq, :] = qabs_ref[0]
        m_ref[...] = jnp.full_like(m_ref, -jnp.inf)
        l_ref[...] = jnp.zeros_like(l_ref)
        acc_ref[...] = jnp.zeros_like(acc_ref)

    qr = qr_ref[0]

    def up_project(lat_bf16):
        return lax.dot_general(wst[...], lat_bf16, (NT, ((), ())), preferred_element_type=F32)

    def key_logits(r, krt_f32):
        nkeys = r.shape[1]
        kraw = r[0:nheads * HP].reshape(nheads, HP, nkeys)
        rs = lax.rsqrt(jnp.sum(kraw * kraw, axis=1) / HP + EPS)
        lraw = r[nheads * HP:nheads * HP + nq].reshape(ntok, nheads, nkeys) * rs[None]
        return lraw.reshape(nq, nkeys) + _dot1(qr, krt_f32)

    def update(state, logits, lat_bf16):
        m_old, l_old, acc_old = state
        m_new = jnp.maximum(m_old, jnp.max(logits, axis=-1, keepdims=True))
        alpha = jnp.exp(m_old - m_new)
        p = jnp.exp(logits - m_new)
        l_new = alpha * l_old + jnp.sum(p, axis=-1, keepdims=True)
        acc_new = alpha * acc_old + jnp.dot(p.astype(BF16), lat_bf16, preferred_element_type=F32)
        return m_new, l_new, acc_new

    def load_group(g):
        pages = range(PPG * g, PPG * (g + 1))
        return jnp.concatenate([latbuf[slot, p] for p in pages], axis=0).astype(BF16)

    gk = PPG * PAGE
    for g in range(PPS // PPG):
        lat_g = load_group(g)
        krt_g = jnp.concatenate([krbuf[slot, p] for p in range(PPG * g, PPG * (g + 1))], axis=1)
        latb[gk * g:gk * (g + 1), :] = lat_g
        lg[:, gk * g:gk * (g + 1)] = key_logits(up_project(lat_g), krt_g)
    state = update((m_ref[...], l_ref[...], acc_ref[...]), lg[...], latb[...])
    m_ref[...], l_ref[...], acc_ref[...] = state

    @pl.when(kt == n_steps - 1)
    def _():
        latn = jnp.concatenate([latn_ref[0], jnp.zeros((PAGE - 8, C_LORA), F32)], axis=0).astype(BF16)
        logits = key_logits(up_project(latn), krn_ref[0])
        qtok = lax.broadcasted_iota(jnp.int32, (nq, PAGE), 0) // nheads
        slot = lax.broadcasted_iota(jnp.int32, (nq, PAGE), 1)
        _, l_fin, acc_fin = update(state, jnp.where(slot <= qtok, logits, -jnp.inf), latn)
        o_ref[0] = acc_fin / l_fin


def _mla_sample(page_table, lat_cache, kr_cache, ci, wukt, qabs, qr, lat_new8, kr_new8):
    nb, npages = page_table.shape
    n_steps = npages // PPS
    nq = qabs.shape[1]

    kr_cache_t = jnp.swapaxes(kr_cache, 2, 3)
    kr_new_t = jnp.pad(jnp.swapaxes(kr_new8, 1, 2), ((0, 0), (0, 0), (0, PAGE - kr_new8.shape[1])))
    grid_spec = pltpu.PrefetchScalarGridSpec(
        num_scalar_prefetch=1,
        grid=(nb, n_steps),
        in_specs=[pl.BlockSpec(memory_space=pl.ANY), pl.BlockSpec(memory_space=pl.ANY),
                  pl.BlockSpec(wukt.shape, lambda b, t, pt: (0, 0)),
                    pl.BlockSpec((1, nq, C_LORA), lambda b, t, pt: (b, 0, 0)),
                    pl.BlockSpec((1, nq, C_ROPE), lambda b, t, pt: (b, 0, 0)),
                    pl.BlockSpec((1, 8, C_LORA), lambda b, t, pt: (b, 0, 0)),
                    pl.BlockSpec((1, C_ROPE, PAGE), lambda b, t, pt: (b, 0, 0))],
        out_specs=pl.BlockSpec((1, nq, C_LORA), lambda b, t, pt: (b, 0, 0)),
        scratch_shapes=[pltpu.VMEM((C_HEADS * HP + nq, C_LORA), BF16),
                        pltpu.VMEM((PPS * PAGE, C_LORA), BF16),
                        pltpu.VMEM((nq, PPS * PAGE), F32),
                        pltpu.VMEM((nq, 1), F32), pltpu.VMEM((nq, 1), F32), pltpu.VMEM((nq, C_LORA), F32),
                        pltpu.VMEM((2, PPS, PAGE, C_LORA), F32), pltpu.VMEM((2, PPS, C_ROPE, PAGE), F32),
                        pltpu.SemaphoreType.DMA((2,)), pltpu.SemaphoreType.DMA((2,))],
    )
    return pl.pallas_call(
        functools.partial(_mla_s_kernel, n_steps=n_steps, n_batch=nb, ci=ci),
        grid_spec=grid_spec,
        out_shape=jax.ShapeDtypeStruct((nb, nq, C_LORA), F32),
        compiler_params=_cparams(("arbitrary", "arbitrary")),
        name="mla_sample",
    )(page_table.reshape(-1), lat_cache, kr_cache_t, wukt, qabs, qr, lat_new8, kr_new_t)


def _mla_uv_kernel(c_ref, w_ref, o_ref):
    o_ref[...] = _dot1(c_ref[...], w_ref[0])


def _mla_uv(ctx, wuv3):
    rows = ctx.shape[0]
    return pl.pallas_call(
        _mla_uv_kernel,
        grid=(C_HEADS,),
        in_specs=[pl.BlockSpec((rows, C_LORA), lambda h: (0, h)),
                  pl.BlockSpec((1, C_LORA, HP), lambda h: (h, 0, 0))],
        out_specs=pl.BlockSpec((rows, HP), lambda h: (0, h)),
        out_shape=jax.ShapeDtypeStruct((rows, C_HEADS * HP), F32),
        compiler_params=_cparams(("parallel",)),
        name="mla_uv",
    )(ctx, wuv3)


def kernel(x_prompt, x_sample, state_hgrn, cache_swa_kv, cache_mla_latent, cache_mla_krope, page_table, c_prompt, c_sample, ada_w, ada_b, norm_g, rel_bias, ab_w_in, ab_w_out, hgrn_lb_logits, hgrn_norm_g, swa_q_g, swa_k_g, mla_w_in, mla_q_a_g, mla_kv_a_g, mla_w_uq, mla_w_uk, mla_w_uv, mla_qn_g, mla_qr_g, mla_kn_g, mla_kr_g, mla_w_o, moe_router_w, moe_router_b, moe_w_gu, moe_b_gu, moe_w_dn, moe_b_dn):
    nb, seq, _ = x_prompt.shape
    nsb, nst, _ = x_sample.shape
    tp, ts = nb * seq, nsb * nst
    assert ts == TM and seq % TM == 0 and nst <= 4
    npt, tpb = tp // TM, seq // TM
    x = jnp.concatenate([x_prompt.reshape(tp, D), x_sample.reshape(ts, D)], axis=0).astype(F32)

    nc = nb + nsb
    c_all = jnp.concatenate([c_prompt, c_sample, jnp.zeros((-nc % 8, D), c_prompt.dtype)], axis=0).astype(F32)
    mods = _adaln(c_all, ada_w, ada_b)

    def mod(s):
        return mods[s, 0:nb].reshape(nb, 1, 3 * D), jnp.repeat(mods[s, nb:nc], nst, axis=0)

    pm, sm = mod(0)
    z = _normmod_mm(x, norm_g[0, 0], pm, sm, ab_w_in[0].astype(BF16), 1792, npt, tpb)
    zs3 = z[tp:].reshape(nsb, nst, z.shape[1])
    oa_p, hg_p = _hgrn_prompt(z, hgrn_lb_logits, hgrn_norm_g[0], nb, seq, 0)
    oa_s, hg_s = _hgrn_sample(zs3, hgrn_lb_logits, hgrn_norm_g[0], state_hgrn, 0)
    ob_p, k_p, v_p = _dilated_prompt(z, swa_q_g[0], swa_k_g[0], _band_tables(rel_bias), nb, seq)
    buf_len = cache_swa_kv.shape[2]
    ob_s, k_s, v_s = _dilated_sample(zs3, swa_q_g[0], swa_k_g[0], cache_swa_kv[0], _sample_tables(rel_bias, nst, buf_len))
    w_out = ab_w_out[0].astype(BF16)
    h = _proj_residual([oa_p, ob_p], [oa_s.reshape(ts, 1024), ob_s.reshape(ts, 1024)],
                       [w_out[:1024], w_out[1024:]], x, pm, sm, npt, tpb)
    pm, sm = mod(1)
    h = _moe_layer(h, norm_g[0, 1], pm, sm, 0, moe_router_w, moe_router_b, moe_w_gu, moe_b_gu, moe_w_dn, moe_b_dn, npt, tpb)

    pm, sm = mod(2)
    w_in1 = jnp.pad(mla_w_in[0], ((0, 0), (0, HP - C_ROPE))).astype(BF16)
    z1 = _normmod_mm(h, norm_g[1, 0], pm, sm, w_in1, w_in1.shape[1], npt, tpb)
    past = page_table.shape[1] * PAGE
    pos = jnp.concatenate([jnp.tile(jnp.arange(seq), nb), jnp.tile(past + jnp.arange(nst), nsb)])
    cos, sin = _rope_tables(pos)
    wq = mla_w_uq[0]
    wn = wq[:, :, :C_NOPE].reshape(C_LORA, C_HEADS * HP).astype(BF16)
    wr = jnp.pad(wq[:, :, C_NOPE:], ((0, 0), (0, 0), (0, HP - C_ROPE))).reshape(C_LORA, C_HEADS * HP).astype(BF16)
    wuk = mla_w_uk[0].reshape(C_LORA, C_HEADS * HP).astype(BF16)
    wuv = mla_w_uv[0].reshape(C_LORA, C_HEADS * HP).astype(BF16)
    lat, kr, krb, qn, qr, qabs = _mla_prep(z1, mla_q_a_g[0], mla_kv_a_g[0], mla_kr_g[0], mla_qn_g[0], mla_qr_g[0],
                                           mla_kn_g[0], wn, wr, wuk, cos, sin, npt)
    kn, v = _mla_kv(lat, wuk, wuv, mla_kn_g[0], tp)
    ctx_p = _mla_flash(qn, qr, kn, krb, v, nb, seq)
    qabs3 = qabs.reshape(nsb, nst * C_HEADS, C_LORA)
    qr_s = qr[tp:].reshape(ts, C_HEADS, HP)[:, :, :C_ROPE].reshape(nsb, nst * C_HEADS, C_ROPE)
    lat_new8 = jnp.pad(lat[tp:].reshape(nsb, nst, C_LORA), ((0, 0), (0, 8 - nst), (0, 0)))
    kr_new8 = jnp.pad(kr[tp:].reshape(nsb, nst, C_ROPE), ((0, 0), (0, 8 - nst), (0, 0)))
    ctx_s = _mla_sample(page_table, cache_mla_latent, cache_mla_krope, 0, wuk.T, qabs3, qr_s, lat_new8, kr_new8)
    wuv3 = jnp.transpose(mla_w_uv[0], (1, 0, 2)).astype(BF16)
    o_s = _mla_uv(ctx_s.reshape(ts, C_HEADS * C_LORA), wuv3)
    h = _proj_residual([ctx_p], [o_s], [mla_w_o[0].astype(BF16)], h, pm, sm, npt, tpb)
    pm, sm = mod(3)
    hp, hs = _moe_layer(h, norm_g[1, 1], pm, sm, 1, moe_router_w, moe_router_b, moe_w_gu, moe_b_gu, moe_w_dn, moe_b_dn,
                        npt, tpb, split=True)

    y_prompt = hp.reshape(nb, seq, D).astype(x_prompt.dtype)
    y_sample = hs.reshape(nsb, nst, D).astype(x_sample.dtype)
    hgrn_prompt = hg_p[None].astype(state_hgrn.dtype)
    hgrn_sample = hg_s.astype(state_hgrn.dtype)
    kv = lambda k, v, b, t: jnp.stack([k.reshape(b, t, 8, 128), v.reshape(b, t, 8, 128)], axis=2)[None]
    swa_prompt = kv(k_p, v_p, nb, seq).astype(cache_swa_kv.dtype)
    swa_sample = kv(k_s, v_s, nsb, nst).astype(cache_swa_kv.dtype)
    lat_prompt = lat[:tp].reshape(1, nb, seq, C_LORA).astype(cache_mla_latent.dtype)
    lat_sample = lat[tp:].reshape(1, nsb, nst, C_LORA).astype(cache_mla_latent.dtype)
    krope_prompt = kr[:tp].reshape(1, nb, seq, C_ROPE).astype(cache_mla_krope.dtype)
    krope_sample = kr[tp:].reshape(1, nsb, nst, C_ROPE).astype(cache_mla_krope.dtype)
    return (y_prompt, y_sample, hgrn_prompt, hgrn_sample, swa_prompt, swa_sample,
            lat_prompt, lat_sample, krope_prompt, krope_sample)
```
